```python
import jax, jax.numpy as jnp
from jax import lax
import numpy as np

D_MODEL = 1024
BATCH = 2
SEQ = 8192
DEPTH = 2

N_MIXERS = 2
N_POOL_LAYERS = (DEPTH + 1) // 2
N_MLA_LAYERS = DEPTH // 2
DEEPNORM_ALPHA = (2 * DEPTH) ** 0.25
DEEPNORM_BETA = (8 * DEPTH) ** -0.25
LN_EPS = 1e-5
RMS_EPS = 1e-6
POOL_WINDOWS = (2, 4, 8, 16)
POOL_GROUPS = 4
POOL_CH = D_MODEL // POOL_GROUPS
N_HEADS = 8
Q_LORA = 384
KV_LORA = 256
QK_NOPE = 128
QK_ROPE = 64
V_HEAD = 128
ROPE_THETA = 10000.0
Q_BLOCK = 128
N_EXPERTS = 256
TOP_K = 8
N_GROUPS = 8
TOPK_GROUPS = 4
EXPERT_FF = 256
SHARED_FF = 256
ROUTED_SCALE = 2.5
DISPATCH_BLOCK = 128

kernel_name = "hybrid_pool_mla_moe_deepnorm"

F32 = jnp.float32


def layer_norm(x, g, b):
    xf = x.astype(F32)
    mu = jnp.mean(xf, axis=-1, keepdims=True)
    var = jnp.mean(jnp.square(xf - mu), axis=-1, keepdims=True)
    return ((xf - mu) * lax.rsqrt(var + LN_EPS) * g.astype(F32) + b.astype(F32)).astype(x.dtype)


def rms_norm(x, g):
    xf = x.astype(F32)
    y = xf * lax.rsqrt(jnp.mean(jnp.square(xf), axis=-1, keepdims=True) + RMS_EPS)
    return (y * g.astype(F32)).astype(x.dtype)


def rope_tables(seq):
    pos = jnp.arange(seq, dtype=F32)
    inv = 1.0 / (ROPE_THETA ** (jnp.arange(0, QK_ROPE, 2, dtype=F32) / QK_ROPE))
    ang = pos[:, None] * inv[None, :]
    return jnp.cos(ang), jnp.sin(ang)


def apply_rope(x, cos, sin):
    xf = x.astype(F32)
    x1, x2 = jnp.split(xf, 2, axis=-1)
    return jnp.concatenate([x1 * cos - x2 * sin, x1 * sin + x2 * cos], axis=-1).astype(x.dtype)


def pool_mixer(x, w_pool, b_pool, scale):
    B, S, D = x.shape
    xg = x.astype(F32).reshape(B, S, POOL_GROUPS, POOL_CH)
    csum = jnp.concatenate([jnp.zeros((B, 1, POOL_GROUPS, POOL_CH), F32), jnp.cumsum(xg, axis=1)], axis=1)
    half = jnp.array(POOL_WINDOWS, jnp.int32) // 2
    pos = jnp.arange(S, dtype=jnp.int32)[:, None]
    lo = jnp.clip(pos - half[None, :], 0, S)
    hi = jnp.clip(pos + half[None, :], 0, S)
    gi = jnp.arange(POOL_GROUPS)[None, :]
    win_sum = csum[:, hi, gi, :] - csum[:, lo, gi, :]
    count = (hi - lo).astype(F32)[None, :, :, None]
    pooled = (win_sum / count - xg).astype(x.dtype)
    y = jnp.einsum('bsgc,gcd->bsgd', pooled, w_pool).reshape(B, S, D) + b_pool
    return y * scale


def mla_mixer(x, w_a, q_norm_g, kv_norm_g, w_uq, w_uk, w_uv, w_o):
    B, S, D = x.shape
    a = x @ w_a
    q_lat = rms_norm(a[..., :Q_LORA], q_norm_g)
    kv_lat = rms_norm(a[..., Q_LORA:Q_LORA + KV_LORA], kv_norm_g)
    cos, sin = rope_tables(S)
    k_rope = apply_rope(a[..., Q_LORA + KV_LORA:], cos, sin)
    q = jnp.einsum('bsl,lhd->bshd', q_lat, w_uq)
    q_nope = q[..., :QK_NOPE]
    q_rope = apply_rope(q[..., QK_NOPE:], cos[:, None, :], sin[:, None, :])
    k_nope = jnp.einsum('bsl,lhd->bshd', kv_lat, w_uk)
    v = jnp.einsum('bsl,lhd->bshd', kv_lat, w_uv)
    sm_scale = (QK_NOPE + QK_ROPE) ** -0.5
    nb = S // Q_BLOCK
    qn_b = q_nope.reshape(B, nb, Q_BLOCK, N_HEADS, QK_NOPE).transpose(1, 0, 2, 3, 4)
    qr_b = q_rope.reshape(B, nb, Q_BLOCK, N_HEADS, QK_ROPE).transpose(1, 0, 2, 3, 4)

    def attend(blk):
        qn, qr = blk
        s = (jnp.einsum('bqhd,bkhd->bhqk', qn, k_nope, preferred_element_type=F32)
             + jnp.einsum('bqhd,bkd->bhqk', qr, k_rope, preferred_element_type=F32)) * sm_scale
        p = jax.nn.softmax(s, axis=-1).astype(v.dtype)
        return jnp.einsum('bhqk,bkhd->bqhd', p, v)

    o = lax.map(attend, (qn_b, qr_b))
    o = o.transpose(1, 0, 2, 3, 4).reshape(B, S, N_HEADS * V_HEAD)
    return o @ w_o


def moe_ffn(h, router_w, router_b, w_gate, w_up, w_down, s_gate, s_up, s_down):
    B, S, D = h.shape
    t = h.reshape(-1, D)
    N = t.shape[0]
    scores = jax.nn.sigmoid(jnp.dot(t, router_w, preferred_element_type=F32))
    biased = scores + router_b.astype(F32)
    per = N_EXPERTS // N_GROUPS
    group_score = lax.top_k(biased.reshape(N, N_GROUPS, per), 2)[0].sum(-1)
    _, gsel = lax.top_k(group_score, TOPK_GROUPS)
    gmask = jax.nn.one_hot(gsel, N_GROUPS, dtype=F32).sum(1) > 0
    emask = jnp.repeat(gmask, per, axis=1)
    _, eidx = lax.top_k(jnp.where(emask, biased, -jnp.inf), TOP_K)
    gate = jnp.take_along_axis(scores, eidx, axis=1)
    gate = gate / jnp.sum(gate, axis=-1, keepdims=True) * ROUTED_SCALE

    A = N * TOP_K
    flat_e = eidx.reshape(-1)
    order = jnp.argsort(flat_e)
    sorted_e = flat_e[order]
    counts = jnp.bincount(flat_e, length=N_EXPERTS)
    padded = (counts + DISPATCH_BLOCK - 1) // DISPATCH_BLOCK * DISPATCH_BLOCK
    start = jnp.cumsum(counts) - counts
    pend = jnp.cumsum(padded)
    pstart = pend - padded
    dest = pstart[sorted_e] + jnp.arange(A, dtype=jnp.int32) - start[sorted_e]
    n_blocks = -(-(A + N_EXPERTS * (DISPATCH_BLOCK - 1)) // DISPATCH_BLOCK)
    P = n_blocks * DISPATCH_BLOCK
    slot_tok = jnp.full((P,), N, jnp.int32).at[dest].set((order // TOP_K).astype(jnp.int32))
    slot_gate = jnp.zeros((P,), F32).at[dest].set(gate.reshape(-1)[order])
    blk_start = jnp.arange(n_blocks, dtype=jnp.int32) * DISPATCH_BLOCK
    blk_exp = jnp.minimum(jnp.searchsorted(pend, blk_start, side='right'), N_EXPERTS - 1)
    t_pad = jnp.concatenate([t, jnp.zeros((1, D), t.dtype)], axis=0)

    def body(acc, blk):
        idx, g, e = blk
        xb = t_pad[idx]
        hb = jax.nn.silu(xb @ w_gate[e]) * (xb @ w_up[e])
        yb = jnp.dot(hb, w_down[e], preferred_element_type=F32) * g[:, None]
        return acc.at[idx].add(yb), None

    acc, _ = lax.scan(body, jnp.zeros((N + 1, D), F32),
                      (slot_tok.reshape(n_blocks, DISPATCH_BLOCK),
                       slot_gate.reshape(n_blocks, DISPATCH_BLOCK), blk_exp))
    shared = (jax.nn.silu(t @ s_gate) * (t @ s_up)) @ s_down
    return (acc[:N] + shared.astype(F32)).astype(h.dtype).reshape(B, S, D)


def setup_inputs(seed: int = 0) -> dict:
    key = jax.random.key(seed)
    ks = jax.random.split(key, 24)
    D = D_MODEL
    beta = DEEPNORM_BETA

    def nrm(k, shape, s):
        return jax.random.normal(k, shape, F32) * s

    def gain(k, shape):
        return 1.0 + 0.02 * jax.random.normal(k, shape, F32)

    return {
        "x": nrm(ks[0], (BATCH, SEQ, D), 1.0),
        "pool_w": nrm(ks[1], (N_POOL_LAYERS, POOL_GROUPS, POOL_CH, POOL_CH), POOL_CH ** -0.5 * beta),
        "pool_b": nrm(ks[2], (N_POOL_LAYERS, D), 0.01),
        "pool_scale": gain(ks[3], (N_POOL_LAYERS, D)),
        "mla_w_a": nrm(ks[4], (N_MLA_LAYERS, D, Q_LORA + KV_LORA + QK_ROPE), D ** -0.5),
        "mla_q_norm_g": gain(ks[5], (N_MLA_LAYERS, Q_LORA)),
        "mla_kv_norm_g": gain(ks[6], (N_MLA_LAYERS, KV_LORA)),
        "mla_w_uq": nrm(ks[7], (N_MLA_LAYERS, Q_LORA, N_HEADS, QK_NOPE + QK_ROPE), Q_LORA ** -0.5),
        "mla_w_uk": nrm(ks[8], (N_MLA_LAYERS, KV_LORA, N_HEADS, QK_NOPE), KV_LORA ** -0.5),
        "mla_w_uv": nrm(ks[9], (N_MLA_LAYERS, KV_LORA, N_HEADS, V_HEAD), KV_LORA ** -0.5 * beta),
        "mla_w_o": nrm(ks[10], (N_MLA_LAYERS, N_HEADS * V_HEAD, D), (N_HEADS * V_HEAD) ** -0.5 * beta),
        "ln_mix_g": gain(ks[11], (DEPTH, D)),
        "ln_mix_b": nrm(ks[12], (DEPTH, D), 0.01),
        "router_w": nrm(ks[13], (DEPTH, D, N_EXPERTS), D ** -0.5),
        "router_b": nrm(ks[14], (DEPTH, N_EXPERTS), 0.01),
        "exp_w_gate": nrm(ks[15], (DEPTH, N_EXPERTS, D, EXPERT_FF), D ** -0.5),
        "exp_w_up": nrm(ks[16], (DEPTH, N_EXPERTS, D, EXPERT_FF), D ** -0.5),
        "exp_w_down": nrm(ks[17], (DEPTH, N_EXPERTS, EXPERT_FF, D), EXPERT_FF ** -0.5 * beta),
        "sh_w_gate": nrm(ks[18], (DEPTH, D, SHARED_FF), D ** -0.5),
        "sh_w_up": nrm(ks[19], (DEPTH, D, SHARED_FF), D ** -0.5),
        "sh_w_down": nrm(ks[20], (DEPTH, SHARED_FF, D), SHARED_FF ** -0.5 * beta),
        "ln_ffn_g": gain(ks[21], (DEPTH, D)),
        "ln_ffn_b": nrm(ks[22], (DEPTH, D), 0.01),
    }


def reference(x, pool_w, pool_b, pool_scale, mla_w_a, mla_q_norm_g, mla_kv_norm_g, mla_w_uq,
              mla_w_uk, mla_w_uv, mla_w_o, ln_mix_g, ln_mix_b, router_w, router_b, exp_w_gate,
              exp_w_up, exp_w_down, sh_w_gate, sh_w_up, sh_w_down, ln_ffn_g, ln_ffn_b):
    h = x
    for i in range(DEPTH):
        j = i // N_MIXERS
        if i % N_MIXERS == 0:
            m = pool_mixer(h, pool_w[j], pool_b[j], pool_scale[j])
        else:
            m = mla_mixer(h, mla_w_a[j], mla_q_norm_g[j], mla_kv_norm_g[j], mla_w_uq[j],
                          mla_w_uk[j], mla_w_uv[j], mla_w_o[j])
        h = layer_norm(DEEPNORM_ALPHA * h + m, ln_mix_g[i], ln_mix_b[i])
        f = moe_ffn(h, router_w[i], router_b[i], exp_w_gate[i], exp_w_up[i], exp_w_down[i],
                    sh_w_gate[i], sh_w_up[i], sh_w_down[i])
        h = layer_norm(DEEPNORM_ALPHA * h + f, ln_ffn_g[i], ln_ffn_b[i])
    return h
```

```python
import functools
import math

import jax
import jax.numpy as jnp
from jax import lax
from jax.experimental import pallas as pl
from jax.experimental.pallas import tpu as pltpu

F32 = jnp.float32
BF16 = jnp.bfloat16

LN_EPS = 1e-5
RMS_EPS = 1e-6
POOL_WINDOWS = (2, 4, 8, 16)
POOL_HALO = 8
ROPE_THETA = 10000.0
TOP_K = 8
N_GROUPS = 8
TOPK_GROUPS = 4
ROUTED_SCALE = 2.5
LANES = 128
VMEM_LIMIT = 48 * 1024 * 1024

_NT = (((1,), (1,)), ((), ()))


def _pick(n, pref):
    t = min(pref, n)
    while n % t:
        t //= 2
    assert t >= 8 and n % t == 0, (n, pref)
    return t


def _layer_norm(z, g, b):
    mu = jnp.mean(z, axis=-1, keepdims=True)
    zc = z - mu
    var = jnp.mean(zc * zc, axis=-1, keepdims=True)
    return zc * lax.rsqrt(var + LN_EPS) * g + b


def _sigmoid(x):
    return 1.0 / (1.0 + jnp.exp(-x))


def _params(sem):
    return pltpu.CompilerParams(dimension_semantics=sem, vmem_limit_bytes=VMEM_LIMIT)


def _pool_ln_kernel(xc_ref, xp_ref, xn_ref, w_ref, b_ref, sc_ref, g_ref, beta_ref, o_ref, ext_ref,
                    *, tile, seq, alpha):
    i = pl.program_id(1)
    n_tiles = pl.num_programs(1)
    xc = xc_ref[0]
    d_model = xc.shape[-1]
    ch = d_model // len(POOL_WINDOWS)
    ext_ref[0:POOL_HALO, :] = jnp.where(i > 0, xp_ref[0], 0.0)
    ext_ref[POOL_HALO:POOL_HALO + tile, :] = xc
    ext_ref[POOL_HALO + tile:2 * POOL_HALO + tile, :] = jnp.where(i < n_tiles - 1, xn_ref[0], 0.0)
    pos = i * tile + lax.broadcasted_iota(jnp.int32, (tile, 1), 0)
    outs = []
    for g, win in enumerate(POOL_WINDOWS):
        half = win // 2
        cols = slice(g * ch, (g + 1) * ch)
        acc = ext_ref[POOL_HALO - half:POOL_HALO - half + tile, cols]
        for j in range(-half + 1, half):
            acc = acc + ext_ref[POOL_HALO + j:POOL_HALO + j + tile, cols]
        cnt = (jnp.minimum(pos + half, seq) - jnp.maximum(pos - half, 0)).astype(F32)
        pooled = acc / cnt - xc[:, cols]
        outs.append(jnp.dot(pooled.astype(BF16), w_ref[g], preferred_element_type=F32))
    y = jnp.concatenate(outs, axis=-1)
    y = (y + b_ref[...]) * sc_ref[...]
    o_ref[0] = _layer_norm(alpha * xc + y, g_ref[...], beta_ref[...])


def _pool_ln(x, w, b, sc, g, beta, alpha):
    B, S, D = x.shape
    tile = _pick(S, 512)
    nh = tile // POOL_HALO
    last = S // POOL_HALO - 1
    row = lambda v: v.reshape(1, D)
    vec = pl.BlockSpec((1, D), lambda bi, i: (0, 0))
    return pl.pallas_call(
        functools.partial(_pool_ln_kernel, tile=tile, seq=S, alpha=alpha),
        grid=(B, S // tile),
        in_specs=[
            pl.BlockSpec((1, tile, D), lambda bi, i: (bi, i, 0)),
            pl.BlockSpec((1, POOL_HALO, D), lambda bi, i: (bi, jnp.maximum(i * nh - 1, 0), 0)),
            pl.BlockSpec((1, POOL_HALO, D), lambda bi, i: (bi, jnp.minimum((i + 1) * nh, last), 0)),
            pl.BlockSpec(w.shape, lambda bi, i: (0, 0, 0)),
            vec, vec, vec, vec,
        ],
        out_specs=pl.BlockSpec((1, tile, D), lambda bi, i: (bi, i, 0)),
        out_shape=jax.ShapeDtypeStruct((B, S, D), F32),
        scratch_shapes=[pltpu.VMEM((tile + 2 * POOL_HALO, D), F32)],
        compiler_params=_params(("parallel", "parallel")),
        name="pool_ln",
    )(x, x, x, w.astype(BF16), row(b), row(sc), row(g), row(beta))


def _router_kernel(h_ref, whi_ref, wlo_ref, rb_ref, tri_ref, eidx_ref, gate_ref, pos_ref, cnt_ref):
    i = pl.program_id(0)

    @pl.when(i == 0)
    def _():
        cnt_ref[...] = jnp.zeros_like(cnt_ref)

    x = h_ref[...]
    xhi = x.astype(BF16)
    xlo = (x - xhi.astype(F32)).astype(BF16)
    whi = whi_ref[...]
    logits = (lax.dot_general(whi, xhi, _NT, preferred_element_type=F32)
              + lax.dot_general(whi, xlo, _NT, preferred_element_type=F32)
              + lax.dot_general(wlo_ref[...], xhi, _NT, preferred_element_type=F32))
    n_exp, tile = logits.shape
    per = n_exp // N_GROUPS
    scores = _sigmoid(logits)
    biased = scores + rb_ref[...]
    neg = -jnp.inf

    sub = lax.broadcasted_iota(jnp.int32, (per, tile), 0).astype(F32)
    gscore = []
    for g in range(N_GROUPS):
        bg = biased[g * per:(g + 1) * per, :]
        m1 = jnp.max(bg, axis=0, keepdims=True)
        i1 = jnp.min(jnp.where(bg == m1, sub, float(per)), axis=0, keepdims=True)
        m2 = jnp.max(jnp.where(sub == i1, neg, bg), axis=0, keepdims=True)
        gscore.append(m1 + m2)
    masked = []
    for g in range(N_GROUPS):
        beat = jnp.zeros_like(gscore[g])
        for o in range(N_GROUPS):
            if o == g:
                continue
            wins = (gscore[o] >= gscore[g]) if o < g else (gscore[o] > gscore[g])
            beat = beat + jnp.where(wins, 1.0, 0.0)
        keep = jnp.broadcast_to(beat < float(TOPK_GROUPS), (per, tile))
        masked.append(jnp.where(keep, biased[g * per:(g + 1) * per, :], neg))
    masked = jnp.concatenate(masked, axis=0)

    row = lax.broadcasted_iota(jnp.int32, (n_exp, tile), 0).astype(F32)
    chosen = jnp.zeros((n_exp, tile), F32)
    idxs, gates = [], []
    for _ in range(TOP_K):
        m = jnp.max(masked, axis=0, keepdims=True)
        idx = jnp.min(jnp.where(masked == m, row, float(n_exp)), axis=0, keepdims=True)
        hit = row == idx
        gates.append(jnp.sum(jnp.where(hit, scores, 0.0), axis=0, keepdims=True))
        masked = jnp.where(hit, neg, masked)
        chosen = jnp.where(hit, 1.0, chosen)
        idxs.append(idx)
    gsum = gates[0]
    for k in range(1, TOP_K):
        gsum = gsum + gates[k]

    before = cnt_ref[...] + jnp.dot(chosen.astype(BF16), tri_ref[...], preferred_element_type=F32)
    for k in range(TOP_K):
        eidx_ref[k:k + 1, :] = idxs[k].astype(jnp.int32)
        gate_ref[k:k + 1, :] = gates[k] / gsum * ROUTED_SCALE
        pos_k = jnp.sum(jnp.where(row == idxs[k], before, 0.0), axis=0, keepdims=True)
        pos_ref[k:k + 1, :] = pos_k.astype(jnp.int32)
    cnt_ref[...] = cnt_ref[...] + jnp.sum(chosen, axis=1, keepdims=True)


def _router(h2, rw, rb):
    N, D = h2.shape
    E = rw.shape[1]
    tile = _pick(N, 512)
    wt = rw.T
    whi = wt.astype(BF16)
    wlo = (wt - whi.astype(F32)).astype(BF16)
    ar = jnp.arange(tile)
    tri = (ar[:, None] < ar[None, :]).astype(BF16)
    const = lambda shape: pl.BlockSpec(shape, lambda i: (0, 0))
    kn = pl.BlockSpec((TOP_K, tile), lambda i: (0, i))
    return pl.pallas_call(
        _router_kernel,
        grid=(N // tile,),
        in_specs=[pl.BlockSpec((tile, D), lambda i: (i, 0)), const((E, D)), const((E, D)),
                  const((E, 1)), const((tile, tile))],
        out_specs=[kn, kn, kn, const((E, 1))],
        out_shape=[jax.ShapeDtypeStruct((TOP_K, N), jnp.int32), jax.ShapeDtypeStruct((TOP_K, N), F32),
                   jax.ShapeDtypeStruct((TOP_K, N), jnp.int32), jax.ShapeDtypeStruct((E, 1), F32)],
        compiler_params=_params(("arbitrary",)),
        name="moe_router",
    )(h2, whi, wlo, rb.reshape(E, 1), tri)


def _row_copies(src_of, dst_of, idx_ref, tile, sem):
    def copies(t):
        return [pltpu.make_async_copy(src_of(k, t, idx_ref[0, 0, k * tile + t]),
                                      dst_of(k, t, idx_ref[0, 0, k * tile + t]), sem)
                for k in range(TOP_K)]
    return copies


def _dispatch_kernel(dest_ref, h_ref, xs_ref, sem, *, tile):
    copies = _row_copies(lambda k, t, d: h_ref.at[pl.ds(t, 1), :],
                         lambda k, t, d: xs_ref.at[pl.ds(d, 1), :], dest_ref, tile, sem)

    def start(t, c):
        for cp in copies(t):
            cp.start()
        return c

    def wait(t, c):
        for cp in copies(t):
            cp.wait()
        return c

    lax.fori_loop(0, tile, start, 0)
    lax.fori_loop(0, tile, wait, 0)


def _tile_major(dest, tile):
    k, n = dest.shape
    return dest.reshape(k, n // tile, tile).transpose(1, 0, 2).reshape(n // tile, 1, k * tile)


def _dispatch(h2, dest, n_slots):
    N, D = h2.shape
    tile = _pick(N, 256)
    return pl.pallas_call(
        functools.partial(_dispatch_kernel, tile=tile),
        grid=(N // tile,),
        in_specs=[pl.BlockSpec((1, 1, TOP_K * tile), lambda i: (i, 0, 0), memory_space=pltpu.SMEM),
                  pl.BlockSpec((tile, D), lambda i: (i, 0))],
        out_specs=pl.BlockSpec(memory_space=pl.ANY),
        out_shape=jax.ShapeDtypeStruct((n_slots, D), F32),
        scratch_shapes=[pltpu.SemaphoreType.DMA(())],
        compiler_params=_params(("arbitrary",)),
        name="moe_dispatch",
    )(_tile_major(dest, tile), h2)


def _expert_kernel(bidx_ref, bexp_ref, bval_ref, xs_ref, wg_ref, wu_ref, wd_ref, o_ref,
                   wg_s, wu_s, wd_s):
    i = pl.program_id(0)
    valid = bval_ref[i]

    @pl.when(valid > 0)
    def _():
        new_expert = jnp.logical_or(i == 0, bexp_ref[i] != bexp_ref[jnp.maximum(i - 1, 0)])

        @pl.when(new_expert)
        def _():
            wg_s[...] = wg_ref[...].astype(BF16)
            wu_s[...] = wu_ref[...].astype(BF16)
            wd_s[...] = wd_ref[...].astype(BF16)

        rows = lax.broadcasted_iota(jnp.int32, (xs_ref.shape[0], 1), 0)
        x = jnp.where(rows < valid, xs_ref[...], 0.0).astype(BF16)
        hg = jnp.dot(x, wg_s[...], preferred_element_type=F32)
        hu = jnp.dot(x, wu_s[...], preferred_element_type=F32)
        hb = hg * _sigmoid(hg) * hu
        o_ref[...] = jnp.dot(hb.astype(BF16), wd_s[...], preferred_element_type=F32)


def _experts(xs, bidx, bexp, bval, wg, wu, wd, layer, block):
    P, D = xs.shape
    Fe = wg.shape[-1]
    rows = pl.BlockSpec((block, D), lambda i, bi, be, bv: (bi[i], 0))
    return pl.pallas_call(
        _expert_kernel,
        grid_spec=pltpu.PrefetchScalarGridSpec(
            num_scalar_prefetch=3,
            grid=(P // block,),
            in_specs=[rows,
                      pl.BlockSpec((None, None, D, Fe), lambda i, bi, be, bv: (layer, be[i], 0, 0)),
                      pl.BlockSpec((None, None, D, Fe), lambda i, bi, be, bv: (layer, be[i], 0, 0)),
                      pl.BlockSpec((None, None, Fe, D), lambda i, bi, be, bv: (layer, be[i], 0, 0))],
            out_specs=rows,
            scratch_shapes=[pltpu.VMEM((D, Fe), BF16), pltpu.VMEM((D, Fe), BF16),
                            pltpu.VMEM((Fe, D), BF16)],
        ),
        out_shape=jax.ShapeDtypeStruct((P, D), F32),
        compiler_params=_params(("arbitrary",)),
        name="moe_experts",
    )(bidx, bexp, bval, xs, wg, wu, wd)


def _combine_kernel(dest_ref, h_ref, gate_ref, ys_ref, sg_ref, su_ref, sd_ref, g_ref, beta_ref,
                    o_ref, buf, sem, *, tile, alpha):
    copies = _row_copies(lambda k, t, d: ys_ref.at[pl.ds(d, 1), :],
                         lambda k, t, d: buf.at[k, pl.ds(t, 1), :], dest_ref, tile, sem)

    def start(t, c):
        for cp in copies(t):
            cp.start()
        return c

    def wait(t, c):
        for cp in copies(t):
            cp.wait()
        return c

    lax.fori_loop(0, tile, start, 0)
    h = h_ref[...]
    hb = h.astype(BF16)
    sgate = jnp.dot(hb, sg_ref[...], preferred_element_type=F32)
    sup = jnp.dot(hb, su_ref[...], preferred_element_type=F32)
    mid = sgate * _sigmoid(sgate) * sup
    acc = jnp.dot(mid.astype(BF16), sd_ref[...], preferred_element_type=F32)
    lax.fori_loop(0, tile, wait, 0)
    gate = gate_ref[...]
    for k in range(TOP_K):
        acc = acc + gate[:, k:k + 1] * buf[k]
    o_ref[...] = _layer_norm(alpha * h + acc, g_ref[...], beta_ref[...])


def _combine(h2, dest, gate_nk, ys, sg, su, sd, g, beta, alpha):
    N, D = h2.shape
    Fs = sg.shape[1]
    tile = _pick(N, 256)
    const = lambda shape: pl.BlockSpec(shape, lambda i: (0, 0))
    return pl.pallas_call(
        functools.partial(_combine_kernel, tile=tile, alpha=alpha),
        grid=(N // tile,),
        in_specs=[pl.BlockSpec((1, 1, TOP_K * tile), lambda i: (i, 0, 0), memory_space=pltpu.SMEM),
                  pl.BlockSpec((tile, D), lambda i: (i, 0)),
                  pl.BlockSpec((tile, TOP_K), lambda i: (i, 0)),
                  pl.BlockSpec(memory_space=pl.ANY),
                  const((D, Fs)), const((D, Fs)), const((Fs, D)), const((1, D)), const((1, D))],
        out_specs=pl.BlockSpec((tile, D), lambda i: (i, 0)),
        out_shape=jax.ShapeDtypeStruct((N, D), F32),
        scratch_shapes=[pltpu.VMEM((TOP_K, tile, D), F32), pltpu.SemaphoreType.DMA(())],
        compiler_params=_params(("arbitrary",)),
        name="moe_combine",
    )(_tile_major(dest, tile), h2, gate_nk, ys, sg.astype(BF16), su.astype(BF16), sd.astype(BF16),
      g.reshape(1, D), beta.reshape(1, D))


EXPERT_BLOCK = 256


def _moe_ln(h2, rw, rb, wg, wu, wd, layer, sg, su, sd, g, beta, alpha):
    N, D = h2.shape
    E = rw.shape[1]
    eidx, gate, pos, cnt = _router(h2, rw, rb)
    blk = EXPERT_BLOCK
    counts = cnt[:, 0].astype(jnp.int32)
    padded = (counts + blk - 1) // blk * blk
    pend = jnp.cumsum(padded)
    pstart = pend - padded
    dest = pstart[eidx] + pos
    n_blocks = -(-(N * TOP_K + E * (blk - 1)) // blk)
    n_used = pend[-1] // blk
    bidx = jnp.minimum(jnp.arange(n_blocks, dtype=jnp.int32), n_used - 1)
    bexp = jnp.minimum(jnp.searchsorted(pend, bidx * blk, side="right"), E - 1).astype(jnp.int32)
    bval = jnp.clip(counts[bexp] - (bidx * blk - pstart[bexp]), 0, blk)
    bval = jnp.where(jnp.arange(n_blocks) < n_used, bval, 0).astype(jnp.int32)
    xs = _dispatch(h2, dest, n_blocks * blk)
    ys = _experts(xs, bidx, bexp, bval, wg, wu, wd, layer, blk)
    return _combine(h2, dest, gate.T, ys, sg, su, sd, g, beta, alpha)


def _mla_proj_kernel(h_ref, wa_ref, qg_ref, kvg_ref, wq_ref, wkv_ref, cos_ref, sin_ref,
                     q_ref, k_ref, v_ref, *, q_lora, kv_lora, heads, q_scale):
    x = h_ref[0].astype(BF16)
    a = jnp.dot(x, wa_ref[...], preferred_element_type=F32)
    cos = cos_ref[...]
    sin = sin_ref[...]

    def rms(v, g):
        return v * lax.rsqrt(jnp.mean(v * v, axis=-1, keepdims=True) + RMS_EPS) * g

    q_lat = rms(a[:, :q_lora], qg_ref[...]).astype(BF16)
    kv_lat = rms(a[:, q_lora:q_lora + kv_lora], kvg_ref[...]).astype(BF16)
    o = q_lora + kv_lora
    k_rope = (a[:, o:o + LANES] * cos + a[:, o + LANES:o + 2 * LANES] * sin).astype(BF16)
    qa = jnp.dot(q_lat, wq_ref[...], preferred_element_type=F32)
    kv = jnp.dot(kv_lat, wkv_ref[...], preferred_element_type=F32)
    hd = heads * LANES
    for h in range(heads):
        c0 = slice(h * LANES, (h + 1) * LANES)
        c1 = slice(hd + h * LANES, hd + (h + 1) * LANES)
        c2 = slice(2 * hd + h * LANES, 2 * hd + (h + 1) * LANES)
        q_ref[0, :, 2 * h * LANES:(2 * h + 1) * LANES] = (qa[:, c0] * q_scale).astype(BF16)
        q_ref[0, :, (2 * h + 1) * LANES:(2 * h + 2) * LANES] = (
            (qa[:, c1] * cos + qa[:, c2] * sin) * q_scale).astype(BF16)
        k_ref[0, :, 2 * h * LANES:(2 * h + 1) * LANES] = kv[:, c0].astype(BF16)
        k_ref[0, :, (2 * h + 1) * LANES:(2 * h + 2) * LANES] = k_rope
    v_ref[0] = kv[:, hd:].astype(BF16)


def _rot_half(w):
    r = w.shape[-1] // 2
    return jnp.concatenate([-w[..., r:], w[..., :r]], axis=-1)


def _pad_lanes(w):
    return jnp.pad(w, [(0, 0)] * (w.ndim - 1) + [(0, LANES - w.shape[-1])])


def _flash_kernel(q_ref, k_ref, v_ref, o_ref, m_ref, l_ref, acc_ref, *, tk):
    q = q_ref[0]
    m_ref[...] = jnp.full_like(m_ref, -jnp.inf)
    l_ref[...] = jnp.zeros_like(l_ref)
    acc_ref[...] = jnp.zeros_like(acc_ref)

    def body(c, carry):
        off = pl.multiple_of(c * tk, tk)
        k = k_ref[0, pl.ds(off, tk), :]
        v = v_ref[0, pl.ds(off, tk), :]
        s = lax.dot_general(q, k, _NT, preferred_element_type=F32)
        m_prev = m_ref[...]
        m_new = jnp.maximum(m_prev, jnp.max(s, axis=1, keepdims=True))
        p = jnp.exp2(s - m_new)
        a = jnp.exp2(m_prev - m_new)
        l_ref[...] = a * l_ref[...] + jnp.sum(p, axis=1, keepdims=True)
        acc_ref[...] = a * acc_ref[...] + jnp.dot(p.astype(BF16), v, preferred_element_type=F32)
        m_ref[...] = m_new
        return carry

    lax.fori_loop(0, k_ref.shape[1] // tk, body, 0)
    o_ref[0] = (acc_ref[...] / l_ref[...]).astype(o_ref.dtype)


def _oproj_ln_kernel(o_ref, h_ref, w_ref, g_ref, beta_ref, out_ref, *, alpha):
    m = jnp.dot(o_ref[0], w_ref[...], preferred_element_type=F32)
    out_ref[0] = _layer_norm(alpha * h_ref[0] + m, g_ref[...], beta_ref[...])


def _mla_ln(h, w_a, qg, kvg, w_uq, w_uk, w_uv, w_o, g, beta, alpha):
    B, S, D = h.shape
    q_lora, heads, qk = w_uq.shape
    kv_lora, _, nope = w_uk.shape
    vd = w_uv.shape[-1]
    rope = qk - nope
    assert nope == LANES and vd == LANES and rope <= LANES and rope % 2 == 0
    kr = w_a[:, q_lora + kv_lora:]
    wa = jnp.concatenate([w_a[:, :q_lora + kv_lora], _pad_lanes(kr), _pad_lanes(_rot_half(kr))],
                         axis=1).astype(BF16)
    wq_r = w_uq[:, :, nope:]
    wq = jnp.concatenate([w_uq[:, :, :nope].reshape(q_lora, heads * nope),
                          _pad_lanes(wq_r).reshape(q_lora, heads * LANES),
                          _pad_lanes(_rot_half(wq_r)).reshape(q_lora, heads * LANES)], axis=1).astype(BF16)
    wkv = jnp.concatenate([w_uk.reshape(kv_lora, heads * nope), w_uv.reshape(kv_lora, heads * vd)],
                          axis=1).astype(BF16)
    pos = jnp.arange(S, dtype=F32)
    inv = 1.0 / (ROPE_THETA ** (jnp.arange(0, rope, 2, dtype=F32) / rope))
    ang = pos[:, None] * inv[None, :]
    cos = _pad_lanes(jnp.concatenate([jnp.cos(ang), jnp.cos(ang)], axis=1))
    sin = _pad_lanes(jnp.concatenate([jnp.sin(ang), jnp.sin(ang)], axis=1))
    q_scale = float(qk) ** -0.5 * math.log2(math.e)

    tp = _pick(S, 256)
    const = lambda a: pl.BlockSpec(a.shape, lambda b, i: (0,) * a.ndim)
    qg2, kvg2 = qg.reshape(1, q_lora), kvg.reshape(1, kv_lora)
    q, k, v = pl.pallas_call(
        functools.partial(_mla_proj_kernel, q_lora=q_lora, kv_lora=kv_lora, heads=heads, q_scale=q_scale),
        grid=(B, S // tp),
        in_specs=[pl.BlockSpec((1, tp, D), lambda b, i: (b, i, 0)), const(wa), const(qg2), const(kvg2),
                  const(wq), const(wkv),
                  pl.BlockSpec((tp, LANES), lambda b, i: (i, 0)), pl.BlockSpec((tp, LANES), lambda b, i: (i, 0))],
        out_specs=[pl.BlockSpec((1, tp, 2 * heads * LANES), lambda b, i: (b, i, 0)),
                   pl.BlockSpec((1, tp, 2 * heads * LANES), lambda b, i: (b, i, 0)),
                   pl.BlockSpec((1, tp, heads * LANES), lambda b, i: (b, i, 0))],
        out_shape=[jax.ShapeDtypeStruct((B, S, 2 * heads * LANES), BF16),
                   jax.ShapeDtypeStruct((B, S, 2 * heads * LANES), BF16),
                   jax.ShapeDtypeStruct((B, S, heads * LANES), BF16)],
        compiler_params=_params(("parallel", "parallel")),
        name="mla_proj",
    )(h, wa, qg2, kvg2, wq, wkv, cos, sin)

    tq = _pick(S, 512)
    tk = _pick(S, 512)
    o = pl.pallas_call(
        functools.partial(_flash_kernel, tk=tk),
        grid=(B, heads, S // tq),
        in_specs=[pl.BlockSpec((1, tq, 2 * LANES), lambda b, hh, i: (b, i, hh)),
                  pl.BlockSpec((1, S, 2 * LANES), lambda b, hh, i: (b, 0, hh)),
                  pl.BlockSpec((1, S, LANES), lambda b, hh, i: (b, 0, hh))],
        out_specs=pl.BlockSpec((1, tq, LANES), lambda b, hh, i: (b, i, hh)),
        out_shape=jax.ShapeDtypeStruct((B, S, heads * LANES), BF16),
        scratch_shapes=[pltpu.VMEM((tq, 1), F32), pltpu.VMEM((tq, 1), F32), pltpu.VMEM((tq, LANES), F32)],
        compiler_params=_params(("parallel", "parallel", "arbitrary")),
        name="mla_flash",
    )(q, k, v)

    to = _pick(S, 512)
    vec = pl.BlockSpec((1, D), lambda b, i: (0, 0))
    return pl.pallas_call(
        functools.partial(_oproj_ln_kernel, alpha=alpha),
        grid=(B, S // to),
        in_specs=[pl.BlockSpec((1, to, heads * vd), lambda b, i: (b, i, 0)),
                  pl.BlockSpec((1, to, D), lambda b, i: (b, i, 0)),
                  pl.BlockSpec((heads * vd, D), lambda b, i: (0, 0)), vec, vec],
        out_specs=pl.BlockSpec((1, to, D), lambda b, i: (b, i, 0)),
        out_shape=jax.ShapeDtypeStruct((B, S, D), F32),
        compiler_params=_params(("parallel", "parallel")),
        name="mla_oproj_ln",
    )(o, h, w_o.astype(BF16), g.reshape(1, D), beta.reshape(1, D))


def kernel(x, pool_w, pool_b, pool_scale, mla_w_a, mla_q_norm_g, mla_kv_norm_g, mla_w_uq, mla_w_uk,
           mla_w_uv, mla_w_o, ln_mix_g, ln_mix_b, router_w, router_b, exp_w_gate, exp_w_up, exp_w_down,
           sh_w_gate, sh_w_up, sh_w_down, ln_ffn_g, ln_ffn_b):
    B, S, D = x.shape
    depth = ln_mix_g.shape[0]
    alpha = (2 * depth) ** 0.25
    h = x
    for i in range(depth):
        j = i // 2
        if i % 2 == 0:
            h = _pool_ln(h, pool_w[j], pool_b[j], pool_scale[j], ln_mix_g[i], ln_mix_b[i], alpha)
        else:
            h = _mla_ln(h, mla_w_a[j], mla_q_norm_g[j], mla_kv_norm_g[j], mla_w_uq[j], mla_w_uk[j],
                        mla_w_uv[j], mla_w_o[j], ln_mix_g[i], ln_mix_b[i], alpha)
        h = _moe_ln(h.reshape(B * S, D), router_w[i], router_b[i], exp_w_gate, exp_w_up, exp_w_down, i,
                    sh_w_gate[i], sh_w_up[i], sh_w_down[i], ln_ffn_g[i], ln_ffn_b[i],
                    alpha).reshape(B, S, D)
    return h
```

```python
import functools
import math

import jax
import jax.numpy as jnp
from jax import lax
from jax.experimental import pallas as pl
from jax.experimental.pallas import tpu as pltpu

F32 = jnp.float32
BF16 = jnp.bfloat16

LN_EPS = 1e-5
RMS_EPS = 1e-6
POOL_WINDOWS = (2, 4, 8, 16)
POOL_HALO = 8
ROPE_THETA = 10000.0
TOP_K = 8
N_GROUPS = 8
TOPK_GROUPS = 4
ROUTED_SCALE = 2.5
LANES = 128
VMEM_LIMIT = 48 * 1024 * 1024

_NT = (((1,), (1,)), ((), ()))


def _pick(n, pref):
    t = min(pref, n)
    while n % t:
        t //= 2
    assert t >= 8 and n % t == 0, (n, pref)
    return t


def _layer_norm(z, g, b):
    mu = jnp.mean(z, axis=-1, keepdims=True)
    zc = z - mu
    var = jnp.mean(zc * zc, axis=-1, keepdims=True)
    return zc * lax.rsqrt(var + LN_EPS) * g + b


def _sigmoid(x):
    return 1.0 / (1.0 + jnp.exp(-x))


def _params(sem):
    return pltpu.CompilerParams(dimension_semantics=sem, vmem_limit_bytes=VMEM_LIMIT)


def _pool_ln_kernel(xc_ref, xp_ref, xn_ref, w_ref, b_ref, sc_ref, g_ref, beta_ref, o_ref, ext_ref,
                    *, tile, seq, alpha):
    i = pl.program_id(1)
    n_tiles = pl.num_programs(1)
    xc = xc_ref[0]
    d_model = xc.shape[-1]
    ch = d_model // len(POOL_WINDOWS)
    ext_ref[0:POOL_HALO, :] = jnp.where(i > 0, xp_ref[0], 0.0)
    ext_ref[POOL_HALO:POOL_HALO + tile, :] = xc
    ext_ref[POOL_HALO + tile:2 * POOL_HALO + tile, :] = jnp.where(i < n_tiles - 1, xn_ref[0], 0.0)
    pos = i * tile + lax.broadcasted_iota(jnp.int32, (tile, 1), 0)
    outs = []
    for g, win in enumerate(POOL_WINDOWS):
        half = win // 2
        cols = slice(g * ch, (g + 1) * ch)
        acc = ext_ref[POOL_HALO - half:POOL_HALO - half + tile, cols]
        for j in range(-half + 1, half):
            acc = acc + ext_ref[POOL_HALO + j:POOL_HALO + j + tile, cols]
        cnt = (jnp.minimum(pos + half, seq) - jnp.maximum(pos - half, 0)).astype(F32)
        pooled = acc / cnt - xc[:, cols]
        outs.append(jnp.dot(pooled.astype(BF16), w_ref[g], preferred_element_type=F32))
    y = jnp.concatenate(outs, axis=-1)
    y = (y + b_ref[...]) * sc_ref[...]
    o_ref[0] = _layer_norm(alpha * xc + y, g_ref[...], beta_ref[...])


def _pool_ln(x, w, b, sc, g, beta, alpha):
    B, S, D = x.shape
    tile = _pick(S, 512)
    nh = tile // POOL_HALO
    last = S // POOL_HALO - 1
    row = lambda v: v.reshape(1, D)
    vec = pl.BlockSpec((1, D), lambda bi, i: (0, 0))
    return pl.pallas_call(
        functools.partial(_pool_ln_kernel, tile=tile, seq=S, alpha=alpha),
        grid=(B, S // tile),
        in_specs=[
            pl.BlockSpec((1, tile, D), lambda bi, i: (bi, i, 0)),
            pl.BlockSpec((1, POOL_HALO, D), lambda bi, i: (bi, jnp.maximum(i * nh - 1, 0), 0)),
            pl.BlockSpec((1, POOL_HALO, D), lambda bi, i: (bi, jnp.minimum((i + 1) * nh, last), 0)),
            pl.BlockSpec(w.shape, lambda bi, i: (0, 0, 0)),
            vec, vec, vec, vec,
        ],
        out_specs=pl.BlockSpec((1, tile, D), lambda bi, i: (bi, i, 0)),
        out_shape=jax.ShapeDtypeStruct((B, S, D), F32),
        scratch_shapes=[pltpu.VMEM((tile + 2 * POOL_HALO, D), F32)],
        compiler_params=_params(("parallel", "parallel")),
        name="pool_ln",
    )(x, x, x, w.astype(BF16), row(b), row(sc), row(g), row(beta))


def _router_kernel(h_ref, whi_ref, wlo_ref, rb_ref, tri_ref, eidx_ref, gate_ref, pos_ref, cnt_ref):
    i = pl.program_id(0)

    @pl.when(i == 0)
    def _():
        cnt_ref[...] = jnp.zeros_like(cnt_ref)

    x = h_ref[...]
    xhi = x.astype(BF16)
    xlo = (x - xhi.astype(F32)).astype(BF16)
    whi = whi_ref[...]
    logits = (lax.dot_general(whi, xhi, _NT, preferred_element_type=F32)
              + lax.dot_general(whi, xlo, _NT, preferred_element_type=F32)
              + lax.dot_general(wlo_ref[...], xhi, _NT, preferred_element_type=F32))
    n_exp, tile = logits.shape
    per = n_exp // N_GROUPS
    scores = _sigmoid(logits)
    biased = scores + rb_ref[...]
    neg = -jnp.inf

    sub = lax.broadcasted_iota(jnp.int32, (per, tile), 0).astype(F32)
    gscore = []
    for g in range(N_GROUPS):
        bg = biased[g * per:(g + 1) * per, :]
        m1 = jnp.max(bg, axis=0, keepdims=True)
        i1 = jnp.min(jnp.where(bg == m1, sub, float(per)), axis=0, keepdims=True)
        m2 = jnp.max(jnp.where(sub == i1, neg, bg), axis=0, keepdims=True)
        gscore.append(m1 + m2)
    masked = []
    for g in range(N_GROUPS):
        beat = jnp.zeros_like(gscore[g])
        for o in range(N_GROUPS):
            if o == g:
                continue
            wins = (gscore[o] >= gscore[g]) if o < g else (gscore[o] > gscore[g])
            beat = beat + jnp.where(wins, 1.0, 0.0)
        keep = jnp.broadcast_to(beat < float(TOPK_GROUPS), (per, tile))
        masked.append(jnp.where(keep, biased[g * per:(g + 1) * per, :], neg))
    masked = jnp.concatenate(masked, axis=0)

    row = lax.broadcasted_iota(jnp.int32, (n_exp, tile), 0).astype(F32)
    chosen = jnp.zeros((n_exp, tile), F32)
    idxs, gates = [], []
    for _ in range(TOP_K):
        m = jnp.max(masked, axis=0, keepdims=True)
        idx = jnp.min(jnp.where(masked == m, row, float(n_exp)), axis=0, keepdims=True)
        hit = row == idx
        gates.append(jnp.sum(jnp.where(hit, scores, 0.0), axis=0, keepdims=True))
        masked = jnp.where(hit, neg, masked)
        chosen = jnp.where(hit, 1.0, chosen)
        idxs.append(idx)
    gsum = gates[0]
    for k in range(1, TOP_K):
        gsum = gsum + gates[k]

    before = cnt_ref[...] + jnp.dot(chosen.astype(BF16), tri_ref[...], preferred_element_type=F32)
    for k in range(TOP_K):
        eidx_ref[k:k + 1, :] = idxs[k].astype(jnp.int32)
        gate_ref[k:k + 1, :] = gates[k] / gsum * ROUTED_SCALE
        pos_k = jnp.sum(jnp.where(row == idxs[k], before, 0.0), axis=0, keepdims=True)
        pos_ref[k:k + 1, :] = pos_k.astype(jnp.int32)
    cnt_ref[...] = cnt_ref[...] + jnp.sum(chosen, axis=1, keepdims=True)


def _router(h2, rw, rb):
    N, D = h2.shape
    E = rw.shape[1]
    tile = _pick(N, 512)
    wt = rw.T
    whi = wt.astype(BF16)
    wlo = (wt - whi.astype(F32)).astype(BF16)
    ar = jnp.arange(tile)
    tri = (ar[:, None] < ar[None, :]).astype(BF16)
    const = lambda shape: pl.BlockSpec(shape, lambda i: (0, 0))
    kn = pl.BlockSpec((TOP_K, tile), lambda i: (0, i))
    return pl.pallas_call(
        _router_kernel,
        grid=(N // tile,),
        in_specs=[pl.BlockSpec((tile, D), lambda i: (i, 0)), const((E, D)), const((E, D)),
                  const((E, 1)), const((tile, tile))],
        out_specs=[kn, kn, kn, const((E, 1))],
        out_shape=[jax.ShapeDtypeStruct((TOP_K, N), jnp.int32), jax.ShapeDtypeStruct((TOP_K, N), F32),
                   jax.ShapeDtypeStruct((TOP_K, N), jnp.int32), jax.ShapeDtypeStruct((E, 1), F32)],
        compiler_params=_params(("arbitrary",)),
        name="moe_router",
    )(h2, whi, wlo, rb.reshape(E, 1), tri)


def _slot_kernel(eidx_ref, pos_ref, pstart_ref, dest_ref):
    n_exp = pstart_ref.shape[0]
    tile = eidx_ref.shape[1]
    row = lax.broadcasted_iota(jnp.int32, (n_exp, tile), 0)
    pstart = pstart_ref[...]
    for k in range(TOP_K):
        base = jnp.sum(jnp.where(row == eidx_ref[k:k + 1, :], pstart, 0.0), axis=0, keepdims=True)
        dest_ref[k:k + 1, :] = base.astype(jnp.int32) + pos_ref[k:k + 1, :]


def _slots(eidx, pos, pstart):
    K, N = eidx.shape
    E = pstart.shape[0]
    assert N * K + E * EXPERT_BLOCK < 2 ** 24
    tile = _pick(N, 512)
    kn = pl.BlockSpec((K, tile), lambda i: (0, i))
    return pl.pallas_call(
        _slot_kernel,
        grid=(N // tile,),
        in_specs=[kn, kn, pl.BlockSpec((E, 1), lambda i: (0, 0))],
        out_specs=kn,
        out_shape=jax.ShapeDtypeStruct((K, N), jnp.int32),
        compiler_params=_params(("parallel",)),
        name="moe_slots",
    )(eidx, pos, pstart.astype(F32).reshape(E, 1))


def _row_copies(src_of, dst_of, idx_ref, tile, sem):
    def copies(t):
        return [pltpu.make_async_copy(src_of(k, t, idx_ref[0, 0, k * tile + t]),
                                      dst_of(k, t, idx_ref[0, 0, k * tile + t]), sem)
                for k in range(TOP_K)]
    return copies


def _dispatch_kernel(dest_ref, h_ref, xs_ref, sem, *, tile):
    copies = _row_copies(lambda k, t, d: h_ref.at[pl.ds(t, 1), :],
                         lambda k, t, d: xs_ref.at[pl.ds(d, 1), :], dest_ref, tile, sem)

    def start(t, c):
        for cp in copies(t):
            cp.start()
        return c

    def wait(t, c):
        for cp in copies(t):
            cp.wait()
        return c

    lax.fori_loop(0, tile, start, 0)
    lax.fori_loop(0, tile, wait, 0)


def _tile_major(dest, tile):
    k, n = dest.shape
    return dest.reshape(k, n // tile, tile).transpose(1, 0, 2).reshape(n // tile, 1, k * tile)


def _dispatch(h2, dest, n_slots):
    N, D = h2.shape
    tile = _pick(N, 256)
    return pl.pallas_call(
        functools.partial(_dispatch_kernel, tile=tile),
        grid=(N // tile,),
        in_specs=[pl.BlockSpec((1, 1, TOP_K * tile), lambda i: (i, 0, 0), memory_space=pltpu.SMEM),
                  pl.BlockSpec((tile, D), lambda i: (i, 0))],
        out_specs=pl.BlockSpec(memory_space=pl.ANY),
        out_shape=jax.ShapeDtypeStruct((n_slots, D), F32),
        scratch_shapes=[pltpu.SemaphoreType.DMA(())],
        compiler_params=_params(("arbitrary",)),
        name="moe_dispatch",
    )(_tile_major(dest, tile), h2)


def _expert_kernel(bidx_ref, bexp_ref, bval_ref, xs_ref, wg_ref, wu_ref, wd_ref, o_ref,
                   wg_s, wu_s, wd_s):
    i = pl.program_id(0)
    valid = bval_ref[i]

    @pl.when(valid > 0)
    def _():
        new_expert = jnp.logical_or(i == 0, bexp_ref[i] != bexp_ref[jnp.maximum(i - 1, 0)])

        @pl.when(new_expert)
        def _():
            wg_s[...] = wg_ref[...].astype(BF16)
            wu_s[...] = wu_ref[...].astype(BF16)
            wd_s[...] = wd_ref[...].astype(BF16)

        rows = lax.broadcasted_iota(jnp.int32, (xs_ref.shape[0], 1), 0)
        x = jnp.where(rows < valid, xs_ref[...], 0.0).astype(BF16)
        hg = jnp.dot(x, wg_s[...], preferred_element_type=F32)
        hu = jnp.dot(x, wu_s[...], preferred_element_type=F32)
        hb = hg * _sigmoid(hg) * hu
        o_ref[...] = jnp.dot(hb.astype(BF16), wd_s[...], preferred_element_type=F32)


def _experts(xs, bidx, bexp, bval, wg, wu, wd, layer, block):
    P, D = xs.shape
    Fe = wg.shape[-1]
    rows = pl.BlockSpec((block, D), lambda i, bi, be, bv: (bi[i], 0))
    return pl.pallas_call(
        _expert_kernel,
        grid_spec=pltpu.PrefetchScalarGridSpec(
            num_scalar_prefetch=3,
            grid=(P // block,),
            in_specs=[rows,
                      pl.BlockSpec((None, None, D, Fe), lambda i, bi, be, bv: (layer, be[i], 0, 0)),
                      pl.BlockSpec((None, None, D, Fe), lambda i, bi, be, bv: (layer, be[i], 0, 0)),
                      pl.BlockSpec((None, None, Fe, D), lambda i, bi, be, bv: (layer, be[i], 0, 0))],
            out_specs=rows,
            scratch_shapes=[pltpu.VMEM((D, Fe), BF16), pltpu.VMEM((D, Fe), BF16),
                            pltpu.VMEM((Fe, D), BF16)],
        ),
        out_shape=jax.ShapeDtypeStruct((P, D), F32),
        compiler_params=_params(("arbitrary",)),
        name="moe_experts",
    )(bidx, bexp, bval, xs, wg, wu, wd)


def _combine_kernel(dest_ref, h_ref, gate_ref, ys_ref, sg_ref, su_ref, sd_ref, g_ref, beta_ref,
                    o_ref, buf, sem, *, tile, alpha):
    copies = _row_copies(lambda k, t, d: ys_ref.at[pl.ds(d, 1), :],
                         lambda k, t, d: buf.at[k, pl.ds(t, 1), :], dest_ref, tile, sem)

    def start(t, c):
        for cp in copies(t):
            cp.start()
        return c

    def wait(t, c):
        for cp in copies(t):
            cp.wait()
        return c

    lax.fori_loop(0, tile, start, 0)
    h = h_ref[...]
    hb = h.astype(BF16)
    sgate = jnp.dot(hb, sg_ref[...], preferred_element_type=F32)
    sup = jnp.dot(hb, su_ref[...], preferred_element_type=F32)
    mid = sgate * _sigmoid(sgate) * sup
    acc = jnp.dot(mid.astype(BF16), sd_ref[...], preferred_element_type=F32)
    lax.fori_loop(0, tile, wait, 0)
    gate = gate_ref[...]
    for k in range(TOP_K):
        acc = acc + gate[:, k:k + 1] * buf[k]
    o_ref[...] = _layer_norm(alpha * h + acc, g_ref[...], beta_ref[...])


def _combine(h2, dest, gate_nk, ys, sg, su, sd, g, beta, alpha):
    N, D = h2.shape
    Fs = sg.shape[1]
    tile = _pick(N, 256)
    const = lambda shape: pl.BlockSpec(shape, lambda i: (0, 0))
    return pl.pallas_call(
        functools.partial(_combine_kernel, tile=tile, alpha=alpha),
        grid=(N // tile,),
        in_specs=[pl.BlockSpec((1, 1, TOP_K * tile), lambda i: (i, 0, 0), memory_space=pltpu.SMEM),
                  pl.BlockSpec((tile, D), lambda i: (i, 0)),
                  pl.BlockSpec((tile, TOP_K), lambda i: (i, 0)),
                  pl.BlockSpec(memory_space=pl.ANY),
                  const((D, Fs)), const((D, Fs)), const((Fs, D)), const((1, D)), const((1, D))],
        out_specs=pl.BlockSpec((tile, D), lambda i: (i, 0)),
        out_shape=jax.ShapeDtypeStruct((N, D), F32),
        scratch_shapes=[pltpu.VMEM((TOP_K, tile, D), F32), pltpu.SemaphoreType.DMA(())],
        compiler_params=_params(("arbitrary",)),
        name="moe_combine",
    )(_tile_major(dest, tile), h2, gate_nk, ys, sg.astype(BF16), su.astype(BF16), sd.astype(BF16),
      g.reshape(1, D), beta.reshape(1, D))


FLASH_TQ = 512
FLASH_TK = 512
EXPERT_BLOCK = 256


def _moe_ln(h2, rw, rb, wg, wu, wd, layer, sg, su, sd, g, beta, alpha):
    N, D = h2.shape
    E = rw.shape[1]
    eidx, gate, pos, cnt = _router(h2, rw, rb)
    blk = EXPERT_BLOCK
    counts = cnt[:, 0].astype(jnp.int32)
    padded = (counts + blk - 1) // blk * blk
    pend = jnp.cumsum(padded)
    pstart = pend - padded
    dest = _slots(eidx, pos, pstart)
    n_blocks = -(-(N * TOP_K + E * (blk - 1)) // blk)
    n_used = pend[-1] // blk
    bidx = jnp.minimum(jnp.arange(n_blocks, dtype=jnp.int32), n_used - 1)
    bexp = jnp.minimum(jnp.searchsorted(pend, bidx * blk, side="right"), E - 1).astype(jnp.int32)
    bval = jnp.clip(counts[bexp] - (bidx * blk - pstart[bexp]), 0, blk)
    bval = jnp.where(jnp.arange(n_blocks) < n_used, bval, 0).astype(jnp.int32)
    xs = _dispatch(h2, dest, n_blocks * blk)
    ys = _experts(xs, bidx, bexp, bval, wg, wu, wd, layer, blk)
    return _combine(h2, dest, gate.T, ys, sg, su, sd, g, beta, alpha)


def _mla_proj_kernel(h_ref, wa_ref, qg_ref, kvg_ref, wqt_ref, wk_ref, wvt_ref, cos_ref, sin_ref,
                     cost_ref, sint_ref, qt_ref, k_ref, vt_ref, *, q_lora, kv_lora, heads, q_scale):
    x = h_ref[0].astype(BF16)
    a = jnp.dot(x, wa_ref[...], preferred_element_type=F32)

    def rms(v, g):
        return v * lax.rsqrt(jnp.mean(v * v, axis=-1, keepdims=True) + RMS_EPS) * g

    q_lat = rms(a[:, :q_lora], qg_ref[...]).astype(BF16)
    kv_lat = rms(a[:, q_lora:q_lora + kv_lora], kvg_ref[...]).astype(BF16)
    o = q_lora + kv_lora
    k_rope = (a[:, o:o + LANES] * cos_ref[...] + a[:, o + LANES:o + 2 * LANES] * sin_ref[...]).astype(BF16)
    qa = lax.dot_general(wqt_ref[...], q_lat, _NT, preferred_element_type=F32)
    kn = jnp.dot(kv_lat, wk_ref[...], preferred_element_type=F32)
    vt_ref[0] = lax.dot_general(wvt_ref[...], kv_lat, _NT, preferred_element_type=F32).astype(BF16)
    cost = cost_ref[...]
    sint = sint_ref[...]
    hd = heads * LANES
    for h in range(heads):
        r0 = slice(h * LANES, (h + 1) * LANES)
        r1 = slice(hd + h * LANES, hd + (h + 1) * LANES)
        r2 = slice(2 * hd + h * LANES, 2 * hd + (h + 1) * LANES)
        qt_ref[0, 2 * h * LANES:(2 * h + 1) * LANES, :] = (qa[r0] * q_scale).astype(BF16)
        qt_ref[0, (2 * h + 1) * LANES:(2 * h + 2) * LANES, :] = (
            (qa[r1] * cost + qa[r2] * sint) * q_scale).astype(BF16)
        k_ref[0, :, 2 * h * LANES:(2 * h + 1) * LANES] = kn[:, r0].astype(BF16)
        k_ref[0, :, (2 * h + 1) * LANES:(2 * h + 2) * LANES] = k_rope


def _rot_half(w):
    r = w.shape[-1] // 2
    return jnp.concatenate([-w[..., r:], w[..., :r]], axis=-1)


def _pad_lanes(w):
    return jnp.pad(w, [(0, 0)] * (w.ndim - 1) + [(0, LANES - w.shape[-1])])


def _flash_kernel(qt_ref, k_ref, vt_ref, o_ref, s_ref, mx_ref, m_ref, l_ref, acc_ref, *, tk):
    qt = qt_ref[0]
    nk = k_ref.shape[1] // tk
    assert nk % 2 == 0 and nk >= 2
    m_ref[...] = jnp.full_like(m_ref, -jnp.inf)
    l_ref[...] = jnp.zeros_like(l_ref)
    acc_ref[...] = jnp.zeros_like(acc_ref)

    def scores(c, slot):
        off = pl.multiple_of(c * tk, tk)
        s = jnp.dot(k_ref[0, pl.ds(off, tk), :], qt, preferred_element_type=F32)
        s_ref[slot] = s
        mx_ref[slot] = jnp.max(s, axis=0, keepdims=True)

    def update(c, slot):
        off = pl.multiple_of(c * tk, tk)
        vt = vt_ref[0, :, pl.ds(off, tk)]
        m_prev = m_ref[...]
        m_new = jnp.maximum(m_prev, mx_ref[slot])
        p = jnp.exp2(s_ref[slot] - m_new)
        a = jnp.exp2(m_prev - m_new)
        l_ref[...] = a * l_ref[...] + jnp.sum(p, axis=0, keepdims=True)
        acc_ref[...] = a * acc_ref[...] + jnp.dot(vt, p.astype(BF16), preferred_element_type=F32)
        m_ref[...] = m_new

    scores(0, 0)

    def body(i, carry):
        c = 2 * i
        scores(c + 1, 1)
        update(c, 0)
        scores(c + 2, 0)
        update(c + 1, 1)
        return carry

    lax.fori_loop(0, nk // 2 - 1, body, 0)
    scores(nk - 1, 1)
    update(nk - 2, 0)
    update(nk - 1, 1)
    o_ref[0] = (acc_ref[...] / l_ref[...]).T.astype(o_ref.dtype)


def _oproj_ln_kernel(o_ref, h_ref, w_ref, g_ref, beta_ref, out_ref, *, alpha):
    m = jnp.dot(o_ref[0], w_ref[...], preferred_element_type=F32)
    out_ref[0] = _layer_norm(alpha * h_ref[0] + m, g_ref[...], beta_ref[...])


def _mla_ln(h, w_a, qg, kvg, w_uq, w_uk, w_uv, w_o, g, beta, alpha):
    B, S, D = h.shape
    q_lora, heads, qk = w_uq.shape
    kv_lora, _, nope = w_uk.shape
    vd = w_uv.shape[-1]
    rope = qk - nope
    assert nope == LANES and vd == LANES and rope <= LANES and rope % 2 == 0
    kr = w_a[:, q_lora + kv_lora:]
    wa = jnp.concatenate([w_a[:, :q_lora + kv_lora], _pad_lanes(kr), _pad_lanes(_rot_half(kr))],
                         axis=1).astype(BF16)
    wq_r = w_uq[:, :, nope:]
    wq = jnp.concatenate([w_uq[:, :, :nope].reshape(q_lora, heads * nope),
                          _pad_lanes(wq_r).reshape(q_lora, heads * LANES),
                          _pad_lanes(_rot_half(wq_r)).reshape(q_lora, heads * LANES)], axis=1).astype(BF16)
    wk = w_uk.reshape(kv_lora, heads * nope).astype(BF16)
    wvt = w_uv.reshape(kv_lora, heads * vd).T.astype(BF16)
    wqt = wq.T
    pos = jnp.arange(S, dtype=F32)
    inv = 1.0 / (ROPE_THETA ** (jnp.arange(0, rope, 2, dtype=F32) / rope))
    ang = pos[:, None] * inv[None, :]
    cos = _pad_lanes(jnp.concatenate([jnp.cos(ang), jnp.cos(ang)], axis=1))
    sin = _pad_lanes(jnp.concatenate([jnp.sin(ang), jnp.sin(ang)], axis=1))
    q_scale = float(qk) ** -0.5 * math.log2(math.e)

    tp = _pick(S, 256)
    const = lambda a: pl.BlockSpec(a.shape, lambda b, i: (0,) * a.ndim)
    qg2, kvg2 = qg.reshape(1, q_lora), kvg.reshape(1, kv_lora)
    tok_major = pl.BlockSpec((tp, LANES), lambda b, i: (i, 0))
    feat_major = pl.BlockSpec((LANES, tp), lambda b, i: (0, i))
    qt, k, vt = pl.pallas_call(
        functools.partial(_mla_proj_kernel, q_lora=q_lora, kv_lora=kv_lora, heads=heads, q_scale=q_scale),
        grid=(B, S // tp),
        in_specs=[pl.BlockSpec((1, tp, D), lambda b, i: (b, i, 0)), const(wa), const(qg2), const(kvg2),
                  const(wqt), const(wk), const(wvt), tok_major, tok_major, feat_major, feat_major],
        out_specs=[pl.BlockSpec((1, 2 * heads * LANES, tp), lambda b, i: (b, 0, i)),
                   pl.BlockSpec((1, tp, 2 * heads * LANES), lambda b, i: (b, i, 0)),
                   pl.BlockSpec((1, heads * LANES, tp), lambda b, i: (b, 0, i))],
        out_shape=[jax.ShapeDtypeStruct((B, 2 * heads * LANES, S), BF16),
                   jax.ShapeDtypeStruct((B, S, 2 * heads * LANES), BF16),
                   jax.ShapeDtypeStruct((B, heads * LANES, S), BF16)],
        compiler_params=_params(("parallel", "parallel")),
        name="mla_proj",
    )(h, wa, qg2, kvg2, wqt, wk, wvt, cos, sin, cos.T, sin.T)

    tq = _pick(S, FLASH_TQ)
    tk = _pick(S, FLASH_TK)
    o = pl.pallas_call(
        functools.partial(_flash_kernel, tk=tk),
        grid=(B, heads, S // tq),
        in_specs=[pl.BlockSpec((1, 2 * LANES, tq), lambda b, hh, i: (b, hh, i)),
                  pl.BlockSpec((1, S, 2 * LANES), lambda b, hh, i: (b, 0, hh)),
                  pl.BlockSpec((1, LANES, S), lambda b, hh, i: (b, hh, 0))],
        out_specs=pl.BlockSpec((1, tq, LANES), lambda b, hh, i: (b, i, hh)),
        out_shape=jax.ShapeDtypeStruct((B, S, heads * LANES), BF16),
        scratch_shapes=[pltpu.VMEM((2, tk, tq), F32), pltpu.VMEM((2, 1, tq), F32),
                        pltpu.VMEM((1, tq), F32), pltpu.VMEM((1, tq), F32), pltpu.VMEM((LANES, tq), F32)],
        compiler_params=_params(("parallel", "parallel", "arbitrary")),
        name="mla_flash",
    )(qt, k, vt)

    to = _pick(S, 512)
    vec = pl.BlockSpec((1, D), lambda b, i: (0, 0))
    return pl.pallas_call(
        functools.partial(_oproj_ln_kernel, alpha=alpha),
        grid=(B, S // to),
        in_specs=[pl.BlockSpec((1, to, heads * vd), lambda b, i: (b, i, 0)),
                  pl.BlockSpec((1, to, D), lambda b, i: (b, i, 0)),
                  pl.BlockSpec((heads * vd, D), lambda b, i: (0, 0)), vec, vec],
        out_specs=pl.BlockSpec((1, to, D), lambda b, i: (b, i, 0)),
        out_shape=jax.ShapeDtypeStruct((B, S, D), F32),
        compiler_params=_params(("parallel", "parallel")),
        name="mla_oproj_ln",
    )(o, h, w_o.astype(BF16), g.reshape(1, D), beta.reshape(1, D))


def kernel(x, pool_w, pool_b, pool_scale, mla_w_a, mla_q_norm_g, mla_kv_norm_g, mla_w_uq, mla_w_uk,
           mla_w_uv, mla_w_o, ln_mix_g, ln_mix_b, router_w, router_b, exp_w_gate, exp_w_up, exp_w_down,
           sh_w_gate, sh_w_up, sh_w_down, ln_ffn_g, ln_ffn_b):
    B, S, D = x.shape
    depth = ln_mix_g.shape[0]
    alpha = (2 * depth) ** 0.25
    h = x
    for i in range(depth):
        j = i // 2
        if i % 2 == 0:
            h = _pool_ln(h, pool_w[j], pool_b[j], pool_scale[j], ln_mix_g[i], ln_mix_b[i], alpha)
        else:
            h = _mla_ln(h, mla_w_a[j], mla_q_norm_g[j], mla_kv_norm_g[j], mla_w_uq[j], mla_w_uk[j],
                        mla_w_uv[j], mla_w_o[j], ln_mix_g[i], ln_mix_b[i], alpha)
        h = _moe_ln(h.reshape(B * S, D), router_w[i], router_b[i], exp_w_gate, exp_w_up, exp_w_down, i,
                    sh_w_gate[i], sh_w_up[i], sh_w_down[i], ln_ffn_g[i], ln_ffn_b[i],
                    alpha).reshape(B, S, D)
    return h
```

```python
import functools
import math

import jax
import jax.numpy as jnp
from jax import lax
from jax.experimental import pallas as pl
from jax.experimental.pallas import tpu as pltpu

F32 = jnp.float32
BF16 = jnp.bfloat16

LN_EPS = 1e-5
RMS_EPS = 1e-6
POOL_WINDOWS = (2, 4, 8, 16)
POOL_HALO = 8
ROPE_THETA = 10000.0
TOP_K = 8
N_GROUPS = 8
TOPK_GROUPS = 4
ROUTED_SCALE = 2.5
LANES = 128
VMEM_LIMIT = 48 * 1024 * 1024

_NT = (((1,), (1,)), ((), ()))


def _pick(n, pref):
    t = min(pref, n)
    while n % t:
        t //= 2
    assert t >= 8 and n % t == 0, (n, pref)
    return t


def _layer_norm(z, g, b):
    mu = jnp.mean(z, axis=-1, keepdims=True)
    zc = z - mu
    var = jnp.mean(zc * zc, axis=-1, keepdims=True)
    return zc * lax.rsqrt(var + LN_EPS) * g + b


def _sigmoid(x):
    return 1.0 / (1.0 + jnp.exp(-x))


def _params(sem):
    return pltpu.CompilerParams(dimension_semantics=sem, vmem_limit_bytes=VMEM_LIMIT)


def _pool_ln_kernel(xc_ref, xp_ref, xn_ref, w_ref, b_ref, sc_ref, g_ref, beta_ref, o_ref, ext_ref,
                    *, tile, seq, alpha):
    i = pl.program_id(1)
    n_tiles = pl.num_programs(1)
    xc = xc_ref[0]
    d_model = xc.shape[-1]
    ch = d_model // len(POOL_WINDOWS)
    ext_ref[0:POOL_HALO, :] = jnp.where(i > 0, xp_ref[0], 0.0)
    ext_ref[POOL_HALO:POOL_HALO + tile, :] = xc
    ext_ref[POOL_HALO + tile:2 * POOL_HALO + tile, :] = jnp.where(i < n_tiles - 1, xn_ref[0], 0.0)
    pos = i * tile + lax.broadcasted_iota(jnp.int32, (tile, 1), 0)
    outs = []
    for g, win in enumerate(POOL_WINDOWS):
        half = win // 2
        cols = slice(g * ch, (g + 1) * ch)
        acc = ext_ref[POOL_HALO - half:POOL_HALO - half + tile, cols]
        for j in range(-half + 1, half):
            acc = acc + ext_ref[POOL_HALO + j:POOL_HALO + j + tile, cols]
        cnt = (jnp.minimum(pos + half, seq) - jnp.maximum(pos - half, 0)).astype(F32)
        pooled = acc / cnt - xc[:, cols]
        outs.append(jnp.dot(pooled.astype(BF16), w_ref[g], preferred_element_type=F32))
    y = jnp.concatenate(outs, axis=-1)
    y = (y + b_ref[...]) * sc_ref[...]
    o_ref[0] = _layer_norm(alpha * xc + y, g_ref[...], beta_ref[...])


def _pool_ln(x, w, b, sc, g, beta, alpha):
    B, S, D = x.shape
    tile = _pick(S, 512)
    nh = tile // POOL_HALO
    last = S // POOL_HALO - 1
    row = lambda v: v.reshape(1, D)
    vec = pl.BlockSpec((1, D), lambda bi, i: (0, 0))
    return pl.pallas_call(
        functools.partial(_pool_ln_kernel, tile=tile, seq=S, alpha=alpha),
        grid=(B, S // tile),
        in_specs=[
            pl.BlockSpec((1, tile, D), lambda bi, i: (bi, i, 0)),
            pl.BlockSpec((1, POOL_HALO, D), lambda bi, i: (bi, jnp.maximum(i * nh - 1, 0), 0)),
            pl.BlockSpec((1, POOL_HALO, D), lambda bi, i: (bi, jnp.minimum((i + 1) * nh, last), 0)),
            pl.BlockSpec(w.shape, lambda bi, i: (0, 0, 0)),
            vec, vec, vec, vec,
        ],
        out_specs=pl.BlockSpec((1, tile, D), lambda bi, i: (bi, i, 0)),
        out_shape=jax.ShapeDtypeStruct((B, S, D), F32),
        scratch_shapes=[pltpu.VMEM((tile + 2 * POOL_HALO, D), F32)],
        compiler_params=_params(("parallel", "parallel")),
        name="pool_ln",
    )(x, x, x, w.astype(BF16), row(b), row(sc), row(g), row(beta))


def _router_kernel(h_ref, whi_ref, wlo_ref, rb_ref, tri_ref, eidx_ref, gate_ref, pos_ref, cnt_ref):
    i = pl.program_id(0)

    @pl.when(i == 0)
    def _():
        cnt_ref[...] = jnp.zeros_like(cnt_ref)

    x = h_ref[...]
    xhi = x.astype(BF16)
    xlo = (x - xhi.astype(F32)).astype(BF16)
    whi = whi_ref[...]
    logits = (lax.dot_general(whi, xhi, _NT, preferred_element_type=F32)
              + lax.dot_general(whi, xlo, _NT, preferred_element_type=F32)
              + lax.dot_general(wlo_ref[...], xhi, _NT, preferred_element_type=F32))
    n_exp, tile = logits.shape
    per = n_exp // N_GROUPS
    scores = _sigmoid(logits)
    biased = scores + rb_ref[...]
    neg = -jnp.inf

    sub = lax.broadcasted_iota(jnp.int32, (per, tile), 0).astype(F32)
    gscore = []
    for g in range(N_GROUPS):
        bg = biased[g * per:(g + 1) * per, :]
        m1 = jnp.max(bg, axis=0, keepdims=True)
        i1 = jnp.min(jnp.where(bg == m1, sub, float(per)), axis=0, keepdims=True)
        m2 = jnp.max(jnp.where(sub == i1, neg, bg), axis=0, keepdims=True)
        gscore.append(m1 + m2)
    masked = []
    for g in range(N_GROUPS):
        beat = jnp.zeros_like(gscore[g])
        for o in range(N_GROUPS):
            if o == g:
                continue
            wins = (gscore[o] >= gscore[g]) if o < g else (gscore[o] > gscore[g])
            beat = beat + jnp.where(wins, 1.0, 0.0)
        keep = jnp.broadcast_to(beat < float(TOPK_GROUPS), (per, tile))
        masked.append(jnp.where(keep, biased[g * per:(g + 1) * per, :], neg))
    masked = jnp.concatenate(masked, axis=0)

    row = lax.broadcasted_iota(jnp.int32, (n_exp, tile), 0).astype(F32)
    chosen = jnp.zeros((n_exp, tile), F32)
    idxs, gates = [], []
    for _ in range(TOP_K):
        m = jnp.max(masked, axis=0, keepdims=True)
        idx = jnp.min(jnp.where(masked == m, row, float(n_exp)), axis=0, keepdims=True)
        hit = row == idx
        gates.append(jnp.sum(jnp.where(hit, scores, 0.0), axis=0, keepdims=True))
        masked = jnp.where(hit, neg, masked)
        chosen = jnp.where(hit, 1.0, chosen)
        idxs.append(idx)
    gsum = gates[0]
    for k in range(1, TOP_K):
        gsum = gsum + gates[k]

    before = cnt_ref[...] + jnp.dot(chosen.astype(BF16), tri_ref[...], preferred_element_type=F32)
    for k in range(TOP_K):
        eidx_ref[k:k + 1, :] = idxs[k].astype(jnp.int32)
        gate_ref[k:k + 1, :] = gates[k] / gsum * ROUTED_SCALE
        pos_k = jnp.sum(jnp.where(row == idxs[k], before, 0.0), axis=0, keepdims=True)
        pos_ref[k:k + 1, :] = pos_k.astype(jnp.int32)
    cnt_ref[...] = cnt_ref[...] + jnp.sum(chosen, axis=1, keepdims=True)


def _router(h2, rw, rb):
    N, D = h2.shape
    E = rw.shape[1]
    tile = _pick(N, 512)
    wt = rw.T
    whi = wt.astype(BF16)
    wlo = (wt - whi.astype(F32)).astype(BF16)
    ar = jnp.arange(tile)
    tri = (ar[:, None] < ar[None, :]).astype(BF16)
    const = lambda shape: pl.BlockSpec(shape, lambda i: (0, 0))
    kn = pl.BlockSpec((TOP_K, tile), lambda i: (0, i))
    return pl.pallas_call(
        _router_kernel,
        grid=(N // tile,),
        in_specs=[pl.BlockSpec((tile, D), lambda i: (i, 0)), const((E, D)), const((E, D)),
                  const((E, 1)), const((tile, tile))],
        out_specs=[kn, kn, kn, const((E, 1))],
        out_shape=[jax.ShapeDtypeStruct((TOP_K, N), jnp.int32), jax.ShapeDtypeStruct((TOP_K, N), F32),
                   jax.ShapeDtypeStruct((TOP_K, N), jnp.int32), jax.ShapeDtypeStruct((E, 1), F32)],
        compiler_params=_params(("arbitrary",)),
        name="moe_router",
    )(h2, whi, wlo, rb.reshape(E, 1), tri)


def _slot_kernel(eidx_ref, pos_ref, pstart_ref, dest_ref):
    n_exp = pstart_ref.shape[0]
    tile = eidx_ref.shape[1]
    row = lax.broadcasted_iota(jnp.int32, (n_exp, tile), 0)
    pstart = pstart_ref[...]
    for k in range(TOP_K):
        base = jnp.sum(jnp.where(row == eidx_ref[k:k + 1, :], pstart, 0.0), axis=0, keepdims=True)
        dest_ref[k:k + 1, :] = base.astype(jnp.int32) + pos_ref[k:k + 1, :]


def _slots(eidx, pos, pstart):
    K, N = eidx.shape
    E = pstart.shape[0]
    assert N * K + E * EXPERT_BLOCK < 2 ** 24
    tile = _pick(N, 512)
    kn = pl.BlockSpec((K, tile), lambda i: (0, i))
    return pl.pallas_call(
        _slot_kernel,
        grid=(N // tile,),
        in_specs=[kn, kn, pl.BlockSpec((E, 1), lambda i: (0, 0))],
        out_specs=kn,
        out_shape=jax.ShapeDtypeStruct((K, N), jnp.int32),
        compiler_params=_params(("parallel",)),
        name="moe_slots",
    )(eidx, pos, pstart.astype(F32).reshape(E, 1))


def _row_copies(src_of, dst_of, idx_ref, tile, sem):
    def copies(t):
        return [pltpu.make_async_copy(src_of(k, t, idx_ref[0, 0, k * tile + t]),
                                      dst_of(k, t, idx_ref[0, 0, k * tile + t]), sem)
                for k in range(TOP_K)]
    return copies


def _pack_pairs(x):
    half = x.shape[-1] // 2
    hi = lax.bitcast_convert_type(x[:, :half].astype(BF16).astype(F32), jnp.uint32)
    lo = lax.bitcast_convert_type(x[:, half:].astype(BF16).astype(F32), jnp.uint32)
    return hi | (lo >> 16)


def _unpack_pairs(w):
    hi = lax.bitcast_convert_type(w & jnp.uint32(0xFFFF0000), F32)
    lo = lax.bitcast_convert_type(w << 16, F32)
    return hi, lo


def _dispatch_kernel(dest_ref, h_ref, xs_ref, pk_ref, sem, *, tile):
    pk_ref[...] = _pack_pairs(h_ref[...])
    copies = _row_copies(lambda k, t, d: pk_ref.at[pl.ds(t, 1), :],
                         lambda k, t, d: xs_ref.at[pl.ds(d, 1), :], dest_ref, tile, sem)

    def start(t, c):
        for k, cp in enumerate(copies(t)):
            cp.start(priority=k % 2)
        return c

    def wait(t, c):
        for cp in copies(t):
            cp.wait()
        return c

    lax.fori_loop(0, tile, start, 0)
    lax.fori_loop(0, tile, wait, 0)


def _tile_major(dest, tile):
    k, n = dest.shape
    return dest.reshape(k, n // tile, tile).transpose(1, 0, 2).reshape(n // tile, 1, k * tile)


def _dispatch(h2, dest, n_slots):
    N, D = h2.shape
    tile = _pick(N, 256)
    return pl.pallas_call(
        functools.partial(_dispatch_kernel, tile=tile),
        grid=(N // tile,),
        in_specs=[pl.BlockSpec((1, 1, TOP_K * tile), lambda i: (i, 0, 0), memory_space=pltpu.SMEM),
                  pl.BlockSpec((tile, D), lambda i: (i, 0))],
        out_specs=pl.BlockSpec(memory_space=pl.ANY),
        out_shape=jax.ShapeDtypeStruct((n_slots, D // 2), jnp.uint32),
        scratch_shapes=[pltpu.VMEM((tile, D // 2), jnp.uint32), pltpu.SemaphoreType.DMA(())],
        compiler_params=_params(("arbitrary",)),
        name="moe_dispatch",
    )(_tile_major(dest, tile), h2)


def _expert_kernel(bidx_ref, bexp_ref, bval_ref, bnew_ref, bslot_ref, bnext_ref, xs_ref,
                   wg_hbm, wu_hbm, wd_hbm, o_ref, wg_f, wu_f, wd_f, wg_s, wu_s, wd_s, sem, *, layer):
    i = pl.program_id(0)
    valid = bval_ref[i]

    def weight_copies(e, slot):
        return [pltpu.make_async_copy(w.at[layer, e], f.at[slot], sem.at[slot])
                for w, f in ((wg_hbm, wg_f), (wu_hbm, wu_f), (wd_hbm, wd_f))]

    @pl.when(i == 0)
    def _():
        for cp in weight_copies(bexp_ref[0], 0):
            cp.start()

    @pl.when(bnew_ref[i] == 1)
    def _():
        slot = bslot_ref[i]
        for cp in weight_copies(bexp_ref[i], slot):
            cp.wait()

        @pl.when(bnext_ref[i] >= 0)
        def _():
            for cp in weight_copies(bnext_ref[i], 1 - slot):
                cp.start()

        wg_s[...] = wg_f[slot].astype(BF16)
        wu_s[...] = wu_f[slot].astype(BF16)
        wd_s[...] = wd_f[slot].astype(BF16)

    @pl.when(valid > 0)
    def _():
        rows = lax.broadcasted_iota(jnp.int32, (xs_ref.shape[0], 1), 0)
        xa, xb = _unpack_pairs(jnp.where(rows < valid, xs_ref[...], jnp.uint32(0)))
        xa, xb = xa.astype(BF16), xb.astype(BF16)
        half = xa.shape[-1]
        hg = (jnp.dot(xa, wg_s[:half, :], preferred_element_type=F32)
              + jnp.dot(xb, wg_s[half:, :], preferred_element_type=F32))
        hu = (jnp.dot(xa, wu_s[:half, :], preferred_element_type=F32)
              + jnp.dot(xb, wu_s[half:, :], preferred_element_type=F32))
        hb = hg * _sigmoid(hg) * hu
        o_ref[...] = _pack_pairs(jnp.dot(hb.astype(BF16), wd_s[...], preferred_element_type=F32))


def _experts(xs, blocks, wg, wu, wd, layer, block):
    P = xs.shape[0]
    D, Fe = wg.shape[-2:]
    rows = pl.BlockSpec((block, D // 2), lambda i, bi, *_: (bi[i], 0))
    hbm = pl.BlockSpec(memory_space=pl.ANY)
    return pl.pallas_call(
        functools.partial(_expert_kernel, layer=layer),
        grid_spec=pltpu.PrefetchScalarGridSpec(
            num_scalar_prefetch=len(blocks),
            grid=(P // block,),
            in_specs=[rows, hbm, hbm, hbm],
            out_specs=rows,
            scratch_shapes=[pltpu.VMEM((2, D, Fe), F32), pltpu.VMEM((2, D, Fe), F32),
                            pltpu.VMEM((2, Fe, D), F32),
                            pltpu.VMEM((D, Fe), BF16), pltpu.VMEM((D, Fe), BF16),
                            pltpu.VMEM((Fe, D), BF16), pltpu.SemaphoreType.DMA((2,))],
        ),
        out_shape=jax.ShapeDtypeStruct((P, D // 2), jnp.uint32),
        compiler_params=_params(("arbitrary",)),
        name="moe_experts",
    )(*blocks, xs, wg, wu, wd)


def _combine_kernel(dest_ref, h_ref, gate_ref, ys_ref, sg_ref, su_ref, sd_ref, g_ref, beta_ref,
                    o_ref, buf, sem, *, tile, alpha):
    copies = _row_copies(lambda k, t, d: ys_ref.at[pl.ds(d, 1), :],
                         lambda k, t, d: buf.at[k, pl.ds(t, 1), :], dest_ref, tile, sem)

    def start(t, c):
        for k, cp in enumerate(copies(t)):
            cp.start(priority=k % 2)
        return c

    def wait(t, c):
        for cp in copies(t):
            cp.wait()
        return c

    lax.fori_loop(0, tile, start, 0)
    h = h_ref[...]
    hb = h.astype(BF16)
    sgate = jnp.dot(hb, sg_ref[...], preferred_element_type=F32)
    sup = jnp.dot(hb, su_ref[...], preferred_element_type=F32)
    mid = sgate * _sigmoid(sgate) * sup
    acc = jnp.dot(mid.astype(BF16), sd_ref[...], preferred_element_type=F32)
    lax.fori_loop(0, tile, wait, 0)
    gate = gate_ref[...]
    half = h.shape[-1] // 2
    acc_a, acc_b = acc[:, :half], acc[:, half:]
    for k in range(TOP_K):
        ya, yb = _unpack_pairs(buf[k])
        acc_a = acc_a + gate[:, k:k + 1] * ya
        acc_b = acc_b + gate[:, k:k + 1] * yb
    acc = jnp.concatenate([acc_a, acc_b], axis=-1)
    o_ref[...] = _layer_norm(alpha * h + acc, g_ref[...], beta_ref[...])


def _combine(h2, dest, gate_nk, ys, sg, su, sd, g, beta, alpha):
    N, D = h2.shape
    Fs = sg.shape[1]
    tile = _pick(N, 256)
    const = lambda shape: pl.BlockSpec(shape, lambda i: (0, 0))
    return pl.pallas_call(
        functools.partial(_combine_kernel, tile=tile, alpha=alpha),
        grid=(N // tile,),
        in_specs=[pl.BlockSpec((1, 1, TOP_K * tile), lambda i: (i, 0, 0), memory_space=pltpu.SMEM),
                  pl.BlockSpec((tile, D), lambda i: (i, 0)),
                  pl.BlockSpec((tile, TOP_K), lambda i: (i, 0)),
                  pl.BlockSpec(memory_space=pl.ANY),
                  const((D, Fs)), const((D, Fs)), const((Fs, D)), const((1, D)), const((1, D))],
        out_specs=pl.BlockSpec((tile, D), lambda i: (i, 0)),
        out_shape=jax.ShapeDtypeStruct((N, D), F32),
        scratch_shapes=[pltpu.VMEM((TOP_K, tile, D // 2), jnp.uint32), pltpu.SemaphoreType.DMA(())],
        compiler_params=_params(("arbitrary",)),
        name="moe_combine",
    )(_tile_major(dest, tile), h2, gate_nk, ys, sg.astype(BF16), su.astype(BF16), sd.astype(BF16),
      g.reshape(1, D), beta.reshape(1, D))


FLASH_TQ = 512
FLASH_TK = 512
EXPERT_BLOCK = 256


def _block_schedule(counts, n_tokens, blk):
    E = counts.shape[0]
    padded = (counts + blk - 1) // blk * blk
    pend = jnp.cumsum(padded)
    pstart = pend - padded
    n_blocks = -(-(n_tokens * TOP_K + E * (blk - 1)) // blk)
    n_used = pend[-1] // blk
    bidx = jnp.minimum(jnp.arange(n_blocks, dtype=jnp.int32), n_used - 1)
    bexp = jnp.minimum(jnp.searchsorted(pend, bidx * blk, side="right"), E - 1).astype(jnp.int32)
    bval = jnp.clip(counts[bexp] - (bidx * blk - pstart[bexp]), 0, blk)
    used = jnp.arange(n_blocks) < n_used
    bval = jnp.where(used, bval, 0).astype(jnp.int32)
    bnew = (used & (jnp.concatenate([jnp.full((1,), -1, jnp.int32), bexp[:-1]]) != bexp)).astype(jnp.int32)
    bslot = ((jnp.cumsum(bnew) - 1) % 2).astype(jnp.int32)
    later = jnp.concatenate([jnp.where(bnew == 1, bexp, E)[1:], jnp.full((1,), E, jnp.int32)])
    bnext = lax.cummin(later, reverse=True)
    bnext = jnp.where(bnext == E, -1, bnext).astype(jnp.int32)
    return pstart, n_blocks, (bidx, bexp, bval, bnew, bslot, bnext)


def _moe_ln(h2, rw, rb, wg, wu, wd, layer, sg, su, sd, g, beta, alpha):
    N, D = h2.shape
    eidx, gate, pos, cnt = _router(h2, rw, rb)
    blk = EXPERT_BLOCK
    pstart, n_blocks, blocks = _block_schedule(cnt[:, 0].astype(jnp.int32), N, blk)
    dest = _slots(eidx, pos, pstart)
    xs = _dispatch(h2, dest, n_blocks * blk)
    ys = _experts(xs, blocks, wg, wu, wd, layer, blk)
    return _combine(h2, dest, gate.T, ys, sg, su, sd, g, beta, alpha)


def _mla_proj_kernel(h_ref, wa_ref, qg_ref, kvg_ref, wqt_ref, wk_ref, wvt_ref, cos_ref, sin_ref,
                     cost_ref, sint_ref, qt_ref, k_ref, vt_ref, *, q_lora, kv_lora, heads, q_scale):
    x = h_ref[0].astype(BF16)
    a = jnp.dot(x, wa_ref[...], preferred_element_type=F32)

    def rms(v, g):
        return v * lax.rsqrt(jnp.mean(v * v, axis=-1, keepdims=True) + RMS_EPS) * g

    q_lat = rms(a[:, :q_lora], qg_ref[...]).astype(BF16)
    kv_lat = rms(a[:, q_lora:q_lora + kv_lora], kvg_ref[...]).astype(BF16)
    o = q_lora + kv_lora
    k_rope = (a[:, o:o + LANES] * cos_ref[...] + a[:, o + LANES:o + 2 * LANES] * sin_ref[...]).astype(BF16)
    qa = lax.dot_general(wqt_ref[...], q_lat, _NT, preferred_element_type=F32)
    kn = jnp.dot(kv_lat, wk_ref[...], preferred_element_type=F32)
    vt_ref[0] = lax.dot_general(wvt_ref[...], kv_lat, _NT, preferred_element_type=F32).astype(BF16)
    cost = cost_ref[...]
    sint = sint_ref[...]
    hd = heads * LANES
    for h in range(heads):
        r0 = slice(h * LANES, (h + 1) * LANES)
        r1 = slice(hd + h * LANES, hd + (h + 1) * LANES)
        r2 = slice(2 * hd + h * LANES, 2 * hd + (h + 1) * LANES)
        qt_ref[0, 2 * h * LANES:(2 * h + 1) * LANES, :] = (qa[r0] * q_scale).astype(BF16)
        qt_ref[0, (2 * h + 1) * LANES:(2 * h + 2) * LANES, :] = (
            (qa[r1] * cost + qa[r2] * sint) * q_scale).astype(BF16)
        k_ref[0, :, 2 * h * LANES:(2 * h + 1) * LANES] = kn[:, r0].astype(BF16)
        k_ref[0, :, (2 * h + 1) * LANES:(2 * h + 2) * LANES] = k_rope


def _rot_half(w):
    r = w.shape[-1] // 2
    return jnp.concatenate([-w[..., r:], w[..., :r]], axis=-1)


def _pad_lanes(w):
    return jnp.pad(w, [(0, 0)] * (w.ndim - 1) + [(0, LANES - w.shape[-1])])


def _flash_kernel(qt_ref, k_ref, vt_ref, o_ref, s_ref, mx_ref, m_ref, l_ref, acc_ref, *, tk):
    qt = qt_ref[0]
    nk = k_ref.shape[1] // tk
    assert nk % 2 == 0 and nk >= 2
    m_ref[...] = jnp.full_like(m_ref, -jnp.inf)
    l_ref[...] = jnp.zeros_like(l_ref)
    acc_ref[...] = jnp.zeros_like(acc_ref)

    def scores(c, slot):
        off = pl.multiple_of(c * tk, tk)
        s = jnp.dot(k_ref[0, pl.ds(off, tk), :], qt, preferred_element_type=F32)
        s_ref[slot] = s
        mx_ref[slot] = jnp.max(s, axis=0, keepdims=True)

    def update(c, slot):
        off = pl.multiple_of(c * tk, tk)
        vt = vt_ref[0, :, pl.ds(off, tk)]
        m_prev = m_ref[...]
        m_new = jnp.maximum(m_prev, mx_ref[slot])
        p = jnp.exp2(s_ref[slot] - m_new)
        a = jnp.exp2(m_prev - m_new)
        l_ref[...] = a * l_ref[...] + jnp.sum(p, axis=0, keepdims=True)
        acc_ref[...] = a * acc_ref[...] + jnp.dot(vt, p.astype(BF16), preferred_element_type=F32)
        m_ref[...] = m_new

    scores(0, 0)

    def body(i, carry):
        c = 2 * i
        scores(c + 1, 1)
        update(c, 0)
        scores(c + 2, 0)
        update(c + 1, 1)
        return carry

    lax.fori_loop(0, nk // 2 - 1, body, 0)
    scores(nk - 1, 1)
    update(nk - 2, 0)
    update(nk - 1, 1)
    o_ref[0] = (acc_ref[...] / l_ref[...]).T.astype(o_ref.dtype)


def _oproj_ln_kernel(o_ref, h_ref, w_ref, g_ref, beta_ref, out_ref, *, alpha):
    m = jnp.dot(o_ref[0], w_ref[...], preferred_element_type=F32)
    out_ref[0] = _layer_norm(alpha * h_ref[0] + m, g_ref[...], beta_ref[...])


def _mla_ln(h, w_a, qg, kvg, w_uq, w_uk, w_uv, w_o, g, beta, alpha):
    B, S, D = h.shape
    q_lora, heads, qk = w_uq.shape
    kv_lora, _, nope = w_uk.shape
    vd = w_uv.shape[-1]
    rope = qk - nope
    assert nope == LANES and vd == LANES and rope <= LANES and rope % 2 == 0
    kr = w_a[:, q_lora + kv_lora:]
    wa = jnp.concatenate([w_a[:, :q_lora + kv_lora], _pad_lanes(kr), _pad_lanes(_rot_half(kr))],
                         axis=1).astype(BF16)
    wq_r = w_uq[:, :, nope:]
    wq = jnp.concatenate([w_uq[:, :, :nope].reshape(q_lora, heads * nope),
                          _pad_lanes(wq_r).reshape(q_lora, heads * LANES),
                          _pad_lanes(_rot_half(wq_r)).reshape(q_lora, heads * LANES)], axis=1).astype(BF16)
    wk = w_uk.reshape(kv_lora, heads * nope).astype(BF16)
    wvt = w_uv.reshape(kv_lora, heads * vd).T.astype(BF16)
    wqt = wq.T
    pos = jnp.arange(S, dtype=F32)
    inv = 1.0 / (ROPE_THETA ** (jnp.arange(0, rope, 2, dtype=F32) / rope))
    ang = pos[:, None] * inv[None, :]
    cos = _pad_lanes(jnp.concatenate([jnp.cos(ang), jnp.cos(ang)], axis=1))
    sin = _pad_lanes(jnp.concatenate([jnp.sin(ang), jnp.sin(ang)], axis=1))
    q_scale = float(qk) ** -0.5 * math.log2(math.e)

    tp = _pick(S, 256)
    const = lambda a: pl.BlockSpec(a.shape, lambda b, i: (0,) * a.ndim)
    qg2, kvg2 = qg.reshape(1, q_lora), kvg.reshape(1, kv_lora)
    tok_major = pl.BlockSpec((tp, LANES), lambda b, i: (i, 0))
    feat_major = pl.BlockSpec((LANES, tp), lambda b, i: (0, i))
    qt, k, vt = pl.pallas_call(
        functools.partial(_mla_proj_kernel, q_lora=q_lora, kv_lora=kv_lora, heads=heads, q_scale=q_scale),
        grid=(B, S // tp),
        in_specs=[pl.BlockSpec((1, tp, D), lambda b, i: (b, i, 0)), const(wa), const(qg2), const(kvg2),
                  const(wqt), const(wk), const(wvt), tok_major, tok_major, feat_major, feat_major],
        out_specs=[pl.BlockSpec((1, 2 * heads * LANES, tp), lambda b, i: (b, 0, i)),
                   pl.BlockSpec((1, tp, 2 * heads * LANES), lambda b, i: (b, i, 0)),
                   pl.BlockSpec((1, heads * LANES, tp), lambda b, i: (b, 0, i))],
        out_shape=[jax.ShapeDtypeStruct((B, 2 * heads * LANES, S), BF16),
                   jax.ShapeDtypeStruct((B, S, 2 * heads * LANES), BF16),
                   jax.ShapeDtypeStruct((B, heads * LANES, S), BF16)],
        compiler_params=_params(("parallel", "parallel")),
        name="mla_proj",
    )(h, wa, qg2, kvg2, wqt, wk, wvt, cos, sin, cos.T, sin.T)

    tq = _pick(S, FLASH_TQ)
    tk = _pick(S, FLASH_TK)
    o = pl.pallas_call(
        functools.partial(_flash_kernel, tk=tk),
        grid=(B, heads, S // tq),
        in_specs=[pl.BlockSpec((1, 2 * LANES, tq), lambda b, hh, i: (b, hh, i)),
                  pl.BlockSpec((1, S, 2 * LANES), lambda b, hh, i: (b, 0, hh)),
                  pl.BlockSpec((1, LANES, S), lambda b, hh, i: (b, hh, 0))],
        out_specs=pl.BlockSpec((1, tq, LANES), lambda b, hh, i: (b, i, hh)),
        out_shape=jax.ShapeDtypeStruct((B, S, heads * LANES), BF16),
        scratch_shapes=[pltpu.VMEM((2, tk, tq), F32), pltpu.VMEM((2, 1, tq), F32),
                        pltpu.VMEM((1, tq), F32), pltpu.VMEM((1, tq), F32), pltpu.VMEM((LANES, tq), F32)],
        compiler_params=_params(("parallel", "parallel", "arbitrary")),
        name="mla_flash",
    )(qt, k, vt)

    to = _pick(S, 512)
    vec = pl.BlockSpec((1, D), lambda b, i: (0, 0))
    return pl.pallas_call(
        functools.partial(_oproj_ln_kernel, alpha=alpha),
        grid=(B, S // to),
        in_specs=[pl.BlockSpec((1, to, heads * vd), lambda b, i: (b, i, 0)),
                  pl.BlockSpec((1, to, D), lambda b, i: (b, i, 0)),
                  pl.BlockSpec((heads * vd, D), lambda b, i: (0, 0)), vec, vec],
        out_specs=pl.BlockSpec((1, to, D), lambda b, i: (b, i, 0)),
        out_shape=jax.ShapeDtypeStruct((B, S, D), F32),
        compiler_params=_params(("parallel", "parallel")),
        name="mla_oproj_ln",
    )(o, h, w_o.astype(BF16), g.reshape(1, D), beta.reshape(1, D))


def kernel(x, pool_w, pool_b, pool_scale, mla_w_a, mla_q_norm_g, mla_kv_norm_g, mla_w_uq, mla_w_uk,
           mla_w_uv, mla_w_o, ln_mix_g, ln_mix_b, router_w, router_b, exp_w_gate, exp_w_up, exp_w_down,
           sh_w_gate, sh_w_up, sh_w_down, ln_ffn_g, ln_ffn_b):
    B, S, D = x.shape
    depth = ln_mix_g.shape[0]
    alpha = (2 * depth) ** 0.25
    h = x
    for i in range(depth):
        j = i // 2
        if i % 2 == 0:
            h = _pool_ln(h, pool_w[j], pool_b[j], pool_scale[j], ln_mix_g[i], ln_mix_b[i], alpha)
        else:
            h = _mla_ln(h, mla_w_a[j], mla_q_norm_g[j], mla_kv_norm_g[j], mla_w_uq[j], mla_w_uk[j],
                        mla_w_uv[j], mla_w_o[j], ln_mix_g[i], ln_mix_b[i], alpha)
        h = _moe_ln(h.reshape(B * S, D), router_w[i], router_b[i], exp_w_gate, exp_w_up, exp_w_down, i,
                    sh_w_gate[i], sh_w_up[i], sh_w_down[i], ln_ffn_g[i], ln_ffn_b[i],
                    alpha).reshape(B, S, D)
    return h
```

```python
import functools
import math

import jax
import jax.numpy as jnp
from jax import lax
from jax.experimental import pallas as pl
from jax.experimental.pallas import tpu as pltpu

F32 = jnp.float32
BF16 = jnp.bfloat16

LN_EPS = 1e-5
RMS_EPS = 1e-6
POOL_WINDOWS = (2, 4, 8, 16)
POOL_HALO = 8
ROPE_THETA = 10000.0
TOP_K = 8
N_GROUPS = 8
TOPK_GROUPS = 4
ROUTED_SCALE = 2.5
LANES = 128
VMEM_LIMIT = 48 * 1024 * 1024

_NT = (((1,), (1,)), ((), ()))


def _pick(n, pref):
    t = min(pref, n)
    while n % t:
        t //= 2
    assert t >= 8 and n % t == 0, (n, pref)
    return t


def _layer_norm(z, g, b):
    mu = jnp.mean(z, axis=-1, keepdims=True)
    zc = z - mu
    var = jnp.mean(zc * zc, axis=-1, keepdims=True)
    return zc * lax.rsqrt(var + LN_EPS) * g + b


def _sigmoid(x):
    return 1.0 / (1.0 + jnp.exp(-x))


def _params(sem):
    return pltpu.CompilerParams(dimension_semantics=sem, vmem_limit_bytes=VMEM_LIMIT)


def _pool_ln_kernel(xc_ref, xp_ref, xn_ref, w_ref, b_ref, sc_ref, g_ref, beta_ref, o_ref, ext_ref,
                    *, tile, seq, alpha):
    i = pl.program_id(1)
    n_tiles = pl.num_programs(1)
    xc = xc_ref[0]
    d_model = xc.shape[-1]
    ch = d_model // len(POOL_WINDOWS)
    ext_ref[0:POOL_HALO, :] = jnp.where(i > 0, xp_ref[0], 0.0)
    ext_ref[POOL_HALO:POOL_HALO + tile, :] = xc
    ext_ref[POOL_HALO + tile:2 * POOL_HALO + tile, :] = jnp.where(i < n_tiles - 1, xn_ref[0], 0.0)
    pos = i * tile + lax.broadcasted_iota(jnp.int32, (tile, 1), 0)
    outs = []
    for g, win in enumerate(POOL_WINDOWS):
        half = win // 2
        cols = slice(g * ch, (g + 1) * ch)
        acc = ext_ref[POOL_HALO - half:POOL_HALO - half + tile, cols]
        for j in range(-half + 1, half):
            acc = acc + ext_ref[POOL_HALO + j:POOL_HALO + j + tile, cols]
        cnt = (jnp.minimum(pos + half, seq) - jnp.maximum(pos - half, 0)).astype(F32)
        pooled = acc / cnt - xc[:, cols]
        outs.append(jnp.dot(pooled.astype(BF16), w_ref[g], preferred_element_type=F32))
    y = jnp.concatenate(outs, axis=-1)
    y = (y + b_ref[...]) * sc_ref[...]
    o_ref[0] = _layer_norm(alpha * xc + y, g_ref[...], beta_ref[...])


def _pool_ln(x, w, b, sc, g, beta, alpha):
    B, S, D = x.shape
    tile = _pick(S, 512)
    nh = tile // POOL_HALO
    last = S // POOL_HALO - 1
    row = lambda v: v.reshape(1, D)
    vec = pl.BlockSpec((1, D), lambda bi, i: (0, 0))
    return pl.pallas_call(
        functools.partial(_pool_ln_kernel, tile=tile, seq=S, alpha=alpha),
        grid=(B, S // tile),
        in_specs=[
            pl.BlockSpec((1, tile, D), lambda bi, i: (bi, i, 0)),
            pl.BlockSpec((1, POOL_HALO, D), lambda bi, i: (bi, jnp.maximum(i * nh - 1, 0), 0)),
            pl.BlockSpec((1, POOL_HALO, D), lambda bi, i: (bi, jnp.minimum((i + 1) * nh, last), 0)),
            pl.BlockSpec(w.shape, lambda bi, i: (0, 0, 0)),
            vec, vec, vec, vec,
        ],
        out_specs=pl.BlockSpec((1, tile, D), lambda bi, i: (bi, i, 0)),
        out_shape=jax.ShapeDtypeStruct((B, S, D), F32),
        scratch_shapes=[pltpu.VMEM((tile + 2 * POOL_HALO, D), F32)],
        compiler_params=_params(("parallel", "parallel")),
        name="pool_ln",
    )(x, x, x, w.astype(BF16), row(b), row(sc), row(g), row(beta))


def _router_kernel(h_ref, whi_ref, wlo_ref, rb_ref, tri_ref, ltri_ref,
                   gate_ref, lpos_ref, cnt_ref, tcnt_ref, toff_ref, tbase_ref):
    i = pl.program_id(0)

    @pl.when(i == 0)
    def _():
        cnt_ref[...] = jnp.zeros_like(cnt_ref)

    x = h_ref[...]
    xhi = x.astype(BF16)
    xlo = (x - xhi.astype(F32)).astype(BF16)
    whi = whi_ref[...]
    logits = (lax.dot_general(whi, xhi, _NT, preferred_element_type=F32)
              + lax.dot_general(whi, xlo, _NT, preferred_element_type=F32)
              + lax.dot_general(wlo_ref[...], xhi, _NT, preferred_element_type=F32))
    n_exp, tile = logits.shape
    per = n_exp // N_GROUPS
    scores = _sigmoid(logits)
    biased = scores + rb_ref[...]
    neg = -jnp.inf

    sub = lax.broadcasted_iota(jnp.int32, (per, tile), 0).astype(F32)
    gscore = []
    for g in range(N_GROUPS):
        bg = biased[g * per:(g + 1) * per, :]
        m1 = jnp.max(bg, axis=0, keepdims=True)
        i1 = jnp.min(jnp.where(bg == m1, sub, float(per)), axis=0, keepdims=True)
        m2 = jnp.max(jnp.where(sub == i1, neg, bg), axis=0, keepdims=True)
        gscore.append(m1 + m2)
    masked = []
    for g in range(N_GROUPS):
        beat = jnp.zeros_like(gscore[g])
        for o in range(N_GROUPS):
            if o == g:
                continue
            wins = (gscore[o] >= gscore[g]) if o < g else (gscore[o] > gscore[g])
            beat = beat + jnp.where(wins, 1.0, 0.0)
        keep = jnp.broadcast_to(beat < float(TOPK_GROUPS), (per, tile))
        masked.append(jnp.where(keep, biased[g * per:(g + 1) * per, :], neg))
    masked = jnp.concatenate(masked, axis=0)

    row = lax.broadcasted_iota(jnp.int32, (n_exp, tile), 0).astype(F32)
    chosen = jnp.zeros((n_exp, tile), F32)
    idxs, gates = [], []
    for _ in range(TOP_K):
        m = jnp.max(masked, axis=0, keepdims=True)
        idx = jnp.min(jnp.where(masked == m, row, float(n_exp)), axis=0, keepdims=True)
        hit = row == idx
        gates.append(jnp.sum(jnp.where(hit, scores, 0.0), axis=0, keepdims=True))
        masked = jnp.where(hit, neg, masked)
        chosen = jnp.where(hit, 1.0, chosen)
        idxs.append(idx)
    gsum = gates[0]
    for k in range(1, TOP_K):
        gsum = gsum + gates[k]

    earlier = jnp.dot(chosen.astype(BF16), tri_ref[...], preferred_element_type=F32)
    count = jnp.sum(chosen, axis=1, keepdims=True)
    lower = jnp.dot(ltri_ref[...], jnp.broadcast_to(count, (n_exp, LANES)).astype(BF16),
                    preferred_element_type=F32)[:, :1]
    order = earlier + lower
    for k in range(TOP_K):
        gate_ref[k:k + 1, :] = gates[k] / gsum * ROUTED_SCALE
        lpos_k = jnp.sum(jnp.where(row == idxs[k], order, 0.0), axis=0, keepdims=True)
        lpos_ref[k:k + 1, :] = lpos_k.astype(jnp.int32)
    tcnt_ref[...] = count.astype(jnp.int32)
    toff_ref[...] = lower.astype(jnp.int32)
    tbase_ref[...] = cnt_ref[...].astype(jnp.int32)
    cnt_ref[...] = cnt_ref[...] + count


def _router(h2, rw, rb):
    N, D = h2.shape
    E = rw.shape[1]
    tile = _pick(N, MOE_TILE)
    assert tile <= 256
    n_tiles = N // tile
    wt = rw.T
    whi = wt.astype(BF16)
    wlo = (wt - whi.astype(F32)).astype(BF16)
    ar = jnp.arange(tile)
    tri = (ar[:, None] < ar[None, :]).astype(BF16)
    ae = jnp.arange(E)
    ltri = (ae[None, :] < ae[:, None]).astype(BF16)
    const = lambda shape: pl.BlockSpec(shape, lambda i: (0, 0))
    kn = pl.BlockSpec((TOP_K, tile), lambda i: (0, i))
    table = pl.BlockSpec((None, E, 1), lambda i: (i, 0, 0))
    table_shape = jax.ShapeDtypeStruct((n_tiles, E, 1), jnp.int32)
    return pl.pallas_call(
        _router_kernel,
        grid=(n_tiles,),
        in_specs=[pl.BlockSpec((tile, D), lambda i: (i, 0)), const((E, D)), const((E, D)),
                  const((E, 1)), const((tile, tile)), const((E, E))],
        out_specs=[kn, kn, const((E, 1)), table, table, table],
        out_shape=[jax.ShapeDtypeStruct((TOP_K, N), F32), jax.ShapeDtypeStruct((TOP_K, N), jnp.int32),
                   jax.ShapeDtypeStruct((E, 1), F32), table_shape, table_shape, table_shape],
        compiler_params=_params(("arbitrary",)),
        name="moe_router",
    )(h2, whi, wlo, rb.reshape(E, 1), tri, ltri)


def _to_slab(slab_ref, x):
    lines = x.shape[1] // LANES
    for a in range(lines):
        slab_ref[pl.ds(a, x.shape[0], stride=lines), :] = x[:, a * LANES:(a + 1) * LANES]


def _from_slab(slab_ref, rows, lines):
    return jnp.concatenate([slab_ref[pl.ds(a, rows, stride=lines), :] for a in range(lines)], axis=-1)


def _run_copy(e, toff_ref, gst_ref, tcnt_ref, local, remote, sem, lines, to_remote):
    n = tcnt_ref[0, 0, e] * lines
    lo = pl.multiple_of(toff_ref[0, 0, e] * lines, 8)
    go = pl.multiple_of(gst_ref[0, 0, e] * lines, 8)
    loc, rem = local.at[pl.ds(lo, n), :], remote.at[pl.ds(go, n), :]
    return pltpu.make_async_copy(loc, rem, sem) if to_remote else pltpu.make_async_copy(rem, loc, sem)


def _wait_all_runs(slab, remote, sem):
    pltpu.make_async_copy(slab, remote.at[pl.ds(0, slab.shape[0]), :], sem).wait()


def _for_each_run(tcnt_ref, fn):
    def body(e, c):
        @pl.when(tcnt_ref[0, 0, e] > 0)
        def _():
            fn(e)
        return c
    lax.fori_loop(0, tcnt_ref.shape[2], body, 0)


def _dispatch_kernel(toff_ref, gst_ref, tcnt_ref, lpos_ref, h_ref, xs_ref, slab, sem, *, lines):
    tile = h_ref.shape[0]
    rows = TOP_K * tile
    srow = lax.broadcasted_iota(jnp.int32, (rows, tile), 0)
    perm = jnp.zeros((rows, tile), F32)
    for k in range(TOP_K):
        perm = jnp.where(srow == lpos_ref[k:k + 1, :], 1.0, perm)
    _to_slab(slab, jnp.dot(perm.astype(BF16), h_ref[...].astype(BF16), preferred_element_type=F32))
    copy = functools.partial(_run_copy, toff_ref=toff_ref, gst_ref=gst_ref, tcnt_ref=tcnt_ref,
                             local=slab, remote=xs_ref, sem=sem, lines=lines, to_remote=True)
    _for_each_run(tcnt_ref, lambda e: copy(e).start())
    _wait_all_runs(slab, xs_ref, sem)


def _run_specs(n_exp):
    return [pl.BlockSpec((1, 1, n_exp), lambda i: (i, 0, 0), memory_space=pltpu.SMEM)] * 3


def _dispatch(h2, lpos, runs, n_slots):
    N, D = h2.shape
    tile = _pick(N, MOE_TILE)
    lines = D // LANES
    assert lines % 8 == 0
    return pl.pallas_call(
        functools.partial(_dispatch_kernel, lines=lines),
        grid=(N // tile,),
        in_specs=_run_specs(runs[0].shape[2]) + [pl.BlockSpec((TOP_K, tile), lambda i: (0, i)),
                                                 pl.BlockSpec((tile, D), lambda i: (i, 0))],
        out_specs=pl.BlockSpec(memory_space=pl.ANY),
        out_shape=jax.ShapeDtypeStruct((n_slots * lines, LANES), F32),
        scratch_shapes=[pltpu.VMEM((TOP_K * tile * lines, LANES), F32), pltpu.SemaphoreType.DMA(())],
        compiler_params=_params(("arbitrary",)),
        name="moe_dispatch",
    )(*runs, lpos, h2)


def _expert_kernel(bidx_ref, bexp_ref, bval_ref, bnew_ref, bslot_ref, bnext_ref, xs_ref,
                   wg_hbm, wu_hbm, wd_hbm, o_ref, wg_f, wu_f, wd_f, wg_s, wu_s, wd_s, sem, *, layer):
    i = pl.program_id(0)
    valid = bval_ref[i]

    def weight_copies(e, slot):
        return [pltpu.make_async_copy(w.at[layer, e], f.at[slot], sem.at[slot])
                for w, f in ((wg_hbm, wg_f), (wu_hbm, wu_f), (wd_hbm, wd_f))]

    @pl.when(i == 0)
    def _():
        for cp in weight_copies(bexp_ref[0], 0):
            cp.start()

    @pl.when(bnew_ref[i] == 1)
    def _():
        slot = bslot_ref[i]
        for cp in weight_copies(bexp_ref[i], slot):
            cp.wait()

        @pl.when(bnext_ref[i] >= 0)
        def _():
            for cp in weight_copies(bnext_ref[i], 1 - slot):
                cp.start()

        wg_s[...] = wg_f[slot].astype(BF16)
        wu_s[...] = wu_f[slot].astype(BF16)
        wd_s[...] = wd_f[slot].astype(BF16)

    @pl.when(valid > 0)
    def _():
        lines = wg_s.shape[0] // LANES
        block = xs_ref.shape[0] // lines
        rows = lax.broadcasted_iota(jnp.int32, (block, 1), 0)
        x = jnp.where(rows < valid, _from_slab(xs_ref, block, lines), 0.0).astype(BF16)
        hg = jnp.dot(x, wg_s[...], preferred_element_type=F32)
        hu = jnp.dot(x, wu_s[...], preferred_element_type=F32)
        hb = hg * _sigmoid(hg) * hu
        _to_slab(o_ref, jnp.dot(hb.astype(BF16), wd_s[...], preferred_element_type=F32))


def _experts(xs, blocks, wg, wu, wd, layer, block):
    D, Fe = wg.shape[-2:]
    lines = D // LANES
    P = xs.shape[0] // lines
    rows = pl.BlockSpec((block * lines, LANES), lambda i, bi, *_: (bi[i], 0))
    hbm = pl.BlockSpec(memory_space=pl.ANY)
    return pl.pallas_call(
        functools.partial(_expert_kernel, layer=layer),
        grid_spec=pltpu.PrefetchScalarGridSpec(
            num_scalar_prefetch=len(blocks),
            grid=(P // block,),
            in_specs=[rows, hbm, hbm, hbm],
            out_specs=rows,
            scratch_shapes=[pltpu.VMEM((2, D, Fe), F32), pltpu.VMEM((2, D, Fe), F32),
                            pltpu.VMEM((2, Fe, D), F32),
                            pltpu.VMEM((D, Fe), BF16), pltpu.VMEM((D, Fe), BF16),
                            pltpu.VMEM((Fe, D), BF16), pltpu.SemaphoreType.DMA((2,))],
        ),
        out_shape=jax.ShapeDtypeStruct((P * lines, LANES), F32),
        compiler_params=_params(("arbitrary",)),
        name="moe_experts",
    )(*blocks, xs, wg, wu, wd)


def _combine_kernel(toff_ref, gst_ref, tcnt_ref, lpos_ref, gate_ref, h_ref, ys_ref, sg_ref, su_ref, sd_ref,
                    g_ref, beta_ref, o_ref, slab, sem, *, lines, alpha):
    copy = functools.partial(_run_copy, toff_ref=toff_ref, gst_ref=gst_ref, tcnt_ref=tcnt_ref,
                             local=slab, remote=ys_ref, sem=sem, lines=lines, to_remote=False)
    _for_each_run(tcnt_ref, lambda e: copy(e).start())
    h = h_ref[...]
    tile = h.shape[0]
    rows = TOP_K * tile
    hb = h.astype(BF16)
    sgate = jnp.dot(hb, sg_ref[...], preferred_element_type=F32)
    sup = jnp.dot(hb, su_ref[...], preferred_element_type=F32)
    mid = sgate * _sigmoid(sgate) * sup
    acc = jnp.dot(mid.astype(BF16), sd_ref[...], preferred_element_type=F32)
    scol = lax.broadcasted_iota(jnp.int32, (tile, rows), 1)
    weights = jnp.zeros((tile, rows), F32)
    for k in range(TOP_K):
        weights = jnp.where(scol == lpos_ref[:, k:k + 1], gate_ref[:, k:k + 1], weights)
    w_hi = weights.astype(BF16)
    w_lo = (weights - w_hi.astype(F32)).astype(BF16)
    _wait_all_runs(slab, ys_ref, sem)
    ys = _from_slab(slab, rows, lines).astype(BF16)
    acc = (acc + jnp.dot(w_hi, ys, preferred_element_type=F32)
           + jnp.dot(w_lo, ys, preferred_element_type=F32))
    o_ref[...] = _layer_norm(alpha * h + acc, g_ref[...], beta_ref[...])


def _combine(h2, lpos_nk, gate_nk, runs, ys, sg, su, sd, g, beta, alpha):
    N, D = h2.shape
    Fs = sg.shape[1]
    tile = _pick(N, MOE_TILE)
    lines = D // LANES
    const = lambda shape: pl.BlockSpec(shape, lambda i: (0, 0))
    nk = pl.BlockSpec((tile, TOP_K), lambda i: (i, 0))
    return pl.pallas_call(
        functools.partial(_combine_kernel, lines=lines, alpha=alpha),
        grid=(N // tile,),
        in_specs=_run_specs(runs[0].shape[2]) + [
            nk, nk, pl.BlockSpec((tile, D), lambda i: (i, 0)), pl.BlockSpec(memory_space=pl.ANY),
            const((D, Fs)), const((D, Fs)), const((Fs, D)), const((1, D)), const((1, D))],
        out_specs=pl.BlockSpec((tile, D), lambda i: (i, 0)),
        out_shape=jax.ShapeDtypeStruct((N, D), F32),
        scratch_shapes=[pltpu.VMEM((TOP_K * tile * lines, LANES), F32), pltpu.SemaphoreType.DMA(())],
        compiler_params=_params(("arbitrary",)),
        name="moe_combine",
    )(*runs, lpos_nk, gate_nk, h2, ys, sg.astype(BF16), su.astype(BF16), sd.astype(BF16),
      g.reshape(1, D), beta.reshape(1, D))


FLASH_TQ = 512
FLASH_TK = 512
MOE_TILE = 256
EXPERT_BLOCK = 256


def _block_schedule(counts, n_tokens, blk):
    E = counts.shape[0]
    padded = (counts + blk - 1) // blk * blk
    pend = jnp.cumsum(padded)
    pstart = pend - padded
    n_blocks = -(-(n_tokens * TOP_K + E * (blk - 1)) // blk)
    n_used = pend[-1] // blk
    bidx = jnp.minimum(jnp.arange(n_blocks, dtype=jnp.int32), n_used - 1)
    first = (bidx * blk)[:, None]
    owns = (pstart[None, :] <= first) & (first < pend[None, :])
    pick = lambda v: jnp.sum(jnp.where(owns, v[None, :], 0), axis=1).astype(jnp.int32)
    bexp = pick(jnp.arange(E, dtype=jnp.int32))
    bval = jnp.clip(pick(counts) - (bidx * blk - pick(pstart)), 0, blk)
    used = jnp.arange(n_blocks) < n_used
    bval = jnp.where(used, bval, 0).astype(jnp.int32)
    bnew = (used & (jnp.concatenate([jnp.full((1,), -1, jnp.int32), bexp[:-1]]) != bexp)).astype(jnp.int32)
    bslot = ((jnp.cumsum(bnew) - 1) % 2).astype(jnp.int32)
    later = jnp.concatenate([jnp.where(bnew == 1, bexp, E)[1:], jnp.full((1,), E, jnp.int32)])
    bnext = lax.cummin(later, reverse=True)
    bnext = jnp.where(bnext == E, -1, bnext).astype(jnp.int32)
    return pstart, n_blocks, (bidx, bexp, bval, bnew, bslot, bnext)


def _moe_ln(h2, rw, rb, wg, wu, wd, layer, sg, su, sd, g, beta, alpha):
    N, D = h2.shape
    E = rw.shape[1]
    gate, lpos, cnt, tcnt, toff, tbase = _router(h2, rw, rb)
    blk = EXPERT_BLOCK
    pstart, n_blocks, blocks = _block_schedule(cnt[:, 0].astype(jnp.int32), N, blk)
    per_tile = lambda a: a.reshape(a.shape[0], 1, E)
    runs = (per_tile(toff), per_tile(tbase) + pstart[None, None, :], per_tile(tcnt))
    xs = _dispatch(h2, lpos, runs, n_blocks * blk)
    ys = _experts(xs, blocks, wg, wu, wd, layer, blk)
    return _combine(h2, lpos.T, gate.T, runs, ys, sg, su, sd, g, beta, alpha)


def _mla_proj_kernel(h_ref, wa_ref, qg_ref, kvg_ref, wqt_ref, wk_ref, wvt_ref, cos_ref, sin_ref,
                     cost_ref, sint_ref, qt_ref, k_ref, vt_ref, *, q_lora, kv_lora, heads, q_scale):
    x = h_ref[0].astype(BF16)
    a = jnp.dot(x, wa_ref[...], preferred_element_type=F32)

    def rms(v, g):
        return v * lax.rsqrt(jnp.mean(v * v, axis=-1, keepdims=True) + RMS_EPS) * g

    q_lat = rms(a[:, :q_lora], qg_ref[...]).astype(BF16)
    kv_lat = rms(a[:, q_lora:q_lora + kv_lora], kvg_ref[...]).astype(BF16)
    o = q_lora + kv_lora
    k_rope = (a[:, o:o + LANES] * cos_ref[...] + a[:, o + LANES:o + 2 * LANES] * sin_ref[...]).astype(BF16)
    qa = lax.dot_general(wqt_ref[...], q_lat, _NT, preferred_element_type=F32)
    kn = jnp.dot(kv_lat, wk_ref[...], preferred_element_type=F32)
    vt_ref[0] = lax.dot_general(wvt_ref[...], kv_lat, _NT, preferred_element_type=F32).astype(BF16)
    cost = cost_ref[...]
    sint = sint_ref[...]
    hd = heads * LANES
    for h in range(heads):
        r0 = slice(h * LANES, (h + 1) * LANES)
        r1 = slice(hd + h * LANES, hd + (h + 1) * LANES)
        r2 = slice(2 * hd + h * LANES, 2 * hd + (h + 1) * LANES)
        qt_ref[0, 2 * h * LANES:(2 * h + 1) * LANES, :] = (qa[r0] * q_scale).astype(BF16)
        qt_ref[0, (2 * h + 1) * LANES:(2 * h + 2) * LANES, :] = (
            (qa[r1] * cost + qa[r2] * sint) * q_scale).astype(BF16)
        k_ref[0, :, 2 * h * LANES:(2 * h + 1) * LANES] = kn[:, r0].astype(BF16)
        k_ref[0, :, (2 * h + 1) * LANES:(2 * h + 2) * LANES] = k_rope


def _rot_half(w):
    r = w.shape[-1] // 2
    return jnp.concatenate([-w[..., r:], w[..., :r]], axis=-1)


def _pad_lanes(w):
    return jnp.pad(w, [(0, 0)] * (w.ndim - 1) + [(0, LANES - w.shape[-1])])


def _flash_kernel(qt_ref, k_ref, vt_ref, o_ref, s_ref, mx_ref, m_ref, l_ref, acc_ref, *, tk):
    qt = qt_ref[0]
    nk = k_ref.shape[1] // tk
    assert nk % 2 == 0 and nk >= 2
    m_ref[...] = jnp.full_like(m_ref, -jnp.inf)
    l_ref[...] = jnp.zeros_like(l_ref)
    acc_ref[...] = jnp.zeros_like(acc_ref)

    def scores(c, slot):
        off = pl.multiple_of(c * tk, tk)
        s = jnp.dot(k_ref[0, pl.ds(off, tk), :], qt, preferred_element_type=F32)
        s_ref[slot] = s
        mx_ref[slot] = jnp.max(s, axis=0, keepdims=True)

    def update(c, slot):
        off = pl.multiple_of(c * tk, tk)
        vt = vt_ref[0, :, pl.ds(off, tk)]
        m_prev = m_ref[...]
        m_new = jnp.maximum(m_prev, mx_ref[slot])
        p = jnp.exp2(s_ref[slot] - m_new)
        a = jnp.exp2(m_prev - m_new)
        l_ref[...] = a * l_ref[...] + jnp.sum(p, axis=0, keepdims=True)
        acc_ref[...] = a * acc_ref[...] + jnp.dot(vt, p.astype(BF16), preferred_element_type=F32)
        m_ref[...] = m_new

    scores(0, 0)

    def body(i, carry):
        c = 2 * i
        scores(c + 1, 1)
        update(c, 0)
        scores(c + 2, 0)
        update(c + 1, 1)
        return carry

    lax.fori_loop(0, nk // 2 - 1, body, 0)
    scores(nk - 1, 1)
    update(nk - 2, 0)
    update(nk - 1, 1)
    o_ref[0] = (acc_ref[...] / l_ref[...]).T.astype(o_ref.dtype)


def _oproj_ln_kernel(o_ref, h_ref, w_ref, g_ref, beta_ref, out_ref, *, alpha):
    m = jnp.dot(o_ref[0], w_ref[...], preferred_element_type=F32)
    out_ref[0] = _layer_norm(alpha * h_ref[0] + m, g_ref[...], beta_ref[...])


def _mla_ln(h, w_a, qg, kvg, w_uq, w_uk, w_uv, w_o, g, beta, alpha):
    B, S, D = h.shape
    q_lora, heads, qk = w_uq.shape
    kv_lora, _, nope = w_uk.shape
    vd = w_uv.shape[-1]
    rope = qk - nope
    assert nope == LANES and vd == LANES and rope <= LANES and rope % 2 == 0
    kr = w_a[:, q_lora + kv_lora:]
    wa = jnp.concatenate([w_a[:, :q_lora + kv_lora], _pad_lanes(kr), _pad_lanes(_rot_half(kr))],
                         axis=1).astype(BF16)
    wq_r = w_uq[:, :, nope:]
    wq = jnp.concatenate([w_uq[:, :, :nope].reshape(q_lora, heads * nope),
                          _pad_lanes(wq_r).reshape(q_lora, heads * LANES),
                          _pad_lanes(_rot_half(wq_r)).reshape(q_lora, heads * LANES)], axis=1).astype(BF16)
    wk = w_uk.reshape(kv_lora, heads * nope).astype(BF16)
    wvt = w_uv.reshape(kv_lora, heads * vd).T.astype(BF16)
    wqt = wq.T
    pos = jnp.arange(S, dtype=F32)
    inv = 1.0 / (ROPE_THETA ** (jnp.arange(0, rope, 2, dtype=F32) / rope))
    ang = pos[:, None] * inv[None, :]
    cos = _pad_lanes(jnp.concatenate([jnp.cos(ang), jnp.cos(ang)], axis=1))
    sin = _pad_lanes(jnp.concatenate([jnp.sin(ang), jnp.sin(ang)], axis=1))
    q_scale = float(qk) ** -0.5 * math.log2(math.e)

    tp = _pick(S, 256)
    const = lambda a: pl.BlockSpec(a.shape, lambda b, i: (0,) * a.ndim)
    qg2, kvg2 = qg.reshape(1, q_lora), kvg.reshape(1, kv_lora)
    tok_major = pl.BlockSpec((tp, LANES), lambda b, i: (i, 0))
    feat_major = pl.BlockSpec((LANES, tp), lambda b, i: (0, i))
    qt, k, vt = pl.pallas_call(
        functools.partial(_mla_proj_kernel, q_lora=q_lora, kv_lora=kv_lora, heads=heads, q_scale=q_scale),
        grid=(B, S // tp),
        in_specs=[pl.BlockSpec((1, tp, D), lambda b, i: (b, i, 0)), const(wa), const(qg2), const(kvg2),
                  const(wqt), const(wk), const(wvt), tok_major, tok_major, feat_major, feat_major],
        out_specs=[pl.BlockSpec((1, 2 * heads * LANES, tp), lambda b, i: (b, 0, i)),
                   pl.BlockSpec((1, tp, 2 * heads * LANES), lambda b, i: (b, i, 0)),
                   pl.BlockSpec((1, heads * LANES, tp), lambda b, i: (b, 0, i))],
        out_shape=[jax.ShapeDtypeStruct((B, 2 * heads * LANES, S), BF16),
                   jax.ShapeDtypeStruct((B, S, 2 * heads * LANES), BF16),
                   jax.ShapeDtypeStruct((B, heads * LANES, S), BF16)],
        compiler_params=_params(("parallel", "parallel")),
        name="mla_proj",
    )(h, wa, qg2, kvg2, wqt, wk, wvt, cos, sin, cos.T, sin.T)

    tq = _pick(S, FLASH_TQ)
    tk = _pick(S, FLASH_TK)
    o = pl.pallas_call(
        functools.partial(_flash_kernel, tk=tk),
        grid=(B, heads, S // tq),
        in_specs=[pl.BlockSpec((1, 2 * LANES, tq), lambda b, hh, i: (b, hh, i)),
                  pl.BlockSpec((1, S, 2 * LANES), lambda b, hh, i: (b, 0, hh)),
                  pl.BlockSpec((1, LANES, S), lambda b, hh, i: (b, hh, 0))],
        out_specs=pl.BlockSpec((1, tq, LANES), lambda b, hh, i: (b, i, hh)),
        out_shape=jax.ShapeDtypeStruct((B, S, heads * LANES), BF16),
        scratch_shapes=[pltpu.VMEM((2, tk, tq), F32), pltpu.VMEM((2, 1, tq), F32),
                        pltpu.VMEM((1, tq), F32), pltpu.VMEM((1, tq), F32), pltpu.VMEM((LANES, tq), F32)],
        compiler_params=_params(("parallel", "parallel", "arbitrary")),
        name="mla_flash",
    )(qt, k, vt)

    to = _pick(S, 512)
    vec = pl.BlockSpec((1, D), lambda b, i: (0, 0))
    return pl.pallas_call(
        functools.partial(_oproj_ln_kernel, alpha=alpha),
        grid=(B, S // to),
        in_specs=[pl.BlockSpec((1, to, heads * vd), lambda b, i: (b, i, 0)),
                  pl.BlockSpec((1, to, D), lambda b, i: (b, i, 0)),
                  pl.BlockSpec((heads * vd, D), lambda b, i: (0, 0)), vec, vec],
        out_specs=pl.BlockSpec((1, to, D), lambda b, i: (b, i, 0)),
        out_shape=jax.ShapeDtypeStruct((B, S, D), F32),
        compiler_params=_params(("parallel", "parallel")),
        name="mla_oproj_ln",
    )(o, h, w_o.astype(BF16), g.reshape(1, D), beta.reshape(1, D))


def kernel(x, pool_w, pool_b, pool_scale, mla_w_a, mla_q_norm_g, mla_kv_norm_g, mla_w_uq, mla_w_uk,
           mla_w_uv, mla_w_o, ln_mix_g, ln_mix_b, router_w, router_b, exp_w_gate, exp_w_up, exp_w_down,
           sh_w_gate, sh_w_up, sh_w_down, ln_ffn_g, ln_ffn_b):
    B, S, D = x.shape
    depth = ln_mix_g.shape[0]
    alpha = (2 * depth) ** 0.25
    h = x
    for i in range(depth):
        j = i // 2
        if i % 2 == 0:
            h = _pool_ln(h, pool_w[j], pool_b[j], pool_scale[j], ln_mix_g[i], ln_mix_b[i], alpha)
        else:
            h = _mla_ln(h, mla_w_a[j], mla_q_norm_g[j], mla_kv_norm_g[j], mla_w_uq[j], mla_w_uk[j],
                        mla_w_uv[j], mla_w_o[j], ln_mix_g[i], ln_mix_b[i], alpha)
        h = _moe_ln(h.reshape(B * S, D), router_w[i], router_b[i], exp_w_gate, exp_w_up, exp_w_down, i,
                    sh_w_gate[i], sh_w_up[i], sh_w_down[i], ln_ffn_g[i], ln_ffn_b[i],
                    alpha).reshape(B, S, D)
    return h
```

```python
import functools
import math

import jax
import jax.numpy as jnp
from jax import lax
from jax.experimental import pallas as pl
from jax.experimental.pallas import tpu as pltpu

F32 = jnp.float32
BF16 = jnp.bfloat16

LN_EPS = 1e-5
RMS_EPS = 1e-6
POOL_WINDOWS = (2, 4, 8, 16)
POOL_HALO = 8
ROPE_THETA = 10000.0
TOP_K = 8
N_GROUPS = 8
TOPK_GROUPS = 4
ROUTED_SCALE = 2.5
LANES = 128
VMEM_LIMIT = 48 * 1024 * 1024

_NT = (((1,), (1,)), ((), ()))


def _pick(n, pref):
    t = min(pref, n)
    while n % t:
        t //= 2
    assert t >= 8 and n % t == 0, (n, pref)
    return t


def _layer_norm(z, g, b):
    mu = jnp.mean(z, axis=-1, keepdims=True)
    zc = z - mu
    var = jnp.mean(zc * zc, axis=-1, keepdims=True)
    return zc * lax.rsqrt(var + LN_EPS) * g + b


def _sigmoid(x):
    return 1.0 / (1.0 + jnp.exp(-x))


def _params(sem):
    return pltpu.CompilerParams(dimension_semantics=sem, vmem_limit_bytes=VMEM_LIMIT)


def _pool_ln_kernel(xc_ref, xp_ref, xn_ref, w_ref, b_ref, sc_ref, g_ref, beta_ref, o_ref, ext_ref,
                    *, tile, seq, alpha):
    i = pl.program_id(1)
    n_tiles = pl.num_programs(1)
    xc = xc_ref[0]
    d_model = xc.shape[-1]
    ch = d_model // len(POOL_WINDOWS)
    ext_ref[0:POOL_HALO, :] = jnp.where(i > 0, xp_ref[0], 0.0)
    ext_ref[POOL_HALO:POOL_HALO + tile, :] = xc
    ext_ref[POOL_HALO + tile:2 * POOL_HALO + tile, :] = jnp.where(i < n_tiles - 1, xn_ref[0], 0.0)
    pos = i * tile + lax.broadcasted_iota(jnp.int32, (tile, 1), 0)
    outs = []
    for g, win in enumerate(POOL_WINDOWS):
        half = win // 2
        cols = slice(g * ch, (g + 1) * ch)
        acc = ext_ref[POOL_HALO - half:POOL_HALO - half + tile, cols]
        for j in range(-half + 1, half):
            acc = acc + ext_ref[POOL_HALO + j:POOL_HALO + j + tile, cols]
        cnt = (jnp.minimum(pos + half, seq) - jnp.maximum(pos - half, 0)).astype(F32)
        pooled = acc / cnt - xc[:, cols]
        outs.append(jnp.dot(pooled.astype(BF16), w_ref[g], preferred_element_type=F32))
    y = jnp.concatenate(outs, axis=-1)
    y = (y + b_ref[...]) * sc_ref[...]
    o_ref[0] = _layer_norm(alpha * xc + y, g_ref[...], beta_ref[...])


def _pool_ln(x, w, b, sc, g, beta, alpha):
    B, S, D = x.shape
    tile = _pick(S, 512)
    nh = tile // POOL_HALO
    last = S // POOL_HALO - 1
    row = lambda v: v.reshape(1, D)
    vec = pl.BlockSpec((1, D), lambda bi, i: (0, 0))
    return pl.pallas_call(
        functools.partial(_pool_ln_kernel, tile=tile, seq=S, alpha=alpha),
        grid=(B, S // tile),
        in_specs=[
            pl.BlockSpec((1, tile, D), lambda bi, i: (bi, i, 0)),
            pl.BlockSpec((1, POOL_HALO, D), lambda bi, i: (bi, jnp.maximum(i * nh - 1, 0), 0)),
            pl.BlockSpec((1, POOL_HALO, D), lambda bi, i: (bi, jnp.minimum((i + 1) * nh, last), 0)),
            pl.BlockSpec(w.shape, lambda bi, i: (0, 0, 0)),
            vec, vec, vec, vec,
        ],
        out_specs=pl.BlockSpec((1, tile, D), lambda bi, i: (bi, i, 0)),
        out_shape=jax.ShapeDtypeStruct((B, S, D), F32),
        scratch_shapes=[pltpu.VMEM((tile + 2 * POOL_HALO, D), F32)],
        compiler_params=_params(("parallel", "parallel")),
        name="pool_ln",
    )(x, x, x, w.astype(BF16), row(b), row(sc), row(g), row(beta))


def _router_kernel(h_ref, whi_ref, wlo_ref, rb_ref, tri_ref, ltri_ref,
                   gate_ref, lpos_ref, cnt_ref, tcnt_ref, toff_ref, tbase_ref):
    i = pl.program_id(0)

    @pl.when(i == 0)
    def _():
        cnt_ref[...] = jnp.zeros_like(cnt_ref)

    x = h_ref[...]
    xhi = x.astype(BF16)
    xlo = (x - xhi.astype(F32)).astype(BF16)
    whi = whi_ref[...]
    logits = (lax.dot_general(whi, xhi, _NT, preferred_element_type=F32)
              + lax.dot_general(whi, xlo, _NT, preferred_element_type=F32)
              + lax.dot_general(wlo_ref[...], xhi, _NT, preferred_element_type=F32))
    n_exp, tile = logits.shape
    per = n_exp // N_GROUPS
    scores = _sigmoid(logits)
    biased = scores + rb_ref[...]
    neg = -jnp.inf

    sub = lax.broadcasted_iota(jnp.int32, (per, tile), 0).astype(F32)
    gscore = []
    for g in range(N_GROUPS):
        bg = biased[g * per:(g + 1) * per, :]
        m1 = jnp.max(bg, axis=0, keepdims=True)
        i1 = jnp.min(jnp.where(bg == m1, sub, float(per)), axis=0, keepdims=True)
        m2 = jnp.max(jnp.where(sub == i1, neg, bg), axis=0, keepdims=True)
        gscore.append(m1 + m2)
    masked = []
    for g in range(N_GROUPS):
        beat = jnp.zeros_like(gscore[g])
        for o in range(N_GROUPS):
            if o == g:
                continue
            wins = (gscore[o] >= gscore[g]) if o < g else (gscore[o] > gscore[g])
            beat = beat + jnp.where(wins, 1.0, 0.0)
        keep = jnp.broadcast_to(beat < float(TOPK_GROUPS), (per, tile))
        masked.append(jnp.where(keep, biased[g * per:(g + 1) * per, :], neg))
    masked = jnp.concatenate(masked, axis=0)

    row = lax.broadcasted_iota(jnp.int32, (n_exp, tile), 0).astype(F32)
    chosen = jnp.zeros((n_exp, tile), F32)
    idxs, gates = [], []
    for _ in range(TOP_K):
        m = jnp.max(masked, axis=0, keepdims=True)
        idx = jnp.min(jnp.where(masked == m, row, float(n_exp)), axis=0, keepdims=True)
        hit = row == idx
        gates.append(jnp.sum(jnp.where(hit, scores, 0.0), axis=0, keepdims=True))
        masked = jnp.where(hit, neg, masked)
        chosen = jnp.where(hit, 1.0, chosen)
        idxs.append(idx)
    gsum = gates[0]
    for k in range(1, TOP_K):
        gsum = gsum + gates[k]

    earlier = jnp.dot(chosen.astype(BF16), tri_ref[...], preferred_element_type=F32)
    count = jnp.sum(chosen, axis=1, keepdims=True)
    lower = jnp.dot(ltri_ref[...], jnp.broadcast_to(count, (n_exp, LANES)).astype(BF16),
                    preferred_element_type=F32)[:, :1]
    order = earlier + lower
    for k in range(TOP_K):
        gate_ref[k:k + 1, :] = gates[k] / gsum * ROUTED_SCALE
        lpos_k = jnp.sum(jnp.where(row == idxs[k], order, 0.0), axis=0, keepdims=True)
        lpos_ref[k:k + 1, :] = lpos_k.astype(jnp.int32)
    tcnt_ref[...] = count.astype(jnp.int32)
    toff_ref[...] = lower.astype(jnp.int32)
    tbase_ref[...] = cnt_ref[...].astype(jnp.int32)
    cnt_ref[...] = cnt_ref[...] + count


def _router(h2, rw, rb):
    N, D = h2.shape
    E = rw.shape[1]
    tile = _pick(N, MOE_TILE)
    assert tile <= 256
    n_tiles = N // tile
    wt = rw.T
    whi = wt.astype(BF16)
    wlo = (wt - whi.astype(F32)).astype(BF16)
    ar = jnp.arange(tile)
    tri = (ar[:, None] < ar[None, :]).astype(BF16)
    ae = jnp.arange(E)
    ltri = (ae[None, :] < ae[:, None]).astype(BF16)
    const = lambda shape: pl.BlockSpec(shape, lambda i: (0, 0))
    kn = pl.BlockSpec((TOP_K, tile), lambda i: (0, i))
    table = pl.BlockSpec((None, E, 1), lambda i: (i, 0, 0))
    table_shape = jax.ShapeDtypeStruct((n_tiles, E, 1), jnp.int32)
    return pl.pallas_call(
        _router_kernel,
        grid=(n_tiles,),
        in_specs=[pl.BlockSpec((tile, D), lambda i: (i, 0)), const((E, D)), const((E, D)),
                  const((E, 1)), const((tile, tile)), const((E, E))],
        out_specs=[kn, kn, const((E, 1)), table, table, table],
        out_shape=[jax.ShapeDtypeStruct((TOP_K, N), F32), jax.ShapeDtypeStruct((TOP_K, N), jnp.int32),
                   jax.ShapeDtypeStruct((E, 1), F32), table_shape, table_shape, table_shape],
        compiler_params=_params(("arbitrary",)),
        name="moe_router",
    )(h2, whi, wlo, rb.reshape(E, 1), tri, ltri)


def _to_slab(slab_ref, x):
    lines = x.shape[1] // LANES
    for a in range(lines):
        slab_ref[pl.ds(a, x.shape[0], stride=lines), :] = x[:, a * LANES:(a + 1) * LANES]


def _from_slab(slab_ref, rows, lines):
    return jnp.concatenate([slab_ref[pl.ds(a, rows, stride=lines), :] for a in range(lines)], axis=-1)


def _run_copy(e, toff_ref, gst_ref, tcnt_ref, local, remote, sem, lines, to_remote):
    n = tcnt_ref[0, 0, e] * lines
    lo = pl.multiple_of(toff_ref[0, 0, e] * lines, 8)
    go = pl.multiple_of(gst_ref[0, 0, e] * lines, 8)
    loc, rem = local.at[pl.ds(lo, n), :], remote.at[pl.ds(go, n), :]
    return pltpu.make_async_copy(loc, rem, sem) if to_remote else pltpu.make_async_copy(rem, loc, sem)


def _wait_all_runs(slab, remote, sem):
    pltpu.make_async_copy(slab, remote.at[pl.ds(0, slab.shape[0]), :], sem).wait()


def _for_each_run(tcnt_ref, fn):
    def body(e, c):
        @pl.when(tcnt_ref[0, 0, e] > 0)
        def _():
            fn(e)
        return c
    lax.fori_loop(0, tcnt_ref.shape[2], body, 0)


def _dispatch_kernel(toff_ref, gst_ref, tcnt_ref, lpos_ref, h_ref, xs_ref, slab, sem, *, lines):
    tile = h_ref.shape[0]
    rows = TOP_K * tile
    srow = lax.broadcasted_iota(jnp.int32, (rows, tile), 0)
    perm = jnp.zeros((rows, tile), F32)
    for k in range(TOP_K):
        perm = jnp.where(srow == lpos_ref[k:k + 1, :], 1.0, perm)
    _to_slab(slab, jnp.dot(perm.astype(BF16), h_ref[...].astype(BF16), preferred_element_type=F32))
    copy = functools.partial(_run_copy, toff_ref=toff_ref, gst_ref=gst_ref, tcnt_ref=tcnt_ref,
                             local=slab, remote=xs_ref, sem=sem, lines=lines, to_remote=True)
    _for_each_run(tcnt_ref, lambda e: copy(e).start())
    _wait_all_runs(slab, xs_ref, sem)


def _run_specs(n_exp):
    return [pl.BlockSpec((1, 1, n_exp), lambda i: (i, 0, 0), memory_space=pltpu.SMEM)] * 3


def _dispatch(h2, lpos, runs, n_slots):
    N, D = h2.shape
    tile = _pick(N, MOE_TILE)
    lines = D // LANES
    assert lines % 8 == 0
    return pl.pallas_call(
        functools.partial(_dispatch_kernel, lines=lines),
        grid=(N // tile,),
        in_specs=_run_specs(runs[0].shape[2]) + [pl.BlockSpec((TOP_K, tile), lambda i: (0, i)),
                                                 pl.BlockSpec((tile, D), lambda i: (i, 0))],
        out_specs=pl.BlockSpec(memory_space=pl.ANY),
        out_shape=jax.ShapeDtypeStruct((n_slots * lines, LANES), F32),
        scratch_shapes=[pltpu.VMEM((TOP_K * tile * lines, LANES), F32), pltpu.SemaphoreType.DMA(())],
        compiler_params=_params(("arbitrary",)),
        name="moe_dispatch",
    )(*runs, lpos, h2)


def _expert_kernel(bidx_ref, bexp_ref, bval_ref, bnew_ref, bslot_ref, bnext_ref, xs_ref,
                   wg_hbm, wu_hbm, wd_hbm, o_ref, wg_f, wu_f, wd_f, wg_s, wu_s, wd_s, sem, *, layer):
    i = pl.program_id(0)
    valid = bval_ref[i]

    def weight_copies(e, slot):
        return [pltpu.make_async_copy(w.at[layer, e], f.at[slot], sem.at[slot])
                for w, f in ((wg_hbm, wg_f), (wu_hbm, wu_f), (wd_hbm, wd_f))]

    @pl.when(i == 0)
    def _():
        for cp in weight_copies(bexp_ref[0], 0):
            cp.start()

    @pl.when(bnew_ref[i] == 1)
    def _():
        slot = bslot_ref[i]
        for cp in weight_copies(bexp_ref[i], slot):
            cp.wait()

        @pl.when(bnext_ref[i] >= 0)
        def _():
            for cp in weight_copies(bnext_ref[i], 1 - slot):
                cp.start()

        wg_s[...] = wg_f[slot].astype(BF16)
        wu_s[...] = wu_f[slot].astype(BF16)
        wd_s[...] = wd_f[slot].astype(BF16)

    @pl.when(valid > 0)
    def _():
        lines = wg_s.shape[0] // LANES
        block = xs_ref.shape[0] // lines
        rows = lax.broadcasted_iota(jnp.int32, (block, 1), 0)
        x = jnp.where(rows < valid, _from_slab(xs_ref, block, lines), 0.0).astype(BF16)
        hg = jnp.dot(x, wg_s[...], preferred_element_type=F32)
        hu = jnp.dot(x, wu_s[...], preferred_element_type=F32)
        hb = hg * _sigmoid(hg) * hu
        _to_slab(o_ref, jnp.dot(hb.astype(BF16), wd_s[...], preferred_element_type=F32))


def _experts(xs, blocks, wg, wu, wd, layer, block):
    D, Fe = wg.shape[-2:]
    lines = D // LANES
    P = xs.shape[0] // lines
    rows = pl.BlockSpec((block * lines, LANES), lambda i, bi, *_: (bi[i], 0))
    hbm = pl.BlockSpec(memory_space=pl.ANY)
    return pl.pallas_call(
        functools.partial(_expert_kernel, layer=layer),
        grid_spec=pltpu.PrefetchScalarGridSpec(
            num_scalar_prefetch=len(blocks),
            grid=(P // block,),
            in_specs=[rows, hbm, hbm, hbm],
            out_specs=rows,
            scratch_shapes=[pltpu.VMEM((2, D, Fe), F32), pltpu.VMEM((2, D, Fe), F32),
                            pltpu.VMEM((2, Fe, D), F32),
                            pltpu.VMEM((D, Fe), BF16), pltpu.VMEM((D, Fe), BF16),
                            pltpu.VMEM((Fe, D), BF16), pltpu.SemaphoreType.DMA((2,))],
        ),
        out_shape=jax.ShapeDtypeStruct((P * lines, LANES), F32),
        compiler_params=_params(("arbitrary",)),
        name="moe_experts",
    )(*blocks, xs, wg, wu, wd)


def _combine_kernel(toff_ref, gst_ref, tcnt_ref, lpos_ref, gate_ref, h_ref, ys_ref, sg_ref, su_ref, sd_ref,
                    g_ref, beta_ref, o_ref, slab, sem, *, lines, alpha):
    copy = functools.partial(_run_copy, toff_ref=toff_ref, gst_ref=gst_ref, tcnt_ref=tcnt_ref,
                             local=slab, remote=ys_ref, sem=sem, lines=lines, to_remote=False)
    _for_each_run(tcnt_ref, lambda e: copy(e).start())
    h = h_ref[...]
    tile = h.shape[0]
    rows = TOP_K * tile
    hb = h.astype(BF16)
    sgate = jnp.dot(hb, sg_ref[...], preferred_element_type=F32)
    sup = jnp.dot(hb, su_ref[...], preferred_element_type=F32)
    mid = sgate * _sigmoid(sgate) * sup
    acc = jnp.dot(mid.astype(BF16), sd_ref[...], preferred_element_type=F32)
    scol = lax.broadcasted_iota(jnp.int32, (tile, rows), 1)
    weights = jnp.zeros((tile, rows), F32)
    for k in range(TOP_K):
        weights = jnp.where(scol == lpos_ref[:, k:k + 1], gate_ref[:, k:k + 1], weights)
    w_hi = weights.astype(BF16)
    w_lo = (weights - w_hi.astype(F32)).astype(BF16)
    _wait_all_runs(slab, ys_ref, sem)
    ys = _from_slab(slab, rows, lines).astype(BF16)
    acc = (acc + jnp.dot(w_hi, ys, preferred_element_type=F32)
           + jnp.dot(w_lo, ys, preferred_element_type=F32))
    o_ref[...] = _layer_norm(alpha * h + acc, g_ref[...], beta_ref[...])


def _combine(h2, lpos_nk, gate_nk, runs, ys, sg, su, sd, g, beta, alpha):
    N, D = h2.shape
    Fs = sg.shape[1]
    tile = _pick(N, MOE_TILE)
    lines = D // LANES
    const = lambda shape: pl.BlockSpec(shape, lambda i: (0, 0))
    nk = pl.BlockSpec((tile, TOP_K), lambda i: (i, 0))
    return pl.pallas_call(
        functools.partial(_combine_kernel, lines=lines, alpha=alpha),
        grid=(N // tile,),
        in_specs=_run_specs(runs[0].shape[2]) + [
            nk, nk, pl.BlockSpec((tile, D), lambda i: (i, 0)), pl.BlockSpec(memory_space=pl.ANY),
            const((D, Fs)), const((D, Fs)), const((Fs, D)), const((1, D)), const((1, D))],
        out_specs=pl.BlockSpec((tile, D), lambda i: (i, 0)),
        out_shape=jax.ShapeDtypeStruct((N, D), F32),
        scratch_shapes=[pltpu.VMEM((TOP_K * tile * lines, LANES), F32), pltpu.SemaphoreType.DMA(())],
        compiler_params=_params(("arbitrary",)),
        name="moe_combine",
    )(*runs, lpos_nk, gate_nk, h2, ys, sg.astype(BF16), su.astype(BF16), sd.astype(BF16),
      g.reshape(1, D), beta.reshape(1, D))


FLASH_TQ = 512
FLASH_TK = 512
MOE_TILE = 256
EXPERT_BLOCK = 256


def _block_schedule(counts, n_tokens, blk):
    E = counts.shape[0]
    padded = (counts + blk - 1) // blk * blk
    pend = jnp.cumsum(padded)
    pstart = pend - padded
    n_blocks = -(-(n_tokens * TOP_K + E * (blk - 1)) // blk)
    n_used = pend[-1] // blk
    bidx = jnp.minimum(jnp.arange(n_blocks, dtype=jnp.int32), n_used - 1)
    first = (bidx * blk)[:, None]
    owns = (pstart[None, :] <= first) & (first < pend[None, :])
    pick = lambda v: jnp.sum(jnp.where(owns, v[None, :], 0), axis=1).astype(jnp.int32)
    bexp = pick(jnp.arange(E, dtype=jnp.int32))
    bval = jnp.clip(pick(counts) - (bidx * blk - pick(pstart)), 0, blk)
    used = jnp.arange(n_blocks) < n_used
    bval = jnp.where(used, bval, 0).astype(jnp.int32)
    bnew = (used & (jnp.concatenate([jnp.full((1,), -1, jnp.int32), bexp[:-1]]) != bexp)).astype(jnp.int32)
    bslot = ((jnp.cumsum(bnew) - 1) % 2).astype(jnp.int32)
    later = jnp.concatenate([jnp.where(bnew == 1, bexp, E)[1:], jnp.full((1,), E, jnp.int32)])
    bnext = lax.cummin(later, reverse=True)
    bnext = jnp.where(bnext == E, -1, bnext).astype(jnp.int32)
    return pstart, n_blocks, (bidx, bexp, bval, bnew, bslot, bnext)


def _moe_ln(h2, rw, rb, wg, wu, wd, layer, sg, su, sd, g, beta, alpha):
    N, D = h2.shape
    E = rw.shape[1]
    gate, lpos, cnt, tcnt, toff, tbase = _router(h2, rw, rb)
    blk = EXPERT_BLOCK
    pstart, n_blocks, blocks = _block_schedule(cnt[:, 0].astype(jnp.int32), N, blk)
    per_tile = lambda a: a.reshape(a.shape[0], 1, E)
    runs = (per_tile(toff), per_tile(tbase) + pstart[None, None, :], per_tile(tcnt))
    xs = _dispatch(h2, lpos, runs, n_blocks * blk)
    ys = _experts(xs, blocks, wg, wu, wd, layer, blk)
    return _combine(h2, lpos.T, gate.T, runs, ys, sg, su, sd, g, beta, alpha)


def _mla_proj_kernel(h_ref, wa_ref, qg_ref, kvg_ref, wqt_ref, wk_ref, wvt_ref, cos_ref, sin_ref,
                     cost_ref, sint_ref, qt_ref, k_ref, vt_ref, *, q_lora, kv_lora, heads, q_scale):
    x = h_ref[0].astype(BF16)
    a = jnp.dot(x, wa_ref[...], preferred_element_type=F32)

    def rms(v, g):
        return v * lax.rsqrt(jnp.mean(v * v, axis=-1, keepdims=True) + RMS_EPS) * g

    q_lat = rms(a[:, :q_lora], qg_ref[...]).astype(BF16)
    kv_lat = rms(a[:, q_lora:q_lora + kv_lora], kvg_ref[...]).astype(BF16)
    o = q_lora + kv_lora
    k_rope = (a[:, o:o + LANES] * cos_ref[...] + a[:, o + LANES:o + 2 * LANES] * sin_ref[...]).astype(BF16)
    qa = lax.dot_general(wqt_ref[...], q_lat, _NT, preferred_element_type=F32)
    kn = jnp.dot(kv_lat, wk_ref[...], preferred_element_type=F32)
    vt_ref[0] = lax.dot_general(wvt_ref[...], kv_lat, _NT, preferred_element_type=F32).astype(BF16)
    cost = cost_ref[...]
    sint = sint_ref[...]
    hd = heads * LANES
    for h in range(heads):
        r0 = slice(h * LANES, (h + 1) * LANES)
        r1 = slice(hd + h * LANES, hd + (h + 1) * LANES)
        r2 = slice(2 * hd + h * LANES, 2 * hd + (h + 1) * LANES)
        qt_ref[0, 2 * h * LANES:(2 * h + 1) * LANES, :] = (qa[r0] * q_scale).astype(BF16)
        qt_ref[0, (2 * h + 1) * LANES:(2 * h + 2) * LANES, :] = (
            (qa[r1] * cost + qa[r2] * sint) * q_scale).astype(BF16)
        k_ref[0, :, 2 * h * LANES:(2 * h + 1) * LANES] = kn[:, r0].astype(BF16)
        k_ref[0, :, (2 * h + 1) * LANES:(2 * h + 2) * LANES] = k_rope


def _rot_half(w):
    r = w.shape[-1] // 2
    return jnp.concatenate([-w[..., r:], w[..., :r]], axis=-1)


def _pad_lanes(w):
    return jnp.pad(w, [(0, 0)] * (w.ndim - 1) + [(0, LANES - w.shape[-1])])


FLASH_HEADROOM = 64.0


def _flash_kernel(qt_ref, k_ref, vt_ref, o_ref, p_ref, s_ref, mx_ref, m_ref, l_ref, acc_ref, *, tk):
    qt = qt_ref[0]
    nk = k_ref.shape[1] // tk
    assert nk % 2 == 0 and nk >= 2

    def scores(c):
        off = pl.multiple_of(c * tk, tk)
        return jnp.dot(k_ref[0, pl.ds(off, tk), :], qt, preferred_element_type=F32)

    s0 = scores(0)
    m0 = jnp.max(s0, axis=0, keepdims=True)
    p0 = jnp.exp2(s0 - m0)
    p_ref[0] = p0.astype(BF16)
    acc_ref[...] = jnp.zeros_like(acc_ref)

    def probs(c, slot, carry):
        l, top = carry
        s = scores(c)
        p = jnp.exp2(s - m0)
        p_ref[slot] = p.astype(BF16)
        return l + jnp.sum(p, axis=0, keepdims=True), jnp.maximum(top, jnp.max(s, axis=0, keepdims=True))

    def accumulate(c, slot):
        off = pl.multiple_of(c * tk, tk)
        acc_ref[...] += jnp.dot(vt_ref[0, :, pl.ds(off, tk)], p_ref[slot], preferred_element_type=F32)

    carry = (jnp.sum(p0, axis=0, keepdims=True), m0)
    for c in range(nk - 1):
        carry = probs(c + 1, (c + 1) % 2, carry)
        accumulate(c, c % 2)
    accumulate(nk - 1, (nk - 1) % 2)
    l, top = carry
    o_ref[0] = (acc_ref[...] / l).T.astype(o_ref.dtype)

    @pl.when(jnp.max(top - m0) > FLASH_HEADROOM)
    def _():
        _flash_online(qt, k_ref, vt_ref, o_ref, s_ref, mx_ref, m_ref, l_ref, acc_ref, tk)


def _flash_online(qt, k_ref, vt_ref, o_ref, s_ref, mx_ref, m_ref, l_ref, acc_ref, tk):
    nk = k_ref.shape[1] // tk
    assert nk % 2 == 0 and nk >= 2
    m_ref[...] = jnp.full_like(m_ref, -jnp.inf)
    l_ref[...] = jnp.zeros_like(l_ref)
    acc_ref[...] = jnp.zeros_like(acc_ref)

    def scores(c, slot):
        off = pl.multiple_of(c * tk, tk)
        s = jnp.dot(k_ref[0, pl.ds(off, tk), :], qt, preferred_element_type=F32)
        s_ref[slot] = s
        mx_ref[slot] = jnp.max(s, axis=0, keepdims=True)

    def update(c, slot):
        off = pl.multiple_of(c * tk, tk)
        vt = vt_ref[0, :, pl.ds(off, tk)]
        m_prev = m_ref[...]
        m_new = jnp.maximum(m_prev, mx_ref[slot])
        p = jnp.exp2(s_ref[slot] - m_new)
        a = jnp.exp2(m_prev - m_new)
        l_ref[...] = a * l_ref[...] + jnp.sum(p, axis=0, keepdims=True)
        acc_ref[...] = a * acc_ref[...] + jnp.dot(vt, p.astype(BF16), preferred_element_type=F32)
        m_ref[...] = m_new

    scores(0, 0)

    def body(i, carry):
        c = 2 * i
        scores(c + 1, 1)
        update(c, 0)
        scores(c + 2, 0)
        update(c + 1, 1)
        return carry

    lax.fori_loop(0, nk // 2 - 1, body, 0)
    scores(nk - 1, 1)
    update(nk - 2, 0)
    update(nk - 1, 1)
    o_ref[0] = (acc_ref[...] / l_ref[...]).T.astype(o_ref.dtype)


def _oproj_ln_kernel(o_ref, h_ref, w_ref, g_ref, beta_ref, out_ref, *, alpha):
    m = jnp.dot(o_ref[0], w_ref[...], preferred_element_type=F32)
    out_ref[0] = _layer_norm(alpha * h_ref[0] + m, g_ref[...], beta_ref[...])


def _mla_ln(h, w_a, qg, kvg, w_uq, w_uk, w_uv, w_o, g, beta, alpha):
    B, S, D = h.shape
    q_lora, heads, qk = w_uq.shape
    kv_lora, _, nope = w_uk.shape
    vd = w_uv.shape[-1]
    rope = qk - nope
    assert nope == LANES and vd == LANES and rope <= LANES and rope % 2 == 0
    kr = w_a[:, q_lora + kv_lora:]
    wa = jnp.concatenate([w_a[:, :q_lora + kv_lora], _pad_lanes(kr), _pad_lanes(_rot_half(kr))],
                         axis=1).astype(BF16)
    wq_r = w_uq[:, :, nope:]
    wq = jnp.concatenate([w_uq[:, :, :nope].reshape(q_lora, heads * nope),
                          _pad_lanes(wq_r).reshape(q_lora, heads * LANES),
                          _pad_lanes(_rot_half(wq_r)).reshape(q_lora, heads * LANES)], axis=1).astype(BF16)
    wk = w_uk.reshape(kv_lora, heads * nope).astype(BF16)
    wvt = w_uv.reshape(kv_lora, heads * vd).T.astype(BF16)
    wqt = wq.T
    pos = jnp.arange(S, dtype=F32)
    inv = 1.0 / (ROPE_THETA ** (jnp.arange(0, rope, 2, dtype=F32) / rope))
    ang = pos[:, None] * inv[None, :]
    cos = _pad_lanes(jnp.concatenate([jnp.cos(ang), jnp.cos(ang)], axis=1))
    sin = _pad_lanes(jnp.concatenate([jnp.sin(ang), jnp.sin(ang)], axis=1))
    q_scale = float(qk) ** -0.5 * math.log2(math.e)

    tp = _pick(S, 256)
    const = lambda a: pl.BlockSpec(a.shape, lambda b, i: (0,) * a.ndim)
    qg2, kvg2 = qg.reshape(1, q_lora), kvg.reshape(1, kv_lora)
    tok_major = pl.BlockSpec((tp, LANES), lambda b, i: (i, 0))
    feat_major = pl.BlockSpec((LANES, tp), lambda b, i: (0, i))
    qt, k, vt = pl.pallas_call(
        functools.partial(_mla_proj_kernel, q_lora=q_lora, kv_lora=kv_lora, heads=heads, q_scale=q_scale),
        grid=(B, S // tp),
        in_specs=[pl.BlockSpec((1, tp, D), lambda b, i: (b, i, 0)), const(wa), const(qg2), const(kvg2),
                  const(wqt), const(wk), const(wvt), tok_major, tok_major, feat_major, feat_major],
        out_specs=[pl.BlockSpec((1, 2 * heads * LANES, tp), lambda b, i: (b, 0, i)),
                   pl.BlockSpec((1, tp, 2 * heads * LANES), lambda b, i: (b, i, 0)),
                   pl.BlockSpec((1, heads * LANES, tp), lambda b, i: (b, 0, i))],
        out_shape=[jax.ShapeDtypeStruct((B, 2 * heads * LANES, S), BF16),
                   jax.ShapeDtypeStruct((B, S, 2 * heads * LANES), BF16),
                   jax.ShapeDtypeStruct((B, heads * LANES, S), BF16)],
        compiler_params=_params(("parallel", "parallel")),
        name="mla_proj",
    )(h, wa, qg2, kvg2, wqt, wk, wvt, cos, sin, cos.T, sin.T)

    tq = _pick(S, FLASH_TQ)
    tk = _pick(S, FLASH_TK)
    o = pl.pallas_call(
        functools.partial(_flash_kernel, tk=tk),
        grid=(B, heads, S // tq),
        in_specs=[pl.BlockSpec((1, 2 * LANES, tq), lambda b, hh, i: (b, hh, i)),
                  pl.BlockSpec((1, S, 2 * LANES), lambda b, hh, i: (b, 0, hh)),
                  pl.BlockSpec((1, LANES, S), lambda b, hh, i: (b, hh, 0))],
        out_specs=pl.BlockSpec((1, tq, LANES), lambda b, hh, i: (b, i, hh)),
        out_shape=jax.ShapeDtypeStruct((B, S, heads * LANES), BF16),
        scratch_shapes=[pltpu.VMEM((2, tk, tq), BF16), pltpu.VMEM((2, tk, tq), F32), pltpu.VMEM((2, 1, tq), F32),
                        pltpu.VMEM((1, tq), F32), pltpu.VMEM((1, tq), F32), pltpu.VMEM((LANES, tq), F32)],
        compiler_params=_params(("parallel", "parallel", "arbitrary")),
        name="mla_flash",
    )(qt, k, vt)

    to = _pick(S, 512)
    vec = pl.BlockSpec((1, D), lambda b, i: (0, 0))
    return pl.pallas_call(
        functools.partial(_oproj_ln_kernel, alpha=alpha),
        grid=(B, S // to),
        in_specs=[pl.BlockSpec((1, to, heads * vd), lambda b, i: (b, i, 0)),
                  pl.BlockSpec((1, to, D), lambda b, i: (b, i, 0)),
                  pl.BlockSpec((heads * vd, D), lambda b, i: (0, 0)), vec, vec],
        out_specs=pl.BlockSpec((1, to, D), lambda b, i: (b, i, 0)),
        out_shape=jax.ShapeDtypeStruct((B, S, D), F32),
        compiler_params=_params(("parallel", "parallel")),
        name="mla_oproj_ln",
    )(o, h, w_o.astype(BF16), g.reshape(1, D), beta.reshape(1, D))


def kernel(x, pool_w, pool_b, pool_scale, mla_w_a, mla_q_norm_g, mla_kv_norm_g, mla_w_uq, mla_w_uk,
           mla_w_uv, mla_w_o, ln_mix_g, ln_mix_b, router_w, router_b, exp_w_gate, exp_w_up, exp_w_down,
           sh_w_gate, sh_w_up, sh_w_down, ln_ffn_g, ln_ffn_b):
    B, S, D = x.shape
    depth = ln_mix_g.shape[0]
    alpha = (2 * depth) ** 0.25
    h = x
    for i in range(depth):
        j = i // 2
        if i % 2 == 0:
            h = _pool_ln(h, pool_w[j], pool_b[j], pool_scale[j], ln_mix_g[i], ln_mix_b[i], alpha)
        else:
            h = _mla_ln(h, mla_w_a[j], mla_q_norm_g[j], mla_kv_norm_g[j], mla_w_uq[j], mla_w_uk[j],
                        mla_w_uv[j], mla_w_o[j], ln_mix_g[i], ln_mix_b[i], alpha)
        h = _moe_ln(h.reshape(B * S, D), router_w[i], router_b[i], exp_w_gate, exp_w_up, exp_w_down, i,
                    sh_w_gate[i], sh_w_up[i], sh_w_down[i], ln_ffn_g[i], ln_ffn_b[i],
                    alpha).reshape(B, S, D)
    return h
```

```python
import functools
import math

import jax
import jax.numpy as jnp
from jax import lax
from jax.experimental import pallas as pl
from jax.experimental.pallas import tpu as pltpu

F32 = jnp.float32
BF16 = jnp.bfloat16

LN_EPS = 1e-5
RMS_EPS = 1e-6
POOL_WINDOWS = (2, 4, 8, 16)
POOL_HALO = 8
ROPE_THETA = 10000.0
TOP_K = 8
N_GROUPS = 8
TOPK_GROUPS = 4
ROUTED_SCALE = 2.5
LANES = 128
VMEM_LIMIT = 48 * 1024 * 1024

_NT = (((1,), (1,)), ((), ()))


def _pick(n, pref):
    t = min(pref, n)
    while n % t:
        t //= 2
    assert t >= 8 and n % t == 0, (n, pref)
    return t


def _layer_norm(z, g, b):
    mu = jnp.mean(z, axis=-1, keepdims=True)
    zc = z - mu
    var = jnp.mean(zc * zc, axis=-1, keepdims=True)
    return zc * lax.rsqrt(var + LN_EPS) * g + b


def _sigmoid(x):
    return 1.0 / (1.0 + jnp.exp(-x))


def _params(sem):
    return pltpu.CompilerParams(dimension_semantics=sem, vmem_limit_bytes=VMEM_LIMIT)


def _pool_ln_kernel(xc_ref, xp_ref, xn_ref, w_ref, b_ref, sc_ref, g_ref, beta_ref, o_ref, ext_ref,
                    *, tile, seq, alpha):
    i = pl.program_id(1)
    n_tiles = pl.num_programs(1)
    xc = xc_ref[0]
    d_model = xc.shape[-1]
    ch = d_model // len(POOL_WINDOWS)
    ext_ref[0:POOL_HALO, :] = jnp.where(i > 0, xp_ref[0], 0.0)
    ext_ref[POOL_HALO:POOL_HALO + tile, :] = xc
    ext_ref[POOL_HALO + tile:2 * POOL_HALO + tile, :] = jnp.where(i < n_tiles - 1, xn_ref[0], 0.0)
    pos = i * tile + lax.broadcasted_iota(jnp.int32, (tile, 1), 0)
    outs = []
    for g, win in enumerate(POOL_WINDOWS):
        half = win // 2
        cols = slice(g * ch, (g + 1) * ch)
        acc = ext_ref[POOL_HALO - half:POOL_HALO - half + tile, cols]
        for j in range(-half + 1, half):
            acc = acc + ext_ref[POOL_HALO + j:POOL_HALO + j + tile, cols]
        cnt = (jnp.minimum(pos + half, seq) - jnp.maximum(pos - half, 0)).astype(F32)
        pooled = acc / cnt - xc[:, cols]
        outs.append(jnp.dot(pooled.astype(BF16), w_ref[g], preferred_element_type=F32))
    y = jnp.concatenate(outs, axis=-1)
    y = (y + b_ref[...]) * sc_ref[...]
    o_ref[0] = _layer_norm(alpha * xc + y, g_ref[...], beta_ref[...])


def _pool_ln(x, w, b, sc, g, beta, alpha):
    B, S, D = x.shape
    tile = _pick(S, 512)
    nh = tile // POOL_HALO
    last = S // POOL_HALO - 1
    row = lambda v: v.reshape(1, D)
    vec = pl.BlockSpec((1, D), lambda bi, i: (0, 0))
    return pl.pallas_call(
        functools.partial(_pool_ln_kernel, tile=tile, seq=S, alpha=alpha),
        grid=(B, S // tile),
        in_specs=[
            pl.BlockSpec((1, tile, D), lambda bi, i: (bi, i, 0)),
            pl.BlockSpec((1, POOL_HALO, D), lambda bi, i: (bi, jnp.maximum(i * nh - 1, 0), 0)),
            pl.BlockSpec((1, POOL_HALO, D), lambda bi, i: (bi, jnp.minimum((i + 1) * nh, last), 0)),
            pl.BlockSpec(w.shape, lambda bi, i: (0, 0, 0)),
            vec, vec, vec, vec,
        ],
        out_specs=pl.BlockSpec((1, tile, D), lambda bi, i: (bi, i, 0)),
        out_shape=jax.ShapeDtypeStruct((B, S, D), F32),
        scratch_shapes=[pltpu.VMEM((tile + 2 * POOL_HALO, D), F32)],
        compiler_params=_params(("parallel", "parallel")),
        name="pool_ln",
    )(x, x, x, w.astype(BF16), row(b), row(sc), row(g), row(beta))


def _router_kernel(h_ref, whi_ref, wlo_ref, rb_ref, tri_ref, ltri_ref,
                   gate_ref, lpos_ref, cnt_ref, tcnt_ref, toff_ref, tbase_ref, ttot_ref):
    i = pl.program_id(0)

    @pl.when(i == 0)
    def _():
        cnt_ref[...] = jnp.zeros_like(cnt_ref)

    x = h_ref[...]
    xhi = x.astype(BF16)
    xlo = (x - xhi.astype(F32)).astype(BF16)
    whi = whi_ref[...]
    logits = (lax.dot_general(whi, xhi, _NT, preferred_element_type=F32)
              + lax.dot_general(whi, xlo, _NT, preferred_element_type=F32)
              + lax.dot_general(wlo_ref[...], xhi, _NT, preferred_element_type=F32))
    n_exp, tile = logits.shape
    per = n_exp // N_GROUPS
    scores = _sigmoid(logits)
    biased = scores + rb_ref[...]
    neg = -jnp.inf

    sub = lax.broadcasted_iota(jnp.int32, (per, tile), 0).astype(F32)
    gscore = []
    for g in range(N_GROUPS):
        bg = biased[g * per:(g + 1) * per, :]
        m1 = jnp.max(bg, axis=0, keepdims=True)
        i1 = jnp.min(jnp.where(bg == m1, sub, float(per)), axis=0, keepdims=True)
        m2 = jnp.max(jnp.where(sub == i1, neg, bg), axis=0, keepdims=True)
        gscore.append(m1 + m2)
    masked = []
    for g in range(N_GROUPS):
        beat = jnp.zeros_like(gscore[g])
        for o in range(N_GROUPS):
            if o == g:
                continue
            wins = (gscore[o] >= gscore[g]) if o < g else (gscore[o] > gscore[g])
            beat = beat + jnp.where(wins, 1.0, 0.0)
        keep = jnp.broadcast_to(beat < float(TOPK_GROUPS), (per, tile))
        masked.append(jnp.where(keep, biased[g * per:(g + 1) * per, :], neg))
    masked = jnp.concatenate(masked, axis=0)

    row = lax.broadcasted_iota(jnp.int32, (n_exp, tile), 0).astype(F32)
    chosen = jnp.zeros((n_exp, tile), F32)
    idxs, gates = [], []
    for _ in range(TOP_K):
        m = jnp.max(masked, axis=0, keepdims=True)
        idx = jnp.min(jnp.where(masked == m, row, float(n_exp)), axis=0, keepdims=True)
        hit = row == idx
        gates.append(jnp.sum(jnp.where(hit, scores, 0.0), axis=0, keepdims=True))
        masked = jnp.where(hit, neg, masked)
        chosen = jnp.where(hit, 1.0, chosen)
        idxs.append(idx)
    gsum = gates[0]
    for k in range(1, TOP_K):
        gsum = gsum + gates[k]

    earlier = jnp.dot(chosen.astype(BF16), tri_ref[...], preferred_element_type=F32)
    count = jnp.sum(chosen, axis=1, keepdims=True)
    run = count + (count - 2.0 * jnp.floor(0.5 * count))
    lower = jnp.dot(ltri_ref[...], jnp.broadcast_to(run, (n_exp, LANES)).astype(BF16),
                    preferred_element_type=F32)[:, :1]
    order = earlier + lower
    for k in range(TOP_K):
        gate_ref[k:k + 1, :] = gates[k] / gsum * ROUTED_SCALE
        lpos_k = jnp.sum(jnp.where(row == idxs[k], order, 0.0), axis=0, keepdims=True)
        lpos_ref[k:k + 1, :] = lpos_k.astype(jnp.int32)
    tcnt_ref[...] = run.astype(jnp.int32)
    toff_ref[...] = lower.astype(jnp.int32)
    tbase_ref[...] = cnt_ref[...].astype(jnp.int32)
    ttot_ref[...] = jnp.sum(run, axis=0, keepdims=True).astype(jnp.int32)
    cnt_ref[...] = cnt_ref[...] + run


def _router(h2, rw, rb):
    N, D = h2.shape
    E = rw.shape[1]
    tile = _pick(N, MOE_TILE)
    assert tile <= 256
    n_tiles = N // tile
    wt = rw.T
    whi = wt.astype(BF16)
    wlo = (wt - whi.astype(F32)).astype(BF16)
    ar = jnp.arange(tile)
    tri = (ar[:, None] < ar[None, :]).astype(BF16)
    ae = jnp.arange(E)
    ltri = (ae[None, :] < ae[:, None]).astype(BF16)
    const = lambda shape: pl.BlockSpec(shape, lambda i: (0, 0))
    kn = pl.BlockSpec((TOP_K, tile), lambda i: (0, i))
    table = pl.BlockSpec((None, E, 1), lambda i: (i, 0, 0))
    table_shape = jax.ShapeDtypeStruct((n_tiles, E, 1), jnp.int32)
    return pl.pallas_call(
        _router_kernel,
        grid=(n_tiles,),
        in_specs=[pl.BlockSpec((tile, D), lambda i: (i, 0)), const((E, D)), const((E, D)),
                  const((E, 1)), const((tile, tile)), const((E, E))],
        out_specs=[kn, kn, const((E, 1)), table, table, table,
                   pl.BlockSpec((None, 1, 1), lambda i: (i, 0, 0))],
        out_shape=[jax.ShapeDtypeStruct((TOP_K, N), F32), jax.ShapeDtypeStruct((TOP_K, N), jnp.int32),
                   jax.ShapeDtypeStruct((E, 1), F32), table_shape, table_shape, table_shape,
                   jax.ShapeDtypeStruct((n_tiles, 1, 1), jnp.int32)],
        compiler_params=_params(("arbitrary",)),
        name="moe_router",
    )(h2, whi, wlo, rb.reshape(E, 1), tri, ltri)


def _pack_pairs(x):
    half = x.shape[-1] // 2
    hi = lax.bitcast_convert_type(x[:, :half].astype(BF16).astype(F32), jnp.uint32)
    lo = lax.bitcast_convert_type(x[:, half:].astype(BF16).astype(F32), jnp.uint32)
    return hi | (lo >> 16)


def _unpack_pairs(w):
    hi = lax.bitcast_convert_type(w & jnp.uint32(0xFFFF0000), F32)
    lo = lax.bitcast_convert_type(w << 16, F32)
    return jnp.concatenate([hi.astype(BF16), lo.astype(BF16)], axis=-1)


def _to_slab(slab_ref, x):
    w = _pack_pairs(x)
    lines = w.shape[1] // LANES
    for a in range(lines):
        slab_ref[pl.ds(a, w.shape[0], stride=lines), :] = w[:, a * LANES:(a + 1) * LANES]


def _from_slab(slab_ref, rows, lines, valid):
    w = jnp.concatenate([slab_ref[pl.ds(a, rows, stride=lines), :] for a in range(lines)], axis=-1)
    r = lax.broadcasted_iota(jnp.int32, (rows, 1), 0)
    return _unpack_pairs(jnp.where(r < valid, w, jnp.uint32(0)))


def _start_runs(toff_ref, gst_ref, tcnt_ref, local, remote, sem, lines, to_remote):
    def body(e, c):
        cnt, off, gst = tcnt_ref[0, 0, e], toff_ref[0, 0, e], gst_ref[0, 0, e]

        @pl.when(cnt > 0)
        def _():
            loc = local.at[pl.ds(pl.multiple_of(off * lines, 8), cnt * lines), :]
            rem = remote.at[pl.ds(pl.multiple_of(gst * lines, 8), cnt * lines), :]
            (pltpu.make_async_copy(loc, rem, sem) if to_remote else pltpu.make_async_copy(rem, loc, sem)).start()
        return c
    lax.fori_loop(0, tcnt_ref.shape[2], body, 0, unroll=RUN_UNROLL)


def _wait_all_runs(slab, remote, sem, total_lines):
    n = pl.multiple_of(total_lines, 8)
    pltpu.make_async_copy(slab.at[pl.ds(0, n), :], remote.at[pl.ds(0, n), :], sem).wait()


def _sorted_rows(tile, n_exp):
    return TOP_K * tile + n_exp


def _dispatch_kernel(toff_ref, gst_ref, tcnt_ref, ttot_ref, lpos_ref, h_ref, xs_ref, slab, sem, *, lines):
    tile = h_ref.shape[0]
    rows = slab.shape[0] // lines
    srow = lax.broadcasted_iota(jnp.int32, (rows, tile), 0)
    perm = jnp.zeros((rows, tile), F32)
    for k in range(TOP_K):
        perm = jnp.where(srow == lpos_ref[k:k + 1, :], 1.0, perm)
    _to_slab(slab, jnp.dot(perm.astype(BF16), h_ref[...].astype(BF16), preferred_element_type=F32))
    _start_runs(toff_ref, gst_ref, tcnt_ref, slab, xs_ref, sem, lines, to_remote=True)
    _wait_all_runs(slab, xs_ref, sem, ttot_ref[0, 0, 0] * lines)


def _run_specs(n_exp):
    smem = lambda n: pl.BlockSpec((1, 1, n), lambda i: (i, 0, 0), memory_space=pltpu.SMEM)
    return [smem(n_exp), smem(n_exp), smem(n_exp), smem(1)]


def _slab_lines(d_model):
    lines = d_model // (2 * LANES)
    assert lines * 2 * LANES == d_model and (2 * lines) % 8 == 0
    return lines


def _dispatch(h2, lpos, runs, n_slots):
    N, D = h2.shape
    E = runs[0].shape[2]
    tile = _pick(N, MOE_TILE)
    lines = _slab_lines(D)
    return pl.pallas_call(
        functools.partial(_dispatch_kernel, lines=lines),
        grid=(N // tile,),
        in_specs=_run_specs(E) + [pl.BlockSpec((TOP_K, tile), lambda i: (0, i)),
                                  pl.BlockSpec((tile, D), lambda i: (i, 0))],
        out_specs=pl.BlockSpec(memory_space=pl.ANY),
        out_shape=jax.ShapeDtypeStruct((n_slots * lines, LANES), jnp.uint32),
        scratch_shapes=[pltpu.VMEM((_sorted_rows(tile, E) * lines, LANES), jnp.uint32),
                        pltpu.SemaphoreType.DMA(())],
        compiler_params=_params(("arbitrary",)),
        name="moe_dispatch",
    )(*runs, lpos, h2)


def _expert_kernel(bidx_ref, bexp_ref, bval_ref, bnew_ref, bslot_ref, bnext_ref, xs_ref,
                   wg_hbm, wu_hbm, wd_hbm, o_ref, wg_f, wu_f, wd_f, wg_s, wu_s, wd_s, sem, *, layer):
    i = pl.program_id(0)
    valid = bval_ref[i]

    def weight_copies(e, slot):
        return [pltpu.make_async_copy(w.at[layer, e], f.at[slot], sem.at[slot])
                for w, f in ((wg_hbm, wg_f), (wu_hbm, wu_f), (wd_hbm, wd_f))]

    @pl.when(i == 0)
    def _():
        for cp in weight_copies(bexp_ref[0], 0):
            cp.start()

    @pl.when(bnew_ref[i] == 1)
    def _():
        slot = bslot_ref[i]
        for cp in weight_copies(bexp_ref[i], slot):
            cp.wait()

        @pl.when(bnext_ref[i] >= 0)
        def _():
            for cp in weight_copies(bnext_ref[i], 1 - slot):
                cp.start()

        wg_s[...] = wg_f[slot].astype(BF16)
        wu_s[...] = wu_f[slot].astype(BF16)
        wd_s[...] = wd_f[slot].astype(BF16)

    @pl.when(valid > 0)
    def _():
        lines = _slab_lines(wg_s.shape[0])
        x = _from_slab(xs_ref, xs_ref.shape[0] // lines, lines, valid)
        hg = jnp.dot(x, wg_s[...], preferred_element_type=F32)
        hu = jnp.dot(x, wu_s[...], preferred_element_type=F32)
        hb = hg * _sigmoid(hg) * hu
        _to_slab(o_ref, jnp.dot(hb.astype(BF16), wd_s[...], preferred_element_type=F32))


def _experts(xs, blocks, wg, wu, wd, layer, block):
    D, Fe = wg.shape[-2:]
    lines = _slab_lines(D)
    P = xs.shape[0] // lines
    rows = pl.BlockSpec((block * lines, LANES), lambda i, bi, *_: (bi[i], 0))
    hbm = pl.BlockSpec(memory_space=pl.ANY)
    return pl.pallas_call(
        functools.partial(_expert_kernel, layer=layer),
        grid_spec=pltpu.PrefetchScalarGridSpec(
            num_scalar_prefetch=len(blocks),
            grid=(P // block,),
            in_specs=[rows, hbm, hbm, hbm],
            out_specs=rows,
            scratch_shapes=[pltpu.VMEM((2, D, Fe), F32), pltpu.VMEM((2, D, Fe), F32),
                            pltpu.VMEM((2, Fe, D), F32),
                            pltpu.VMEM((D, Fe), BF16), pltpu.VMEM((D, Fe), BF16),
                            pltpu.VMEM((Fe, D), BF16), pltpu.SemaphoreType.DMA((2,))],
        ),
        out_shape=jax.ShapeDtypeStruct((P * lines, LANES), jnp.uint32),
        compiler_params=_params(("arbitrary",)),
        name="moe_experts",
    )(*blocks, xs, wg, wu, wd)


def _combine_kernel(toff_ref, gst_ref, tcnt_ref, ttot_ref, lpos_ref, gate_ref, h_ref, ys_ref, sg_ref, su_ref,
                    sd_ref, g_ref, beta_ref, o_ref, slab, sem, *, lines, alpha):
    _start_runs(toff_ref, gst_ref, tcnt_ref, slab, ys_ref, sem, lines, to_remote=False)
    h = h_ref[...]
    tile = h.shape[0]
    rows = slab.shape[0] // lines
    hb = h.astype(BF16)
    sgate = jnp.dot(hb, sg_ref[...], preferred_element_type=F32)
    sup = jnp.dot(hb, su_ref[...], preferred_element_type=F32)
    mid = sgate * _sigmoid(sgate) * sup
    acc = jnp.dot(mid.astype(BF16), sd_ref[...], preferred_element_type=F32)
    scol = lax.broadcasted_iota(jnp.int32, (tile, rows), 1)
    weights = jnp.zeros((tile, rows), F32)
    for k in range(TOP_K):
        weights = jnp.where(scol == lpos_ref[:, k:k + 1], gate_ref[:, k:k + 1], weights)
    w_hi = weights.astype(BF16)
    w_lo = (weights - w_hi.astype(F32)).astype(BF16)
    total = ttot_ref[0, 0, 0]
    _wait_all_runs(slab, ys_ref, sem, total * lines)
    ys = _from_slab(slab, rows, lines, total)
    acc = (acc + jnp.dot(w_hi, ys, preferred_element_type=F32)
           + jnp.dot(w_lo, ys, preferred_element_type=F32))
    o_ref[...] = _layer_norm(alpha * h + acc, g_ref[...], beta_ref[...])


def _combine(h2, lpos_nk, gate_nk, runs, ys, sg, su, sd, g, beta, alpha):
    N, D = h2.shape
    Fs = sg.shape[1]
    E = runs[0].shape[2]
    tile = _pick(N, MOE_TILE)
    lines = _slab_lines(D)
    const = lambda shape: pl.BlockSpec(shape, lambda i: (0, 0))
    nk = pl.BlockSpec((tile, TOP_K), lambda i: (i, 0))
    return pl.pallas_call(
        functools.partial(_combine_kernel, lines=lines, alpha=alpha),
        grid=(N // tile,),
        in_specs=_run_specs(E) + [
            nk, nk, pl.BlockSpec((tile, D), lambda i: (i, 0)), pl.BlockSpec(memory_space=pl.ANY),
            const((D, Fs)), const((D, Fs)), const((Fs, D)), const((1, D)), const((1, D))],
        out_specs=pl.BlockSpec((tile, D), lambda i: (i, 0)),
        out_shape=jax.ShapeDtypeStruct((N, D), F32),
        scratch_shapes=[pltpu.VMEM((_sorted_rows(tile, E) * lines, LANES), jnp.uint32),
                        pltpu.SemaphoreType.DMA(())],
        compiler_params=_params(("arbitrary",)),
        name="moe_combine",
    )(*runs, lpos_nk, gate_nk, h2, ys, sg.astype(BF16), su.astype(BF16), sd.astype(BF16),
      g.reshape(1, D), beta.reshape(1, D))


FLASH_TQ = 512
FLASH_TK = 512
MOE_TILE = 256
RUN_UNROLL = 8
EXPERT_BLOCK = 256


def _block_schedule(counts, max_rows, blk):
    E = counts.shape[0]
    padded = (counts + blk - 1) // blk * blk
    pend = jnp.cumsum(padded)
    pstart = pend - padded
    n_blocks = -(-(max_rows + E * (blk - 1)) // blk)
    n_used = pend[-1] // blk
    bidx = jnp.minimum(jnp.arange(n_blocks, dtype=jnp.int32), n_used - 1)
    first = (bidx * blk)[:, None]
    owns = (pstart[None, :] <= first) & (first < pend[None, :])
    pick = lambda v: jnp.sum(jnp.where(owns, v[None, :], 0), axis=1).astype(jnp.int32)
    bexp = pick(jnp.arange(E, dtype=jnp.int32))
    bval = jnp.clip(pick(counts) - (bidx * blk - pick(pstart)), 0, blk)
    used = jnp.arange(n_blocks) < n_used
    bval = jnp.where(used, bval, 0).astype(jnp.int32)
    bnew = (used & (jnp.concatenate([jnp.full((1,), -1, jnp.int32), bexp[:-1]]) != bexp)).astype(jnp.int32)
    bslot = ((jnp.cumsum(bnew) - 1) % 2).astype(jnp.int32)
    later = jnp.concatenate([jnp.where(bnew == 1, bexp, E)[1:], jnp.full((1,), E, jnp.int32)])
    bnext = lax.cummin(later, reverse=True)
    bnext = jnp.where(bnext == E, -1, bnext).astype(jnp.int32)
    return pstart, n_blocks, (bidx, bexp, bval, bnew, bslot, bnext)


def _moe_ln(h2, rw, rb, wg, wu, wd, layer, sg, su, sd, g, beta, alpha):
    N, D = h2.shape
    E = rw.shape[1]
    gate, lpos, cnt, tcnt, toff, tbase, ttot = _router(h2, rw, rb)
    blk = EXPERT_BLOCK
    n_tiles = ttot.shape[0]
    pstart, n_blocks, blocks = _block_schedule(cnt[:, 0].astype(jnp.int32), N * TOP_K + n_tiles * E, blk)
    per_tile = lambda a: a.reshape(n_tiles, 1, E)
    runs = (per_tile(toff), per_tile(tbase) + pstart[None, None, :], per_tile(tcnt), ttot)
    xs = _dispatch(h2, lpos, runs, n_blocks * blk)
    ys = _experts(xs, blocks, wg, wu, wd, layer, blk)
    return _combine(h2, lpos.T, gate.T, runs, ys, sg, su, sd, g, beta, alpha)


def _mla_proj_kernel(h_ref, wa_ref, qg_ref, kvg_ref, wqt_ref, wk_ref, wvt_ref, cos_ref, sin_ref,
                     cost_ref, sint_ref, qt_ref, k_ref, vt_ref, *, q_lora, kv_lora, heads, q_scale):
    x = h_ref[0].astype(BF16)
    a = jnp.dot(x, wa_ref[...], preferred_element_type=F32)

    def rms(v, g):
        return v * lax.rsqrt(jnp.mean(v * v, axis=-1, keepdims=True) + RMS_EPS) * g

    q_lat = rms(a[:, :q_lora], qg_ref[...]).astype(BF16)
    kv_lat = rms(a[:, q_lora:q_lora + kv_lora], kvg_ref[...]).astype(BF16)
    o = q_lora + kv_lora
    k_rope = (a[:, o:o + LANES] * cos_ref[...] + a[:, o + LANES:o + 2 * LANES] * sin_ref[...]).astype(BF16)
    qa = lax.dot_general(wqt_ref[...], q_lat, _NT, preferred_element_type=F32)
    kn = jnp.dot(kv_lat, wk_ref[...], preferred_element_type=F32)
    vt_ref[0] = lax.dot_general(wvt_ref[...], kv_lat, _NT, preferred_element_type=F32).astype(BF16)
    cost = cost_ref[...]
    sint = sint_ref[...]
    hd = heads * LANES
    for h in range(heads):
        r0 = slice(h * LANES, (h + 1) * LANES)
        r1 = slice(hd + h * LANES, hd + (h + 1) * LANES)
        r2 = slice(2 * hd + h * LANES, 2 * hd + (h + 1) * LANES)
        qt_ref[0, 2 * h * LANES:(2 * h + 1) * LANES, :] = (qa[r0] * q_scale).astype(BF16)
        qt_ref[0, (2 * h + 1) * LANES:(2 * h + 2) * LANES, :] = (
            (qa[r1] * cost + qa[r2] * sint) * q_scale).astype(BF16)
        k_ref[0, :, 2 * h * LANES:(2 * h + 1) * LANES] = kn[:, r0].astype(BF16)
        k_ref[0, :, (2 * h + 1) * LANES:(2 * h + 2) * LANES] = k_rope


def _rot_half(w):
    r = w.shape[-1] // 2
    return jnp.concatenate([-w[..., r:], w[..., :r]], axis=-1)


def _pad_lanes(w):
    return jnp.pad(w, [(0, 0)] * (w.ndim - 1) + [(0, LANES - w.shape[-1])])


FLASH_HEADROOM = 64.0


def _flash_kernel(qt_ref, k_ref, vt_ref, o_ref, p_ref, s_ref, mx_ref, m_ref, l_ref, acc_ref, *, tk):
    qt = qt_ref[0]
    nk = k_ref.shape[1] // tk
    assert nk % 2 == 0 and nk >= 2

    def scores(c):
        off = pl.multiple_of(c * tk, tk)
        return jnp.dot(k_ref[0, pl.ds(off, tk), :], qt, preferred_element_type=F32)

    s0 = scores(0)
    m0 = jnp.max(s0, axis=0, keepdims=True)
    p0 = jnp.exp2(s0 - m0)
    p_ref[0] = p0.astype(BF16)
    acc_ref[...] = jnp.zeros_like(acc_ref)

    def probs(c, slot, carry):
        l, top = carry
        s = scores(c)
        p = jnp.exp2(s - m0)
        p_ref[slot] = p.astype(BF16)
        return l + jnp.sum(p, axis=0, keepdims=True), jnp.maximum(top, jnp.max(s, axis=0, keepdims=True))

    def accumulate(c, slot):
        off = pl.multiple_of(c * tk, tk)
        acc_ref[...] += jnp.dot(vt_ref[0, :, pl.ds(off, tk)], p_ref[slot], preferred_element_type=F32)

    carry = (jnp.sum(p0, axis=0, keepdims=True), m0)
    for c in range(nk - 1):
        carry = probs(c + 1, (c + 1) % 2, carry)
        accumulate(c, c % 2)
    accumulate(nk - 1, (nk - 1) % 2)
    l, top = carry
    o_ref[0] = (acc_ref[...] / l).T.astype(o_ref.dtype)

    @pl.when(jnp.max(top - m0) > FLASH_HEADROOM)
    def _():
        _flash_online(qt, k_ref, vt_ref, o_ref, s_ref, mx_ref, m_ref, l_ref, acc_ref, tk)


def _flash_online(qt, k_ref, vt_ref, o_ref, s_ref, mx_ref, m_ref, l_ref, acc_ref, tk):
    nk = k_ref.shape[1] // tk
    assert nk % 2 == 0 and nk >= 2
    m_ref[...] = jnp.full_like(m_ref, -jnp.inf)
    l_ref[...] = jnp.zeros_like(l_ref)
    acc_ref[...] = jnp.zeros_like(acc_ref)

    def scores(c, slot):
        off = pl.multiple_of(c * tk, tk)
        s = jnp.dot(k_ref[0, pl.ds(off, tk), :], qt, preferred_element_type=F32)
        s_ref[slot] = s
        mx_ref[slot] = jnp.max(s, axis=0, keepdims=True)

    def update(c, slot):
        off = pl.multiple_of(c * tk, tk)
        vt = vt_ref[0, :, pl.ds(off, tk)]
        m_prev = m_ref[...]
        m_new = jnp.maximum(m_prev, mx_ref[slot])
        p = jnp.exp2(s_ref[slot] - m_new)
        a = jnp.exp2(m_prev - m_new)
        l_ref[...] = a * l_ref[...] + jnp.sum(p, axis=0, keepdims=True)
        acc_ref[...] = a * acc_ref[...] + jnp.dot(vt, p.astype(BF16), preferred_element_type=F32)
        m_ref[...] = m_new

    scores(0, 0)

    def body(i, carry):
        c = 2 * i
        scores(c + 1, 1)
        update(c, 0)
        scores(c + 2, 0)
        update(c + 1, 1)
        return carry

    lax.fori_loop(0, nk // 2 - 1, body, 0)
    scores(nk - 1, 1)
    update(nk - 2, 0)
    update(nk - 1, 1)
    o_ref[0] = (acc_ref[...] / l_ref[...]).T.astype(o_ref.dtype)


def _oproj_ln_kernel(o_ref, h_ref, w_ref, g_ref, beta_ref, out_ref, *, alpha):
    m = jnp.dot(o_ref[0], w_ref[...], preferred_element_type=F32)
    out_ref[0] = _layer_norm(alpha * h_ref[0] + m, g_ref[...], beta_ref[...])


def _mla_ln(h, w_a, qg, kvg, w_uq, w_uk, w_uv, w_o, g, beta, alpha):
    B, S, D = h.shape
    q_lora, heads, qk = w_uq.shape
    kv_lora, _, nope = w_uk.shape
    vd = w_uv.shape[-1]
    rope = qk - nope
    assert nope == LANES and vd == LANES and rope <= LANES and rope % 2 == 0
    kr = w_a[:, q_lora + kv_lora:]
    wa = jnp.concatenate([w_a[:, :q_lora + kv_lora], _pad_lanes(kr), _pad_lanes(_rot_half(kr))],
                         axis=1).astype(BF16)
    wq_r = w_uq[:, :, nope:]
    wq = jnp.concatenate([w_uq[:, :, :nope].reshape(q_lora, heads * nope),
                          _pad_lanes(wq_r).reshape(q_lora, heads * LANES),
                          _pad_lanes(_rot_half(wq_r)).reshape(q_lora, heads * LANES)], axis=1).astype(BF16)
    wk = w_uk.reshape(kv_lora, heads * nope).astype(BF16)
    wvt = w_uv.reshape(kv_lora, heads * vd).T.astype(BF16)
    wqt = wq.T
    pos = jnp.arange(S, dtype=F32)
    inv = 1.0 / (ROPE_THETA ** (jnp.arange(0, rope, 2, dtype=F32) / rope))
    ang = pos[:, None] * inv[None, :]
    cos = _pad_lanes(jnp.concatenate([jnp.cos(ang), jnp.cos(ang)], axis=1))
    sin = _pad_lanes(jnp.concatenate([jnp.sin(ang), jnp.sin(ang)], axis=1))
    q_scale = float(qk) ** -0.5 * math.log2(math.e)

    tp = _pick(S, 256)
    const = lambda a: pl.BlockSpec(a.shape, lambda b, i: (0,) * a.ndim)
    qg2, kvg2 = qg.reshape(1, q_lora), kvg.reshape(1, kv_lora)
    tok_major = pl.BlockSpec((tp, LANES), lambda b, i: (i, 0))
    feat_major = pl.BlockSpec((LANES, tp), lambda b, i: (0, i))
    qt, k, vt = pl.pallas_call(
        functools.partial(_mla_proj_kernel, q_lora=q_lora, kv_lora=kv_lora, heads=heads, q_scale=q_scale),
        grid=(B, S // tp),
        in_specs=[pl.BlockSpec((1, tp, D), lambda b, i: (b, i, 0)), const(wa), const(qg2), const(kvg2),
                  const(wqt), const(wk), const(wvt), tok_major, tok_major, feat_major, feat_major],
        out_specs=[pl.BlockSpec((1, 2 * heads * LANES, tp), lambda b, i: (b, 0, i)),
                   pl.BlockSpec((1, tp, 2 * heads * LANES), lambda b, i: (b, i, 0)),
                   pl.BlockSpec((1, heads * LANES, tp), lambda b, i: (b, 0, i))],
        out_shape=[jax.ShapeDtypeStruct((B, 2 * heads * LANES, S), BF16),
                   jax.ShapeDtypeStruct((B, S, 2 * heads * LANES), BF16),
                   jax.ShapeDtypeStruct((B, heads * LANES, S), BF16)],
        compiler_params=_params(("parallel", "parallel")),
        name="mla_proj",
    )(h, wa, qg2, kvg2, wqt, wk, wvt, cos, sin, cos.T, sin.T)

    tq = _pick(S, FLASH_TQ)
    tk = _pick(S, FLASH_TK)
    o = pl.pallas_call(
        functools.partial(_flash_kernel, tk=tk),
        grid=(B, heads, S // tq),
        in_specs=[pl.BlockSpec((1, 2 * LANES, tq), lambda b, hh, i: (b, hh, i)),
                  pl.BlockSpec((1, S, 2 * LANES), lambda b, hh, i: (b, 0, hh)),
                  pl.BlockSpec((1, LANES, S), lambda b, hh, i: (b, hh, 0))],
        out_specs=pl.BlockSpec((1, tq, LANES), lambda b, hh, i: (b, i, hh)),
        out_shape=jax.ShapeDtypeStruct((B, S, heads * LANES), BF16),
        scratch_shapes=[pltpu.VMEM((2, tk, tq), BF16), pltpu.VMEM((2, tk, tq), F32), pltpu.VMEM((2, 1, tq), F32),
                        pltpu.VMEM((1, tq), F32), pltpu.VMEM((1, tq), F32), pltpu.VMEM((LANES, tq), F32)],
        compiler_params=_params(("parallel", "parallel", "arbitrary")),
        name="mla_flash",
    )(qt, k, vt)

    to = _pick(S, 512)
    vec = pl.BlockSpec((1, D), lambda b, i: (0, 0))
    return pl.pallas_call(
        functools.partial(_oproj_ln_kernel, alpha=alpha),
        grid=(B, S // to),
        in_specs=[pl.BlockSpec((1, to, heads * vd), lambda b, i: (b, i, 0)),
                  pl.BlockSpec((1, to, D), lambda b, i: (b, i, 0)),
                  pl.BlockSpec((heads * vd, D), lambda b, i: (0, 0)), vec, vec],
        out_specs=pl.BlockSpec((1, to, D), lambda b, i: (b, i, 0)),
        out_shape=jax.ShapeDtypeStruct((B, S, D), F32),
        compiler_params=_params(("parallel", "parallel")),
        name="mla_oproj_ln",
    )(o, h, w_o.astype(BF16), g.reshape(1, D), beta.reshape(1, D))


def kernel(x, pool_w, pool_b, pool_scale, mla_w_a, mla_q_norm_g, mla_kv_norm_g, mla_w_uq, mla_w_uk,
           mla_w_uv, mla_w_o, ln_mix_g, ln_mix_b, router_w, router_b, exp_w_gate, exp_w_up, exp_w_down,
           sh_w_gate, sh_w_up, sh_w_down, ln_ffn_g, ln_ffn_b):
    B, S, D = x.shape
    depth = ln_mix_g.shape[0]
    alpha = (2 * depth) ** 0.25
    h = x
    for i in range(depth):
        j = i // 2
        if i % 2 == 0:
            h = _pool_ln(h, pool_w[j], pool_b[j], pool_scale[j], ln_mix_g[i], ln_mix_b[i], alpha)
        else:
            h = _mla_ln(h, mla_w_a[j], mla_q_norm_g[j], mla_kv_norm_g[j], mla_w_uq[j], mla_w_uk[j],
                        mla_w_uv[j], mla_w_o[j], ln_mix_g[i], ln_mix_b[i], alpha)
        h = _moe_ln(h.reshape(B * S, D), router_w[i], router_b[i], exp_w_gate, exp_w_up, exp_w_down, i,
                    sh_w_gate[i], sh_w_up[i], sh_w_down[i], ln_ffn_g[i], ln_ffn_b[i],
                    alpha).reshape(B, S, D)
    return h
```

```python
import functools
import math

import jax
import jax.numpy as jnp
from jax import lax
from jax.experimental import pallas as pl
from jax.experimental.pallas import tpu as pltpu

F32 = jnp.float32
BF16 = jnp.bfloat16

LN_EPS = 1e-5
RMS_EPS = 1e-6
POOL_WINDOWS = (2, 4, 8, 16)
POOL_HALO = 8
ROPE_THETA = 10000.0
TOP_K = 8
N_GROUPS = 8
TOPK_GROUPS = 4
ROUTED_SCALE = 2.5
LANES = 128
VMEM_LIMIT = 48 * 1024 * 1024

_NT = (((1,), (1,)), ((), ()))


def _pick(n, pref):
    t = min(pref, n)
    while n % t:
        t //= 2
    assert t >= 8 and n % t == 0, (n, pref)
    return t


def _layer_norm(z, g, b):
    mu = jnp.mean(z, axis=-1, keepdims=True)
    zc = z - mu
    var = jnp.mean(zc * zc, axis=-1, keepdims=True)
    return zc * lax.rsqrt(var + LN_EPS) * g + b


def _sigmoid(x):
    return 1.0 / (1.0 + jnp.exp(-x))


def _params(sem):
    return pltpu.CompilerParams(dimension_semantics=sem, vmem_limit_bytes=VMEM_LIMIT)


def _pool_ln_kernel(xc_ref, xp_ref, xn_ref, w_ref, b_ref, sc_ref, g_ref, beta_ref, o_ref, ext_ref,
                    *, tile, seq, alpha):
    i = pl.program_id(1)
    n_tiles = pl.num_programs(1)
    xc = xc_ref[0]
    d_model = xc.shape[-1]
    ch = d_model // len(POOL_WINDOWS)
    ext_ref[0:POOL_HALO, :] = jnp.where(i > 0, xp_ref[0], 0.0)
    ext_ref[POOL_HALO:POOL_HALO + tile, :] = xc
    ext_ref[POOL_HALO + tile:2 * POOL_HALO + tile, :] = jnp.where(i < n_tiles - 1, xn_ref[0], 0.0)
    pos = i * tile + lax.broadcasted_iota(jnp.int32, (tile, 1), 0)
    outs = []
    for g, win in enumerate(POOL_WINDOWS):
        half = win // 2
        cols = slice(g * ch, (g + 1) * ch)
        acc = ext_ref[POOL_HALO - half:POOL_HALO - half + tile, cols]
        for j in range(-half + 1, half):
            acc = acc + ext_ref[POOL_HALO + j:POOL_HALO + j + tile, cols]
        cnt = (jnp.minimum(pos + half, seq) - jnp.maximum(pos - half, 0)).astype(F32)
        pooled = acc / cnt - xc[:, cols]
        outs.append(jnp.dot(pooled.astype(BF16), w_ref[g], preferred_element_type=F32))
    y = jnp.concatenate(outs, axis=-1)
    y = (y + b_ref[...]) * sc_ref[...]
    o_ref[0] = _layer_norm(alpha * xc + y, g_ref[...], beta_ref[...])


def _pool_ln(x, w, b, sc, g, beta, alpha):
    B, S, D = x.shape
    tile = _pick(S, 512)
    nh = tile // POOL_HALO
    last = S // POOL_HALO - 1
    row = lambda v: v.reshape(1, D)
    vec = pl.BlockSpec((1, D), lambda bi, i: (0, 0))
    return pl.pallas_call(
        functools.partial(_pool_ln_kernel, tile=tile, seq=S, alpha=alpha),
        grid=(B, S // tile),
        in_specs=[
            pl.BlockSpec((1, tile, D), lambda bi, i: (bi, i, 0)),
            pl.BlockSpec((1, POOL_HALO, D), lambda bi, i: (bi, jnp.maximum(i * nh - 1, 0), 0)),
            pl.BlockSpec((1, POOL_HALO, D), lambda bi, i: (bi, jnp.minimum((i + 1) * nh, last), 0)),
            pl.BlockSpec(w.shape, lambda bi, i: (0, 0, 0)),
            vec, vec, vec, vec,
        ],
        out_specs=pl.BlockSpec((1, tile, D), lambda bi, i: (bi, i, 0)),
        out_shape=jax.ShapeDtypeStruct((B, S, D), F32),
        scratch_shapes=[pltpu.VMEM((tile + 2 * POOL_HALO, D), F32)],
        compiler_params=_params(("parallel", "parallel")),
        name="pool_ln",
    )(x, x, x, w.astype(BF16), row(b), row(sc), row(g), row(beta))


def _router_kernel(h_ref, whi_ref, wlo_ref, rb_ref, tri_ref, ltri_ref,
                   gate_ref, lpos_ref, cnt_ref, tcnt_ref, toff_ref, tbase_ref, ttot_ref):
    i = pl.program_id(0)

    @pl.when(i == 0)
    def _():
        cnt_ref[...] = jnp.zeros_like(cnt_ref)

    x = h_ref[...]
    xhi = x.astype(BF16)
    xlo = (x - xhi.astype(F32)).astype(BF16)
    whi = whi_ref[...]
    logits = (lax.dot_general(whi, xhi, _NT, preferred_element_type=F32)
              + lax.dot_general(whi, xlo, _NT, preferred_element_type=F32)
              + lax.dot_general(wlo_ref[...], xhi, _NT, preferred_element_type=F32))
    n_exp, tile = logits.shape
    per = n_exp // N_GROUPS
    scores = _sigmoid(logits)
    biased = scores + rb_ref[...]
    neg = -jnp.inf

    sub = lax.broadcasted_iota(jnp.int32, (per, tile), 0).astype(F32)
    gscore = []
    for g in range(N_GROUPS):
        bg = biased[g * per:(g + 1) * per, :]
        m1 = jnp.max(bg, axis=0, keepdims=True)
        i1 = jnp.min(jnp.where(bg == m1, sub, float(per)), axis=0, keepdims=True)
        m2 = jnp.max(jnp.where(sub == i1, neg, bg), axis=0, keepdims=True)
        gscore.append(m1 + m2)
    masked = []
    for g in range(N_GROUPS):
        beat = jnp.zeros_like(gscore[g])
        for o in range(N_GROUPS):
            if o == g:
                continue
            wins = (gscore[o] >= gscore[g]) if o < g else (gscore[o] > gscore[g])
            beat = beat + jnp.where(wins, 1.0, 0.0)
        keep = jnp.broadcast_to(beat < float(TOPK_GROUPS), (per, tile))
        masked.append(jnp.where(keep, biased[g * per:(g + 1) * per, :], neg))
    masked = jnp.concatenate(masked, axis=0)

    row = lax.broadcasted_iota(jnp.int32, (n_exp, tile), 0).astype(F32)
    chosen = jnp.zeros((n_exp, tile), F32)
    idxs, gates = [], []
    for _ in range(TOP_K):
        m = jnp.max(masked, axis=0, keepdims=True)
        idx = jnp.min(jnp.where(masked == m, row, float(n_exp)), axis=0, keepdims=True)
        hit = row == idx
        gates.append(jnp.sum(jnp.where(hit, scores, 0.0), axis=0, keepdims=True))
        masked = jnp.where(hit, neg, masked)
        chosen = jnp.where(hit, 1.0, chosen)
        idxs.append(idx)
    gsum = gates[0]
    for k in range(1, TOP_K):
        gsum = gsum + gates[k]

    earlier = jnp.dot(chosen.astype(BF16), tri_ref[...], preferred_element_type=F32)
    count = jnp.sum(chosen, axis=1, keepdims=True)
    run = count + (count - 2.0 * jnp.floor(0.5 * count))
    lower = jnp.dot(ltri_ref[...], jnp.broadcast_to(run, (n_exp, LANES)).astype(BF16),
                    preferred_element_type=F32)[:, :1]
    order = earlier + lower
    for k in range(TOP_K):
        gate_ref[k:k + 1, :] = gates[k] / gsum * ROUTED_SCALE
        lpos_k = jnp.sum(jnp.where(row == idxs[k], order, 0.0), axis=0, keepdims=True)
        lpos_ref[k:k + 1, :] = lpos_k.astype(jnp.int32)
    tcnt_ref[...] = run.astype(jnp.int32)
    toff_ref[...] = lower.astype(jnp.int32)
    tbase_ref[...] = cnt_ref[...].astype(jnp.int32)
    ttot_ref[...] = jnp.sum(run, axis=0, keepdims=True).astype(jnp.int32)
    cnt_ref[...] = cnt_ref[...] + run


def _router(h2, rw, rb):
    N, D = h2.shape
    E = rw.shape[1]
    tile = _pick(N, MOE_TILE)
    assert tile <= 256
    n_tiles = N // tile
    wt = rw.T
    whi = wt.astype(BF16)
    wlo = (wt - whi.astype(F32)).astype(BF16)
    ar = jnp.arange(tile)
    tri = (ar[:, None] < ar[None, :]).astype(BF16)
    ae = jnp.arange(E)
    ltri = (ae[None, :] < ae[:, None]).astype(BF16)
    const = lambda shape: pl.BlockSpec(shape, lambda i: (0, 0))
    kn = pl.BlockSpec((TOP_K, tile), lambda i: (0, i))
    table = pl.BlockSpec((None, E, 1), lambda i: (i, 0, 0))
    table_shape = jax.ShapeDtypeStruct((n_tiles, E, 1), jnp.int32)
    return pl.pallas_call(
        _router_kernel,
        grid=(n_tiles,),
        in_specs=[pl.BlockSpec((tile, D), lambda i: (i, 0)), const((E, D)), const((E, D)),
                  const((E, 1)), const((tile, tile)), const((E, E))],
        out_specs=[kn, kn, const((E, 1)), table, table, table,
                   pl.BlockSpec((None, 1, 1), lambda i: (i, 0, 0))],
        out_shape=[jax.ShapeDtypeStruct((TOP_K, N), F32), jax.ShapeDtypeStruct((TOP_K, N), jnp.int32),
                   jax.ShapeDtypeStruct((E, 1), F32), table_shape, table_shape, table_shape,
                   jax.ShapeDtypeStruct((n_tiles, 1, 1), jnp.int32)],
        compiler_params=_params(("arbitrary",)),
        name="moe_router",
    )(h2, whi, wlo, rb.reshape(E, 1), tri, ltri)


def _pack_pairs(x, rounded=False):
    half = x.shape[-1] // 2
    bits = lambda v: lax.bitcast_convert_type(v if rounded else v.astype(BF16).astype(F32), jnp.uint32)
    return (bits(x[:, :half]) & jnp.uint32(0xFFFF0000)) | (bits(x[:, half:]) >> 16)


def _unpack_pairs(w):
    hi = lax.bitcast_convert_type(w & jnp.uint32(0xFFFF0000), F32)
    lo = lax.bitcast_convert_type(w << 16, F32)
    return jnp.concatenate([hi.astype(BF16), lo.astype(BF16)], axis=-1)


def _to_slab(slab_ref, x, rounded=False):
    w = _pack_pairs(x, rounded)
    lines = w.shape[1] // LANES
    for a in range(lines):
        slab_ref[pl.ds(a, w.shape[0], stride=lines), :] = w[:, a * LANES:(a + 1) * LANES]


def _from_slab(slab_ref, rows, lines, valid):
    w = jnp.concatenate([slab_ref[pl.ds(a, rows, stride=lines), :] for a in range(lines)], axis=-1)
    r = lax.broadcasted_iota(jnp.int32, (rows, 1), 0)
    return _unpack_pairs(jnp.where(r < valid, w, jnp.uint32(0)))


def _start_runs(toff_ref, gst_ref, tcnt_ref, local, remote, sem, lines, to_remote):
    def body(e, c):
        cnt, off, gst = tcnt_ref[0, 0, e], toff_ref[0, 0, e], gst_ref[0, 0, e]

        @pl.when(cnt > 0)
        def _():
            loc = local.at[pl.ds(pl.multiple_of(off * lines, 8), cnt * lines), :]
            rem = remote.at[pl.ds(pl.multiple_of(gst * lines, 8), cnt * lines), :]
            (pltpu.make_async_copy(loc, rem, sem) if to_remote else pltpu.make_async_copy(rem, loc, sem)).start()
        return c
    lax.fori_loop(0, tcnt_ref.shape[2], body, 0, unroll=RUN_UNROLL)


def _wait_all_runs(slab, remote, sem, total_lines):
    n = pl.multiple_of(total_lines, 8)
    pltpu.make_async_copy(slab.at[pl.ds(0, n), :], remote.at[pl.ds(0, n), :], sem).wait()


def _sorted_rows(tile, n_exp):
    return TOP_K * tile + n_exp


def _dispatch_kernel(toff_ref, gst_ref, tcnt_ref, ttot_ref, lpos_ref, h_ref, xs_ref, slab, sem, *, lines):
    tile = h_ref.shape[0]
    rows = slab.shape[0] // lines
    srow = lax.broadcasted_iota(jnp.int32, (rows, tile), 0)
    perm = jnp.zeros((rows, tile), F32)
    for k in range(TOP_K):
        perm = jnp.where(srow == lpos_ref[k:k + 1, :], 1.0, perm)
    _to_slab(slab, jnp.dot(perm.astype(BF16), h_ref[...].astype(BF16), preferred_element_type=F32),
             rounded=True)
    _start_runs(toff_ref, gst_ref, tcnt_ref, slab, xs_ref, sem, lines, to_remote=True)
    _wait_all_runs(slab, xs_ref, sem, ttot_ref[0, 0, 0] * lines)


def _run_specs(n_exp):
    smem = lambda n: pl.BlockSpec((1, 1, n), lambda i: (i, 0, 0), memory_space=pltpu.SMEM)
    return [smem(n_exp), smem(n_exp), smem(n_exp), smem(1)]


def _slab_lines(d_model):
    lines = d_model // (2 * LANES)
    assert lines * 2 * LANES == d_model and (2 * lines) % 8 == 0
    return lines


def _dispatch(h2, lpos, runs, n_slots):
    N, D = h2.shape
    E = runs[0].shape[2]
    tile = _pick(N, MOE_TILE)
    lines = _slab_lines(D)
    return pl.pallas_call(
        functools.partial(_dispatch_kernel, lines=lines),
        grid=(N // tile,),
        in_specs=_run_specs(E) + [pl.BlockSpec((TOP_K, tile), lambda i: (0, i)),
                                  pl.BlockSpec((tile, D), lambda i: (i, 0))],
        out_specs=pl.BlockSpec(memory_space=pl.ANY),
        out_shape=jax.ShapeDtypeStruct((n_slots * lines, LANES), jnp.uint32),
        scratch_shapes=[pltpu.VMEM((_sorted_rows(tile, E) * lines, LANES), jnp.uint32),
                        pltpu.SemaphoreType.DMA(())],
        compiler_params=_params(("arbitrary",)),
        name="moe_dispatch",
    )(*runs, lpos, h2)


def _expert_kernel(bidx_ref, bexp_ref, bval_ref, bnew_ref, bslot_ref, bnext_ref, xs_ref,
                   wg_hbm, wu_hbm, wd_hbm, o_ref, wg_f, wu_f, wd_f, wg_s, wu_s, wd_s, sem, *, layer):
    i = pl.program_id(0)
    valid = bval_ref[i]

    def weight_copies(e, slot):
        return [pltpu.make_async_copy(w.at[layer, e], f.at[slot], sem.at[slot])
                for w, f in ((wg_hbm, wg_f), (wu_hbm, wu_f), (wd_hbm, wd_f))]

    @pl.when(i == 0)
    def _():
        for cp in weight_copies(bexp_ref[0], 0):
            cp.start()

    @pl.when(bnew_ref[i] == 1)
    def _():
        slot = bslot_ref[i]
        for cp in weight_copies(bexp_ref[i], slot):
            cp.wait()

        @pl.when(bnext_ref[i] >= 0)
        def _():
            for cp in weight_copies(bnext_ref[i], 1 - slot):
                cp.start()

        wg_s[...] = wg_f[slot].astype(BF16)
        wu_s[...] = wu_f[slot].astype(BF16)
        wd_s[...] = wd_f[slot].astype(BF16)

    lines = _slab_lines(wg_s.shape[0])
    block = xs_ref.shape[0] // lines

    def ffn(rows):
        x = _from_slab(xs_ref, rows, lines, valid)
        hg = jnp.dot(x, wg_s[...], preferred_element_type=F32)
        hu = jnp.dot(x, wu_s[...], preferred_element_type=F32)
        hb = hg * _sigmoid(hg) * hu
        y = jnp.dot(hb.astype(BF16), wd_s[...], preferred_element_type=F32)
        return jnp.concatenate([y, jnp.zeros((block - rows, y.shape[1]), F32)], axis=0) if rows < block else y

    lower = 0
    for rows in _expert_tiers(block):
        @pl.when(jnp.logical_and(valid > lower, valid <= rows))
        def _(rows=rows):
            _to_slab(o_ref, ffn(rows))
        lower = rows


def _experts(xs, blocks, wg, wu, wd, layer, block):
    D, Fe = wg.shape[-2:]
    lines = _slab_lines(D)
    P = xs.shape[0] // lines
    rows = pl.BlockSpec((block * lines, LANES), lambda i, bi, *_: (bi[i], 0))
    hbm = pl.BlockSpec(memory_space=pl.ANY)
    return pl.pallas_call(
        functools.partial(_expert_kernel, layer=layer),
        grid_spec=pltpu.PrefetchScalarGridSpec(
            num_scalar_prefetch=len(blocks),
            grid=(P // block,),
            in_specs=[rows, hbm, hbm, hbm],
            out_specs=rows,
            scratch_shapes=[pltpu.VMEM((2, D, Fe), F32), pltpu.VMEM((2, D, Fe), F32),
                            pltpu.VMEM((2, Fe, D), F32),
                            pltpu.VMEM((D, Fe), BF16), pltpu.VMEM((D, Fe), BF16),
                            pltpu.VMEM((Fe, D), BF16), pltpu.SemaphoreType.DMA((2,))],
        ),
        out_shape=jax.ShapeDtypeStruct((P * lines, LANES), jnp.uint32),
        compiler_params=_params(("arbitrary",)),
        name="moe_experts",
    )(*blocks, xs, wg, wu, wd)


def _combine_kernel(toff_ref, gst_ref, tcnt_ref, ttot_ref, lpos_ref, gate_ref, h_ref, ys_ref, sg_ref, su_ref,
                    sd_ref, g_ref, beta_ref, o_ref, slab, sem, *, lines, alpha):
    _start_runs(toff_ref, gst_ref, tcnt_ref, slab, ys_ref, sem, lines, to_remote=False)
    h = h_ref[...]
    tile = h.shape[0]
    rows = slab.shape[0] // lines
    hb = h.astype(BF16)
    sgate = jnp.dot(hb, sg_ref[...], preferred_element_type=F32)
    sup = jnp.dot(hb, su_ref[...], preferred_element_type=F32)
    mid = sgate * _sigmoid(sgate) * sup
    acc = jnp.dot(mid.astype(BF16), sd_ref[...], preferred_element_type=F32)
    scol = lax.broadcasted_iota(jnp.int32, (tile, rows), 1)
    weights = jnp.zeros((tile, rows), F32)
    for k in range(TOP_K):
        weights = jnp.where(scol == lpos_ref[:, k:k + 1], gate_ref[:, k:k + 1], weights)
    total = ttot_ref[0, 0, 0]
    _wait_all_runs(slab, ys_ref, sem, total * lines)
    ys = _from_slab(slab, rows, lines, total)
    acc = acc + jnp.dot(weights.astype(BF16), ys, preferred_element_type=F32)
    o_ref[...] = _layer_norm(alpha * h + acc, g_ref[...], beta_ref[...])


def _combine(h2, lpos_nk, gate_nk, runs, ys, sg, su, sd, g, beta, alpha):
    N, D = h2.shape
    Fs = sg.shape[1]
    E = runs[0].shape[2]
    tile = _pick(N, MOE_TILE)
    lines = _slab_lines(D)
    const = lambda shape: pl.BlockSpec(shape, lambda i: (0, 0))
    nk = pl.BlockSpec((tile, TOP_K), lambda i: (i, 0))
    return pl.pallas_call(
        functools.partial(_combine_kernel, lines=lines, alpha=alpha),
        grid=(N // tile,),
        in_specs=_run_specs(E) + [
            nk, nk, pl.BlockSpec((tile, D), lambda i: (i, 0)), pl.BlockSpec(memory_space=pl.ANY),
            const((D, Fs)), const((D, Fs)), const((Fs, D)), const((1, D)), const((1, D))],
        out_specs=pl.BlockSpec((tile, D), lambda i: (i, 0)),
        out_shape=jax.ShapeDtypeStruct((N, D), F32),
        scratch_shapes=[pltpu.VMEM((_sorted_rows(tile, E) * lines, LANES), jnp.uint32),
                        pltpu.SemaphoreType.DMA(())],
        compiler_params=_params(("arbitrary",)),
        name="moe_combine",
    )(*runs, lpos_nk, gate_nk, h2, ys, sg.astype(BF16), su.astype(BF16), sd.astype(BF16),
      g.reshape(1, D), beta.reshape(1, D))


FLASH_TQ = 512
FLASH_TK = 512
MOE_TILE = 256
RUN_UNROLL = 8
EXPERT_BLOCK = 512


def _expert_tiers(block):
    return (block // 4, block // 2, block)


def _block_schedule(counts, max_rows, blk):
    E = counts.shape[0]
    padded = (counts + blk - 1) // blk * blk
    pend = jnp.cumsum(padded)
    pstart = pend - padded
    n_blocks = -(-(max_rows + E * (blk - 1)) // blk)
    n_used = pend[-1] // blk
    bidx = jnp.minimum(jnp.arange(n_blocks, dtype=jnp.int32), n_used - 1)
    first = (bidx * blk)[:, None]
    owns = (pstart[None, :] <= first) & (first < pend[None, :])
    pick = lambda v: jnp.sum(jnp.where(owns, v[None, :], 0), axis=1).astype(jnp.int32)
    bexp = pick(jnp.arange(E, dtype=jnp.int32))
    bval = jnp.clip(pick(counts) - (bidx * blk - pick(pstart)), 0, blk)
    used = jnp.arange(n_blocks) < n_used
    bval = jnp.where(used, bval, 0).astype(jnp.int32)
    bnew = (used & (jnp.concatenate([jnp.full((1,), -1, jnp.int32), bexp[:-1]]) != bexp)).astype(jnp.int32)
    bslot = ((jnp.cumsum(bnew) - 1) % 2).astype(jnp.int32)
    later = jnp.concatenate([jnp.where(bnew == 1, bexp, E)[1:], jnp.full((1,), E, jnp.int32)])
    bnext = lax.cummin(later, reverse=True)
    bnext = jnp.where(bnext == E, -1, bnext).astype(jnp.int32)
    return pstart, n_blocks, (bidx, bexp, bval, bnew, bslot, bnext)


def _moe_ln(h2, rw, rb, wg, wu, wd, layer, sg, su, sd, g, beta, alpha):
    N, D = h2.shape
    E = rw.shape[1]
    gate, lpos, cnt, tcnt, toff, tbase, ttot = _router(h2, rw, rb)
    blk = EXPERT_BLOCK
    n_tiles = ttot.shape[0]
    pstart, n_blocks, blocks = _block_schedule(cnt[:, 0].astype(jnp.int32), N * TOP_K + n_tiles * E, blk)
    per_tile = lambda a: a.reshape(n_tiles, 1, E)
    runs = (per_tile(toff), per_tile(tbase) + pstart[None, None, :], per_tile(tcnt), ttot)
    xs = _dispatch(h2, lpos, runs, n_blocks * blk)
    ys = _experts(xs, blocks, wg, wu, wd, layer, blk)
    return _combine(h2, lpos.T, gate.T, runs, ys, sg, su, sd, g, beta, alpha)


def _mla_proj_kernel(h_ref, wa_ref, qg_ref, kvg_ref, wqt_ref, wk_ref, wvt_ref, cos_ref, sin_ref,
                     cost_ref, sint_ref, qt_ref, k_ref, vt_ref, *, q_lora, kv_lora, heads, q_scale):
    x = h_ref[0].astype(BF16)
    a = jnp.dot(x, wa_ref[...], preferred_element_type=F32)

    def rms(v, g):
        return v * lax.rsqrt(jnp.mean(v * v, axis=-1, keepdims=True) + RMS_EPS) * g

    q_lat = rms(a[:, :q_lora], qg_ref[...]).astype(BF16)
    kv_lat = rms(a[:, q_lora:q_lora + kv_lora], kvg_ref[...]).astype(BF16)
    o = q_lora + kv_lora
    k_rope = (a[:, o:o + LANES] * cos_ref[...] + a[:, o + LANES:o + 2 * LANES] * sin_ref[...]).astype(BF16)
    qa = lax.dot_general(wqt_ref[...], q_lat, _NT, preferred_element_type=F32)
    kn = jnp.dot(kv_lat, wk_ref[...], preferred_element_type=F32)
    vt_ref[0] = lax.dot_general(wvt_ref[...], kv_lat, _NT, preferred_element_type=F32).astype(BF16)
    cost = cost_ref[...]
    sint = sint_ref[...]
    hd = heads * LANES
    for h in range(heads):
        r0 = slice(h * LANES, (h + 1) * LANES)
        r1 = slice(hd + h * LANES, hd + (h + 1) * LANES)
        r2 = slice(2 * hd + h * LANES, 2 * hd + (h + 1) * LANES)
        qt_ref[0, 2 * h * LANES:(2 * h + 1) * LANES, :] = (qa[r0] * q_scale).astype(BF16)
        qt_ref[0, (2 * h + 1) * LANES:(2 * h + 2) * LANES, :] = (
            (qa[r1] * cost + qa[r2] * sint) * q_scale).astype(BF16)
        k_ref[0, :, 2 * h * LANES:(2 * h + 1) * LANES] = kn[:, r0].astype(BF16)
        k_ref[0, :, (2 * h + 1) * LANES:(2 * h + 2) * LANES] = k_rope


def _rot_half(w):
    r = w.shape[-1] // 2
    return jnp.concatenate([-w[..., r:], w[..., :r]], axis=-1)


def _pad_lanes(w):
    return jnp.pad(w, [(0, 0)] * (w.ndim - 1) + [(0, LANES - w.shape[-1])])


FLASH_HEADROOM = 64.0


def _flash_kernel(qt_ref, k_ref, vt_ref, o_ref, p_ref, s_ref, mx_ref, m_ref, l_ref, acc_ref, *, tk):
    qt = qt_ref[0]
    nk = k_ref.shape[1] // tk
    assert nk % 2 == 0 and nk >= 2

    def scores(c):
        off = pl.multiple_of(c * tk, tk)
        return jnp.dot(k_ref[0, pl.ds(off, tk), :], qt, preferred_element_type=F32)

    s0 = scores(0)
    m0 = jnp.max(s0, axis=0, keepdims=True)
    p0 = jnp.exp2(s0 - m0)
    p_ref[0] = p0.astype(BF16)
    acc_ref[...] = jnp.zeros_like(acc_ref)

    def probs(c, slot, carry):
        l, top = carry
        s = scores(c)
        p = jnp.exp2(s - m0)
        p_ref[slot] = p.astype(BF16)
        return l + jnp.sum(p, axis=0, keepdims=True), jnp.maximum(top, jnp.max(s, axis=0, keepdims=True))

    def accumulate(c, slot):
        off = pl.multiple_of(c * tk, tk)
        acc_ref[...] += jnp.dot(vt_ref[0, :, pl.ds(off, tk)], p_ref[slot], preferred_element_type=F32)

    carry = (jnp.sum(p0, axis=0, keepdims=True), m0)
    for c in range(nk - 1):
        carry = probs(c + 1, (c + 1) % 2, carry)
        accumulate(c, c % 2)
    accumulate(nk - 1, (nk - 1) % 2)
    l, top = carry
    o_ref[0] = (acc_ref[...] / l).T.astype(o_ref.dtype)

    @pl.when(jnp.max(top - m0) > FLASH_HEADROOM)
    def _():
        _flash_online(qt, k_ref, vt_ref, o_ref, s_ref, mx_ref, m_ref, l_ref, acc_ref, tk)


def _flash_online(qt, k_ref, vt_ref, o_ref, s_ref, mx_ref, m_ref, l_ref, acc_ref, tk):
    nk = k_ref.shape[1] // tk
    assert nk % 2 == 0 and nk >= 2
    m_ref[...] = jnp.full_like(m_ref, -jnp.inf)
    l_ref[...] = jnp.zeros_like(l_ref)
    acc_ref[...] = jnp.zeros_like(acc_ref)

    def scores(c, slot):
        off = pl.multiple_of(c * tk, tk)
        s = jnp.dot(k_ref[0, pl.ds(off, tk), :], qt, preferred_element_type=F32)
        s_ref[slot] = s
        mx_ref[slot] = jnp.max(s, axis=0, keepdims=True)

    def update(c, slot):
        off = pl.multiple_of(c * tk, tk)
        vt = vt_ref[0, :, pl.ds(off, tk)]
        m_prev = m_ref[...]
        m_new = jnp.maximum(m_prev, mx_ref[slot])
        p = jnp.exp2(s_ref[slot] - m_new)
        a = jnp.exp2(m_prev - m_new)
        l_ref[...] = a * l_ref[...] + jnp.sum(p, axis=0, keepdims=True)
        acc_ref[...] = a * acc_ref[...] + jnp.dot(vt, p.astype(BF16), preferred_element_type=F32)
        m_ref[...] = m_new

    scores(0, 0)

    def body(i, carry):
        c = 2 * i
        scores(c + 1, 1)
        update(c, 0)
        scores(c + 2, 0)
        update(c + 1, 1)
        return carry

    lax.fori_loop(0, nk // 2 - 1, body, 0)
    scores(nk - 1, 1)
    update(nk - 2, 0)
    update(nk - 1, 1)
    o_ref[0] = (acc_ref[...] / l_ref[...]).T.astype(o_ref.dtype)


def _oproj_ln_kernel(o_ref, h_ref, w_ref, g_ref, beta_ref, out_ref, *, alpha):
    m = jnp.dot(o_ref[0], w_ref[...], preferred_element_type=F32)
    out_ref[0] = _layer_norm(alpha * h_ref[0] + m, g_ref[...], beta_ref[...])


def _mla_ln(h, w_a, qg, kvg, w_uq, w_uk, w_uv, w_o, g, beta, alpha):
    B, S, D = h.shape
    q_lora, heads, qk = w_uq.shape
    kv_lora, _, nope = w_uk.shape
    vd = w_uv.shape[-1]
    rope = qk - nope
    assert nope == LANES and vd == LANES and rope <= LANES and rope % 2 == 0
    kr = w_a[:, q_lora + kv_lora:]
    wa = jnp.concatenate([w_a[:, :q_lora + kv_lora], _pad_lanes(kr), _pad_lanes(_rot_half(kr))],
                         axis=1).astype(BF16)
    wq_r = w_uq[:, :, nope:]
    wq = jnp.concatenate([w_uq[:, :, :nope].reshape(q_lora, heads * nope),
                          _pad_lanes(wq_r).reshape(q_lora, heads * LANES),
                          _pad_lanes(_rot_half(wq_r)).reshape(q_lora, heads * LANES)], axis=1).astype(BF16)
    wk = w_uk.reshape(kv_lora, heads * nope).astype(BF16)
    wvt = w_uv.reshape(kv_lora, heads * vd).T.astype(BF16)
    wqt = wq.T
    pos = jnp.arange(S, dtype=F32)
    inv = 1.0 / (ROPE_THETA ** (jnp.arange(0, rope, 2, dtype=F32) / rope))
    ang = pos[:, None] * inv[None, :]
    cos = _pad_lanes(jnp.concatenate([jnp.cos(ang), jnp.cos(ang)], axis=1))
    sin = _pad_lanes(jnp.concatenate([jnp.sin(ang), jnp.sin(ang)], axis=1))
    q_scale = float(qk) ** -0.5 * math.log2(math.e)

    tp = _pick(S, 256)
    const = lambda a: pl.BlockSpec(a.shape, lambda b, i: (0,) * a.ndim)
    qg2, kvg2 = qg.reshape(1, q_lora), kvg.reshape(1, kv_lora)
    tok_major = pl.BlockSpec((tp, LANES), lambda b, i: (i, 0))
    feat_major = pl.BlockSpec((LANES, tp), lambda b, i: (0, i))
    qt, k, vt = pl.pallas_call(
        functools.partial(_mla_proj_kernel, q_lora=q_lora, kv_lora=kv_lora, heads=heads, q_scale=q_scale),
        grid=(B, S // tp),
        in_specs=[pl.BlockSpec((1, tp, D), lambda b, i: (b, i, 0)), const(wa), const(qg2), const(kvg2),
                  const(wqt), const(wk), const(wvt), tok_major, tok_major, feat_major, feat_major],
        out_specs=[pl.BlockSpec((1, 2 * heads * LANES, tp), lambda b, i: (b, 0, i)),
                   pl.BlockSpec((1, tp, 2 * heads * LANES), lambda b, i: (b, i, 0)),
                   pl.BlockSpec((1, heads * LANES, tp), lambda b, i: (b, 0, i))],
        out_shape=[jax.ShapeDtypeStruct((B, 2 * heads * LANES, S), BF16),
                   jax.ShapeDtypeStruct((B, S, 2 * heads * LANES), BF16),
                   jax.ShapeDtypeStruct((B, heads * LANES, S), BF16)],
        compiler_params=_params(("parallel", "parallel")),
        name="mla_proj",
    )(h, wa, qg2, kvg2, wqt, wk, wvt, cos, sin, cos.T, sin.T)

    tq = _pick(S, FLASH_TQ)
    tk = _pick(S, FLASH_TK)
    o = pl.pallas_call(
        functools.partial(_flash_kernel, tk=tk),
        grid=(B, heads, S // tq),
        in_specs=[pl.BlockSpec((1, 2 * LANES, tq), lambda b, hh, i: (b, hh, i)),
                  pl.BlockSpec((1, S, 2 * LANES), lambda b, hh, i: (b, 0, hh)),
                  pl.BlockSpec((1, LANES, S), lambda b, hh, i: (b, hh, 0))],
        out_specs=pl.BlockSpec((1, tq, LANES), lambda b, hh, i: (b, i, hh)),
        out_shape=jax.ShapeDtypeStruct((B, S, heads * LANES), BF16),
        scratch_shapes=[pltpu.VMEM((2, tk, tq), BF16), pltpu.VMEM((2, tk, tq), F32), pltpu.VMEM((2, 1, tq), F32),
                        pltpu.VMEM((1, tq), F32), pltpu.VMEM((1, tq), F32), pltpu.VMEM((LANES, tq), F32)],
        compiler_params=_params(("parallel", "parallel", "arbitrary")),
        name="mla_flash",
    )(qt, k, vt)

    to = _pick(S, 512)
    vec = pl.BlockSpec((1, D), lambda b, i: (0, 0))
    return pl.pallas_call(
        functools.partial(_oproj_ln_kernel, alpha=alpha),
        grid=(B, S // to),
        in_specs=[pl.BlockSpec((1, to, heads * vd), lambda b, i: (b, i, 0)),
                  pl.BlockSpec((1, to, D), lambda b, i: (b, i, 0)),
                  pl.BlockSpec((heads * vd, D), lambda b, i: (0, 0)), vec, vec],
        out_specs=pl.BlockSpec((1, to, D), lambda b, i: (b, i, 0)),
        out_shape=jax.ShapeDtypeStruct((B, S, D), F32),
        compiler_params=_params(("parallel", "parallel")),
        name="mla_oproj_ln",
    )(o, h, w_o.astype(BF16), g.reshape(1, D), beta.reshape(1, D))


def kernel(x, pool_w, pool_b, pool_scale, mla_w_a, mla_q_norm_g, mla_kv_norm_g, mla_w_uq, mla_w_uk,
           mla_w_uv, mla_w_o, ln_mix_g, ln_mix_b, router_w, router_b, exp_w_gate, exp_w_up, exp_w_down,
           sh_w_gate, sh_w_up, sh_w_down, ln_ffn_g, ln_ffn_b):
    B, S, D = x.shape
    depth = ln_mix_g.shape[0]
    alpha = (2 * depth) ** 0.25
    h = x
    for i in range(depth):
        j = i // 2
        if i % 2 == 0:
            h = _pool_ln(h, pool_w[j], pool_b[j], pool_scale[j], ln_mix_g[i], ln_mix_b[i], alpha)
        else:
            h = _mla_ln(h, mla_w_a[j], mla_q_norm_g[j], mla_kv_norm_g[j], mla_w_uq[j], mla_w_uk[j],
                        mla_w_uv[j], mla_w_o[j], ln_mix_g[i], ln_mix_b[i], alpha)
        h = _moe_ln(h.reshape(B * S, D), router_w[i], router_b[i], exp_w_gate, exp_w_up, exp_w_down, i,
                    sh_w_gate[i], sh_w_up[i], sh_w_down[i], ln_ffn_g[i], ln_ffn_b[i],
                    alpha).reshape(B, S, D)
    return h
```

```python
import functools
import math

import jax
import jax.numpy as jnp
from jax import lax
from jax.experimental import pallas as pl
from jax.experimental.pallas import tpu as pltpu

F32 = jnp.float32
BF16 = jnp.bfloat16

LN_EPS = 1e-5
RMS_EPS = 1e-6
POOL_WINDOWS = (2, 4, 8, 16)
POOL_HALO = 8
ROPE_THETA = 10000.0
TOP_K = 8
N_GROUPS = 8
TOPK_GROUPS = 4
ROUTED_SCALE = 2.5
LANES = 128
VMEM_LIMIT = 48 * 1024 * 1024

_NT = (((1,), (1,)), ((), ()))


def _pick(n, pref):
    t = min(pref, n)
    while n % t:
        t //= 2
    assert t >= 8 and n % t == 0, (n, pref)
    return t


def _layer_norm(z, g, b):
    mu = jnp.mean(z, axis=-1, keepdims=True)
    zc = z - mu
    var = jnp.mean(zc * zc, axis=-1, keepdims=True)
    return zc * lax.rsqrt(var + LN_EPS) * g + b


def _sigmoid(x):
    return 1.0 / (1.0 + jnp.exp(-x))


def _params(sem):
    return pltpu.CompilerParams(dimension_semantics=sem, vmem_limit_bytes=VMEM_LIMIT)


def _pool_ln_kernel(xc_ref, xp_ref, xn_ref, w_ref, b_ref, sc_ref, g_ref, beta_ref, o_ref, ext_ref,
                    *, tile, seq, alpha):
    i = pl.program_id(1)
    n_tiles = pl.num_programs(1)
    xc = xc_ref[0]
    d_model = xc.shape[-1]
    ch = d_model // len(POOL_WINDOWS)
    ext_ref[0:POOL_HALO, :] = jnp.where(i > 0, xp_ref[0], 0.0)
    ext_ref[POOL_HALO:POOL_HALO + tile, :] = xc
    ext_ref[POOL_HALO + tile:2 * POOL_HALO + tile, :] = jnp.where(i < n_tiles - 1, xn_ref[0], 0.0)
    pos = i * tile + lax.broadcasted_iota(jnp.int32, (tile, 1), 0)
    outs = []
    for g, win in enumerate(POOL_WINDOWS):
        half = win // 2
        cols = slice(g * ch, (g + 1) * ch)
        acc = ext_ref[POOL_HALO - half:POOL_HALO - half + tile, cols]
        for j in range(-half + 1, half):
            acc = acc + ext_ref[POOL_HALO + j:POOL_HALO + j + tile, cols]
        cnt = (jnp.minimum(pos + half, seq) - jnp.maximum(pos - half, 0)).astype(F32)
        pooled = acc / cnt - xc[:, cols]
        outs.append(jnp.dot(pooled.astype(BF16), w_ref[g], preferred_element_type=F32))
    y = jnp.concatenate(outs, axis=-1)
    y = (y + b_ref[...]) * sc_ref[...]
    o_ref[0] = _layer_norm(alpha * xc + y, g_ref[...], beta_ref[...])


def _pool_ln(x, w, b, sc, g, beta, alpha):
    B, S, D = x.shape
    tile = _pick(S, 512)
    nh = tile // POOL_HALO
    last = S // POOL_HALO - 1
    row = lambda v: v.reshape(1, D)
    vec = pl.BlockSpec((1, D), lambda bi, i: (0, 0))
    return pl.pallas_call(
        functools.partial(_pool_ln_kernel, tile=tile, seq=S, alpha=alpha),
        grid=(B, S // tile),
        in_specs=[
            pl.BlockSpec((1, tile, D), lambda bi, i: (bi, i, 0)),
            pl.BlockSpec((1, POOL_HALO, D), lambda bi, i: (bi, jnp.maximum(i * nh - 1, 0), 0)),
            pl.BlockSpec((1, POOL_HALO, D), lambda bi, i: (bi, jnp.minimum((i + 1) * nh, last), 0)),
            pl.BlockSpec(w.shape, lambda bi, i: (0, 0, 0)),
            vec, vec, vec, vec,
        ],
        out_specs=pl.BlockSpec((1, tile, D), lambda bi, i: (bi, i, 0)),
        out_shape=jax.ShapeDtypeStruct((B, S, D), F32),
        scratch_shapes=[pltpu.VMEM((tile + 2 * POOL_HALO, D), F32)],
        compiler_params=_params(("parallel", "parallel")),
        name="pool_ln",
    )(x, x, x, w.astype(BF16), row(b), row(sc), row(g), row(beta))


def _router_kernel(h_ref, whi_ref, wlo_ref, rb_ref, tri_ref, ltri_ref,
                   gate_ref, lpos_ref, cnt_ref, tcnt_ref, toff_ref, tbase_ref, ttot_ref):
    i = pl.program_id(0)

    @pl.when(i == 0)
    def _():
        cnt_ref[...] = jnp.zeros_like(cnt_ref)

    x = h_ref[...]
    xhi = x.astype(BF16)
    xlo = (x - xhi.astype(F32)).astype(BF16)
    whi = whi_ref[...]
    logits = (lax.dot_general(whi, xhi, _NT, preferred_element_type=F32)
              + lax.dot_general(whi, xlo, _NT, preferred_element_type=F32)
              + lax.dot_general(wlo_ref[...], xhi, _NT, preferred_element_type=F32))
    n_exp, tile = logits.shape
    per = n_exp // N_GROUPS
    scores = _sigmoid(logits)
    biased = scores + rb_ref[...]
    neg = -jnp.inf

    sub = lax.broadcasted_iota(jnp.int32, (per, tile), 0).astype(F32)
    gscore = []
    for g in range(N_GROUPS):
        bg = biased[g * per:(g + 1) * per, :]
        m1 = jnp.max(bg, axis=0, keepdims=True)
        i1 = jnp.min(jnp.where(bg == m1, sub, float(per)), axis=0, keepdims=True)
        m2 = jnp.max(jnp.where(sub == i1, neg, bg), axis=0, keepdims=True)
        gscore.append(m1 + m2)
    masked = []
    for g in range(N_GROUPS):
        beat = jnp.zeros_like(gscore[g])
        for o in range(N_GROUPS):
            if o == g:
                continue
            wins = (gscore[o] >= gscore[g]) if o < g else (gscore[o] > gscore[g])
            beat = beat + jnp.where(wins, 1.0, 0.0)
        keep = jnp.broadcast_to(beat < float(TOPK_GROUPS), (per, tile))
        masked.append(jnp.where(keep, biased[g * per:(g + 1) * per, :], neg))
    masked = jnp.concatenate(masked, axis=0)

    row = lax.broadcasted_iota(jnp.int32, (n_exp, tile), 0).astype(F32)
    chosen = jnp.zeros((n_exp, tile), F32)
    idxs, gates = [], []
    for _ in range(TOP_K):
        m = jnp.max(masked, axis=0, keepdims=True)
        idx = jnp.min(jnp.where(masked == m, row, float(n_exp)), axis=0, keepdims=True)
        hit = row == idx
        gates.append(jnp.sum(jnp.where(hit, scores, 0.0), axis=0, keepdims=True))
        masked = jnp.where(hit, neg, masked)
        chosen = jnp.where(hit, 1.0, chosen)
        idxs.append(idx)
    gsum = gates[0]
    for k in range(1, TOP_K):
        gsum = gsum + gates[k]

    earlier = jnp.dot(chosen.astype(BF16), tri_ref[...], preferred_element_type=F32)
    count = jnp.sum(chosen, axis=1, keepdims=True)
    run = count + (count - 2.0 * jnp.floor(0.5 * count))
    lower = jnp.dot(ltri_ref[...], jnp.broadcast_to(run, (n_exp, LANES)).astype(BF16),
                    preferred_element_type=F32)[:, :1]
    order = earlier + lower
    for k in range(TOP_K):
        gate_ref[k:k + 1, :] = gates[k] / gsum * ROUTED_SCALE
        lpos_k = jnp.sum(jnp.where(row == idxs[k], order, 0.0), axis=0, keepdims=True)
        lpos_ref[k:k + 1, :] = lpos_k.astype(jnp.int32)
    tcnt_ref[...] = run.astype(jnp.int32)
    toff_ref[...] = lower.astype(jnp.int32)
    tbase_ref[...] = cnt_ref[...].astype(jnp.int32)
    ttot_ref[...] = jnp.sum(run, axis=0, keepdims=True).astype(jnp.int32)
    cnt_ref[...] = cnt_ref[...] + run


def _router(h2, rw, rb):
    N, D = h2.shape
    E = rw.shape[1]
    tile = _pick(N, MOE_TILE)
    assert tile <= 256
    n_tiles = N // tile
    wt = rw.T
    whi = wt.astype(BF16)
    wlo = (wt - whi.astype(F32)).astype(BF16)
    ar = jnp.arange(tile)
    tri = (ar[:, None] < ar[None, :]).astype(BF16)
    ae = jnp.arange(E)
    ltri = (ae[None, :] < ae[:, None]).astype(BF16)
    const = lambda shape: pl.BlockSpec(shape, lambda i: (0, 0))
    kn = pl.BlockSpec((TOP_K, tile), lambda i: (0, i))
    table = pl.BlockSpec((None, E, 1), lambda i: (i, 0, 0))
    table_shape = jax.ShapeDtypeStruct((n_tiles, E, 1), jnp.int32)
    return pl.pallas_call(
        _router_kernel,
        grid=(n_tiles,),
        in_specs=[pl.BlockSpec((tile, D), lambda i: (i, 0)), const((E, D)), const((E, D)),
                  const((E, 1)), const((tile, tile)), const((E, E))],
        out_specs=[kn, kn, const((E, 1)), table, table, table,
                   pl.BlockSpec((None, 1, 1), lambda i: (i, 0, 0))],
        out_shape=[jax.ShapeDtypeStruct((TOP_K, N), F32), jax.ShapeDtypeStruct((TOP_K, N), jnp.int32),
                   jax.ShapeDtypeStruct((E, 1), F32), table_shape, table_shape, table_shape,
                   jax.ShapeDtypeStruct((n_tiles, 1, 1), jnp.int32)],
        compiler_params=_params(("arbitrary",)),
        name="moe_router",
    )(h2, whi, wlo, rb.reshape(E, 1), tri, ltri)


def _pack_pairs(x, rounded=False):
    half = x.shape[-1] // 2
    bits = lambda v: lax.bitcast_convert_type(v if rounded else v.astype(BF16).astype(F32), jnp.uint32)
    return (bits(x[:, :half]) & jnp.uint32(0xFFFF0000)) | (bits(x[:, half:]) >> 16)


def _unpack_pairs(w):
    hi = lax.bitcast_convert_type(w & jnp.uint32(0xFFFF0000), F32)
    lo = lax.bitcast_convert_type(w << 16, F32)
    return jnp.concatenate([hi.astype(BF16), lo.astype(BF16)], axis=-1)


def _to_slab(slab_ref, x, rounded=False):
    w = _pack_pairs(x, rounded)
    lines = w.shape[1] // LANES
    for a in range(lines):
        slab_ref[pl.ds(a, w.shape[0], stride=lines), :] = w[:, a * LANES:(a + 1) * LANES]


def _from_slab(slab_ref, rows, lines, valid):
    w = jnp.concatenate([slab_ref[pl.ds(a, rows, stride=lines), :] for a in range(lines)], axis=-1)
    r = lax.broadcasted_iota(jnp.int32, (rows, 1), 0)
    return _unpack_pairs(jnp.where(r < valid, w, jnp.uint32(0)))


def _start_runs(toff_ref, gst_ref, tcnt_ref, local, remote, sem, lines, to_remote):
    def body(e, c):
        cnt, off, gst = tcnt_ref[0, 0, e], toff_ref[0, 0, e], gst_ref[0, 0, e]

        @pl.when(cnt > 0)
        def _():
            loc = local.at[pl.ds(pl.multiple_of(off * lines, 8), cnt * lines), :]
            rem = remote.at[pl.ds(pl.multiple_of(gst * lines, 8), cnt * lines), :]
            (pltpu.make_async_copy(loc, rem, sem) if to_remote else pltpu.make_async_copy(rem, loc, sem)).start()
        return c
    lax.fori_loop(0, tcnt_ref.shape[2], body, 0, unroll=RUN_UNROLL)


def _wait_all_runs(slab, remote, sem, total_lines):
    n = pl.multiple_of(total_lines, 8)
    pltpu.make_async_copy(slab.at[pl.ds(0, n), :], remote.at[pl.ds(0, n), :], sem).wait()


def _sorted_rows(tile, n_exp):
    return TOP_K * tile + n_exp


def _dispatch_kernel(toff_ref, gst_ref, tcnt_ref, ttot_ref, lpos_ref, h_ref, xs_ref, slab, sem, *, lines):
    tile = h_ref.shape[0]
    rows = slab.shape[0] // lines
    srow = lax.broadcasted_iota(jnp.int32, (rows, tile), 0)
    perm = jnp.zeros((rows, tile), F32)
    for k in range(TOP_K):
        perm = jnp.where(srow == lpos_ref[k:k + 1, :], 1.0, perm)
    _to_slab(slab, jnp.dot(perm.astype(BF16), h_ref[...].astype(BF16), preferred_element_type=F32),
             rounded=True)
    _start_runs(toff_ref, gst_ref, tcnt_ref, slab, xs_ref, sem, lines, to_remote=True)
    _wait_all_runs(slab, xs_ref, sem, ttot_ref[0, 0, 0] * lines)


def _run_specs(n_exp):
    smem = lambda n: pl.BlockSpec((1, 1, n), lambda i: (i, 0, 0), memory_space=pltpu.SMEM)
    return [smem(n_exp), smem(n_exp), smem(n_exp), smem(1)]


def _slab_lines(d_model):
    lines = d_model // (2 * LANES)
    assert lines * 2 * LANES == d_model and (2 * lines) % 8 == 0
    return lines


def _dispatch(h2, lpos, runs, n_slots):
    N, D = h2.shape
    E = runs[0].shape[2]
    tile = _pick(N, MOE_TILE)
    lines = _slab_lines(D)
    return pl.pallas_call(
        functools.partial(_dispatch_kernel, lines=lines),
        grid=(N // tile,),
        in_specs=_run_specs(E) + [pl.BlockSpec((TOP_K, tile), lambda i: (0, i)),
                                  pl.BlockSpec((tile, D), lambda i: (i, 0))],
        out_specs=pl.BlockSpec(memory_space=pl.ANY),
        out_shape=jax.ShapeDtypeStruct((n_slots * lines, LANES), jnp.uint32),
        scratch_shapes=[pltpu.VMEM((_sorted_rows(tile, E) * lines, LANES), jnp.uint32),
                        pltpu.SemaphoreType.DMA(())],
        compiler_params=_params(("arbitrary",)),
        name="moe_dispatch",
    )(*runs, lpos, h2)


def _expert_kernel(pstart_ref, cnt_ref, xs_hbm, wg_hbm, wu_hbm, wd_hbm, ys_hbm,
                   xbuf, ybuf, xo, yo, wg_f, wu_f, wd_f, wg_s, wu_s, wd_s, wsem, xsem, ysem, osem,
                   *, layer, lines, cap, chunk):
    e = pl.program_id(0)
    last = pl.num_programs(0) - 1
    slot = lax.rem(e, 2)

    def weight_copies(ex, sl):
        return [pltpu.make_async_copy(w.at[layer, ex], f.at[sl], wsem.at[sl])
                for w, f in ((wg_hbm, wg_f), (wu_hbm, wu_f), (wd_hbm, wd_f))]

    def head_lines(ex):
        return jnp.minimum(cnt_ref[ex], cap) * lines

    def first_line(ex, row=0):
        return pl.multiple_of((pstart_ref[ex] + row) * lines, 8)

    def x_copy(ex, sl):
        n = head_lines(ex)
        return pltpu.make_async_copy(xs_hbm.at[pl.ds(first_line(ex), n), :], xbuf.at[sl, pl.ds(0, n), :],
                                     xsem.at[sl])

    def y_copy(ex, sl):
        n = head_lines(ex)
        return pltpu.make_async_copy(ybuf.at[sl, pl.ds(0, n), :], ys_hbm.at[pl.ds(first_line(ex), n), :],
                                     ysem.at[sl])

    @pl.when(e == 0)
    def _():
        for cp in weight_copies(0, 0):
            cp.start()

        @pl.when(cnt_ref[0] > 0)
        def _():
            x_copy(0, 0).start()

    for cp in weight_copies(e, slot):
        cp.wait()
    nxt = jnp.minimum(e + 1, last)

    @pl.when(e < last)
    def _():
        for cp in weight_copies(nxt, 1 - slot):
            cp.start()

        @pl.when(cnt_ref[nxt] > 0)
        def _():
            x_copy(nxt, 1 - slot).start()

    wg_s[...] = wg_f[slot].astype(BF16)
    wu_s[...] = wu_f[slot].astype(BF16)
    wd_s[...] = wd_f[slot].astype(BF16)

    before = jnp.maximum(e - 2, 0)

    @pl.when(jnp.logical_and(e >= 2, cnt_ref[before] > 0))
    def _():
        y_copy(before, slot).wait()

    def ffn(x_ref, rows, valid):
        x = _from_slab(x_ref, rows, lines, valid)
        hg = jnp.dot(x, wg_s[...], preferred_element_type=F32)
        hu = jnp.dot(x, wu_s[...], preferred_element_type=F32)
        hb = hg * _sigmoid(hg) * hu
        return jnp.dot(hb.astype(BF16), wd_s[...], preferred_element_type=F32)

    count = cnt_ref[e]

    @pl.when(count > 0)
    def _():
        x_copy(e, slot).wait()
        head = jnp.minimum(count, cap)
        lower = 0
        for rows in range(chunk, cap + 1, chunk):
            @pl.when(jnp.logical_and(head > lower, head <= rows))
            def _(rows=rows):
                _to_slab(ybuf.at[slot], ffn(xbuf.at[slot], rows, head))
            lower = rows
        y_copy(e, slot).start()

        @pl.when(count > cap)
        def _():
            def extra(j, carry):
                row = cap + j * chunk
                n = jnp.minimum(chunk, count - row) * lines
                fetch = pltpu.make_async_copy(xs_hbm.at[pl.ds(first_line(e, row), n), :],
                                              xo.at[pl.ds(0, n), :], osem)
                fetch.start()
                fetch.wait()
                _to_slab(yo, ffn(xo, chunk, count - row))
                store = pltpu.make_async_copy(yo.at[pl.ds(0, n), :],
                                              ys_hbm.at[pl.ds(first_line(e, row), n), :], osem)
                store.start()
                store.wait()
                return carry
            lax.fori_loop(0, (count - cap + chunk - 1) // chunk, extra, 0)

    @pl.when(e == last)
    def _():
        @pl.when(count > 0)
        def _():
            y_copy(e, slot).wait()
        prev = jnp.maximum(e - 1, 0)

        @pl.when(jnp.logical_and(e >= 1, cnt_ref[prev] > 0))
        def _():
            y_copy(prev, 1 - slot).wait()


def _experts(xs, pstart, counts, wg, wu, wd, layer):
    E, D, Fe = wg.shape[-3:]
    lines = _slab_lines(D)
    cap, chunk = EXPERT_CAP, EXPERT_CHUNK
    assert cap % chunk == 0 and (chunk * lines) % 8 == 0
    hbm = pl.BlockSpec(memory_space=pl.ANY)
    slab = lambda rows: pltpu.VMEM(rows, jnp.uint32)
    return pl.pallas_call(
        functools.partial(_expert_kernel, layer=layer, lines=lines, cap=cap, chunk=chunk),
        grid_spec=pltpu.PrefetchScalarGridSpec(
            num_scalar_prefetch=2,
            grid=(E,),
            in_specs=[hbm, hbm, hbm, hbm],
            out_specs=hbm,
            scratch_shapes=[slab((2, cap * lines, LANES)), slab((2, cap * lines, LANES)),
                            slab((chunk * lines, LANES)), slab((chunk * lines, LANES)),
                            pltpu.VMEM((2, D, Fe), F32), pltpu.VMEM((2, D, Fe), F32),
                            pltpu.VMEM((2, Fe, D), F32),
                            pltpu.VMEM((D, Fe), BF16), pltpu.VMEM((D, Fe), BF16), pltpu.VMEM((Fe, D), BF16),
                            pltpu.SemaphoreType.DMA((2,)), pltpu.SemaphoreType.DMA((2,)),
                            pltpu.SemaphoreType.DMA((2,)), pltpu.SemaphoreType.DMA(())],
        ),
        out_shape=jax.ShapeDtypeStruct(xs.shape, jnp.uint32),
        compiler_params=_params(("arbitrary",)),
        name="moe_experts",
    )(pstart, counts, xs, wg, wu, wd)


def _combine_kernel(toff_ref, gst_ref, tcnt_ref, ttot_ref, lpos_ref, gate_ref, h_ref, ys_ref, sg_ref, su_ref,
                    sd_ref, g_ref, beta_ref, o_ref, slab, sem, *, lines, alpha):
    _start_runs(toff_ref, gst_ref, tcnt_ref, slab, ys_ref, sem, lines, to_remote=False)
    h = h_ref[...]
    tile = h.shape[0]
    rows = slab.shape[0] // lines
    hb = h.astype(BF16)
    sgate = jnp.dot(hb, sg_ref[...], preferred_element_type=F32)
    sup = jnp.dot(hb, su_ref[...], preferred_element_type=F32)
    mid = sgate * _sigmoid(sgate) * sup
    acc = jnp.dot(mid.astype(BF16), sd_ref[...], preferred_element_type=F32)
    scol = lax.broadcasted_iota(jnp.int32, (tile, rows), 1)
    weights = jnp.zeros((tile, rows), F32)
    for k in range(TOP_K):
        weights = jnp.where(scol == lpos_ref[:, k:k + 1], gate_ref[:, k:k + 1], weights)
    total = ttot_ref[0, 0, 0]
    _wait_all_runs(slab, ys_ref, sem, total * lines)
    ys = _from_slab(slab, rows, lines, total)
    acc = acc + jnp.dot(weights.astype(BF16), ys, preferred_element_type=F32)
    o_ref[...] = _layer_norm(alpha * h + acc, g_ref[...], beta_ref[...])


def _combine(h2, lpos_nk, gate_nk, runs, ys, sg, su, sd, g, beta, alpha):
    N, D = h2.shape
    Fs = sg.shape[1]
    E = runs[0].shape[2]
    tile = _pick(N, MOE_TILE)
    lines = _slab_lines(D)
    const = lambda shape: pl.BlockSpec(shape, lambda i: (0, 0))
    nk = pl.BlockSpec((tile, TOP_K), lambda i: (i, 0))
    return pl.pallas_call(
        functools.partial(_combine_kernel, lines=lines, alpha=alpha),
        grid=(N // tile,),
        in_specs=_run_specs(E) + [
            nk, nk, pl.BlockSpec((tile, D), lambda i: (i, 0)), pl.BlockSpec(memory_space=pl.ANY),
            const((D, Fs)), const((D, Fs)), const((Fs, D)), const((1, D)), const((1, D))],
        out_specs=pl.BlockSpec((tile, D), lambda i: (i, 0)),
        out_shape=jax.ShapeDtypeStruct((N, D), F32),
        scratch_shapes=[pltpu.VMEM((_sorted_rows(tile, E) * lines, LANES), jnp.uint32),
                        pltpu.SemaphoreType.DMA(())],
        compiler_params=_params(("arbitrary",)),
        name="moe_combine",
    )(*runs, lpos_nk, gate_nk, h2, ys, sg.astype(BF16), su.astype(BF16), sd.astype(BF16),
      g.reshape(1, D), beta.reshape(1, D))


FLASH_TQ = 512
FLASH_TK = 512
MOE_TILE = 256
RUN_UNROLL = 8
EXPERT_CAP = 1024
EXPERT_CHUNK = 128


def _moe_ln(h2, rw, rb, wg, wu, wd, layer, sg, su, sd, g, beta, alpha):
    N, D = h2.shape
    E = rw.shape[1]
    gate, lpos, cnt, tcnt, toff, tbase, ttot = _router(h2, rw, rb)
    n_tiles = ttot.shape[0]
    counts = cnt[:, 0].astype(jnp.int32)
    pstart = (jnp.cumsum(counts) - counts).astype(jnp.int32)
    n_slots = N * TOP_K + n_tiles * E
    per_tile = lambda a: a.reshape(n_tiles, 1, E)
    runs = (per_tile(toff), per_tile(tbase) + pstart[None, None, :], per_tile(tcnt), ttot)
    xs = _dispatch(h2, lpos, runs, n_slots)
    ys = _experts(xs, pstart, counts, wg, wu, wd, layer)
    return _combine(h2, lpos.T, gate.T, runs, ys, sg, su, sd, g, beta, alpha)


def _mla_proj_kernel(h_ref, wa_ref, qg_ref, kvg_ref, wqt_ref, wk_ref, wvt_ref, cos_ref, sin_ref,
                     cost_ref, sint_ref, qt_ref, k_ref, vt_ref, *, q_lora, kv_lora, heads, q_scale):
    x = h_ref[0].astype(BF16)
    a = jnp.dot(x, wa_ref[...], preferred_element_type=F32)

    def rms(v, g):
        return v * lax.rsqrt(jnp.mean(v * v, axis=-1, keepdims=True) + RMS_EPS) * g

    q_lat = rms(a[:, :q_lora], qg_ref[...]).astype(BF16)
    kv_lat = rms(a[:, q_lora:q_lora + kv_lora], kvg_ref[...]).astype(BF16)
    o = q_lora + kv_lora
    k_rope = (a[:, o:o + LANES] * cos_ref[...] + a[:, o + LANES:o + 2 * LANES] * sin_ref[...]).astype(BF16)
    qa = lax.dot_general(wqt_ref[...], q_lat, _NT, preferred_element_type=F32)
    kn = jnp.dot(kv_lat, wk_ref[...], preferred_element_type=F32)
    vt_ref[0] = lax.dot_general(wvt_ref[...], kv_lat, _NT, preferred_element_type=F32).astype(BF16)
    cost = cost_ref[...]
    sint = sint_ref[...]
    hd = heads * LANES
    for h in range(heads):
        r0 = slice(h * LANES, (h + 1) * LANES)
        r1 = slice(hd + h * LANES, hd + (h + 1) * LANES)
        r2 = slice(2 * hd + h * LANES, 2 * hd + (h + 1) * LANES)
        qt_ref[0, 2 * h * LANES:(2 * h + 1) * LANES, :] = (qa[r0] * q_scale).astype(BF16)
        qt_ref[0, (2 * h + 1) * LANES:(2 * h + 2) * LANES, :] = (
            (qa[r1] * cost + qa[r2] * sint) * q_scale).astype(BF16)
        k_ref[0, :, 2 * h * LANES:(2 * h + 1) * LANES] = kn[:, r0].astype(BF16)
        k_ref[0, :, (2 * h + 1) * LANES:(2 * h + 2) * LANES] = k_rope


def _rot_half(w):
    r = w.shape[-1] // 2
    return jnp.concatenate([-w[..., r:], w[..., :r]], axis=-1)


def _pad_lanes(w):
    return jnp.pad(w, [(0, 0)] * (w.ndim - 1) + [(0, LANES - w.shape[-1])])


FLASH_HEADROOM = 64.0


def _flash_kernel(qt_ref, k_ref, vt_ref, o_ref, p_ref, s_ref, mx_ref, m_ref, l_ref, acc_ref, *, tk):
    qt = qt_ref[0]
    nk = k_ref.shape[1] // tk
    assert nk % 2 == 0 and nk >= 2

    def scores(c):
        off = pl.multiple_of(c * tk, tk)
        return jnp.dot(k_ref[0, pl.ds(off, tk), :], qt, preferred_element_type=F32)

    s0 = scores(0)
    m0 = jnp.max(s0, axis=0, keepdims=True)
    p0 = jnp.exp2(s0 - m0)
    p_ref[0] = p0.astype(BF16)
    acc_ref[...] = jnp.zeros_like(acc_ref)

    def probs(c, slot, carry):
        l, top = carry
        s = scores(c)
        p = jnp.exp2(s - m0)
        p_ref[slot] = p.astype(BF16)
        return l + jnp.sum(p, axis=0, keepdims=True), jnp.maximum(top, jnp.max(s, axis=0, keepdims=True))

    def accumulate(c, slot):
        off = pl.multiple_of(c * tk, tk)
        acc_ref[...] += jnp.dot(vt_ref[0, :, pl.ds(off, tk)], p_ref[slot], preferred_element_type=F32)

    carry = (jnp.sum(p0, axis=0, keepdims=True), m0)
    for c in range(nk - 1):
        carry = probs(c + 1, (c + 1) % 2, carry)
        accumulate(c, c % 2)
    accumulate(nk - 1, (nk - 1) % 2)
    l, top = carry
    o_ref[0] = (acc_ref[...] / l).T.astype(o_ref.dtype)

    @pl.when(jnp.max(top - m0) > FLASH_HEADROOM)
    def _():
        _flash_online(qt, k_ref, vt_ref, o_ref, s_ref, mx_ref, m_ref, l_ref, acc_ref, tk)


def _flash_online(qt, k_ref, vt_ref, o_ref, s_ref, mx_ref, m_ref, l_ref, acc_ref, tk):
    nk = k_ref.shape[1] // tk
    assert nk % 2 == 0 and nk >= 2
    m_ref[...] = jnp.full_like(m_ref, -jnp.inf)
    l_ref[...] = jnp.zeros_like(l_ref)
    acc_ref[...] = jnp.zeros_like(acc_ref)

    def scores(c, slot):
        off = pl.multiple_of(c * tk, tk)
        s = jnp.dot(k_ref[0, pl.ds(off, tk), :], qt, preferred_element_type=F32)
        s_ref[slot] = s
        mx_ref[slot] = jnp.max(s, axis=0, keepdims=True)

    def update(c, slot):
        off = pl.multiple_of(c * tk, tk)
        vt = vt_ref[0, :, pl.ds(off, tk)]
        m_prev = m_ref[...]
        m_new = jnp.maximum(m_prev, mx_ref[slot])
        p = jnp.exp2(s_ref[slot] - m_new)
        a = jnp.exp2(m_prev - m_new)
        l_ref[...] = a * l_ref[...] + jnp.sum(p, axis=0, keepdims=True)
        acc_ref[...] = a * acc_ref[...] + jnp.dot(vt, p.astype(BF16), preferred_element_type=F32)
        m_ref[...] = m_new

    scores(0, 0)

    def body(i, carry):
        c = 2 * i
        scores(c + 1, 1)
        update(c, 0)
        scores(c + 2, 0)
        update(c + 1, 1)
        return carry

    lax.fori_loop(0, nk // 2 - 1, body, 0)
    scores(nk - 1, 1)
    update(nk - 2, 0)
    update(nk - 1, 1)
    o_ref[0] = (acc_ref[...] / l_ref[...]).T.astype(o_ref.dtype)


def _oproj_ln_kernel(o_ref, h_ref, w_ref, g_ref, beta_ref, out_ref, *, alpha):
    m = jnp.dot(o_ref[0], w_ref[...], preferred_element_type=F32)
    out_ref[0] = _layer_norm(alpha * h_ref[0] + m, g_ref[...], beta_ref[...])


def _mla_ln(h, w_a, qg, kvg, w_uq, w_uk, w_uv, w_o, g, beta, alpha):
    B, S, D = h.shape
    q_lora, heads, qk = w_uq.shape
    kv_lora, _, nope = w_uk.shape
    vd = w_uv.shape[-1]
    rope = qk - nope
    assert nope == LANES and vd == LANES and rope <= LANES and rope % 2 == 0
    kr = w_a[:, q_lora + kv_lora:]
    wa = jnp.concatenate([w_a[:, :q_lora + kv_lora], _pad_lanes(kr), _pad_lanes(_rot_half(kr))],
                         axis=1).astype(BF16)
    wq_r = w_uq[:, :, nope:]
    wq = jnp.concatenate([w_uq[:, :, :nope].reshape(q_lora, heads * nope),
                          _pad_lanes(wq_r).reshape(q_lora, heads * LANES),
                          _pad_lanes(_rot_half(wq_r)).reshape(q_lora, heads * LANES)], axis=1).astype(BF16)
    wk = w_uk.reshape(kv_lora, heads * nope).astype(BF16)
    wvt = w_uv.reshape(kv_lora, heads * vd).T.astype(BF16)
    wqt = wq.T
    pos = jnp.arange(S, dtype=F32)
    inv = 1.0 / (ROPE_THETA ** (jnp.arange(0, rope, 2, dtype=F32) / rope))
    ang = pos[:, None] * inv[None, :]
    cos = _pad_lanes(jnp.concatenate([jnp.cos(ang), jnp.cos(ang)], axis=1))
    sin = _pad_lanes(jnp.concatenate([jnp.sin(ang), jnp.sin(ang)], axis=1))
    q_scale = float(qk) ** -0.5 * math.log2(math.e)

    tp = _pick(S, 256)
    const = lambda a: pl.BlockSpec(a.shape, lambda b, i: (0,) * a.ndim)
    qg2, kvg2 = qg.reshape(1, q_lora), kvg.reshape(1, kv_lora)
    tok_major = pl.BlockSpec((tp, LANES), lambda b, i: (i, 0))
    feat_major = pl.BlockSpec((LANES, tp), lambda b, i: (0, i))
    qt, k, vt = pl.pallas_call(
        functools.partial(_mla_proj_kernel, q_lora=q_lora, kv_lora=kv_lora, heads=heads, q_scale=q_scale),
        grid=(B, S // tp),
        in_specs=[pl.BlockSpec((1, tp, D), lambda b, i: (b, i, 0)), const(wa), const(qg2), const(kvg2),
                  const(wqt), const(wk), const(wvt), tok_major, tok_major, feat_major, feat_major],
        out_specs=[pl.BlockSpec((1, 2 * heads * LANES, tp), lambda b, i: (b, 0, i)),
                   pl.BlockSpec((1, tp, 2 * heads * LANES), lambda b, i: (b, i, 0)),
                   pl.BlockSpec((1, heads * LANES, tp), lambda b, i: (b, 0, i))],
        out_shape=[jax.ShapeDtypeStruct((B, 2 * heads * LANES, S), BF16),
                   jax.ShapeDtypeStruct((B, S, 2 * heads * LANES), BF16),
                   jax.ShapeDtypeStruct((B, heads * LANES, S), BF16)],
        compiler_params=_params(("parallel", "parallel")),
        name="mla_proj",
    )(h, wa, qg2, kvg2, wqt, wk, wvt, cos, sin, cos.T, sin.T)

    tq = _pick(S, FLASH_TQ)
    tk = _pick(S, FLASH_TK)
    o = pl.pallas_call(
        functools.partial(_flash_kernel, tk=tk),
        grid=(B, heads, S // tq),
        in_specs=[pl.BlockSpec((1, 2 * LANES, tq), lambda b, hh, i: (b, hh, i)),
                  pl.BlockSpec((1, S, 2 * LANES), lambda b, hh, i: (b, 0, hh)),
                  pl.BlockSpec((1, LANES, S), lambda b, hh, i: (b, hh, 0))],
        out_specs=pl.BlockSpec((1, tq, LANES), lambda b, hh, i: (b, i, hh)),
        out_shape=jax.ShapeDtypeStruct((B, S, heads * LANES), BF16),
        scratch_shapes=[pltpu.VMEM((2, tk, tq), BF16), pltpu.VMEM((2, tk, tq), F32), pltpu.VMEM((2, 1, tq), F32),
                        pltpu.VMEM((1, tq), F32), pltpu.VMEM((1, tq), F32), pltpu.VMEM((LANES, tq), F32)],
        compiler_params=_params(("parallel", "parallel", "arbitrary")),
        name="mla_flash",
    )(qt, k, vt)

    to = _pick(S, 512)
    vec = pl.BlockSpec((1, D), lambda b, i: (0, 0))
    return pl.pallas_call(
        functools.partial(_oproj_ln_kernel, alpha=alpha),
        grid=(B, S // to),
        in_specs=[pl.BlockSpec((1, to, heads * vd), lambda b, i: (b, i, 0)),
                  pl.BlockSpec((1, to, D), lambda b, i: (b, i, 0)),
                  pl.BlockSpec((heads * vd, D), lambda b, i: (0, 0)), vec, vec],
        out_specs=pl.BlockSpec((1, to, D), lambda b, i: (b, i, 0)),
        out_shape=jax.ShapeDtypeStruct((B, S, D), F32),
        compiler_params=_params(("parallel", "parallel")),
        name="mla_oproj_ln",
    )(o, h, w_o.astype(BF16), g.reshape(1, D), beta.reshape(1, D))


def kernel(x, pool_w, pool_b, pool_scale, mla_w_a, mla_q_norm_g, mla_kv_norm_g, mla_w_uq, mla_w_uk,
           mla_w_uv, mla_w_o, ln_mix_g, ln_mix_b, router_w, router_b, exp_w_gate, exp_w_up, exp_w_down,
           sh_w_gate, sh_w_up, sh_w_down, ln_ffn_g, ln_ffn_b):
    B, S, D = x.shape
    depth = ln_mix_g.shape[0]
    alpha = (2 * depth) ** 0.25
    h = x
    for i in range(depth):
        j = i // 2
        if i % 2 == 0:
            h = _pool_ln(h, pool_w[j], pool_b[j], pool_scale[j], ln_mix_g[i], ln_mix_b[i], alpha)
        else:
            h = _mla_ln(h, mla_w_a[j], mla_q_norm_g[j], mla_kv_norm_g[j], mla_w_uq[j], mla_w_uk[j],
                        mla_w_uv[j], mla_w_o[j], ln_mix_g[i], ln_mix_b[i], alpha)
        h = _moe_ln(h.reshape(B * S, D), router_w[i], router_b[i], exp_w_gate, exp_w_up, exp_w_down, i,
                    sh_w_gate[i], sh_w_up[i], sh_w_down[i], ln_ffn_g[i], ln_ffn_b[i],
                    alpha).reshape(B, S, D)
    return h
```

```python
import functools
import math

import jax
import jax.numpy as jnp
from jax import lax
from jax.experimental import pallas as pl
from jax.experimental.pallas import tpu as pltpu

F32 = jnp.float32
BF16 = jnp.bfloat16

LN_EPS = 1e-5
RMS_EPS = 1e-6
POOL_WINDOWS = (2, 4, 8, 16)
POOL_HALO = 8
ROPE_THETA = 10000.0
TOP_K = 8
N_GROUPS = 8
TOPK_GROUPS = 4
ROUTED_SCALE = 2.5
LANES = 128
VMEM_LIMIT = 48 * 1024 * 1024

_NT = (((1,), (1,)), ((), ()))


def _pick(n, pref):
    t = min(pref, n)
    while n % t:
        t //= 2
    assert t >= 8 and n % t == 0, (n, pref)
    return t


def _layer_norm(z, g, b):
    mu = jnp.mean(z, axis=-1, keepdims=True)
    zc = z - mu
    var = jnp.mean(zc * zc, axis=-1, keepdims=True)
    return zc * lax.rsqrt(var + LN_EPS) * g + b


def _sigmoid(x):
    return 1.0 / (1.0 + jnp.exp(-x))


def _params(sem):
    return pltpu.CompilerParams(dimension_semantics=sem, vmem_limit_bytes=VMEM_LIMIT)


def _pool_ln_kernel(xc_ref, xp_ref, xn_ref, w_ref, b_ref, sc_ref, g_ref, beta_ref, o_ref, ext_ref,
                    *, tile, seq, alpha):
    i = pl.program_id(1)
    n_tiles = pl.num_programs(1)
    xc = xc_ref[0]
    d_model = xc.shape[-1]
    ch = d_model // len(POOL_WINDOWS)
    ext_ref[0:POOL_HALO, :] = jnp.where(i > 0, xp_ref[0], 0.0)
    ext_ref[POOL_HALO:POOL_HALO + tile, :] = xc
    ext_ref[POOL_HALO + tile:2 * POOL_HALO + tile, :] = jnp.where(i < n_tiles - 1, xn_ref[0], 0.0)
    pos = i * tile + lax.broadcasted_iota(jnp.int32, (tile, 1), 0)
    outs = []
    for g, win in enumerate(POOL_WINDOWS):
        half = win // 2
        cols = slice(g * ch, (g + 1) * ch)
        acc = ext_ref[POOL_HALO - half:POOL_HALO - half + tile, cols]
        for j in range(-half + 1, half):
            acc = acc + ext_ref[POOL_HALO + j:POOL_HALO + j + tile, cols]
        cnt = (jnp.minimum(pos + half, seq) - jnp.maximum(pos - half, 0)).astype(F32)
        pooled = acc / cnt - xc[:, cols]
        outs.append(jnp.dot(pooled.astype(BF16), w_ref[g], preferred_element_type=F32))
    y = jnp.concatenate(outs, axis=-1)
    y = (y + b_ref[...]) * sc_ref[...]
    o_ref[0] = _layer_norm(alpha * xc + y, g_ref[...], beta_ref[...])


def _pool_ln(x, w, b, sc, g, beta, alpha):
    B, S, D = x.shape
    tile = _pick(S, 512)
    nh = tile // POOL_HALO
    last = S // POOL_HALO - 1
    row = lambda v: v.reshape(1, D)
    vec = pl.BlockSpec((1, D), lambda bi, i: (0, 0))
    return pl.pallas_call(
        functools.partial(_pool_ln_kernel, tile=tile, seq=S, alpha=alpha),
        grid=(B, S // tile),
        in_specs=[
            pl.BlockSpec((1, tile, D), lambda bi, i: (bi, i, 0)),
            pl.BlockSpec((1, POOL_HALO, D), lambda bi, i: (bi, jnp.maximum(i * nh - 1, 0), 0)),
            pl.BlockSpec((1, POOL_HALO, D), lambda bi, i: (bi, jnp.minimum((i + 1) * nh, last), 0)),
            pl.BlockSpec(w.shape, lambda bi, i: (0, 0, 0)),
            vec, vec, vec, vec,
        ],
        out_specs=pl.BlockSpec((1, tile, D), lambda bi, i: (bi, i, 0)),
        out_shape=jax.ShapeDtypeStruct((B, S, D), F32),
        scratch_shapes=[pltpu.VMEM((tile + 2 * POOL_HALO, D), F32)],
        compiler_params=_params(("parallel", "parallel")),
        name="pool_ln",
    )(x, x, x, w.astype(BF16), row(b), row(sc), row(g), row(beta))


def _router_kernel(h_ref, whi_ref, wlo_ref, rb_ref, tri_ref, ltri_ref,
                   gate_ref, lpos_ref, cnt_ref, tcnt_ref, toff_ref, tbase_ref, ttot_ref):
    i = pl.program_id(0)

    @pl.when(i == 0)
    def _():
        cnt_ref[...] = jnp.zeros_like(cnt_ref)

    x = h_ref[...]
    xhi = x.astype(BF16)
    xlo = (x - xhi.astype(F32)).astype(BF16)
    whi = whi_ref[...]
    logits = (lax.dot_general(whi, xhi, _NT, preferred_element_type=F32)
              + lax.dot_general(whi, xlo, _NT, preferred_element_type=F32)
              + lax.dot_general(wlo_ref[...], xhi, _NT, preferred_element_type=F32))
    n_exp, tile = logits.shape
    per = n_exp // N_GROUPS
    scores = _sigmoid(logits)
    biased = scores + rb_ref[...]
    neg = -jnp.inf

    sub = lax.broadcasted_iota(jnp.int32, (per, tile), 0).astype(F32)
    gscore = []
    for g in range(N_GROUPS):
        bg = biased[g * per:(g + 1) * per, :]
        m1 = jnp.max(bg, axis=0, keepdims=True)
        i1 = jnp.min(jnp.where(bg == m1, sub, float(per)), axis=0, keepdims=True)
        m2 = jnp.max(jnp.where(sub == i1, neg, bg), axis=0, keepdims=True)
        gscore.append(m1 + m2)
    masked = []
    for g in range(N_GROUPS):
        beat = jnp.zeros_like(gscore[g])
        for o in range(N_GROUPS):
            if o == g:
                continue
            wins = (gscore[o] >= gscore[g]) if o < g else (gscore[o] > gscore[g])
            beat = beat + jnp.where(wins, 1.0, 0.0)
        keep = jnp.broadcast_to(beat < float(TOPK_GROUPS), (per, tile))
        masked.append(jnp.where(keep, biased[g * per:(g + 1) * per, :], neg))
    masked = jnp.concatenate(masked, axis=0)

    row = lax.broadcasted_iota(jnp.int32, (n_exp, tile), 0).astype(F32)
    chosen = jnp.zeros((n_exp, tile), F32)
    idxs, gates = [], []
    for _ in range(TOP_K):
        m = jnp.max(masked, axis=0, keepdims=True)
        idx = jnp.min(jnp.where(masked == m, row, float(n_exp)), axis=0, keepdims=True)
        hit = row == idx
        gates.append(jnp.sum(jnp.where(hit, scores, 0.0), axis=0, keepdims=True))
        masked = jnp.where(hit, neg, masked)
        chosen = jnp.where(hit, 1.0, chosen)
        idxs.append(idx)
    gsum = gates[0]
    for k in range(1, TOP_K):
        gsum = gsum + gates[k]

    earlier = jnp.dot(chosen.astype(BF16), tri_ref[...], preferred_element_type=F32)
    count = jnp.sum(chosen, axis=1, keepdims=True)
    run = count + (count - 2.0 * jnp.floor(0.5 * count))
    lower = jnp.dot(ltri_ref[...], jnp.broadcast_to(run, (n_exp, LANES)).astype(BF16),
                    preferred_element_type=F32)[:, :1]
    order = earlier + lower
    for k in range(TOP_K):
        gate_ref[k:k + 1, :] = gates[k] / gsum * ROUTED_SCALE
        lpos_k = jnp.sum(jnp.where(row == idxs[k], order, 0.0), axis=0, keepdims=True)
        lpos_ref[k:k + 1, :] = lpos_k.astype(jnp.int32)
    tcnt_ref[...] = run.astype(jnp.int32)
    toff_ref[...] = lower.astype(jnp.int32)
    tbase_ref[...] = cnt_ref[...].astype(jnp.int32)
    ttot_ref[...] = jnp.sum(run, axis=0, keepdims=True).astype(jnp.int32)
    cnt_ref[...] = cnt_ref[...] + run


def _router(h2, rw, rb):
    N, D = h2.shape
    E = rw.shape[1]
    tile = _pick(N, MOE_TILE)
    assert tile <= 256
    n_tiles = N // tile
    wt = rw.T
    whi = wt.astype(BF16)
    wlo = (wt - whi.astype(F32)).astype(BF16)
    ar = jnp.arange(tile)
    tri = (ar[:, None] < ar[None, :]).astype(BF16)
    ae = jnp.arange(E)
    ltri = (ae[None, :] < ae[:, None]).astype(BF16)
    const = lambda shape: pl.BlockSpec(shape, lambda i: (0, 0))
    kn = pl.BlockSpec((TOP_K, tile), lambda i: (0, i))
    table = pl.BlockSpec((None, E, 1), lambda i: (i, 0, 0))
    table_shape = jax.ShapeDtypeStruct((n_tiles, E, 1), jnp.int32)
    return pl.pallas_call(
        _router_kernel,
        grid=(n_tiles,),
        in_specs=[pl.BlockSpec((tile, D), lambda i: (i, 0)), const((E, D)), const((E, D)),
                  const((E, 1)), const((tile, tile)), const((E, E))],
        out_specs=[kn, kn, const((E, 1)), table, table, table,
                   pl.BlockSpec((None, 1, 1), lambda i: (i, 0, 0))],
        out_shape=[jax.ShapeDtypeStruct((TOP_K, N), F32), jax.ShapeDtypeStruct((TOP_K, N), jnp.int32),
                   jax.ShapeDtypeStruct((E, 1), F32), table_shape, table_shape, table_shape,
                   jax.ShapeDtypeStruct((n_tiles, 1, 1), jnp.int32)],
        compiler_params=_params(("arbitrary",)),
        name="moe_router",
    )(h2, whi, wlo, rb.reshape(E, 1), tri, ltri)


def _pack_pairs(x, rounded=False):
    half = x.shape[-1] // 2
    bits = lambda v: lax.bitcast_convert_type(v if rounded else v.astype(BF16).astype(F32), jnp.uint32)
    return (bits(x[:, :half]) & jnp.uint32(0xFFFF0000)) | (bits(x[:, half:]) >> 16)


def _unpack_pairs(w):
    hi = lax.bitcast_convert_type(w & jnp.uint32(0xFFFF0000), F32)
    lo = lax.bitcast_convert_type(w << 16, F32)
    return jnp.concatenate([hi.astype(BF16), lo.astype(BF16)], axis=-1)


def _to_slab(slab_ref, x, rounded=False):
    w = _pack_pairs(x, rounded)
    lines = w.shape[1] // LANES
    for a in range(lines):
        slab_ref[pl.ds(a, w.shape[0], stride=lines), :] = w[:, a * LANES:(a + 1) * LANES]


def _from_slab(slab_ref, rows, lines, valid, first=0):
    w = jnp.concatenate([slab_ref[pl.ds(first * lines + a, rows, stride=lines), :] for a in range(lines)],
                        axis=-1)
    r = first + lax.broadcasted_iota(jnp.int32, (rows, 1), 0)
    return _unpack_pairs(jnp.where(r < valid, w, jnp.uint32(0)))


def _start_runs(toff_ref, gst_ref, tcnt_ref, local, remote, sem, lines, to_remote):
    def body(e, c):
        cnt, off, gst = tcnt_ref[0, 0, e], toff_ref[0, 0, e], gst_ref[0, 0, e]

        @pl.when(cnt > 0)
        def _():
            loc = local.at[pl.ds(pl.multiple_of(off * lines, 8), cnt * lines), :]
            rem = remote.at[pl.ds(pl.multiple_of(gst * lines, 8), cnt * lines), :]
            (pltpu.make_async_copy(loc, rem, sem) if to_remote else pltpu.make_async_copy(rem, loc, sem)).start()
        return c
    lax.fori_loop(0, tcnt_ref.shape[2], body, 0, unroll=RUN_UNROLL)


def _wait_all_runs(slab, remote, sem, total_lines):
    n = pl.multiple_of(total_lines, 8)
    pltpu.make_async_copy(slab.at[pl.ds(0, n), :], remote.at[pl.ds(0, n), :], sem).wait()


def _sorted_rows(tile, n_exp):
    return TOP_K * tile + n_exp


def _dispatch_kernel(toff_ref, gst_ref, tcnt_ref, ttot_ref, tprev_ref, lpos_ref, h_ref, xs_ref, slabs, sems,
                     *, lines):
    i = pl.program_id(0)
    slot = lax.rem(i, 2)
    slab = slabs.at[slot]
    tile = h_ref.shape[0]
    rows = slab.shape[0] // lines
    srow = lax.broadcasted_iota(jnp.int32, (rows, tile), 0)
    perm = jnp.zeros((rows, tile), F32)
    for k in range(TOP_K):
        perm = jnp.where(srow == lpos_ref[k:k + 1, :], 1.0, perm)
    _to_slab(slab, jnp.dot(perm.astype(BF16), h_ref[...].astype(BF16), preferred_element_type=F32),
             rounded=True)
    _start_runs(toff_ref, gst_ref, tcnt_ref, slab, xs_ref, sems.at[slot], lines, to_remote=True)

    @pl.when(i > 0)
    def _():
        _wait_all_runs(slabs.at[1 - slot], xs_ref, sems.at[1 - slot], tprev_ref[0, 0, 0] * lines)

    @pl.when(i == pl.num_programs(0) - 1)
    def _():
        _wait_all_runs(slab, xs_ref, sems.at[slot], ttot_ref[0, 0, 0] * lines)


def _run_specs(n_exp):
    smem = lambda n: pl.BlockSpec((1, 1, n), lambda i: (i, 0, 0), memory_space=pltpu.SMEM)
    return [smem(n_exp), smem(n_exp), smem(n_exp), smem(1)]


def _slab_lines(d_model):
    lines = d_model // (2 * LANES)
    assert lines * 2 * LANES == d_model and (2 * lines) % 8 == 0
    return lines


def _dispatch(h2, lpos, runs, n_slots):
    N, D = h2.shape
    E = runs[0].shape[2]
    tile = _pick(N, MOE_TILE)
    lines = _slab_lines(D)
    return pl.pallas_call(
        functools.partial(_dispatch_kernel, lines=lines),
        grid=(N // tile,),
        in_specs=_run_specs(E) + [
            pl.BlockSpec((1, 1, 1), lambda i: (jnp.maximum(i - 1, 0), 0, 0), memory_space=pltpu.SMEM),
            pl.BlockSpec((TOP_K, tile), lambda i: (0, i)), pl.BlockSpec((tile, D), lambda i: (i, 0))],
        out_specs=pl.BlockSpec(memory_space=pl.ANY),
        out_shape=jax.ShapeDtypeStruct((n_slots * lines, LANES), jnp.uint32),
        scratch_shapes=[pltpu.VMEM((2, _sorted_rows(tile, E) * lines, LANES), jnp.uint32),
                        pltpu.SemaphoreType.DMA((2,))],
        compiler_params=_params(("arbitrary",)),
        name="moe_dispatch",
    )(*runs, runs[3], lpos, h2)


def _expert_kernel(pstart_ref, cnt_ref, xs_hbm, wg_hbm, wu_hbm, wd_hbm, ys_hbm,
                   xbuf, ybuf, xo, yo, wg_f, wu_f, wd_f, wg_s, wu_s, wd_s, wsem, xsem, ysem, osem,
                   *, layer, lines, cap, chunk):
    e = pl.program_id(0)
    last = pl.num_programs(0) - 1
    slot = lax.rem(e, 2)

    def weight_copies(ex, sl):
        return [pltpu.make_async_copy(w.at[layer, ex], f.at[sl], wsem.at[sl])
                for w, f in ((wg_hbm, wg_f), (wu_hbm, wu_f), (wd_hbm, wd_f))]

    def head_lines(ex):
        return jnp.minimum(cnt_ref[ex], cap) * lines

    def first_line(ex, row=0):
        return pl.multiple_of((pstart_ref[ex] + row) * lines, 8)

    def x_copy(ex, sl):
        n = head_lines(ex)
        return pltpu.make_async_copy(xs_hbm.at[pl.ds(first_line(ex), n), :], xbuf.at[sl, pl.ds(0, n), :],
                                     xsem.at[sl])

    def y_copy(ex, sl):
        n = head_lines(ex)
        return pltpu.make_async_copy(ybuf.at[sl, pl.ds(0, n), :], ys_hbm.at[pl.ds(first_line(ex), n), :],
                                     ysem.at[sl])

    @pl.when(e == 0)
    def _():
        for cp in weight_copies(0, 0):
            cp.start()

        @pl.when(cnt_ref[0] > 0)
        def _():
            x_copy(0, 0).start()

    for cp in weight_copies(e, slot):
        cp.wait()
    nxt = jnp.minimum(e + 1, last)

    @pl.when(e < last)
    def _():
        for cp in weight_copies(nxt, 1 - slot):
            cp.start()

        @pl.when(cnt_ref[nxt] > 0)
        def _():
            x_copy(nxt, 1 - slot).start()

    wg_s[...] = wg_f[slot].astype(BF16)
    wu_s[...] = wu_f[slot].astype(BF16)
    wd_s[...] = wd_f[slot].astype(BF16)

    before = jnp.maximum(e - 2, 0)

    @pl.when(jnp.logical_and(e >= 2, cnt_ref[before] > 0))
    def _():
        y_copy(before, slot).wait()

    def ffn(x_ref, rows, valid):
        x = _from_slab(x_ref, rows, lines, valid)
        hg = jnp.dot(x, wg_s[...], preferred_element_type=F32)
        hu = jnp.dot(x, wu_s[...], preferred_element_type=F32)
        hb = hg * _sigmoid(hg) * hu
        return jnp.dot(hb.astype(BF16), wd_s[...], preferred_element_type=F32)

    count = cnt_ref[e]

    @pl.when(count > 0)
    def _():
        x_copy(e, slot).wait()
        head = jnp.minimum(count, cap)
        lower = 0
        for rows in range(chunk, cap + 1, chunk):
            @pl.when(jnp.logical_and(head > lower, head <= rows))
            def _(rows=rows):
                _to_slab(ybuf.at[slot], ffn(xbuf.at[slot], rows, head))
            lower = rows
        y_copy(e, slot).start()

        @pl.when(count > cap)
        def _():
            def extra(j, carry):
                row = cap + j * chunk
                n = jnp.minimum(chunk, count - row) * lines
                fetch = pltpu.make_async_copy(xs_hbm.at[pl.ds(first_line(e, row), n), :],
                                              xo.at[pl.ds(0, n), :], osem)
                fetch.start()
                fetch.wait()
                _to_slab(yo, ffn(xo, chunk, count - row))
                store = pltpu.make_async_copy(yo.at[pl.ds(0, n), :],
                                              ys_hbm.at[pl.ds(first_line(e, row), n), :], osem)
                store.start()
                store.wait()
                return carry
            lax.fori_loop(0, (count - cap + chunk - 1) // chunk, extra, 0)

    @pl.when(e == last)
    def _():
        @pl.when(count > 0)
        def _():
            y_copy(e, slot).wait()
        prev = jnp.maximum(e - 1, 0)

        @pl.when(jnp.logical_and(e >= 1, cnt_ref[prev] > 0))
        def _():
            y_copy(prev, 1 - slot).wait()


def _experts(xs, pstart, counts, wg, wu, wd, layer):
    E, D, Fe = wg.shape[-3:]
    lines = _slab_lines(D)
    cap, chunk = EXPERT_CAP, EXPERT_CHUNK
    assert cap % chunk == 0 and (chunk * lines) % 8 == 0
    hbm = pl.BlockSpec(memory_space=pl.ANY)
    slab = lambda rows: pltpu.VMEM(rows, jnp.uint32)
    return pl.pallas_call(
        functools.partial(_expert_kernel, layer=layer, lines=lines, cap=cap, chunk=chunk),
        grid_spec=pltpu.PrefetchScalarGridSpec(
            num_scalar_prefetch=2,
            grid=(E,),
            in_specs=[hbm, hbm, hbm, hbm],
            out_specs=hbm,
            scratch_shapes=[slab((2, cap * lines, LANES)), slab((2, cap * lines, LANES)),
                            slab((chunk * lines, LANES)), slab((chunk * lines, LANES)),
                            pltpu.VMEM((2, D, Fe), F32), pltpu.VMEM((2, D, Fe), F32),
                            pltpu.VMEM((2, Fe, D), F32),
                            pltpu.VMEM((D, Fe), BF16), pltpu.VMEM((D, Fe), BF16), pltpu.VMEM((Fe, D), BF16),
                            pltpu.SemaphoreType.DMA((2,)), pltpu.SemaphoreType.DMA((2,)),
                            pltpu.SemaphoreType.DMA((2,)), pltpu.SemaphoreType.DMA(())],
        ),
        out_shape=jax.ShapeDtypeStruct(xs.shape, jnp.uint32),
        compiler_params=_params(("arbitrary",)),
        name="moe_experts",
    )(pstart, counts, xs, wg, wu, wd)


def _combine_kernel(toff_ref, gst_ref, tcnt_ref, ttot_ref, lpos_ref, gate_ref, h_ref, ys_ref, sg_ref, su_ref,
                    sd_ref, g_ref, beta_ref, o_ref, slab, sem, *, lines, alpha):
    _start_runs(toff_ref, gst_ref, tcnt_ref, slab, ys_ref, sem, lines, to_remote=False)
    h = h_ref[...]
    tile = h.shape[0]
    rows = slab.shape[0] // lines
    hb = h.astype(BF16)
    sgate = jnp.dot(hb, sg_ref[...], preferred_element_type=F32)
    sup = jnp.dot(hb, su_ref[...], preferred_element_type=F32)
    mid = sgate * _sigmoid(sgate) * sup
    acc = jnp.dot(mid.astype(BF16), sd_ref[...], preferred_element_type=F32)
    total = ttot_ref[0, 0, 0]
    _wait_all_runs(slab, ys_ref, sem, total * lines)
    chunk = max(c for c in range(8, 257, 8) if rows % c == 0)
    for first in range(0, rows, chunk):
        scol = first + lax.broadcasted_iota(jnp.int32, (tile, chunk), 1)
        weights = jnp.zeros((tile, chunk), F32)
        for k in range(TOP_K):
            weights = jnp.where(scol == lpos_ref[:, k:k + 1], gate_ref[:, k:k + 1], weights)
        acc = acc + jnp.dot(weights.astype(BF16), _from_slab(slab, chunk, lines, total, first),
                            preferred_element_type=F32)
    o_ref[...] = _layer_norm(alpha * h + acc, g_ref[...], beta_ref[...])


def _combine(h2, lpos_nk, gate_nk, runs, ys, sg, su, sd, g, beta, alpha):
    N, D = h2.shape
    Fs = sg.shape[1]
    E = runs[0].shape[2]
    tile = _pick(N, MOE_TILE)
    lines = _slab_lines(D)
    const = lambda shape: pl.BlockSpec(shape, lambda i: (0, 0))
    nk = pl.BlockSpec((tile, TOP_K), lambda i: (i, 0))
    return pl.pallas_call(
        functools.partial(_combine_kernel, lines=lines, alpha=alpha),
        grid=(N // tile,),
        in_specs=_run_specs(E) + [
            nk, nk, pl.BlockSpec((tile, D), lambda i: (i, 0)), pl.BlockSpec(memory_space=pl.ANY),
            const((D, Fs)), const((D, Fs)), const((Fs, D)), const((1, D)), const((1, D))],
        out_specs=pl.BlockSpec((tile, D), lambda i: (i, 0)),
        out_shape=jax.ShapeDtypeStruct((N, D), F32),
        scratch_shapes=[pltpu.VMEM((_sorted_rows(tile, E) * lines, LANES), jnp.uint32),
                        pltpu.SemaphoreType.DMA(())],
        compiler_params=_params(("arbitrary",)),
        name="moe_combine",
    )(*runs, lpos_nk, gate_nk, h2, ys, sg.astype(BF16), su.astype(BF16), sd.astype(BF16),
      g.reshape(1, D), beta.reshape(1, D))


FLASH_TQ = 512
FLASH_TK = 512
MOE_TILE = 256
RUN_UNROLL = 8
EXPERT_CAP = 1024
EXPERT_CHUNK = 128


def _moe_ln(h2, rw, rb, wg, wu, wd, layer, sg, su, sd, g, beta, alpha):
    N, D = h2.shape
    E = rw.shape[1]
    gate, lpos, cnt, tcnt, toff, tbase, ttot = _router(h2, rw, rb)
    n_tiles = ttot.shape[0]
    counts = cnt[:, 0].astype(jnp.int32)
    pstart = (jnp.cumsum(counts) - counts).astype(jnp.int32)
    n_slots = N * TOP_K + n_tiles * E
    per_tile = lambda a: a.reshape(n_tiles, 1, E)
    runs = (per_tile(toff), per_tile(tbase) + pstart[None, None, :], per_tile(tcnt), ttot)
    xs = _dispatch(h2, lpos, runs, n_slots)
    ys = _experts(xs, pstart, counts, wg, wu, wd, layer)
    return _combine(h2, lpos.T, gate.T, runs, ys, sg, su, sd, g, beta, alpha)


def _mla_proj_kernel(h_ref, wa_ref, qg_ref, kvg_ref, wqt_ref, wk_ref, wvt_ref, cos_ref, sin_ref,
                     cost_ref, sint_ref, qt_ref, k_ref, vt_ref, *, q_lora, kv_lora, heads, q_scale):
    x = h_ref[0].astype(BF16)
    a = jnp.dot(x, wa_ref[...], preferred_element_type=F32)

    def rms(v, g):
        return v * lax.rsqrt(jnp.mean(v * v, axis=-1, keepdims=True) + RMS_EPS) * g

    q_lat = rms(a[:, :q_lora], qg_ref[...]).astype(BF16)
    kv_lat = rms(a[:, q_lora:q_lora + kv_lora], kvg_ref[...]).astype(BF16)
    o = q_lora + kv_lora
    k_rope = (a[:, o:o + LANES] * cos_ref[...] + a[:, o + LANES:o + 2 * LANES] * sin_ref[...]).astype(BF16)
    qa = lax.dot_general(wqt_ref[...], q_lat, _NT, preferred_element_type=F32)
    kn = jnp.dot(kv_lat, wk_ref[...], preferred_element_type=F32)
    vt_ref[0] = lax.dot_general(wvt_ref[...], kv_lat, _NT, preferred_element_type=F32).astype(BF16)
    cost = cost_ref[...]
    sint = sint_ref[...]
    hd = heads * LANES
    for h in range(heads):
        r0 = slice(h * LANES, (h + 1) * LANES)
        r1 = slice(hd + h * LANES, hd + (h + 1) * LANES)
        r2 = slice(2 * hd + h * LANES, 2 * hd + (h + 1) * LANES)
        qt_ref[0, 2 * h * LANES:(2 * h + 1) * LANES, :] = (qa[r0] * q_scale).astype(BF16)
        qt_ref[0, (2 * h + 1) * LANES:(2 * h + 2) * LANES, :] = (
            (qa[r1] * cost + qa[r2] * sint) * q_scale).astype(BF16)
        k_ref[0, :, 2 * h * LANES:(2 * h + 1) * LANES] = kn[:, r0].astype(BF16)
        k_ref[0, :, (2 * h + 1) * LANES:(2 * h + 2) * LANES] = k_rope


def _rot_half(w):
    r = w.shape[-1] // 2
    return jnp.concatenate([-w[..., r:], w[..., :r]], axis=-1)


def _pad_lanes(w):
    return jnp.pad(w, [(0, 0)] * (w.ndim - 1) + [(0, LANES - w.shape[-1])])


FLASH_HEADROOM = 64.0


def _flash_kernel(qt_ref, k_ref, vt_ref, o_ref, p_ref, s_ref, mx_ref, m_ref, l_ref, acc_ref, *, tk):
    qt = qt_ref[0]
    nk = k_ref.shape[1] // tk
    assert nk % 2 == 0 and nk >= 2

    def scores(c):
        off = pl.multiple_of(c * tk, tk)
        return jnp.dot(k_ref[0, pl.ds(off, tk), :], qt, preferred_element_type=F32)

    s0 = scores(0)
    m0 = jnp.max(s0, axis=0, keepdims=True)
    p0 = jnp.exp2(s0 - m0)
    p_ref[0] = p0.astype(BF16)
    acc_ref[...] = jnp.zeros_like(acc_ref)

    def probs(c, slot, carry):
        l, top = carry
        s = scores(c)
        p = jnp.exp2(s - m0)
        p_ref[slot] = p.astype(BF16)
        return l + jnp.sum(p, axis=0, keepdims=True), jnp.maximum(top, jnp.max(s, axis=0, keepdims=True))

    def accumulate(c, slot):
        off = pl.multiple_of(c * tk, tk)
        acc_ref[...] += jnp.dot(vt_ref[0, :, pl.ds(off, tk)], p_ref[slot], preferred_element_type=F32)

    carry = (jnp.sum(p0, axis=0, keepdims=True), m0)
    for c in range(nk - 1):
        carry = probs(c + 1, (c + 1) % 2, carry)
        accumulate(c, c % 2)
    accumulate(nk - 1, (nk - 1) % 2)
    l, top = carry
    o_ref[0] = (acc_ref[...] / l).T.astype(o_ref.dtype)

    @pl.when(jnp.max(top - m0) > FLASH_HEADROOM)
    def _():
        _flash_online(qt, k_ref, vt_ref, o_ref, s_ref, mx_ref, m_ref, l_ref, acc_ref, tk)


def _flash_online(qt, k_ref, vt_ref, o_ref, s_ref, mx_ref, m_ref, l_ref, acc_ref, tk):
    nk = k_ref.shape[1] // tk
    assert nk % 2 == 0 and nk >= 2
    m_ref[...] = jnp.full_like(m_ref, -jnp.inf)
    l_ref[...] = jnp.zeros_like(l_ref)
    acc_ref[...] = jnp.zeros_like(acc_ref)

    def scores(c, slot):
        off = pl.multiple_of(c * tk, tk)
        s = jnp.dot(k_ref[0, pl.ds(off, tk), :], qt, preferred_element_type=F32)
        s_ref[slot] = s
        mx_ref[slot] = jnp.max(s, axis=0, keepdims=True)

    def update(c, slot):
        off = pl.multiple_of(c * tk, tk)
        vt = vt_ref[0, :, pl.ds(off, tk)]
        m_prev = m_ref[...]
        m_new = jnp.maximum(m_prev, mx_ref[slot])
        p = jnp.exp2(s_ref[slot] - m_new)
        a = jnp.exp2(m_prev - m_new)
        l_ref[...] = a * l_ref[...] + jnp.sum(p, axis=0, keepdims=True)
        acc_ref[...] = a * acc_ref[...] + jnp.dot(vt, p.astype(BF16), preferred_element_type=F32)
        m_ref[...] = m_new

    scores(0, 0)

    def body(i, carry):
        c = 2 * i
        scores(c + 1, 1)
        update(c, 0)
        scores(c + 2, 0)
        update(c + 1, 1)
        return carry

    lax.fori_loop(0, nk // 2 - 1, body, 0)
    scores(nk - 1, 1)
    update(nk - 2, 0)
    update(nk - 1, 1)
    o_ref[0] = (acc_ref[...] / l_ref[...]).T.astype(o_ref.dtype)


def _oproj_ln_kernel(o_ref, h_ref, w_ref, g_ref, beta_ref, out_ref, *, alpha):
    m = jnp.dot(o_ref[0], w_ref[...], preferred_element_type=F32)
    out_ref[0] = _layer_norm(alpha * h_ref[0] + m, g_ref[...], beta_ref[...])


def _mla_ln(h, w_a, qg, kvg, w_uq, w_uk, w_uv, w_o, g, beta, alpha):
    B, S, D = h.shape
    q_lora, heads, qk = w_uq.shape
    kv_lora, _, nope = w_uk.shape
    vd = w_uv.shape[-1]
    rope = qk - nope
    assert nope == LANES and vd == LANES and rope <= LANES and rope % 2 == 0
    kr = w_a[:, q_lora + kv_lora:]
    wa = jnp.concatenate([w_a[:, :q_lora + kv_lora], _pad_lanes(kr), _pad_lanes(_rot_half(kr))],
                         axis=1).astype(BF16)
    wq_r = w_uq[:, :, nope:]
    wq = jnp.concatenate([w_uq[:, :, :nope].reshape(q_lora, heads * nope),
                          _pad_lanes(wq_r).reshape(q_lora, heads * LANES),
                          _pad_lanes(_rot_half(wq_r)).reshape(q_lora, heads * LANES)], axis=1).astype(BF16)
    wk = w_uk.reshape(kv_lora, heads * nope).astype(BF16)
    wvt = w_uv.reshape(kv_lora, heads * vd).T.astype(BF16)
    wqt = wq.T
    pos = jnp.arange(S, dtype=F32)
    inv = 1.0 / (ROPE_THETA ** (jnp.arange(0, rope, 2, dtype=F32) / rope))
    ang = pos[:, None] * inv[None, :]
    cos = _pad_lanes(jnp.concatenate([jnp.cos(ang), jnp.cos(ang)], axis=1))
    sin = _pad_lanes(jnp.concatenate([jnp.sin(ang), jnp.sin(ang)], axis=1))
    q_scale = float(qk) ** -0.5 * math.log2(math.e)

    tp = _pick(S, 256)
    const = lambda a: pl.BlockSpec(a.shape, lambda b, i: (0,) * a.ndim)
    qg2, kvg2 = qg.reshape(1, q_lora), kvg.reshape(1, kv_lora)
    tok_major = pl.BlockSpec((tp, LANES), lambda b, i: (i, 0))
    feat_major = pl.BlockSpec((LANES, tp), lambda b, i: (0, i))
    qt, k, vt = pl.pallas_call(
        functools.partial(_mla_proj_kernel, q_lora=q_lora, kv_lora=kv_lora, heads=heads, q_scale=q_scale),
        grid=(B, S // tp),
        in_specs=[pl.BlockSpec((1, tp, D), lambda b, i: (b, i, 0)), const(wa), const(qg2), const(kvg2),
                  const(wqt), const(wk), const(wvt), tok_major, tok_major, feat_major, feat_major],
        out_specs=[pl.BlockSpec((1, 2 * heads * LANES, tp), lambda b, i: (b, 0, i)),
                   pl.BlockSpec((1, tp, 2 * heads * LANES), lambda b, i: (b, i, 0)),
                   pl.BlockSpec((1, heads * LANES, tp), lambda b, i: (b, 0, i))],
        out_shape=[jax.ShapeDtypeStruct((B, 2 * heads * LANES, S), BF16),
                   jax.ShapeDtypeStruct((B, S, 2 * heads * LANES), BF16),
                   jax.ShapeDtypeStruct((B, heads * LANES, S), BF16)],
        compiler_params=_params(("parallel", "parallel")),
        name="mla_proj",
    )(h, wa, qg2, kvg2, wqt, wk, wvt, cos, sin, cos.T, sin.T)

    tq = _pick(S, FLASH_TQ)
    tk = _pick(S, FLASH_TK)
    o = pl.pallas_call(
        functools.partial(_flash_kernel, tk=tk),
        grid=(B, heads, S // tq),
        in_specs=[pl.BlockSpec((1, 2 * LANES, tq), lambda b, hh, i: (b, hh, i)),
                  pl.BlockSpec((1, S, 2 * LANES), lambda b, hh, i: (b, 0, hh)),
                  pl.BlockSpec((1, LANES, S), lambda b, hh, i: (b, hh, 0))],
        out_specs=pl.BlockSpec((1, tq, LANES), lambda b, hh, i: (b, i, hh)),
        out_shape=jax.ShapeDtypeStruct((B, S, heads * LANES), BF16),
        scratch_shapes=[pltpu.VMEM((2, tk, tq), BF16), pltpu.VMEM((2, tk, tq), F32), pltpu.VMEM((2, 1, tq), F32),
                        pltpu.VMEM((1, tq), F32), pltpu.VMEM((1, tq), F32), pltpu.VMEM((LANES, tq), F32)],
        compiler_params=_params(("parallel", "parallel", "arbitrary")),
        name="mla_flash",
    )(qt, k, vt)

    to = _pick(S, 512)
    vec = pl.BlockSpec((1, D), lambda b, i: (0, 0))
    return pl.pallas_call(
        functools.partial(_oproj_ln_kernel, alpha=alpha),
        grid=(B, S // to),
        in_specs=[pl.BlockSpec((1, to, heads * vd), lambda b, i: (b, i, 0)),
                  pl.BlockSpec((1, to, D), lambda b, i: (b, i, 0)),
                  pl.BlockSpec((heads * vd, D), lambda b, i: (0, 0)), vec, vec],
        out_specs=pl.BlockSpec((1, to, D), lambda b, i: (b, i, 0)),
        out_shape=jax.ShapeDtypeStruct((B, S, D), F32),
        compiler_params=_params(("parallel", "parallel")),
        name="mla_oproj_ln",
    )(o, h, w_o.astype(BF16), g.reshape(1, D), beta.reshape(1, D))


def kernel(x, pool_w, pool_b, pool_scale, mla_w_a, mla_q_norm_g, mla_kv_norm_g, mla_w_uq, mla_w_uk,
           mla_w_uv, mla_w_o, ln_mix_g, ln_mix_b, router_w, router_b, exp_w_gate, exp_w_up, exp_w_down,
           sh_w_gate, sh_w_up, sh_w_down, ln_ffn_g, ln_ffn_b):
    B, S, D = x.shape
    depth = ln_mix_g.shape[0]
    alpha = (2 * depth) ** 0.25
    h = x
    for i in range(depth):
        j = i // 2
        if i % 2 == 0:
            h = _pool_ln(h, pool_w[j], pool_b[j], pool_scale[j], ln_mix_g[i], ln_mix_b[i], alpha)
        else:
            h = _mla_ln(h, mla_w_a[j], mla_q_norm_g[j], mla_kv_norm_g[j], mla_w_uq[j], mla_w_uk[j],
                        mla_w_uv[j], mla_w_o[j], ln_mix_g[i], ln_mix_b[i], alpha)
        h = _moe_ln(h.reshape(B * S, D), router_w[i], router_b[i], exp_w_gate, exp_w_up, exp_w_down, i,
                    sh_w_gate[i], sh_w_up[i], sh_w_down[i], ln_ffn_g[i], ln_ffn_b[i],
                    alpha).reshape(B, S, D)
    return h
```

```python
import functools
import math

import jax
import jax.numpy as jnp
from jax import lax
from jax.experimental import pallas as pl
from jax.experimental.pallas import tpu as pltpu

F32 = jnp.float32
BF16 = jnp.bfloat16

LN_EPS = 1e-5
RMS_EPS = 1e-6
POOL_WINDOWS = (2, 4, 8, 16)
POOL_HALO = 8
ROPE_THETA = 10000.0
TOP_K = 8
N_GROUPS = 8
TOPK_GROUPS = 4
ROUTED_SCALE = 2.5
LANES = 128
VMEM_LIMIT = 48 * 1024 * 1024

_NT = (((1,), (1,)), ((), ()))


def _pick(n, pref):
    t = min(pref, n)
    while n % t:
        t //= 2
    assert t >= 8 and n % t == 0, (n, pref)
    return t


def _layer_norm(z, g, b):
    mu = jnp.mean(z, axis=-1, keepdims=True)
    zc = z - mu
    var = jnp.mean(zc * zc, axis=-1, keepdims=True)
    return zc * lax.rsqrt(var + LN_EPS) * g + b


def _sigmoid(x):
    return 1.0 / (1.0 + jnp.exp(-x))


def _params(sem):
    return pltpu.CompilerParams(dimension_semantics=sem, vmem_limit_bytes=VMEM_LIMIT)


def _pool_ln_kernel(xc_ref, xp_ref, xn_ref, w_ref, b_ref, sc_ref, g_ref, beta_ref, o_ref, ext_ref,
                    *, tile, seq, alpha):
    i = pl.program_id(1)
    n_tiles = pl.num_programs(1)
    xc = xc_ref[0]
    d_model = xc.shape[-1]
    ch = d_model // len(POOL_WINDOWS)
    ext_ref[0:POOL_HALO, :] = jnp.where(i > 0, xp_ref[0], 0.0)
    ext_ref[POOL_HALO:POOL_HALO + tile, :] = xc
    ext_ref[POOL_HALO + tile:2 * POOL_HALO + tile, :] = jnp.where(i < n_tiles - 1, xn_ref[0], 0.0)
    pos = i * tile + lax.broadcasted_iota(jnp.int32, (tile, 1), 0)
    outs = []
    for g, win in enumerate(POOL_WINDOWS):
        half = win // 2
        cols = slice(g * ch, (g + 1) * ch)
        acc = ext_ref[POOL_HALO - half:POOL_HALO - half + tile, cols]
        for j in range(-half + 1, half):
            acc = acc + ext_ref[POOL_HALO + j:POOL_HALO + j + tile, cols]
        cnt = (jnp.minimum(pos + half, seq) - jnp.maximum(pos - half, 0)).astype(F32)
        pooled = acc / cnt - xc[:, cols]
        outs.append(jnp.dot(pooled.astype(BF16), w_ref[g], preferred_element_type=F32))
    y = jnp.concatenate(outs, axis=-1)
    y = (y + b_ref[...]) * sc_ref[...]
    o_ref[0] = _layer_norm(alpha * xc + y, g_ref[...], beta_ref[...])


def _pool_ln(x, w, b, sc, g, beta, alpha):
    B, S, D = x.shape
    tile = _pick(S, 512)
    nh = tile // POOL_HALO
    last = S // POOL_HALO - 1
    row = lambda v: v.reshape(1, D)
    vec = pl.BlockSpec((1, D), lambda bi, i: (0, 0))
    return pl.pallas_call(
        functools.partial(_pool_ln_kernel, tile=tile, seq=S, alpha=alpha),
        grid=(B, S // tile),
        in_specs=[
            pl.BlockSpec((1, tile, D), lambda bi, i: (bi, i, 0)),
            pl.BlockSpec((1, POOL_HALO, D), lambda bi, i: (bi, jnp.maximum(i * nh - 1, 0), 0)),
            pl.BlockSpec((1, POOL_HALO, D), lambda bi, i: (bi, jnp.minimum((i + 1) * nh, last), 0)),
            pl.BlockSpec(w.shape, lambda bi, i: (0, 0, 0)),
            vec, vec, vec, vec,
        ],
        out_specs=pl.BlockSpec((1, tile, D), lambda bi, i: (bi, i, 0)),
        out_shape=jax.ShapeDtypeStruct((B, S, D), F32),
        scratch_shapes=[pltpu.VMEM((tile + 2 * POOL_HALO, D), F32)],
        compiler_params=_params(("parallel", "parallel")),
        name="pool_ln",
    )(x, x, x, w.astype(BF16), row(b), row(sc), row(g), row(beta))


def _router_kernel(h_ref, whi_ref, wlo_ref, rb_ref, tri_ref, ltri_ref,
                   gate_ref, lpos_ref, cnt_ref, tcnt_ref, toff_ref, tbase_ref, ttot_ref):
    i = pl.program_id(0)

    @pl.when(i == 0)
    def _():
        cnt_ref[...] = jnp.zeros_like(cnt_ref)

    x = h_ref[...]
    xhi = x.astype(BF16)
    xlo = (x - xhi.astype(F32)).astype(BF16)
    whi = whi_ref[...]
    logits = (lax.dot_general(whi, xhi, _NT, preferred_element_type=F32)
              + lax.dot_general(whi, xlo, _NT, preferred_element_type=F32)
              + lax.dot_general(wlo_ref[...], xhi, _NT, preferred_element_type=F32))
    n_exp, tile = logits.shape
    per = n_exp // N_GROUPS
    scores = _sigmoid(logits)
    biased = scores + rb_ref[...]
    neg = -jnp.inf

    sub = lax.broadcasted_iota(jnp.int32, (per, tile), 0).astype(F32)
    gscore = []
    for g in range(N_GROUPS):
        bg = biased[g * per:(g + 1) * per, :]
        m1 = jnp.max(bg, axis=0, keepdims=True)
        i1 = jnp.min(jnp.where(bg == m1, sub, float(per)), axis=0, keepdims=True)
        m2 = jnp.max(jnp.where(sub == i1, neg, bg), axis=0, keepdims=True)
        gscore.append(m1 + m2)
    masked = []
    for g in range(N_GROUPS):
        beat = jnp.zeros_like(gscore[g])
        for o in range(N_GROUPS):
            if o == g:
                continue
            wins = (gscore[o] >= gscore[g]) if o < g else (gscore[o] > gscore[g])
            beat = beat + jnp.where(wins, 1.0, 0.0)
        keep = jnp.broadcast_to(beat < float(TOPK_GROUPS), (per, tile))
        masked.append(jnp.where(keep, biased[g * per:(g + 1) * per, :], neg))
    masked = jnp.concatenate(masked, axis=0)

    row = lax.broadcasted_iota(jnp.int32, (n_exp, tile), 0).astype(F32)
    chosen = jnp.zeros((n_exp, tile), F32)
    idxs, gates = [], []
    for _ in range(TOP_K):
        m = jnp.max(masked, axis=0, keepdims=True)
        idx = jnp.min(jnp.where(masked == m, row, float(n_exp)), axis=0, keepdims=True)
        hit = row == idx
        gates.append(jnp.sum(jnp.where(hit, scores, 0.0), axis=0, keepdims=True))
        masked = jnp.where(hit, neg, masked)
        chosen = jnp.where(hit, 1.0, chosen)
        idxs.append(idx)
    gsum = gates[0]
    for k in range(1, TOP_K):
        gsum = gsum + gates[k]

    earlier = jnp.dot(chosen.astype(BF16), tri_ref[...], preferred_element_type=F32)
    count = jnp.sum(chosen, axis=1, keepdims=True)
    run = count + (count - 2.0 * jnp.floor(0.5 * count))
    lower = jnp.dot(ltri_ref[...], jnp.broadcast_to(run, (n_exp, LANES)).astype(BF16),
                    preferred_element_type=F32)[:, :1]
    order = earlier + lower
    for k in range(TOP_K):
        gate_ref[k:k + 1, :] = gates[k] / gsum * ROUTED_SCALE
        lpos_k = jnp.sum(jnp.where(row == idxs[k], order, 0.0), axis=0, keepdims=True)
        lpos_ref[k:k + 1, :] = lpos_k.astype(jnp.int32)
    tcnt_ref[...] = run.astype(jnp.int32)
    toff_ref[...] = lower.astype(jnp.int32)
    tbase_ref[...] = cnt_ref[...].astype(jnp.int32)
    ttot_ref[...] = jnp.sum(run, axis=0, keepdims=True).astype(jnp.int32)
    cnt_ref[...] = cnt_ref[...] + run


def _router(h2, rw, rb):
    N, D = h2.shape
    E = rw.shape[1]
    tile = _pick(N, MOE_TILE)
    assert tile <= 256
    n_tiles = N // tile
    wt = rw.T
    whi = wt.astype(BF16)
    wlo = (wt - whi.astype(F32)).astype(BF16)
    ar = jnp.arange(tile)
    tri = (ar[:, None] < ar[None, :]).astype(BF16)
    ae = jnp.arange(E)
    ltri = (ae[None, :] < ae[:, None]).astype(BF16)
    const = lambda shape: pl.BlockSpec(shape, lambda i: (0, 0))
    kn = pl.BlockSpec((TOP_K, tile), lambda i: (0, i))
    table = pl.BlockSpec((None, E, 1), lambda i: (i, 0, 0))
    table_shape = jax.ShapeDtypeStruct((n_tiles, E, 1), jnp.int32)
    return pl.pallas_call(
        _router_kernel,
        grid=(n_tiles,),
        in_specs=[pl.BlockSpec((tile, D), lambda i: (i, 0)), const((E, D)), const((E, D)),
                  const((E, 1)), const((tile, tile)), const((E, E))],
        out_specs=[kn, kn, const((E, 1)), table, table, table,
                   pl.BlockSpec((None, 1, 1), lambda i: (i, 0, 0))],
        out_shape=[jax.ShapeDtypeStruct((TOP_K, N), F32), jax.ShapeDtypeStruct((TOP_K, N), jnp.int32),
                   jax.ShapeDtypeStruct((E, 1), F32), table_shape, table_shape, table_shape,
                   jax.ShapeDtypeStruct((n_tiles, 1, 1), jnp.int32)],
        compiler_params=_params(("arbitrary",)),
        name="moe_router",
    )(h2, whi, wlo, rb.reshape(E, 1), tri, ltri)


def _pack_pairs(x, rounded=False):
    half = x.shape[-1] // 2
    bits = lambda v: lax.bitcast_convert_type(v if rounded else v.astype(BF16).astype(F32), jnp.uint32)
    return (bits(x[:, :half]) & jnp.uint32(0xFFFF0000)) | (bits(x[:, half:]) >> 16)


def _unpack_pairs(w):
    hi = lax.bitcast_convert_type(w & jnp.uint32(0xFFFF0000), F32)
    lo = lax.bitcast_convert_type(w << 16, F32)
    return jnp.concatenate([hi.astype(BF16), lo.astype(BF16)], axis=-1)


def _to_slab(slab_ref, x, rounded=False):
    w = _pack_pairs(x, rounded)
    lines = w.shape[1] // LANES
    for a in range(lines):
        slab_ref[pl.ds(a, w.shape[0], stride=lines), :] = w[:, a * LANES:(a + 1) * LANES]


def _from_slab(slab_ref, rows, lines, valid, first=0):
    w = jnp.concatenate([slab_ref[pl.ds(first * lines + a, rows, stride=lines), :] for a in range(lines)],
                        axis=-1)
    r = first + lax.broadcasted_iota(jnp.int32, (rows, 1), 0)
    return _unpack_pairs(jnp.where(r < valid, w, jnp.uint32(0)))


def _start_runs(toff_ref, gst_ref, tcnt_ref, local, remote, sem, lines, to_remote):
    def body(e, c):
        cnt, off, gst = tcnt_ref[0, 0, e], toff_ref[0, 0, e], gst_ref[0, 0, e]

        @pl.when(cnt > 0)
        def _():
            loc = local.at[pl.ds(pl.multiple_of(off * lines, 8), cnt * lines), :]
            rem = remote.at[pl.ds(pl.multiple_of(gst * lines, 8), cnt * lines), :]
            (pltpu.make_async_copy(loc, rem, sem) if to_remote else pltpu.make_async_copy(rem, loc, sem)).start()
        return c
    lax.fori_loop(0, tcnt_ref.shape[2], body, 0, unroll=RUN_UNROLL)


def _wait_all_runs(slab, remote, sem, total_lines):
    n = pl.multiple_of(total_lines, 8)
    pltpu.make_async_copy(slab.at[pl.ds(0, n), :], remote.at[pl.ds(0, n), :], sem).wait()


def _sorted_rows(tile, n_exp):
    return TOP_K * tile + n_exp


def _dispatch_kernel(toff_ref, gst_ref, tcnt_ref, ttot_ref, tprev_ref, lpos_ref, h_ref, xs_ref, slabs, sems,
                     *, lines):
    i = pl.program_id(0)
    slot = lax.rem(i, 2)
    slab = slabs.at[slot]
    tile = h_ref.shape[0]
    rows = slab.shape[0] // lines
    srow = lax.broadcasted_iota(jnp.int32, (rows, tile), 0)
    perm = jnp.zeros((rows, tile), F32)
    for k in range(TOP_K):
        perm = jnp.where(srow == lpos_ref[k:k + 1, :], 1.0, perm)
    _to_slab(slab, jnp.dot(perm.astype(BF16), h_ref[...].astype(BF16), preferred_element_type=F32),
             rounded=True)
    _start_runs(toff_ref, gst_ref, tcnt_ref, slab, xs_ref, sems.at[slot], lines, to_remote=True)

    @pl.when(i > 0)
    def _():
        _wait_all_runs(slabs.at[1 - slot], xs_ref, sems.at[1 - slot], tprev_ref[0, 0, 0] * lines)

    @pl.when(i == pl.num_programs(0) - 1)
    def _():
        _wait_all_runs(slab, xs_ref, sems.at[slot], ttot_ref[0, 0, 0] * lines)


def _run_specs(n_exp):
    smem = lambda n: pl.BlockSpec((1, 1, n), lambda i: (i, 0, 0), memory_space=pltpu.SMEM)
    return [smem(n_exp), smem(n_exp), smem(n_exp), smem(1)]


def _slab_lines(d_model):
    lines = d_model // (2 * LANES)
    assert lines * 2 * LANES == d_model and (2 * lines) % 8 == 0
    return lines


def _dispatch(h2, lpos, runs, n_slots):
    N, D = h2.shape
    E = runs[0].shape[2]
    tile = _pick(N, MOE_TILE)
    lines = _slab_lines(D)
    return pl.pallas_call(
        functools.partial(_dispatch_kernel, lines=lines),
        grid=(N // tile,),
        in_specs=_run_specs(E) + [
            pl.BlockSpec((1, 1, 1), lambda i: (jnp.maximum(i - 1, 0), 0, 0), memory_space=pltpu.SMEM),
            pl.BlockSpec((TOP_K, tile), lambda i: (0, i)), pl.BlockSpec((tile, D), lambda i: (i, 0))],
        out_specs=pl.BlockSpec(memory_space=pl.ANY),
        out_shape=jax.ShapeDtypeStruct((n_slots * lines, LANES), jnp.uint32),
        scratch_shapes=[pltpu.VMEM((2, _sorted_rows(tile, E) * lines, LANES), jnp.uint32),
                        pltpu.SemaphoreType.DMA((2,))],
        compiler_params=_params(("arbitrary",)),
        name="moe_dispatch",
    )(*runs, runs[3], lpos, h2)


def _expert_kernel(pstart_ref, cnt_ref, xs_hbm, wg_hbm, wu_hbm, wd_hbm, ys_hbm,
                   xbuf, ybuf, xo, yo, wg_f, wu_f, wd_f, wg_s, wu_s, wd_s, wsem, xsem, ysem, osem,
                   *, layer, lines, cap, chunk):
    e = pl.program_id(0)
    last = pl.num_programs(0) - 1
    slot = lax.rem(e, 2)

    def weight_copies(ex, sl):
        return [pltpu.make_async_copy(w.at[layer, ex], f.at[sl], wsem.at[sl])
                for w, f in ((wg_hbm, wg_f), (wu_hbm, wu_f), (wd_hbm, wd_f))]

    def head_lines(ex):
        return jnp.minimum(cnt_ref[ex], cap) * lines

    def first_line(ex, row=0):
        return pl.multiple_of((pstart_ref[ex] + row) * lines, 8)

    def x_copy(ex, sl):
        n = head_lines(ex)
        return pltpu.make_async_copy(xs_hbm.at[pl.ds(first_line(ex), n), :], xbuf.at[sl, pl.ds(0, n), :],
                                     xsem.at[sl])

    def y_copy(ex, sl):
        n = head_lines(ex)
        return pltpu.make_async_copy(ybuf.at[sl, pl.ds(0, n), :], ys_hbm.at[pl.ds(first_line(ex), n), :],
                                     ysem.at[sl])

    @pl.when(e == 0)
    def _():
        for cp in weight_copies(0, 0):
            cp.start()

        @pl.when(cnt_ref[0] > 0)
        def _():
            x_copy(0, 0).start()

    for cp in weight_copies(e, slot):
        cp.wait()
    nxt = jnp.minimum(e + 1, last)

    @pl.when(e < last)
    def _():
        for cp in weight_copies(nxt, 1 - slot):
            cp.start()

        @pl.when(cnt_ref[nxt] > 0)
        def _():
            x_copy(nxt, 1 - slot).start()

    wg_s[...] = wg_f[slot].astype(BF16)
    wu_s[...] = wu_f[slot].astype(BF16)
    wd_s[...] = wd_f[slot].astype(BF16)

    before = jnp.maximum(e - 2, 0)

    @pl.when(jnp.logical_and(e >= 2, cnt_ref[before] > 0))
    def _():
        y_copy(before, slot).wait()

    def ffn(x_ref, rows, valid):
        x = _from_slab(x_ref, rows, lines, valid)
        hg = jnp.dot(x, wg_s[...], preferred_element_type=F32)
        hu = jnp.dot(x, wu_s[...], preferred_element_type=F32)
        hb = hg * _sigmoid(hg) * hu
        return jnp.dot(hb.astype(BF16), wd_s[...], preferred_element_type=F32)

    count = cnt_ref[e]

    @pl.when(count > 0)
    def _():
        x_copy(e, slot).wait()
        head = jnp.minimum(count, cap)
        lower = 0
        for rows in range(chunk, cap + 1, chunk):
            @pl.when(jnp.logical_and(head > lower, head <= rows))
            def _(rows=rows):
                _to_slab(ybuf.at[slot], ffn(xbuf.at[slot], rows, head))
            lower = rows
        y_copy(e, slot).start()

        @pl.when(count > cap)
        def _():
            def extra(j, carry):
                row = cap + j * chunk
                n = jnp.minimum(chunk, count - row) * lines
                fetch = pltpu.make_async_copy(xs_hbm.at[pl.ds(first_line(e, row), n), :],
                                              xo.at[pl.ds(0, n), :], osem)
                fetch.start()
                fetch.wait()
                _to_slab(yo, ffn(xo, chunk, count - row))
                store = pltpu.make_async_copy(yo.at[pl.ds(0, n), :],
                                              ys_hbm.at[pl.ds(first_line(e, row), n), :], osem)
                store.start()
                store.wait()
                return carry
            lax.fori_loop(0, (count - cap + chunk - 1) // chunk, extra, 0)

    @pl.when(e == last)
    def _():
        @pl.when(count > 0)
        def _():
            y_copy(e, slot).wait()
        prev = jnp.maximum(e - 1, 0)

        @pl.when(jnp.logical_and(e >= 1, cnt_ref[prev] > 0))
        def _():
            y_copy(prev, 1 - slot).wait()


def _experts(xs, pstart, counts, wg, wu, wd, layer):
    E, D, Fe = wg.shape[-3:]
    lines = _slab_lines(D)
    cap, chunk = EXPERT_CAP, EXPERT_CHUNK
    assert cap % chunk == 0 and (chunk * lines) % 8 == 0
    hbm = pl.BlockSpec(memory_space=pl.ANY)
    slab = lambda rows: pltpu.VMEM(rows, jnp.uint32)
    return pl.pallas_call(
        functools.partial(_expert_kernel, layer=layer, lines=lines, cap=cap, chunk=chunk),
        grid_spec=pltpu.PrefetchScalarGridSpec(
            num_scalar_prefetch=2,
            grid=(E,),
            in_specs=[hbm, hbm, hbm, hbm],
            out_specs=hbm,
            scratch_shapes=[slab((2, cap * lines, LANES)), slab((2, cap * lines, LANES)),
                            slab((chunk * lines, LANES)), slab((chunk * lines, LANES)),
                            pltpu.VMEM((2, D, Fe), F32), pltpu.VMEM((2, D, Fe), F32),
                            pltpu.VMEM((2, Fe, D), F32),
                            pltpu.VMEM((D, Fe), BF16), pltpu.VMEM((D, Fe), BF16), pltpu.VMEM((Fe, D), BF16),
                            pltpu.SemaphoreType.DMA((2,)), pltpu.SemaphoreType.DMA((2,)),
                            pltpu.SemaphoreType.DMA((2,)), pltpu.SemaphoreType.DMA(())],
        ),
        out_shape=jax.ShapeDtypeStruct(xs.shape, jnp.uint32),
        compiler_params=_params(("arbitrary",)),
        name="moe_experts",
    )(pstart, counts, xs, wg, wu, wd)


def _combine_kernel(toff_ref, gst_ref, tcnt_ref, ttot_ref, ntoff_ref, ngst_ref, ntcnt_ref, lpos_ref, gate_ref,
                    h_ref, ys_ref, sg_ref, su_ref, sd_ref, g_ref, beta_ref, o_ref, slabs, sems, *, lines, alpha):
    i = pl.program_id(0)
    slot = lax.rem(i, 2)
    slab, sem = slabs.at[slot], sems.at[slot]

    @pl.when(i == 0)
    def _():
        _start_runs(toff_ref, gst_ref, tcnt_ref, slab, ys_ref, sem, lines, to_remote=False)

    @pl.when(i < pl.num_programs(0) - 1)
    def _():
        _start_runs(ntoff_ref, ngst_ref, ntcnt_ref, slabs.at[1 - slot], ys_ref, sems.at[1 - slot], lines,
                    to_remote=False)

    h = h_ref[...]
    tile = h.shape[0]
    rows = slab.shape[0] // lines
    hb = h.astype(BF16)
    sgate = jnp.dot(hb, sg_ref[...], preferred_element_type=F32)
    sup = jnp.dot(hb, su_ref[...], preferred_element_type=F32)
    mid = sgate * _sigmoid(sgate) * sup
    acc = jnp.dot(mid.astype(BF16), sd_ref[...], preferred_element_type=F32)
    total = ttot_ref[0, 0, 0]
    _wait_all_runs(slab, ys_ref, sem, total * lines)
    chunk = max(c for c in range(8, 257, 8) if rows % c == 0)
    for first in range(0, rows, chunk):
        scol = first + lax.broadcasted_iota(jnp.int32, (tile, chunk), 1)
        weights = jnp.zeros((tile, chunk), F32)
        for k in range(TOP_K):
            weights = jnp.where(scol == lpos_ref[:, k:k + 1], gate_ref[:, k:k + 1], weights)
        acc = acc + jnp.dot(weights.astype(BF16), _from_slab(slab, chunk, lines, total, first),
                            preferred_element_type=F32)
    o_ref[...] = _layer_norm(alpha * h + acc, g_ref[...], beta_ref[...])


def _combine(h2, lpos_nk, gate_nk, runs, ys, sg, su, sd, g, beta, alpha):
    N, D = h2.shape
    Fs = sg.shape[1]
    E = runs[0].shape[2]
    tile = _pick(N, MOE_TILE)
    lines = _slab_lines(D)
    const = lambda shape: pl.BlockSpec(shape, lambda i: (0, 0))
    nk = pl.BlockSpec((tile, TOP_K), lambda i: (i, 0))
    last = N // tile - 1
    following = pl.BlockSpec((1, 1, E), lambda i: (jnp.minimum(i + 1, last), 0, 0), memory_space=pltpu.SMEM)
    return pl.pallas_call(
        functools.partial(_combine_kernel, lines=lines, alpha=alpha),
        grid=(N // tile,),
        in_specs=_run_specs(E) + [following] * 3 + [
            nk, nk, pl.BlockSpec((tile, D), lambda i: (i, 0)), pl.BlockSpec(memory_space=pl.ANY),
            const((D, Fs)), const((D, Fs)), const((Fs, D)), const((1, D)), const((1, D))],
        out_specs=pl.BlockSpec((tile, D), lambda i: (i, 0)),
        out_shape=jax.ShapeDtypeStruct((N, D), F32),
        scratch_shapes=[pltpu.VMEM((2, _sorted_rows(tile, E) * lines, LANES), jnp.uint32),
                        pltpu.SemaphoreType.DMA((2,))],
        compiler_params=_params(("arbitrary",)),
        name="moe_combine",
    )(*runs, *runs[:3], lpos_nk, gate_nk, h2, ys, sg.astype(BF16), su.astype(BF16), sd.astype(BF16),
      g.reshape(1, D), beta.reshape(1, D))


FLASH_TQ = 512
FLASH_TK = 512
MOE_TILE = 256
RUN_UNROLL = 8
EXPERT_CAP = 1024
EXPERT_CHUNK = 128


def _moe_ln(h2, rw, rb, wg, wu, wd, layer, sg, su, sd, g, beta, alpha):
    N, D = h2.shape
    E = rw.shape[1]
    gate, lpos, cnt, tcnt, toff, tbase, ttot = _router(h2, rw, rb)
    n_tiles = ttot.shape[0]
    counts = cnt[:, 0].astype(jnp.int32)
    pstart = (jnp.cumsum(counts) - counts).astype(jnp.int32)
    n_slots = N * TOP_K + n_tiles * E
    per_tile = lambda a: a.reshape(n_tiles, 1, E)
    runs = (per_tile(toff), per_tile(tbase) + pstart[None, None, :], per_tile(tcnt), ttot)
    xs = _dispatch(h2, lpos, runs, n_slots)
    ys = _experts(xs, pstart, counts, wg, wu, wd, layer)
    return _combine(h2, lpos.T, gate.T, runs, ys, sg, su, sd, g, beta, alpha)


def _mla_proj_kernel(h_ref, wa_ref, qg_ref, kvg_ref, wqt_ref, wk_ref, wvt_ref, cos_ref, sin_ref,
                     cost_ref, sint_ref, qt_ref, k_ref, vt_ref, *, q_lora, kv_lora, heads, q_scale):
    x = h_ref[0].astype(BF16)
    a = jnp.dot(x, wa_ref[...], preferred_element_type=F32)

    def rms(v, g):
        return v * lax.rsqrt(jnp.mean(v * v, axis=-1, keepdims=True) + RMS_EPS) * g

    q_lat = rms(a[:, :q_lora], qg_ref[...]).astype(BF16)
    kv_lat = rms(a[:, q_lora:q_lora + kv_lora], kvg_ref[...]).astype(BF16)
    o = q_lora + kv_lora
    k_rope = (a[:, o:o + LANES] * cos_ref[...] + a[:, o + LANES:o + 2 * LANES] * sin_ref[...]).astype(BF16)
    qa = lax.dot_general(wqt_ref[...], q_lat, _NT, preferred_element_type=F32)
    kn = jnp.dot(kv_lat, wk_ref[...], preferred_element_type=F32)
    vt_ref[0] = lax.dot_general(wvt_ref[...], kv_lat, _NT, preferred_element_type=F32).astype(BF16)
    cost = cost_ref[...]
    sint = sint_ref[...]
    hd = heads * LANES
    for h in range(heads):
        r0 = slice(h * LANES, (h + 1) * LANES)
        r1 = slice(hd + h * LANES, hd + (h + 1) * LANES)
        r2 = slice(2 * hd + h * LANES, 2 * hd + (h + 1) * LANES)
        qt_ref[0, 2 * h * LANES:(2 * h + 1) * LANES, :] = (qa[r0] * q_scale).astype(BF16)
        qt_ref[0, (2 * h + 1) * LANES:(2 * h + 2) * LANES, :] = (
            (qa[r1] * cost + qa[r2] * sint) * q_scale).astype(BF16)
        k_ref[0, :, 2 * h * LANES:(2 * h + 1) * LANES] = kn[:, r0].astype(BF16)
        k_ref[0, :, (2 * h + 1) * LANES:(2 * h + 2) * LANES] = k_rope


def _rot_half(w):
    r = w.shape[-1] // 2
    return jnp.concatenate([-w[..., r:], w[..., :r]], axis=-1)


def _pad_lanes(w):
    return jnp.pad(w, [(0, 0)] * (w.ndim - 1) + [(0, LANES - w.shape[-1])])


FLASH_HEADROOM = 64.0


def _flash_kernel(qt_ref, k_ref, vt_ref, o_ref, p_ref, s_ref, mx_ref, m_ref, l_ref, acc_ref, *, tk):
    qt = qt_ref[0]
    nk = k_ref.shape[1] // tk
    assert nk % 2 == 0 and nk >= 2

    def scores(c):
        off = pl.multiple_of(c * tk, tk)
        return jnp.dot(k_ref[0, pl.ds(off, tk), :], qt, preferred_element_type=F32)

    s0 = scores(0)
    m0 = jnp.max(s0, axis=0, keepdims=True)
    p0 = jnp.exp2(s0 - m0)
    p_ref[0] = p0.astype(BF16)
    acc_ref[...] = jnp.zeros_like(acc_ref)

    def probs(c, slot, carry):
        l, top = carry
        s = scores(c)
        p = jnp.exp2(s - m0)
        p_ref[slot] = p.astype(BF16)
        return l + jnp.sum(p, axis=0, keepdims=True), jnp.maximum(top, jnp.max(s, axis=0, keepdims=True))

    def accumulate(c, slot):
        off = pl.multiple_of(c * tk, tk)
        acc_ref[...] += jnp.dot(vt_ref[0, :, pl.ds(off, tk)], p_ref[slot], preferred_element_type=F32)

    carry = (jnp.sum(p0, axis=0, keepdims=True), m0)
    for c in range(nk - 1):
        carry = probs(c + 1, (c + 1) % 2, carry)
        accumulate(c, c % 2)
    accumulate(nk - 1, (nk - 1) % 2)
    l, top = carry
    o_ref[0] = (acc_ref[...] / l).T.astype(o_ref.dtype)

    @pl.when(jnp.max(top - m0) > FLASH_HEADROOM)
    def _():
        _flash_online(qt, k_ref, vt_ref, o_ref, s_ref, mx_ref, m_ref, l_ref, acc_ref, tk)


def _flash_online(qt, k_ref, vt_ref, o_ref, s_ref, mx_ref, m_ref, l_ref, acc_ref, tk):
    nk = k_ref.shape[1] // tk
    assert nk % 2 == 0 and nk >= 2
    m_ref[...] = jnp.full_like(m_ref, -jnp.inf)
    l_ref[...] = jnp.zeros_like(l_ref)
    acc_ref[...] = jnp.zeros_like(acc_ref)

    def scores(c, slot):
        off = pl.multiple_of(c * tk, tk)
        s = jnp.dot(k_ref[0, pl.ds(off, tk), :], qt, preferred_element_type=F32)
        s_ref[slot] = s
        mx_ref[slot] = jnp.max(s, axis=0, keepdims=True)

    def update(c, slot):
        off = pl.multiple_of(c * tk, tk)
        vt = vt_ref[0, :, pl.ds(off, tk)]
        m_prev = m_ref[...]
        m_new = jnp.maximum(m_prev, mx_ref[slot])
        p = jnp.exp2(s_ref[slot] - m_new)
        a = jnp.exp2(m_prev - m_new)
        l_ref[...] = a * l_ref[...] + jnp.sum(p, axis=0, keepdims=True)
        acc_ref[...] = a * acc_ref[...] + jnp.dot(vt, p.astype(BF16), preferred_element_type=F32)
        m_ref[...] = m_new

    scores(0, 0)

    def body(i, carry):
        c = 2 * i
        scores(c + 1, 1)
        update(c, 0)
        scores(c + 2, 0)
        update(c + 1, 1)
        return carry

    lax.fori_loop(0, nk // 2 - 1, body, 0)
    scores(nk - 1, 1)
    update(nk - 2, 0)
    update(nk - 1, 1)
    o_ref[0] = (acc_ref[...] / l_ref[...]).T.astype(o_ref.dtype)


def _oproj_ln_kernel(o_ref, h_ref, w_ref, g_ref, beta_ref, out_ref, *, alpha):
    m = jnp.dot(o_ref[0], w_ref[...], preferred_element_type=F32)
    out_ref[0] = _layer_norm(alpha * h_ref[0] + m, g_ref[...], beta_ref[...])


def _mla_ln(h, w_a, qg, kvg, w_uq, w_uk, w_uv, w_o, g, beta, alpha):
    B, S, D = h.shape
    q_lora, heads, qk = w_uq.shape
    kv_lora, _, nope = w_uk.shape
    vd = w_uv.shape[-1]
    rope = qk - nope
    assert nope == LANES and vd == LANES and rope <= LANES and rope % 2 == 0
    kr = w_a[:, q_lora + kv_lora:]
    wa = jnp.concatenate([w_a[:, :q_lora + kv_lora], _pad_lanes(kr), _pad_lanes(_rot_half(kr))],
                         axis=1).astype(BF16)
    wq_r = w_uq[:, :, nope:]
    wq = jnp.concatenate([w_uq[:, :, :nope].reshape(q_lora, heads * nope),
                          _pad_lanes(wq_r).reshape(q_lora, heads * LANES),
                          _pad_lanes(_rot_half(wq_r)).reshape(q_lora, heads * LANES)], axis=1).astype(BF16)
    wk = w_uk.reshape(kv_lora, heads * nope).astype(BF16)
    wvt = w_uv.reshape(kv_lora, heads * vd).T.astype(BF16)
    wqt = wq.T
    pos = jnp.arange(S, dtype=F32)
    inv = 1.0 / (ROPE_THETA ** (jnp.arange(0, rope, 2, dtype=F32) / rope))
    ang = pos[:, None] * inv[None, :]
    cos = _pad_lanes(jnp.concatenate([jnp.cos(ang), jnp.cos(ang)], axis=1))
    sin = _pad_lanes(jnp.concatenate([jnp.sin(ang), jnp.sin(ang)], axis=1))
    q_scale = float(qk) ** -0.5 * math.log2(math.e)

    tp = _pick(S, 256)
    const = lambda a: pl.BlockSpec(a.shape, lambda b, i: (0,) * a.ndim)
    qg2, kvg2 = qg.reshape(1, q_lora), kvg.reshape(1, kv_lora)
    tok_major = pl.BlockSpec((tp, LANES), lambda b, i: (i, 0))
    feat_major = pl.BlockSpec((LANES, tp), lambda b, i: (0, i))
    qt, k, vt = pl.pallas_call(
        functools.partial(_mla_proj_kernel, q_lora=q_lora, kv_lora=kv_lora, heads=heads, q_scale=q_scale),
        grid=(B, S // tp),
        in_specs=[pl.BlockSpec((1, tp, D), lambda b, i: (b, i, 0)), const(wa), const(qg2), const(kvg2),
                  const(wqt), const(wk), const(wvt), tok_major, tok_major, feat_major, feat_major],
        out_specs=[pl.BlockSpec((1, 2 * heads * LANES, tp), lambda b, i: (b, 0, i)),
                   pl.BlockSpec((1, tp, 2 * heads * LANES), lambda b, i: (b, i, 0)),
                   pl.BlockSpec((1, heads * LANES, tp), lambda b, i: (b, 0, i))],
        out_shape=[jax.ShapeDtypeStruct((B, 2 * heads * LANES, S), BF16),
                   jax.ShapeDtypeStruct((B, S, 2 * heads * LANES), BF16),
                   jax.ShapeDtypeStruct((B, heads * LANES, S), BF16)],
        compiler_params=_params(("parallel", "parallel")),
        name="mla_proj",
    )(h, wa, qg2, kvg2, wqt, wk, wvt, cos, sin, cos.T, sin.T)

    tq = _pick(S, FLASH_TQ)
    tk = _pick(S, FLASH_TK)
    o = pl.pallas_call(
        functools.partial(_flash_kernel, tk=tk),
        grid=(B, heads, S // tq),
        in_specs=[pl.BlockSpec((1, 2 * LANES, tq), lambda b, hh, i: (b, hh, i)),
                  pl.BlockSpec((1, S, 2 * LANES), lambda b, hh, i: (b, 0, hh)),
                  pl.BlockSpec((1, LANES, S), lambda b, hh, i: (b, hh, 0))],
        out_specs=pl.BlockSpec((1, tq, LANES), lambda b, hh, i: (b, i, hh)),
        out_shape=jax.ShapeDtypeStruct((B, S, heads * LANES), BF16),
        scratch_shapes=[pltpu.VMEM((2, tk, tq), BF16), pltpu.VMEM((2, tk, tq), F32), pltpu.VMEM((2, 1, tq), F32),
                        pltpu.VMEM((1, tq), F32), pltpu.VMEM((1, tq), F32), pltpu.VMEM((LANES, tq), F32)],
        compiler_params=_params(("parallel", "parallel", "arbitrary")),
        name="mla_flash",
    )(qt, k, vt)

    to = _pick(S, 512)
    vec = pl.BlockSpec((1, D), lambda b, i: (0, 0))
    return pl.pallas_call(
        functools.partial(_oproj_ln_kernel, alpha=alpha),
        grid=(B, S // to),
        in_specs=[pl.BlockSpec((1, to, heads * vd), lambda b, i: (b, i, 0)),
                  pl.BlockSpec((1, to, D), lambda b, i: (b, i, 0)),
                  pl.BlockSpec((heads * vd, D), lambda b, i: (0, 0)), vec, vec],
        out_specs=pl.BlockSpec((1, to, D), lambda b, i: (b, i, 0)),
        out_shape=jax.ShapeDtypeStruct((B, S, D), F32),
        compiler_params=_params(("parallel", "parallel")),
        name="mla_oproj_ln",
    )(o, h, w_o.astype(BF16), g.reshape(1, D), beta.reshape(1, D))


def kernel(x, pool_w, pool_b, pool_scale, mla_w_a, mla_q_norm_g, mla_kv_norm_g, mla_w_uq, mla_w_uk,
           mla_w_uv, mla_w_o, ln_mix_g, ln_mix_b, router_w, router_b, exp_w_gate, exp_w_up, exp_w_down,
           sh_w_gate, sh_w_up, sh_w_down, ln_ffn_g, ln_ffn_b):
    B, S, D = x.shape
    depth = ln_mix_g.shape[0]
    alpha = (2 * depth) ** 0.25
    h = x
    for i in range(depth):
        j = i // 2
        if i % 2 == 0:
            h = _pool_ln(h, pool_w[j], pool_b[j], pool_scale[j], ln_mix_g[i], ln_mix_b[i], alpha)
        else:
            h = _mla_ln(h, mla_w_a[j], mla_q_norm_g[j], mla_kv_norm_g[j], mla_w_uq[j], mla_w_uk[j],
                        mla_w_uv[j], mla_w_o[j], ln_mix_g[i], ln_mix_b[i], alpha)
        h = _moe_ln(h.reshape(B * S, D), router_w[i], router_b[i], exp_w_gate, exp_w_up, exp_w_down, i,
                    sh_w_gate[i], sh_w_up[i], sh_w_down[i], ln_ffn_g[i], ln_ffn_b[i],
                    alpha).reshape(B, S, D)
    return h
```

```python
import functools
import math

import jax
import jax.numpy as jnp
from jax import lax
from jax.experimental import pallas as pl
from jax.experimental.pallas import tpu as pltpu

F32 = jnp.float32
BF16 = jnp.bfloat16

LN_EPS = 1e-5
RMS_EPS = 1e-6
POOL_WINDOWS = (2, 4, 8, 16)
POOL_HALO = 8
ROPE_THETA = 10000.0
TOP_K = 8
N_GROUPS = 8
TOPK_GROUPS = 4
ROUTED_SCALE = 2.5
LANES = 128
VMEM_LIMIT = 48 * 1024 * 1024

_NT = (((1,), (1,)), ((), ()))


def _pick(n, pref):
    t = min(pref, n)
    while n % t:
        t //= 2
    assert t >= 8 and n % t == 0, (n, pref)
    return t


def _layer_norm(z, g, b):
    mu = jnp.mean(z, axis=-1, keepdims=True)
    zc = z - mu
    var = jnp.mean(zc * zc, axis=-1, keepdims=True)
    return zc * lax.rsqrt(var + LN_EPS) * g + b


def _sigmoid(x):
    return 1.0 / (1.0 + jnp.exp(-x))


def _params(sem):
    return pltpu.CompilerParams(dimension_semantics=sem, vmem_limit_bytes=VMEM_LIMIT)


def _pool_ln_kernel(xc_ref, xp_ref, xn_ref, w_ref, b_ref, sc_ref, g_ref, beta_ref, o_ref, ext_ref,
                    *, tile, seq, alpha):
    i = pl.program_id(1)
    n_tiles = pl.num_programs(1)
    xc = xc_ref[0]
    d_model = xc.shape[-1]
    ch = d_model // len(POOL_WINDOWS)
    ext_ref[0:POOL_HALO, :] = jnp.where(i > 0, xp_ref[0], 0.0)
    ext_ref[POOL_HALO:POOL_HALO + tile, :] = xc
    ext_ref[POOL_HALO + tile:2 * POOL_HALO + tile, :] = jnp.where(i < n_tiles - 1, xn_ref[0], 0.0)
    pos = i * tile + lax.broadcasted_iota(jnp.int32, (tile, 1), 0)
    outs = []
    for g, win in enumerate(POOL_WINDOWS):
        half = win // 2
        cols = slice(g * ch, (g + 1) * ch)
        acc = ext_ref[POOL_HALO - half:POOL_HALO - half + tile, cols]
        for j in range(-half + 1, half):
            acc = acc + ext_ref[POOL_HALO + j:POOL_HALO + j + tile, cols]
        cnt = (jnp.minimum(pos + half, seq) - jnp.maximum(pos - half, 0)).astype(F32)
        pooled = acc / cnt - xc[:, cols]
        outs.append(jnp.dot(pooled.astype(BF16), w_ref[g], preferred_element_type=F32))
    y = jnp.concatenate(outs, axis=-1)
    y = (y + b_ref[...]) * sc_ref[...]
    o_ref[0] = _layer_norm(alpha * xc + y, g_ref[...], beta_ref[...])


def _pool_ln(x, w, b, sc, g, beta, alpha):
    B, S, D = x.shape
    tile = _pick(S, 512)
    nh = tile // POOL_HALO
    last = S // POOL_HALO - 1
    row = lambda v: v.reshape(1, D)
    vec = pl.BlockSpec((1, D), lambda bi, i: (0, 0))
    return pl.pallas_call(
        functools.partial(_pool_ln_kernel, tile=tile, seq=S, alpha=alpha),
        grid=(B, S // tile),
        in_specs=[
            pl.BlockSpec((1, tile, D), lambda bi, i: (bi, i, 0)),
            pl.BlockSpec((1, POOL_HALO, D), lambda bi, i: (bi, jnp.maximum(i * nh - 1, 0), 0)),
            pl.BlockSpec((1, POOL_HALO, D), lambda bi, i: (bi, jnp.minimum((i + 1) * nh, last), 0)),
            pl.BlockSpec(w.shape, lambda bi, i: (0, 0, 0)),
            vec, vec, vec, vec,
        ],
        out_specs=pl.BlockSpec((1, tile, D), lambda bi, i: (bi, i, 0)),
        out_shape=jax.ShapeDtypeStruct((B, S, D), F32),
        scratch_shapes=[pltpu.VMEM((tile + 2 * POOL_HALO, D), F32)],
        compiler_params=_params(("parallel", "parallel")),
        name="pool_ln",
    )(x, x, x, w.astype(BF16), row(b), row(sc), row(g), row(beta))


def _router_kernel(h_ref, whi_ref, wlo_ref, rb_ref, tri_ref, ltri_ref,
                   gate_ref, lpos_ref, cnt_ref, tcnt_ref, toff_ref, tbase_ref, ttot_ref):
    i = pl.program_id(0)

    @pl.when(i == 0)
    def _():
        cnt_ref[...] = jnp.zeros_like(cnt_ref)

    x = h_ref[...]
    xhi = x.astype(BF16)
    xlo = (x - xhi.astype(F32)).astype(BF16)
    whi = whi_ref[...]
    logits = (lax.dot_general(whi, xhi, _NT, preferred_element_type=F32)
              + lax.dot_general(whi, xlo, _NT, preferred_element_type=F32)
              + lax.dot_general(wlo_ref[...], xhi, _NT, preferred_element_type=F32))
    n_exp, tile = logits.shape
    per = n_exp // N_GROUPS
    scores = _sigmoid(logits)
    biased = scores + rb_ref[...]
    neg = -jnp.inf

    sub = lax.broadcasted_iota(jnp.int32, (per, tile), 0).astype(F32)
    gscore = []
    for g in range(N_GROUPS):
        bg = biased[g * per:(g + 1) * per, :]
        m1 = jnp.max(bg, axis=0, keepdims=True)
        i1 = jnp.min(jnp.where(bg == m1, sub, float(per)), axis=0, keepdims=True)
        m2 = jnp.max(jnp.where(sub == i1, neg, bg), axis=0, keepdims=True)
        gscore.append(m1 + m2)
    masked = []
    for g in range(N_GROUPS):
        beat = jnp.zeros_like(gscore[g])
        for o in range(N_GROUPS):
            if o == g:
                continue
            wins = (gscore[o] >= gscore[g]) if o < g else (gscore[o] > gscore[g])
            beat = beat + jnp.where(wins, 1.0, 0.0)
        keep = jnp.broadcast_to(beat < float(TOPK_GROUPS), (per, tile))
        masked.append(jnp.where(keep, biased[g * per:(g + 1) * per, :], neg))
    masked = jnp.concatenate(masked, axis=0)

    row = lax.broadcasted_iota(jnp.int32, (n_exp, tile), 0).astype(F32)
    chosen = jnp.zeros((n_exp, tile), F32)
    idxs, gates = [], []
    for _ in range(TOP_K):
        m = jnp.max(masked, axis=0, keepdims=True)
        idx = jnp.min(jnp.where(masked == m, row, float(n_exp)), axis=0, keepdims=True)
        hit = row == idx
        gates.append(jnp.sum(jnp.where(hit, scores, 0.0), axis=0, keepdims=True))
        masked = jnp.where(hit, neg, masked)
        chosen = jnp.where(hit, 1.0, chosen)
        idxs.append(idx)
    gsum = gates[0]
    for k in range(1, TOP_K):
        gsum = gsum + gates[k]

    earlier = jnp.dot(chosen.astype(BF16), tri_ref[...], preferred_element_type=F32)
    count = jnp.sum(chosen, axis=1, keepdims=True)
    run = count + (count - 2.0 * jnp.floor(0.5 * count))
    lower = jnp.dot(ltri_ref[...], jnp.broadcast_to(run, (n_exp, LANES)).astype(BF16),
                    preferred_element_type=F32)[:, :1]
    order = earlier + lower
    for k in range(TOP_K):
        gate_ref[k:k + 1, :] = gates[k] / gsum * ROUTED_SCALE
        lpos_k = jnp.sum(jnp.where(row == idxs[k], order, 0.0), axis=0, keepdims=True)
        lpos_ref[k:k + 1, :] = lpos_k.astype(jnp.int32)
    tcnt_ref[...] = run.astype(jnp.int32)
    toff_ref[...] = lower.astype(jnp.int32)
    tbase_ref[...] = cnt_ref[...].astype(jnp.int32)
    ttot_ref[...] = jnp.sum(run, axis=0, keepdims=True).astype(jnp.int32)
    cnt_ref[...] = cnt_ref[...] + run


def _router(h2, rw, rb):
    N, D = h2.shape
    E = rw.shape[1]
    tile = _pick(N, MOE_TILE)
    assert tile <= 256
    n_tiles = N // tile
    wt = rw.T
    whi = wt.astype(BF16)
    wlo = (wt - whi.astype(F32)).astype(BF16)
    ar = jnp.arange(tile)
    tri = (ar[:, None] < ar[None, :]).astype(BF16)
    ae = jnp.arange(E)
    ltri = (ae[None, :] < ae[:, None]).astype(BF16)
    const = lambda shape: pl.BlockSpec(shape, lambda i: (0, 0))
    kn = pl.BlockSpec((TOP_K, tile), lambda i: (0, i))
    table = pl.BlockSpec((None, E, 1), lambda i: (i, 0, 0))
    table_shape = jax.ShapeDtypeStruct((n_tiles, E, 1), jnp.int32)
    return pl.pallas_call(
        _router_kernel,
        grid=(n_tiles,),
        in_specs=[pl.BlockSpec((tile, D), lambda i: (i, 0)), const((E, D)), const((E, D)),
                  const((E, 1)), const((tile, tile)), const((E, E))],
        out_specs=[kn, kn, const((E, 1)), table, table, table,
                   pl.BlockSpec((None, 1, 1), lambda i: (i, 0, 0))],
        out_shape=[jax.ShapeDtypeStruct((TOP_K, N), F32), jax.ShapeDtypeStruct((TOP_K, N), jnp.int32),
                   jax.ShapeDtypeStruct((E, 1), F32), table_shape, table_shape, table_shape,
                   jax.ShapeDtypeStruct((n_tiles, 1, 1), jnp.int32)],
        compiler_params=_params(("arbitrary",)),
        name="moe_router",
    )(h2, whi, wlo, rb.reshape(E, 1), tri, ltri)


def _pack_pairs(x, rounded=False):
    half = x.shape[-1] // 2
    bits = lambda v: lax.bitcast_convert_type(v if rounded else v.astype(BF16).astype(F32), jnp.uint32)
    return (bits(x[:, :half]) & jnp.uint32(0xFFFF0000)) | (bits(x[:, half:]) >> 16)


def _unpack_pairs(w):
    hi = lax.bitcast_convert_type(w & jnp.uint32(0xFFFF0000), F32)
    lo = lax.bitcast_convert_type(w << 16, F32)
    return jnp.concatenate([hi.astype(BF16), lo.astype(BF16)], axis=-1)


def _to_slab(slab_ref, x, rounded=False, first=0):
    w = _pack_pairs(x, rounded)
    lines = w.shape[1] // LANES
    for a in range(lines):
        slab_ref[pl.ds(first * lines + a, w.shape[0], stride=lines), :] = w[:, a * LANES:(a + 1) * LANES]


def _from_slab(slab_ref, rows, lines, valid, first=0):
    w = jnp.concatenate([slab_ref[pl.ds(first * lines + a, rows, stride=lines), :] for a in range(lines)],
                        axis=-1)
    r = first + lax.broadcasted_iota(jnp.int32, (rows, 1), 0)
    return _unpack_pairs(jnp.where(r < valid, w, jnp.uint32(0)))


def _start_runs(toff_ref, gst_ref, tcnt_ref, local, remote, sem, lines, to_remote):
    def body(e, c):
        cnt, off, gst = tcnt_ref[0, 0, e], toff_ref[0, 0, e], gst_ref[0, 0, e]

        @pl.when(cnt > 0)
        def _():
            loc = local.at[pl.ds(pl.multiple_of(off * lines, 8), cnt * lines), :]
            rem = remote.at[pl.ds(pl.multiple_of(gst * lines, 8), cnt * lines), :]
            (pltpu.make_async_copy(loc, rem, sem) if to_remote else pltpu.make_async_copy(rem, loc, sem)).start()
        return c
    lax.fori_loop(0, tcnt_ref.shape[2], body, 0, unroll=RUN_UNROLL)


def _wait_all_runs(slab, remote, sem, total_lines):
    n = pl.multiple_of(total_lines, 8)
    pltpu.make_async_copy(slab.at[pl.ds(0, n), :], remote.at[pl.ds(0, n), :], sem).wait()


def _sorted_rows(tile, n_exp):
    return TOP_K * tile + n_exp


def _dispatch_kernel(toff_ref, gst_ref, tcnt_ref, ttot_ref, tprev_ref, lpos_ref, h_ref, xs_ref, slabs, sems,
                     *, lines):
    i = pl.program_id(0)
    slot = lax.rem(i, 2)
    slab = slabs.at[slot]
    tile = h_ref.shape[0]
    rows = slab.shape[0] // lines
    hb = h_ref[...].astype(BF16)
    chunk = max(c for c in range(8, 257, 8) if rows % c == 0)
    srow = lax.broadcasted_iota(jnp.int32, (chunk, tile), 0).astype(F32).astype(BF16)
    one, zero = jnp.ones((), BF16), jnp.zeros((), BF16)
    for first in range(0, rows, chunk):
        perm = jnp.full((chunk, tile), zero)
        for k in range(TOP_K):
            local = lpos_ref[k:k + 1, :] - first
            inside = jnp.logical_and(local >= 0, local < chunk)
            local = jnp.where(inside, local, -1).astype(F32).astype(BF16)
            perm = jnp.where(srow == local, one, perm)
        _to_slab(slab, jnp.dot(perm, hb, preferred_element_type=F32), rounded=True, first=first)
    _start_runs(toff_ref, gst_ref, tcnt_ref, slab, xs_ref, sems.at[slot], lines, to_remote=True)

    @pl.when(i > 0)
    def _():
        _wait_all_runs(slabs.at[1 - slot], xs_ref, sems.at[1 - slot], tprev_ref[0, 0, 0] * lines)

    @pl.when(i == pl.num_programs(0) - 1)
    def _():
        _wait_all_runs(slab, xs_ref, sems.at[slot], ttot_ref[0, 0, 0] * lines)


def _run_specs(n_exp):
    smem = lambda n: pl.BlockSpec((1, 1, n), lambda i: (i, 0, 0), memory_space=pltpu.SMEM)
    return [smem(n_exp), smem(n_exp), smem(n_exp), smem(1)]


def _slab_lines(d_model):
    lines = d_model // (2 * LANES)
    assert lines * 2 * LANES == d_model and (2 * lines) % 8 == 0
    return lines


def _dispatch(h2, lpos, runs, n_slots):
    N, D = h2.shape
    E = runs[0].shape[2]
    tile = _pick(N, MOE_TILE)
    lines = _slab_lines(D)
    return pl.pallas_call(
        functools.partial(_dispatch_kernel, lines=lines),
        grid=(N // tile,),
        in_specs=_run_specs(E) + [
            pl.BlockSpec((1, 1, 1), lambda i: (jnp.maximum(i - 1, 0), 0, 0), memory_space=pltpu.SMEM),
            pl.BlockSpec((TOP_K, tile), lambda i: (0, i)), pl.BlockSpec((tile, D), lambda i: (i, 0))],
        out_specs=pl.BlockSpec(memory_space=pl.ANY),
        out_shape=jax.ShapeDtypeStruct((n_slots * lines, LANES), jnp.uint32),
        scratch_shapes=[pltpu.VMEM((2, _sorted_rows(tile, E) * lines, LANES), jnp.uint32),
                        pltpu.SemaphoreType.DMA((2,))],
        compiler_params=_params(("arbitrary",)),
        name="moe_dispatch",
    )(*runs, runs[3], lpos, h2)


def _expert_kernel(pstart_ref, cnt_ref, xs_hbm, wg_hbm, wu_hbm, wd_hbm, ys_hbm,
                   xbuf, ybuf, xo, yo, wg_f, wu_f, wd_f, wg_s, wu_s, wd_s, wsem, xsem, ysem, osem,
                   *, layer, lines, cap, chunk):
    e = pl.program_id(0)
    last = pl.num_programs(0) - 1
    slot = lax.rem(e, 2)

    def weight_copies(ex, sl):
        return [pltpu.make_async_copy(w.at[layer, ex], f.at[sl], wsem.at[sl])
                for w, f in ((wg_hbm, wg_f), (wu_hbm, wu_f), (wd_hbm, wd_f))]

    def head_lines(ex):
        return jnp.minimum(cnt_ref[ex], cap) * lines

    def first_line(ex, row=0):
        return pl.multiple_of((pstart_ref[ex] + row) * lines, 8)

    def x_copy(ex, sl):
        n = head_lines(ex)
        return pltpu.make_async_copy(xs_hbm.at[pl.ds(first_line(ex), n), :], xbuf.at[sl, pl.ds(0, n), :],
                                     xsem.at[sl])

    def y_copy(ex, sl):
        n = head_lines(ex)
        return pltpu.make_async_copy(ybuf.at[sl, pl.ds(0, n), :], ys_hbm.at[pl.ds(first_line(ex), n), :],
                                     ysem.at[sl])

    @pl.when(e == 0)
    def _():
        for cp in weight_copies(0, 0):
            cp.start()

        @pl.when(cnt_ref[0] > 0)
        def _():
            x_copy(0, 0).start()

    for cp in weight_copies(e, slot):
        cp.wait()
    nxt = jnp.minimum(e + 1, last)

    @pl.when(e < last)
    def _():
        for cp in weight_copies(nxt, 1 - slot):
            cp.start()

        @pl.when(cnt_ref[nxt] > 0)
        def _():
            x_copy(nxt, 1 - slot).start()

    wg_s[...] = wg_f[slot].astype(BF16)
    wu_s[...] = wu_f[slot].astype(BF16)
    wd_s[...] = wd_f[slot].astype(BF16)

    before = jnp.maximum(e - 2, 0)

    @pl.when(jnp.logical_and(e >= 2, cnt_ref[before] > 0))
    def _():
        y_copy(before, slot).wait()

    def ffn(x_ref, rows, valid):
        x = _from_slab(x_ref, rows, lines, valid)
        hg = jnp.dot(x, wg_s[...], preferred_element_type=F32)
        hu = jnp.dot(x, wu_s[...], preferred_element_type=F32)
        hb = hg * _sigmoid(hg) * hu
        return jnp.dot(hb.astype(BF16), wd_s[...], preferred_element_type=F32)

    count = cnt_ref[e]

    @pl.when(count > 0)
    def _():
        x_copy(e, slot).wait()
        head = jnp.minimum(count, cap)
        lower = 0
        for rows in range(chunk, cap + 1, chunk):
            @pl.when(jnp.logical_and(head > lower, head <= rows))
            def _(rows=rows):
                _to_slab(ybuf.at[slot], ffn(xbuf.at[slot], rows, head))
            lower = rows
        y_copy(e, slot).start()

        @pl.when(count > cap)
        def _():
            def extra(j, carry):
                row = cap + j * chunk
                n = jnp.minimum(chunk, count - row) * lines
                fetch = pltpu.make_async_copy(xs_hbm.at[pl.ds(first_line(e, row), n), :],
                                              xo.at[pl.ds(0, n), :], osem)
                fetch.start()
                fetch.wait()
                _to_slab(yo, ffn(xo, chunk, count - row))
                store = pltpu.make_async_copy(yo.at[pl.ds(0, n), :],
                                              ys_hbm.at[pl.ds(first_line(e, row), n), :], osem)
                store.start()
                store.wait()
                return carry
            lax.fori_loop(0, (count - cap + chunk - 1) // chunk, extra, 0)

    @pl.when(e == last)
    def _():
        @pl.when(count > 0)
        def _():
            y_copy(e, slot).wait()
        prev = jnp.maximum(e - 1, 0)

        @pl.when(jnp.logical_and(e >= 1, cnt_ref[prev] > 0))
        def _():
            y_copy(prev, 1 - slot).wait()


def _experts(xs, pstart, counts, wg, wu, wd, layer):
    E, D, Fe = wg.shape[-3:]
    lines = _slab_lines(D)
    cap, chunk = EXPERT_CAP, EXPERT_CHUNK
    assert cap % chunk == 0 and (chunk * lines) % 8 == 0
    hbm = pl.BlockSpec(memory_space=pl.ANY)
    slab = lambda rows: pltpu.VMEM(rows, jnp.uint32)
    return pl.pallas_call(
        functools.partial(_expert_kernel, layer=layer, lines=lines, cap=cap, chunk=chunk),
        grid_spec=pltpu.PrefetchScalarGridSpec(
            num_scalar_prefetch=2,
            grid=(E,),
            in_specs=[hbm, hbm, hbm, hbm],
            out_specs=hbm,
            scratch_shapes=[slab((2, cap * lines, LANES)), slab((2, cap * lines, LANES)),
                            slab((chunk * lines, LANES)), slab((chunk * lines, LANES)),
                            pltpu.VMEM((2, D, Fe), F32), pltpu.VMEM((2, D, Fe), F32),
                            pltpu.VMEM((2, Fe, D), F32),
                            pltpu.VMEM((D, Fe), BF16), pltpu.VMEM((D, Fe), BF16), pltpu.VMEM((Fe, D), BF16),
                            pltpu.SemaphoreType.DMA((2,)), pltpu.SemaphoreType.DMA((2,)),
                            pltpu.SemaphoreType.DMA((2,)), pltpu.SemaphoreType.DMA(())],
        ),
        out_shape=jax.ShapeDtypeStruct(xs.shape, jnp.uint32),
        compiler_params=_params(("arbitrary",)),
        name="moe_experts",
    )(pstart, counts, xs, wg, wu, wd)


def _combine_kernel(toff_ref, gst_ref, tcnt_ref, ttot_ref, ntoff_ref, ngst_ref, ntcnt_ref, lpos_ref, gate_ref,
                    h_ref, ys_ref, sg_ref, su_ref, sd_ref, g_ref, beta_ref, o_ref, slabs, sems, *, lines, alpha):
    i = pl.program_id(0)
    slot = lax.rem(i, 2)
    slab, sem = slabs.at[slot], sems.at[slot]

    @pl.when(i == 0)
    def _():
        _start_runs(toff_ref, gst_ref, tcnt_ref, slab, ys_ref, sem, lines, to_remote=False)

    @pl.when(i < pl.num_programs(0) - 1)
    def _():
        _start_runs(ntoff_ref, ngst_ref, ntcnt_ref, slabs.at[1 - slot], ys_ref, sems.at[1 - slot], lines,
                    to_remote=False)

    h = h_ref[...]
    tile = h.shape[0]
    rows = slab.shape[0] // lines
    hb = h.astype(BF16)
    sgate = jnp.dot(hb, sg_ref[...], preferred_element_type=F32)
    sup = jnp.dot(hb, su_ref[...], preferred_element_type=F32)
    mid = sgate * _sigmoid(sgate) * sup
    acc = jnp.dot(mid.astype(BF16), sd_ref[...], preferred_element_type=F32)
    total = ttot_ref[0, 0, 0]
    _wait_all_runs(slab, ys_ref, sem, total * lines)
    chunk = max(c for c in range(8, 257, 8) if rows % c == 0)
    scol = lax.broadcasted_iota(jnp.int32, (tile, chunk), 1).astype(F32).astype(BF16)
    gates = [gate_ref[:, k:k + 1].astype(BF16) for k in range(TOP_K)]
    for first in range(0, rows, chunk):
        weights = jnp.zeros((tile, chunk), BF16)
        for k in range(TOP_K):
            local = lpos_ref[:, k:k + 1] - first
            inside = jnp.logical_and(local >= 0, local < chunk)
            local = jnp.where(inside, local, -1).astype(F32).astype(BF16)
            weights = jnp.where(scol == local, gates[k], weights)
        acc = acc + jnp.dot(weights, _from_slab(slab, chunk, lines, total, first),
                            preferred_element_type=F32)
    o_ref[...] = _layer_norm(alpha * h + acc, g_ref[...], beta_ref[...])


def _combine(h2, lpos_nk, gate_nk, runs, ys, sg, su, sd, g, beta, alpha):
    N, D = h2.shape
    Fs = sg.shape[1]
    E = runs[0].shape[2]
    tile = _pick(N, MOE_TILE)
    lines = _slab_lines(D)
    const = lambda shape: pl.BlockSpec(shape, lambda i: (0, 0))
    nk = pl.BlockSpec((tile, TOP_K), lambda i: (i, 0))
    last = N // tile - 1
    following = pl.BlockSpec((1, 1, E), lambda i: (jnp.minimum(i + 1, last), 0, 0), memory_space=pltpu.SMEM)
    return pl.pallas_call(
        functools.partial(_combine_kernel, lines=lines, alpha=alpha),
        grid=(N // tile,),
        in_specs=_run_specs(E) + [following] * 3 + [
            nk, nk, pl.BlockSpec((tile, D), lambda i: (i, 0)), pl.BlockSpec(memory_space=pl.ANY),
            const((D, Fs)), const((D, Fs)), const((Fs, D)), const((1, D)), const((1, D))],
        out_specs=pl.BlockSpec((tile, D), lambda i: (i, 0)),
        out_shape=jax.ShapeDtypeStruct((N, D), F32),
        scratch_shapes=[pltpu.VMEM((2, _sorted_rows(tile, E) * lines, LANES), jnp.uint32),
                        pltpu.SemaphoreType.DMA((2,))],
        compiler_params=_params(("arbitrary",)),
        name="moe_combine",
    )(*runs, *runs[:3], lpos_nk, gate_nk, h2, ys, sg.astype(BF16), su.astype(BF16), sd.astype(BF16),
      g.reshape(1, D), beta.reshape(1, D))


FLASH_TQ = 512
FLASH_TK = 512
MOE_TILE = 256
RUN_UNROLL = 8
EXPERT_CAP = 1024
EXPERT_CHUNK = 128


def _moe_ln(h2, rw, rb, wg, wu, wd, layer, sg, su, sd, g, beta, alpha):
    N, D = h2.shape
    E = rw.shape[1]
    gate, lpos, cnt, tcnt, toff, tbase, ttot = _router(h2, rw, rb)
    n_tiles = ttot.shape[0]
    counts = cnt[:, 0].astype(jnp.int32)
    pstart = (jnp.cumsum(counts) - counts).astype(jnp.int32)
    n_slots = N * TOP_K + n_tiles * E
    per_tile = lambda a: a.reshape(n_tiles, 1, E)
    runs = (per_tile(toff), per_tile(tbase) + pstart[None, None, :], per_tile(tcnt), ttot)
    xs = _dispatch(h2, lpos, runs, n_slots)
    ys = _experts(xs, pstart, counts, wg, wu, wd, layer)
    return _combine(h2, lpos.T, gate.T, runs, ys, sg, su, sd, g, beta, alpha)


def _mla_proj_kernel(h_ref, wa_ref, qg_ref, kvg_ref, wqt_ref, wk_ref, wvt_ref, cos_ref, sin_ref,
                     cost_ref, sint_ref, qt_ref, k_ref, vt_ref, *, q_lora, kv_lora, heads, q_scale):
    x = h_ref[0].astype(BF16)
    a = jnp.dot(x, wa_ref[...], preferred_element_type=F32)

    def rms(v, g):
        return v * lax.rsqrt(jnp.mean(v * v, axis=-1, keepdims=True) + RMS_EPS) * g

    q_lat = rms(a[:, :q_lora], qg_ref[...]).astype(BF16)
    kv_lat = rms(a[:, q_lora:q_lora + kv_lora], kvg_ref[...]).astype(BF16)
    o = q_lora + kv_lora
    k_rope = (a[:, o:o + LANES] * cos_ref[...] + a[:, o + LANES:o + 2 * LANES] * sin_ref[...]).astype(BF16)
    qa = lax.dot_general(wqt_ref[...], q_lat, _NT, preferred_element_type=F32)
    kn = jnp.dot(kv_lat, wk_ref[...], preferred_element_type=F32)
    vt_ref[0] = lax.dot_general(wvt_ref[...], kv_lat, _NT, preferred_element_type=F32).astype(BF16)
    cost = cost_ref[...]
    sint = sint_ref[...]
    hd = heads * LANES
    for h in range(heads):
        r0 = slice(h * LANES, (h + 1) * LANES)
        r1 = slice(hd + h * LANES, hd + (h + 1) * LANES)
        r2 = slice(2 * hd + h * LANES, 2 * hd + (h + 1) * LANES)
        qt_ref[0, 2 * h * LANES:(2 * h + 1) * LANES, :] = (qa[r0] * q_scale).astype(BF16)
        qt_ref[0, (2 * h + 1) * LANES:(2 * h + 2) * LANES, :] = (
            (qa[r1] * cost + qa[r2] * sint) * q_scale).astype(BF16)
        k_ref[0, :, 2 * h * LANES:(2 * h + 1) * LANES] = kn[:, r0].astype(BF16)
        k_ref[0, :, (2 * h + 1) * LANES:(2 * h + 2) * LANES] = k_rope


def _rot_half(w):
    r = w.shape[-1] // 2
    return jnp.concatenate([-w[..., r:], w[..., :r]], axis=-1)


def _pad_lanes(w):
    return jnp.pad(w, [(0, 0)] * (w.ndim - 1) + [(0, LANES - w.shape[-1])])


FLASH_HEADROOM = 64.0


def _flash_kernel(qt_ref, k_ref, vt_ref, o_ref, p_ref, s_ref, mx_ref, m_ref, l_ref, acc_ref, *, tk):
    qt = qt_ref[0]
    nk = k_ref.shape[1] // tk
    assert nk % 2 == 0 and nk >= 2

    def scores(c):
        off = pl.multiple_of(c * tk, tk)
        return jnp.dot(k_ref[0, pl.ds(off, tk), :], qt, preferred_element_type=F32)

    s0 = scores(0)
    m0 = jnp.max(s0, axis=0, keepdims=True)
    p0 = jnp.exp2(s0 - m0)
    p_ref[0] = p0.astype(BF16)
    acc_ref[...] = jnp.zeros_like(acc_ref)

    def probs(c, slot, carry):
        l, top = carry
        s = scores(c)
        p = jnp.exp2(s - m0)
        p_ref[slot] = p.astype(BF16)
        return l + jnp.sum(p, axis=0, keepdims=True), jnp.maximum(top, jnp.max(s, axis=0, keepdims=True))

    def accumulate(c, slot):
        off = pl.multiple_of(c * tk, tk)
        acc_ref[...] += jnp.dot(vt_ref[0, :, pl.ds(off, tk)], p_ref[slot], preferred_element_type=F32)

    carry = (jnp.sum(p0, axis=0, keepdims=True), m0)
    for c in range(nk - 1):
        carry = probs(c + 1, (c + 1) % 2, carry)
        accumulate(c, c % 2)
    accumulate(nk - 1, (nk - 1) % 2)
    l, top = carry
    o_ref[0] = (acc_ref[...] / l).T.astype(o_ref.dtype)

    @pl.when(jnp.max(top - m0) > FLASH_HEADROOM)
    def _():
        _flash_online(qt, k_ref, vt_ref, o_ref, s_ref, mx_ref, m_ref, l_ref, acc_ref, tk)


def _flash_online(qt, k_ref, vt_ref, o_ref, s_ref, mx_ref, m_ref, l_ref, acc_ref, tk):
    nk = k_ref.shape[1] // tk
    assert nk % 2 == 0 and nk >= 2
    m_ref[...] = jnp.full_like(m_ref, -jnp.inf)
    l_ref[...] = jnp.zeros_like(l_ref)
    acc_ref[...] = jnp.zeros_like(acc_ref)

    def scores(c, slot):
        off = pl.multiple_of(c * tk, tk)
        s = jnp.dot(k_ref[0, pl.ds(off, tk), :], qt, preferred_element_type=F32)
        s_ref[slot] = s
        mx_ref[slot] = jnp.max(s, axis=0, keepdims=True)

    def update(c, slot):
        off = pl.multiple_of(c * tk, tk)
        vt = vt_ref[0, :, pl.ds(off, tk)]
        m_prev = m_ref[...]
        m_new = jnp.maximum(m_prev, mx_ref[slot])
        p = jnp.exp2(s_ref[slot] - m_new)
        a = jnp.exp2(m_prev - m_new)
        l_ref[...] = a * l_ref[...] + jnp.sum(p, axis=0, keepdims=True)
        acc_ref[...] = a * acc_ref[...] + jnp.dot(vt, p.astype(BF16), preferred_element_type=F32)
        m_ref[...] = m_new

    scores(0, 0)

    def body(i, carry):
        c = 2 * i
        scores(c + 1, 1)
        update(c, 0)
        scores(c + 2, 0)
        update(c + 1, 1)
        return carry

    lax.fori_loop(0, nk // 2 - 1, body, 0)
    scores(nk - 1, 1)
    update(nk - 2, 0)
    update(nk - 1, 1)
    o_ref[0] = (acc_ref[...] / l_ref[...]).T.astype(o_ref.dtype)


def _oproj_ln_kernel(o_ref, h_ref, w_ref, g_ref, beta_ref, out_ref, *, alpha):
    m = jnp.dot(o_ref[0], w_ref[...], preferred_element_type=F32)
    out_ref[0] = _layer_norm(alpha * h_ref[0] + m, g_ref[...], beta_ref[...])


def _mla_ln(h, w_a, qg, kvg, w_uq, w_uk, w_uv, w_o, g, beta, alpha):
    B, S, D = h.shape
    q_lora, heads, qk = w_uq.shape
    kv_lora, _, nope = w_uk.shape
    vd = w_uv.shape[-1]
    rope = qk - nope
    assert nope == LANES and vd == LANES and rope <= LANES and rope % 2 == 0
    kr = w_a[:, q_lora + kv_lora:]
    wa = jnp.concatenate([w_a[:, :q_lora + kv_lora], _pad_lanes(kr), _pad_lanes(_rot_half(kr))],
                         axis=1).astype(BF16)
    wq_r = w_uq[:, :, nope:]
    wq = jnp.concatenate([w_uq[:, :, :nope].reshape(q_lora, heads * nope),
                          _pad_lanes(wq_r).reshape(q_lora, heads * LANES),
                          _pad_lanes(_rot_half(wq_r)).reshape(q_lora, heads * LANES)], axis=1).astype(BF16)
    wk = w_uk.reshape(kv_lora, heads * nope).astype(BF16)
    wvt = w_uv.reshape(kv_lora, heads * vd).T.astype(BF16)
    wqt = wq.T
    pos = jnp.arange(S, dtype=F32)
    inv = 1.0 / (ROPE_THETA ** (jnp.arange(0, rope, 2, dtype=F32) / rope))
    ang = pos[:, None] * inv[None, :]
    cos = _pad_lanes(jnp.concatenate([jnp.cos(ang), jnp.cos(ang)], axis=1))
    sin = _pad_lanes(jnp.concatenate([jnp.sin(ang), jnp.sin(ang)], axis=1))
    q_scale = float(qk) ** -0.5 * math.log2(math.e)

    tp = _pick(S, 256)
    const = lambda a: pl.BlockSpec(a.shape, lambda b, i: (0,) * a.ndim)
    qg2, kvg2 = qg.reshape(1, q_lora), kvg.reshape(1, kv_lora)
    tok_major = pl.BlockSpec((tp, LANES), lambda b, i: (i, 0))
    feat_major = pl.BlockSpec((LANES, tp), lambda b, i: (0, i))
    qt, k, vt = pl.pallas_call(
        functools.partial(_mla_proj_kernel, q_lora=q_lora, kv_lora=kv_lora, heads=heads, q_scale=q_scale),
        grid=(B, S // tp),
        in_specs=[pl.BlockSpec((1, tp, D), lambda b, i: (b, i, 0)), const(wa), const(qg2), const(kvg2),
                  const(wqt), const(wk), const(wvt), tok_major, tok_major, feat_major, feat_major],
        out_specs=[pl.BlockSpec((1, 2 * heads * LANES, tp), lambda b, i: (b, 0, i)),
                   pl.BlockSpec((1, tp, 2 * heads * LANES), lambda b, i: (b, i, 0)),
                   pl.BlockSpec((1, heads * LANES, tp), lambda b, i: (b, 0, i))],
        out_shape=[jax.ShapeDtypeStruct((B, 2 * heads * LANES, S), BF16),
                   jax.ShapeDtypeStruct((B, S, 2 * heads * LANES), BF16),
                   jax.ShapeDtypeStruct((B, heads * LANES, S), BF16)],
        compiler_params=_params(("parallel", "parallel")),
        name="mla_proj",
    )(h, wa, qg2, kvg2, wqt, wk, wvt, cos, sin, cos.T, sin.T)

    tq = _pick(S, FLASH_TQ)
    tk = _pick(S, FLASH_TK)
    o = pl.pallas_call(
        functools.partial(_flash_kernel, tk=tk),
        grid=(B, heads, S // tq),
        in_specs=[pl.BlockSpec((1, 2 * LANES, tq), lambda b, hh, i: (b, hh, i)),
                  pl.BlockSpec((1, S, 2 * LANES), lambda b, hh, i: (b, 0, hh)),
                  pl.BlockSpec((1, LANES, S), lambda b, hh, i: (b, hh, 0))],
        out_specs=pl.BlockSpec((1, tq, LANES), lambda b, hh, i: (b, i, hh)),
        out_shape=jax.ShapeDtypeStruct((B, S, heads * LANES), BF16),
        scratch_shapes=[pltpu.VMEM((2, tk, tq), BF16), pltpu.VMEM((2, tk, tq), F32), pltpu.VMEM((2, 1, tq), F32),
                        pltpu.VMEM((1, tq), F32), pltpu.VMEM((1, tq), F32), pltpu.VMEM((LANES, tq), F32)],
        compiler_params=_params(("parallel", "parallel", "arbitrary")),
        name="mla_flash",
    )(qt, k, vt)

    to = _pick(S, 512)
    vec = pl.BlockSpec((1, D), lambda b, i: (0, 0))
    return pl.pallas_call(
        functools.partial(_oproj_ln_kernel, alpha=alpha),
        grid=(B, S // to),
        in_specs=[pl.BlockSpec((1, to, heads * vd), lambda b, i: (b, i, 0)),
                  pl.BlockSpec((1, to, D), lambda b, i: (b, i, 0)),
                  pl.BlockSpec((heads * vd, D), lambda b, i: (0, 0)), vec, vec],
        out_specs=pl.BlockSpec((1, to, D), lambda b, i: (b, i, 0)),
        out_shape=jax.ShapeDtypeStruct((B, S, D), F32),
        compiler_params=_params(("parallel", "parallel")),
        name="mla_oproj_ln",
    )(o, h, w_o.astype(BF16), g.reshape(1, D), beta.reshape(1, D))


def kernel(x, pool_w, pool_b, pool_scale, mla_w_a, mla_q_norm_g, mla_kv_norm_g, mla_w_uq, mla_w_uk,
           mla_w_uv, mla_w_o, ln_mix_g, ln_mix_b, router_w, router_b, exp_w_gate, exp_w_up, exp_w_down,
           sh_w_gate, sh_w_up, sh_w_down, ln_ffn_g, ln_ffn_b):
    B, S, D = x.shape
    depth = ln_mix_g.shape[0]
    alpha = (2 * depth) ** 0.25
    h = x
    for i in range(depth):
        j = i // 2
        if i % 2 == 0:
            h = _pool_ln(h, pool_w[j], pool_b[j], pool_scale[j], ln_mix_g[i], ln_mix_b[i], alpha)
        else:
            h = _mla_ln(h, mla_w_a[j], mla_q_norm_g[j], mla_kv_norm_g[j], mla_w_uq[j], mla_w_uk[j],
                        mla_w_uv[j], mla_w_o[j], ln_mix_g[i], ln_mix_b[i], alpha)
        h = _moe_ln(h.reshape(B * S, D), router_w[i], router_b[i], exp_w_gate, exp_w_up, exp_w_down, i,
                    sh_w_gate[i], sh_w_up[i], sh_w_down[i], ln_ffn_g[i], ln_ffn_b[i],
                    alpha).reshape(B, S, D)
    return h
```

```python
import functools
import math

import jax
import jax.numpy as jnp
from jax import lax
from jax.experimental import pallas as pl
from jax.experimental.pallas import tpu as pltpu

F32 = jnp.float32
BF16 = jnp.bfloat16

LN_EPS = 1e-5
RMS_EPS = 1e-6
POOL_WINDOWS = (2, 4, 8, 16)
POOL_HALO = 8
ROPE_THETA = 10000.0
TOP_K = 8
N_GROUPS = 8
TOPK_GROUPS = 4
ROUTED_SCALE = 2.5
LANES = 128
VMEM_LIMIT = 48 * 1024 * 1024

_NT = (((1,), (1,)), ((), ()))


def _pick(n, pref):
    t = min(pref, n)
    while n % t:
        t //= 2
    assert t >= 8 and n % t == 0, (n, pref)
    return t


def _layer_norm(z, g, b):
    mu = jnp.mean(z, axis=-1, keepdims=True)
    zc = z - mu
    var = jnp.mean(zc * zc, axis=-1, keepdims=True)
    return zc * lax.rsqrt(var + LN_EPS) * g + b


def _sigmoid(x):
    return 1.0 / (1.0 + jnp.exp(-x))


def _params(sem):
    return pltpu.CompilerParams(dimension_semantics=sem, vmem_limit_bytes=VMEM_LIMIT)


def _pool_ln_kernel(xc_ref, xp_ref, xn_ref, w_ref, b_ref, sc_ref, g_ref, beta_ref, o_ref, ext_ref,
                    *, tile, seq, alpha):
    i = pl.program_id(1)
    n_tiles = pl.num_programs(1)
    xc = xc_ref[0]
    d_model = xc.shape[-1]
    ch = d_model // len(POOL_WINDOWS)
    ext_ref[0:POOL_HALO, :] = jnp.where(i > 0, xp_ref[0], 0.0)
    ext_ref[POOL_HALO:POOL_HALO + tile, :] = xc
    ext_ref[POOL_HALO + tile:2 * POOL_HALO + tile, :] = jnp.where(i < n_tiles - 1, xn_ref[0], 0.0)
    pos = i * tile + lax.broadcasted_iota(jnp.int32, (tile, 1), 0)
    outs = []
    for g, win in enumerate(POOL_WINDOWS):
        half = win // 2
        cols = slice(g * ch, (g + 1) * ch)
        acc = ext_ref[POOL_HALO - half:POOL_HALO - half + tile, cols]
        for j in range(-half + 1, half):
            acc = acc + ext_ref[POOL_HALO + j:POOL_HALO + j + tile, cols]
        cnt = (jnp.minimum(pos + half, seq) - jnp.maximum(pos - half, 0)).astype(F32)
        pooled = acc / cnt - xc[:, cols]
        outs.append(jnp.dot(pooled.astype(BF16), w_ref[g], preferred_element_type=F32))
    y = jnp.concatenate(outs, axis=-1)
    y = (y + b_ref[...]) * sc_ref[...]
    o_ref[0] = _layer_norm(alpha * xc + y, g_ref[...], beta_ref[...])


def _pool_ln(x, w, b, sc, g, beta, alpha):
    B, S, D = x.shape
    tile = _pick(S, 512)
    nh = tile // POOL_HALO
    last = S // POOL_HALO - 1
    row = lambda v: v.reshape(1, D)
    vec = pl.BlockSpec((1, D), lambda bi, i: (0, 0))
    return pl.pallas_call(
        functools.partial(_pool_ln_kernel, tile=tile, seq=S, alpha=alpha),
        grid=(B, S // tile),
        in_specs=[
            pl.BlockSpec((1, tile, D), lambda bi, i: (bi, i, 0)),
            pl.BlockSpec((1, POOL_HALO, D), lambda bi, i: (bi, jnp.maximum(i * nh - 1, 0), 0)),
            pl.BlockSpec((1, POOL_HALO, D), lambda bi, i: (bi, jnp.minimum((i + 1) * nh, last), 0)),
            pl.BlockSpec(w.shape, lambda bi, i: (0, 0, 0)),
            vec, vec, vec, vec,
        ],
        out_specs=pl.BlockSpec((1, tile, D), lambda bi, i: (bi, i, 0)),
        out_shape=jax.ShapeDtypeStruct((B, S, D), F32),
        scratch_shapes=[pltpu.VMEM((tile + 2 * POOL_HALO, D), F32)],
        compiler_params=_params(("parallel", "parallel")),
        name="pool_ln",
    )(x, x, x, w.astype(BF16), row(b), row(sc), row(g), row(beta))


def _router_kernel(h_ref, whi_ref, wlo_ref, rb_ref, tri_ref, ltri_ref,
                   gate_ref, lpos_ref, cnt_ref, tcnt_ref, toff_ref, tbase_ref, ttot_ref):
    i = pl.program_id(0)

    @pl.when(i == 0)
    def _():
        cnt_ref[...] = jnp.zeros_like(cnt_ref)

    x = h_ref[...]
    xhi = x.astype(BF16)
    xlo = (x - xhi.astype(F32)).astype(BF16)
    whi = whi_ref[...]
    logits = (lax.dot_general(whi, xhi, _NT, preferred_element_type=F32)
              + lax.dot_general(whi, xlo, _NT, preferred_element_type=F32)
              + lax.dot_general(wlo_ref[...], xhi, _NT, preferred_element_type=F32))
    n_exp, tile = logits.shape
    per = n_exp // N_GROUPS
    scores = _sigmoid(logits)
    biased = scores + rb_ref[...]
    neg = -jnp.inf

    sub = lax.broadcasted_iota(jnp.int32, (per, tile), 0).astype(F32)
    gscore = []
    for g in range(N_GROUPS):
        bg = biased[g * per:(g + 1) * per, :]
        m1 = jnp.max(bg, axis=0, keepdims=True)
        i1 = jnp.min(jnp.where(bg == m1, sub, float(per)), axis=0, keepdims=True)
        m2 = jnp.max(jnp.where(sub == i1, neg, bg), axis=0, keepdims=True)
        gscore.append(m1 + m2)
    masked = []
    for g in range(N_GROUPS):
        beat = jnp.zeros_like(gscore[g])
        for o in range(N_GROUPS):
            if o == g:
                continue
            wins = (gscore[o] >= gscore[g]) if o < g else (gscore[o] > gscore[g])
            beat = beat + jnp.where(wins, 1.0, 0.0)
        keep = jnp.broadcast_to(beat < float(TOPK_GROUPS), (per, tile))
        masked.append(jnp.where(keep, biased[g * per:(g + 1) * per, :], neg))
    masked = jnp.concatenate(masked, axis=0)

    row = lax.broadcasted_iota(jnp.int32, (n_exp, tile), 0).astype(F32)
    chosen = jnp.zeros((n_exp, tile), F32)
    idxs, gates = [], []
    for _ in range(TOP_K):
        m = jnp.max(masked, axis=0, keepdims=True)
        idx = jnp.min(jnp.where(masked == m, row, float(n_exp)), axis=0, keepdims=True)
        hit = row == idx
        gates.append(jnp.sum(jnp.where(hit, scores, 0.0), axis=0, keepdims=True))
        masked = jnp.where(hit, neg, masked)
        chosen = jnp.where(hit, 1.0, chosen)
        idxs.append(idx)
    gsum = gates[0]
    for k in range(1, TOP_K):
        gsum = gsum + gates[k]

    earlier = jnp.dot(chosen.astype(BF16), tri_ref[...], preferred_element_type=F32)
    count = jnp.sum(chosen, axis=1, keepdims=True)
    run = count + (count - 2.0 * jnp.floor(0.5 * count))
    lower = jnp.dot(ltri_ref[...], jnp.broadcast_to(run, (n_exp, LANES)).astype(BF16),
                    preferred_element_type=F32)[:, :1]
    order = earlier + lower
    for k in range(TOP_K):
        gate_ref[k:k + 1, :] = gates[k] / gsum * ROUTED_SCALE
        lpos_k = jnp.sum(jnp.where(row == idxs[k], order, 0.0), axis=0, keepdims=True)
        lpos_ref[k:k + 1, :] = lpos_k.astype(jnp.int32)
    tcnt_ref[...] = run.astype(jnp.int32)
    toff_ref[...] = lower.astype(jnp.int32)
    tbase_ref[...] = cnt_ref[...].astype(jnp.int32)
    ttot_ref[...] = jnp.sum(run, axis=0, keepdims=True).astype(jnp.int32)
    cnt_ref[...] = cnt_ref[...] + run


def _router(h2, rw, rb):
    N, D = h2.shape
    E = rw.shape[1]
    tile = _pick(N, MOE_TILE)
    assert tile <= 256
    n_tiles = N // tile
    wt = rw.T
    whi = wt.astype(BF16)
    wlo = (wt - whi.astype(F32)).astype(BF16)
    ar = jnp.arange(tile)
    tri = (ar[:, None] < ar[None, :]).astype(BF16)
    ae = jnp.arange(E)
    ltri = (ae[None, :] < ae[:, None]).astype(BF16)
    const = lambda shape: pl.BlockSpec(shape, lambda i: (0, 0))
    kn = pl.BlockSpec((TOP_K, tile), lambda i: (0, i))
    table = pl.BlockSpec((None, E, 1), lambda i: (i, 0, 0))
    table_shape = jax.ShapeDtypeStruct((n_tiles, E, 1), jnp.int32)
    return pl.pallas_call(
        _router_kernel,
        grid=(n_tiles,),
        in_specs=[pl.BlockSpec((tile, D), lambda i: (i, 0)), const((E, D)), const((E, D)),
                  const((E, 1)), const((tile, tile)), const((E, E))],
        out_specs=[kn, kn, const((E, 1)), table, table, table,
                   pl.BlockSpec((None, 1, 1), lambda i: (i, 0, 0))],
        out_shape=[jax.ShapeDtypeStruct((TOP_K, N), F32), jax.ShapeDtypeStruct((TOP_K, N), jnp.int32),
                   jax.ShapeDtypeStruct((E, 1), F32), table_shape, table_shape, table_shape,
                   jax.ShapeDtypeStruct((n_tiles, 1, 1), jnp.int32)],
        compiler_params=_params(("arbitrary",)),
        name="moe_router",
    )(h2, whi, wlo, rb.reshape(E, 1), tri, ltri)


def _pack_pairs(x, rounded=False):
    half = x.shape[-1] // 2
    bits = lambda v: lax.bitcast_convert_type(v if rounded else v.astype(BF16).astype(F32), jnp.uint32)
    return (bits(x[:, :half]) & jnp.uint32(0xFFFF0000)) | (bits(x[:, half:]) >> 16)


def _unpack_pairs(w):
    hi = lax.bitcast_convert_type(w & jnp.uint32(0xFFFF0000), F32)
    lo = lax.bitcast_convert_type(w << 16, F32)
    return jnp.concatenate([hi.astype(BF16), lo.astype(BF16)], axis=-1)


def _to_slab(slab_ref, x, rounded=False, first=0):
    w = _pack_pairs(x, rounded)
    lines = w.shape[1] // LANES
    for a in range(lines):
        slab_ref[pl.ds(first * lines + a, w.shape[0], stride=lines), :] = w[:, a * LANES:(a + 1) * LANES]


def _from_slab(slab_ref, rows, lines, valid, first=0):
    w = jnp.concatenate([slab_ref[pl.ds(first * lines + a, rows, stride=lines), :] for a in range(lines)],
                        axis=-1)
    r = first + lax.broadcasted_iota(jnp.int32, (rows, 1), 0)
    return _unpack_pairs(jnp.where(r < valid, w, jnp.uint32(0)))


def _start_runs(toff_ref, gst_ref, tcnt_ref, local, remote, sem, lines, to_remote):
    def body(e, c):
        cnt, off, gst = tcnt_ref[0, 0, e], toff_ref[0, 0, e], gst_ref[0, 0, e]

        @pl.when(cnt > 0)
        def _():
            loc = local.at[pl.ds(pl.multiple_of(off * lines, 8), cnt * lines), :]
            rem = remote.at[pl.ds(pl.multiple_of(gst * lines, 8), cnt * lines), :]
            (pltpu.make_async_copy(loc, rem, sem) if to_remote else pltpu.make_async_copy(rem, loc, sem)).start()
        return c
    lax.fori_loop(0, tcnt_ref.shape[2], body, 0, unroll=RUN_UNROLL)


def _wait_all_runs(slab, remote, sem, total_lines):
    n = pl.multiple_of(total_lines, 8)
    pltpu.make_async_copy(slab.at[pl.ds(0, n), :], remote.at[pl.ds(0, n), :], sem).wait()


def _sorted_rows(tile, n_exp):
    return TOP_K * tile + n_exp


def _dispatch_kernel(toff_ref, gst_ref, tcnt_ref, ttot_ref, tprev_ref, lpos_ref, h_ref, xs_ref, slabs, sems,
                     *, lines):
    i = pl.program_id(0)
    slot = lax.rem(i, 2)
    slab = slabs.at[slot]
    tile = h_ref.shape[0]
    rows = slab.shape[0] // lines
    hb = h_ref[...].astype(BF16)
    chunk = max(c for c in range(8, 257, 8) if rows % c == 0)
    srow = lax.broadcasted_iota(jnp.int32, (chunk, tile), 0).astype(F32).astype(BF16)
    one, zero = jnp.ones((), BF16), jnp.zeros((), BF16)
    for first in range(0, rows, chunk):
        perm = jnp.full((chunk, tile), zero)
        for k in range(TOP_K):
            local = lpos_ref[k:k + 1, :] - first
            inside = jnp.logical_and(local >= 0, local < chunk)
            local = jnp.where(inside, local, -1).astype(F32).astype(BF16)
            perm = jnp.where(srow == local, one, perm)
        _to_slab(slab, jnp.dot(perm, hb, preferred_element_type=F32), rounded=True, first=first)
    _start_runs(toff_ref, gst_ref, tcnt_ref, slab, xs_ref, sems.at[slot], lines, to_remote=True)

    @pl.when(i > 0)
    def _():
        _wait_all_runs(slabs.at[1 - slot], xs_ref, sems.at[1 - slot], tprev_ref[0, 0, 0] * lines)

    @pl.when(i == pl.num_programs(0) - 1)
    def _():
        _wait_all_runs(slab, xs_ref, sems.at[slot], ttot_ref[0, 0, 0] * lines)


def _run_specs(n_exp):
    smem = lambda n: pl.BlockSpec((1, 1, n), lambda i: (i, 0, 0), memory_space=pltpu.SMEM)
    return [smem(n_exp), smem(n_exp), smem(n_exp), smem(1)]


def _slab_lines(d_model):
    lines = d_model // (2 * LANES)
    assert lines * 2 * LANES == d_model and (2 * lines) % 8 == 0
    return lines


def _dispatch(h2, lpos, runs, n_slots):
    N, D = h2.shape
    E = runs[0].shape[2]
    tile = _pick(N, MOE_TILE)
    lines = _slab_lines(D)
    return pl.pallas_call(
        functools.partial(_dispatch_kernel, lines=lines),
        grid=(N // tile,),
        in_specs=_run_specs(E) + [
            pl.BlockSpec((1, 1, 1), lambda i: (jnp.maximum(i - 1, 0), 0, 0), memory_space=pltpu.SMEM),
            pl.BlockSpec((TOP_K, tile), lambda i: (0, i)), pl.BlockSpec((tile, D), lambda i: (i, 0))],
        out_specs=pl.BlockSpec(memory_space=pl.ANY),
        out_shape=jax.ShapeDtypeStruct((n_slots * lines, LANES), jnp.uint32),
        scratch_shapes=[pltpu.VMEM((2, _sorted_rows(tile, E) * lines, LANES), jnp.uint32),
                        pltpu.SemaphoreType.DMA((2,))],
        compiler_params=_params(("arbitrary",)),
        name="moe_dispatch",
    )(*runs, runs[3], lpos, h2)


def _expert_kernel(pstart_ref, cnt_ref, xs_hbm, wg_hbm, wu_hbm, wd_hbm, ys_hbm,
                   xbuf, ybuf, xo, yo, wg_f, wu_f, wd_f, wg_s, wu_s, wd_s, wsem, xsem, ysem, osem,
                   *, layer, lines, cap, chunk):
    e = pl.program_id(0)
    last = pl.num_programs(0) - 1
    slot = lax.rem(e, 2)

    def weight_copies(ex, sl):
        return [pltpu.make_async_copy(w.at[layer, ex], f.at[sl], wsem.at[sl])
                for w, f in ((wg_hbm, wg_f), (wu_hbm, wu_f), (wd_hbm, wd_f))]

    def head_lines(ex):
        return jnp.minimum(cnt_ref[ex], cap) * lines

    def first_line(ex, row=0):
        return pl.multiple_of((pstart_ref[ex] + row) * lines, 8)

    def x_copy(ex, sl):
        n = head_lines(ex)
        return pltpu.make_async_copy(xs_hbm.at[pl.ds(first_line(ex), n), :], xbuf.at[sl, pl.ds(0, n), :],
                                     xsem.at[sl])

    def y_copy(ex, sl):
        n = head_lines(ex)
        return pltpu.make_async_copy(ybuf.at[sl, pl.ds(0, n), :], ys_hbm.at[pl.ds(first_line(ex), n), :],
                                     ysem.at[sl])

    @pl.when(e == 0)
    def _():
        for cp in weight_copies(0, 0):
            cp.start()

        @pl.when(cnt_ref[0] > 0)
        def _():
            x_copy(0, 0).start()

    for cp in weight_copies(e, slot):
        cp.wait()
    nxt = jnp.minimum(e + 1, last)

    @pl.when(e < last)
    def _():
        for cp in weight_copies(nxt, 1 - slot):
            cp.start()

        @pl.when(cnt_ref[nxt] > 0)
        def _():
            x_copy(nxt, 1 - slot).start()

    wg_s[...] = wg_f[slot].astype(BF16)
    wu_s[...] = wu_f[slot].astype(BF16)
    wd_s[...] = wd_f[slot].astype(BF16)

    before = jnp.maximum(e - 2, 0)

    @pl.when(jnp.logical_and(e >= 2, cnt_ref[before] > 0))
    def _():
        y_copy(before, slot).wait()

    def ffn(x_ref, rows, valid):
        x = _from_slab(x_ref, rows, lines, valid)
        hg = jnp.dot(x, wg_s[...], preferred_element_type=F32)
        hu = jnp.dot(x, wu_s[...], preferred_element_type=F32)
        hb = hg * _sigmoid(hg) * hu
        return jnp.dot(hb.astype(BF16), wd_s[...], preferred_element_type=F32)

    count = cnt_ref[e]

    @pl.when(count > 0)
    def _():
        x_copy(e, slot).wait()
        head = jnp.minimum(count, cap)
        lower = 0
        for rows in range(chunk, cap + 1, chunk):
            @pl.when(jnp.logical_and(head > lower, head <= rows))
            def _(rows=rows):
                _to_slab(ybuf.at[slot], ffn(xbuf.at[slot], rows, head))
            lower = rows
        y_copy(e, slot).start()

        @pl.when(count > cap)
        def _():
            def extra(j, carry):
                row = cap + j * chunk
                n = jnp.minimum(chunk, count - row) * lines
                fetch = pltpu.make_async_copy(xs_hbm.at[pl.ds(first_line(e, row), n), :],
                                              xo.at[pl.ds(0, n), :], osem)
                fetch.start()
                fetch.wait()
                _to_slab(yo, ffn(xo, chunk, count - row))
                store = pltpu.make_async_copy(yo.at[pl.ds(0, n), :],
                                              ys_hbm.at[pl.ds(first_line(e, row), n), :], osem)
                store.start()
                store.wait()
                return carry
            lax.fori_loop(0, (count - cap + chunk - 1) // chunk, extra, 0)

    @pl.when(e == last)
    def _():
        @pl.when(count > 0)
        def _():
            y_copy(e, slot).wait()
        prev = jnp.maximum(e - 1, 0)

        @pl.when(jnp.logical_and(e >= 1, cnt_ref[prev] > 0))
        def _():
            y_copy(prev, 1 - slot).wait()


def _experts(xs, pstart, counts, wg, wu, wd, layer):
    E, D, Fe = wg.shape[-3:]
    lines = _slab_lines(D)
    cap, chunk = EXPERT_CAP, EXPERT_CHUNK
    assert cap % chunk == 0 and (chunk * lines) % 8 == 0
    hbm = pl.BlockSpec(memory_space=pl.ANY)
    slab = lambda rows: pltpu.VMEM(rows, jnp.uint32)
    return pl.pallas_call(
        functools.partial(_expert_kernel, layer=layer, lines=lines, cap=cap, chunk=chunk),
        grid_spec=pltpu.PrefetchScalarGridSpec(
            num_scalar_prefetch=2,
            grid=(E,),
            in_specs=[hbm, hbm, hbm, hbm],
            out_specs=hbm,
            scratch_shapes=[slab((2, cap * lines, LANES)), slab((2, cap * lines, LANES)),
                            slab((chunk * lines, LANES)), slab((chunk * lines, LANES)),
                            pltpu.VMEM((2, D, Fe), F32), pltpu.VMEM((2, D, Fe), F32),
                            pltpu.VMEM((2, Fe, D), F32),
                            pltpu.VMEM((D, Fe), BF16), pltpu.VMEM((D, Fe), BF16), pltpu.VMEM((Fe, D), BF16),
                            pltpu.SemaphoreType.DMA((2,)), pltpu.SemaphoreType.DMA((2,)),
                            pltpu.SemaphoreType.DMA((2,)), pltpu.SemaphoreType.DMA(())],
        ),
        out_shape=jax.ShapeDtypeStruct(xs.shape, jnp.uint32),
        compiler_params=_params(("arbitrary",)),
        name="moe_experts",
    )(pstart, counts, xs, wg, wu, wd)


def _combine_kernel(toff_ref, gst_ref, tcnt_ref, ttot_ref, ntoff_ref, ngst_ref, ntcnt_ref, lpos_ref, gate_ref,
                    h_ref, ys_ref, sg_ref, su_ref, sd_ref, g_ref, beta_ref, o_ref, slabs, sems, *, lines, alpha):
    i = pl.program_id(0)
    slot = lax.rem(i, 2)
    slab, sem = slabs.at[slot], sems.at[slot]

    @pl.when(i == 0)
    def _():
        _start_runs(toff_ref, gst_ref, tcnt_ref, slab, ys_ref, sem, lines, to_remote=False)

    @pl.when(i < pl.num_programs(0) - 1)
    def _():
        _start_runs(ntoff_ref, ngst_ref, ntcnt_ref, slabs.at[1 - slot], ys_ref, sems.at[1 - slot], lines,
                    to_remote=False)

    h = h_ref[...]
    tile = h.shape[0]
    rows = slab.shape[0] // lines
    hb = h.astype(BF16)
    sgate = jnp.dot(hb, sg_ref[...], preferred_element_type=F32)
    sup = jnp.dot(hb, su_ref[...], preferred_element_type=F32)
    mid = sgate * _sigmoid(sgate) * sup
    acc = jnp.dot(mid.astype(BF16), sd_ref[...], preferred_element_type=F32)
    total = ttot_ref[0, 0, 0]
    _wait_all_runs(slab, ys_ref, sem, total * lines)
    chunk = max(c for c in range(8, 257, 8) if rows % c == 0)
    scol = lax.broadcasted_iota(jnp.int32, (tile, chunk), 1).astype(F32).astype(BF16)
    gates = [gate_ref[:, k:k + 1].astype(BF16) for k in range(TOP_K)]
    for first in range(0, rows, chunk):
        weights = jnp.zeros((tile, chunk), BF16)
        for k in range(TOP_K):
            local = lpos_ref[:, k:k + 1] - first
            inside = jnp.logical_and(local >= 0, local < chunk)
            local = jnp.where(inside, local, -1).astype(F32).astype(BF16)
            weights = jnp.where(scol == local, gates[k], weights)
        acc = acc + jnp.dot(weights, _from_slab(slab, chunk, lines, total, first),
                            preferred_element_type=F32)
    o_ref[...] = _layer_norm(alpha * h + acc, g_ref[...], beta_ref[...])


def _combine(h2, lpos_nk, gate_nk, runs, ys, sg, su, sd, g, beta, alpha):
    N, D = h2.shape
    Fs = sg.shape[1]
    E = runs[0].shape[2]
    tile = _pick(N, MOE_TILE)
    lines = _slab_lines(D)
    const = lambda shape: pl.BlockSpec(shape, lambda i: (0, 0))
    nk = pl.BlockSpec((tile, TOP_K), lambda i: (i, 0))
    last = N // tile - 1
    following = pl.BlockSpec((1, 1, E), lambda i: (jnp.minimum(i + 1, last), 0, 0), memory_space=pltpu.SMEM)
    return pl.pallas_call(
        functools.partial(_combine_kernel, lines=lines, alpha=alpha),
        grid=(N // tile,),
        in_specs=_run_specs(E) + [following] * 3 + [
            nk, nk, pl.BlockSpec((tile, D), lambda i: (i, 0)), pl.BlockSpec(memory_space=pl.ANY),
            const((D, Fs)), const((D, Fs)), const((Fs, D)), const((1, D)), const((1, D))],
        out_specs=pl.BlockSpec((tile, D), lambda i: (i, 0)),
        out_shape=jax.ShapeDtypeStruct((N, D), F32),
        scratch_shapes=[pltpu.VMEM((2, _sorted_rows(tile, E) * lines, LANES), jnp.uint32),
                        pltpu.SemaphoreType.DMA((2,))],
        compiler_params=_params(("arbitrary",)),
        name="moe_combine",
    )(*runs, *runs[:3], lpos_nk, gate_nk, h2, ys, sg.astype(BF16), su.astype(BF16), sd.astype(BF16),
      g.reshape(1, D), beta.reshape(1, D))


FLASH_TQ = 512
FLASH_TK = 512
MOE_TILE = 256
RUN_UNROLL = 8
EXPERT_CAP = 1024
EXPERT_CHUNK = 64


def _moe_ln(h2, rw, rb, wg, wu, wd, layer, sg, su, sd, g, beta, alpha):
    N, D = h2.shape
    E = rw.shape[1]
    gate, lpos, cnt, tcnt, toff, tbase, ttot = _router(h2, rw, rb)
    n_tiles = ttot.shape[0]
    counts = cnt[:, 0].astype(jnp.int32)
    pstart = (jnp.cumsum(counts) - counts).astype(jnp.int32)
    n_slots = N * TOP_K + n_tiles * E
    per_tile = lambda a: a.reshape(n_tiles, 1, E)
    runs = (per_tile(toff), per_tile(tbase) + pstart[None, None, :], per_tile(tcnt), ttot)
    xs = _dispatch(h2, lpos, runs, n_slots)
    ys = _experts(xs, pstart, counts, wg, wu, wd, layer)
    return _combine(h2, lpos.T, gate.T, runs, ys, sg, su, sd, g, beta, alpha)


def _mla_proj_kernel(h_ref, wa_ref, qg_ref, kvg_ref, wqt_ref, wk_ref, wvt_ref, cos_ref, sin_ref,
                     cost_ref, sint_ref, qt_ref, k_ref, vt_ref, *, q_lora, kv_lora, heads, q_scale):
    x = h_ref[0].astype(BF16)
    a = jnp.dot(x, wa_ref[...], preferred_element_type=F32)

    def rms(v, g):
        return v * lax.rsqrt(jnp.mean(v * v, axis=-1, keepdims=True) + RMS_EPS) * g

    q_lat = rms(a[:, :q_lora], qg_ref[...]).astype(BF16)
    kv_lat = rms(a[:, q_lora:q_lora + kv_lora], kvg_ref[...]).astype(BF16)
    o = q_lora + kv_lora
    k_rope = (a[:, o:o + LANES] * cos_ref[...] + a[:, o + LANES:o + 2 * LANES] * sin_ref[...]).astype(BF16)
    qa = lax.dot_general(wqt_ref[...], q_lat, _NT, preferred_element_type=F32)
    kn = jnp.dot(kv_lat, wk_ref[...], preferred_element_type=F32)
    vt_ref[0] = lax.dot_general(wvt_ref[...], kv_lat, _NT, preferred_element_type=F32).astype(BF16)
    cost = cost_ref[...]
    sint = sint_ref[...]
    hd = heads * LANES
    for h in range(heads):
        r0 = slice(h * LANES, (h + 1) * LANES)
        r1 = slice(hd + h * LANES, hd + (h + 1) * LANES)
        r2 = slice(2 * hd + h * LANES, 2 * hd + (h + 1) * LANES)
        qt_ref[0, 2 * h * LANES:(2 * h + 1) * LANES, :] = (qa[r0] * q_scale).astype(BF16)
        qt_ref[0, (2 * h + 1) * LANES:(2 * h + 2) * LANES, :] = (
            (qa[r1] * cost + qa[r2] * sint) * q_scale).astype(BF16)
        k_ref[0, :, 2 * h * LANES:(2 * h + 1) * LANES] = kn[:, r0].astype(BF16)
        k_ref[0, :, (2 * h + 1) * LANES:(2 * h + 2) * LANES] = k_rope


def _rot_half(w):
    r = w.shape[-1] // 2
    return jnp.concatenate([-w[..., r:], w[..., :r]], axis=-1)


def _pad_lanes(w):
    return jnp.pad(w, [(0, 0)] * (w.ndim - 1) + [(0, LANES - w.shape[-1])])


FLASH_HEADROOM = 64.0
FLASH_PROBE = 16


def _flash_kernel(qt_ref, k_ref, vt_ref, o_ref, p_ref, s_ref, mx_ref, m_ref, l_ref, acc_ref, *, tk):
    qt = qt_ref[0]
    nk = k_ref.shape[1] // tk
    assert nk % 2 == 0 and nk >= 2

    def scores(c):
        off = pl.multiple_of(c * tk, tk)
        return jnp.dot(k_ref[0, pl.ds(off, tk), :], qt, preferred_element_type=F32)

    m0 = jnp.max(jnp.dot(k_ref[0, 0:FLASH_PROBE, :], qt, preferred_element_type=F32), axis=0, keepdims=True)
    acc_ref[...] = jnp.zeros_like(acc_ref)

    def probs(c, slot, carry):
        l, top = carry
        s = scores(c)
        p = jnp.exp2(s - m0)
        p_ref[slot] = p.astype(BF16)
        return l + jnp.sum(p, axis=0, keepdims=True), jnp.maximum(top, jnp.max(s, axis=0, keepdims=True))

    def accumulate(c, slot):
        off = pl.multiple_of(c * tk, tk)
        acc_ref[...] += jnp.dot(vt_ref[0, :, pl.ds(off, tk)], p_ref[slot], preferred_element_type=F32)

    carry = probs(0, 0, (jnp.zeros_like(m0), m0))
    for c in range(nk - 1):
        carry = probs(c + 1, (c + 1) % 2, carry)
        accumulate(c, c % 2)
    accumulate(nk - 1, (nk - 1) % 2)
    l, top = carry
    o_ref[0] = (acc_ref[...] / l).T.astype(o_ref.dtype)

    @pl.when(jnp.max(top - m0) > FLASH_HEADROOM)
    def _():
        _flash_online(qt, k_ref, vt_ref, o_ref, s_ref, mx_ref, m_ref, l_ref, acc_ref, tk)


def _flash_online(qt, k_ref, vt_ref, o_ref, s_ref, mx_ref, m_ref, l_ref, acc_ref, tk):
    nk = k_ref.shape[1] // tk
    assert nk % 2 == 0 and nk >= 2
    m_ref[...] = jnp.full_like(m_ref, -jnp.inf)
    l_ref[...] = jnp.zeros_like(l_ref)
    acc_ref[...] = jnp.zeros_like(acc_ref)

    def scores(c, slot):
        off = pl.multiple_of(c * tk, tk)
        s = jnp.dot(k_ref[0, pl.ds(off, tk), :], qt, preferred_element_type=F32)
        s_ref[slot] = s
        mx_ref[slot] = jnp.max(s, axis=0, keepdims=True)

    def update(c, slot):
        off = pl.multiple_of(c * tk, tk)
        vt = vt_ref[0, :, pl.ds(off, tk)]
        m_prev = m_ref[...]
        m_new = jnp.maximum(m_prev, mx_ref[slot])
        p = jnp.exp2(s_ref[slot] - m_new)
        a = jnp.exp2(m_prev - m_new)
        l_ref[...] = a * l_ref[...] + jnp.sum(p, axis=0, keepdims=True)
        acc_ref[...] = a * acc_ref[...] + jnp.dot(vt, p.astype(BF16), preferred_element_type=F32)
        m_ref[...] = m_new

    scores(0, 0)

    def body(i, carry):
        c = 2 * i
        scores(c + 1, 1)
        update(c, 0)
        scores(c + 2, 0)
        update(c + 1, 1)
        return carry

    lax.fori_loop(0, nk // 2 - 1, body, 0)
    scores(nk - 1, 1)
    update(nk - 2, 0)
    update(nk - 1, 1)
    o_ref[0] = (acc_ref[...] / l_ref[...]).T.astype(o_ref.dtype)


def _oproj_ln_kernel(o_ref, h_ref, w_ref, g_ref, beta_ref, out_ref, *, alpha):
    m = jnp.dot(o_ref[0], w_ref[...], preferred_element_type=F32)
    out_ref[0] = _layer_norm(alpha * h_ref[0] + m, g_ref[...], beta_ref[...])


def _mla_ln(h, w_a, qg, kvg, w_uq, w_uk, w_uv, w_o, g, beta, alpha):
    B, S, D = h.shape
    q_lora, heads, qk = w_uq.shape
    kv_lora, _, nope = w_uk.shape
    vd = w_uv.shape[-1]
    rope = qk - nope
    assert nope == LANES and vd == LANES and rope <= LANES and rope % 2 == 0
    kr = w_a[:, q_lora + kv_lora:]
    wa = jnp.concatenate([w_a[:, :q_lora + kv_lora], _pad_lanes(kr), _pad_lanes(_rot_half(kr))],
                         axis=1).astype(BF16)
    wq_r = w_uq[:, :, nope:]
    wq = jnp.concatenate([w_uq[:, :, :nope].reshape(q_lora, heads * nope),
                          _pad_lanes(wq_r).reshape(q_lora, heads * LANES),
                          _pad_lanes(_rot_half(wq_r)).reshape(q_lora, heads * LANES)], axis=1).astype(BF16)
    wk = w_uk.reshape(kv_lora, heads * nope).astype(BF16)
    wvt = w_uv.reshape(kv_lora, heads * vd).T.astype(BF16)
    wqt = wq.T
    pos = jnp.arange(S, dtype=F32)
    inv = 1.0 / (ROPE_THETA ** (jnp.arange(0, rope, 2, dtype=F32) / rope))
    ang = pos[:, None] * inv[None, :]
    cos = _pad_lanes(jnp.concatenate([jnp.cos(ang), jnp.cos(ang)], axis=1))
    sin = _pad_lanes(jnp.concatenate([jnp.sin(ang), jnp.sin(ang)], axis=1))
    q_scale = float(qk) ** -0.5 * math.log2(math.e)

    tp = _pick(S, 256)
    const = lambda a: pl.BlockSpec(a.shape, lambda b, i: (0,) * a.ndim)
    qg2, kvg2 = qg.reshape(1, q_lora), kvg.reshape(1, kv_lora)
    tok_major = pl.BlockSpec((tp, LANES), lambda b, i: (i, 0))
    feat_major = pl.BlockSpec((LANES, tp), lambda b, i: (0, i))
    qt, k, vt = pl.pallas_call(
        functools.partial(_mla_proj_kernel, q_lora=q_lora, kv_lora=kv_lora, heads=heads, q_scale=q_scale),
        grid=(B, S // tp),
        in_specs=[pl.BlockSpec((1, tp, D), lambda b, i: (b, i, 0)), const(wa), const(qg2), const(kvg2),
                  const(wqt), const(wk), const(wvt), tok_major, tok_major, feat_major, feat_major],
        out_specs=[pl.BlockSpec((1, 2 * heads * LANES, tp), lambda b, i: (b, 0, i)),
                   pl.BlockSpec((1, tp, 2 * heads * LANES), lambda b, i: (b, i, 0)),
                   pl.BlockSpec((1, heads * LANES, tp), lambda b, i: (b, 0, i))],
        out_shape=[jax.ShapeDtypeStruct((B, 2 * heads * LANES, S), BF16),
                   jax.ShapeDtypeStruct((B, S, 2 * heads * LANES), BF16),
                   jax.ShapeDtypeStruct((B, heads * LANES, S), BF16)],
        compiler_params=_params(("parallel", "parallel")),
        name="mla_proj",
    )(h, wa, qg2, kvg2, wqt, wk, wvt, cos, sin, cos.T, sin.T)

    tq = _pick(S, FLASH_TQ)
    tk = _pick(S, FLASH_TK)
    o = pl.pallas_call(
        functools.partial(_flash_kernel, tk=tk),
        grid=(B, heads, S // tq),
        in_specs=[pl.BlockSpec((1, 2 * LANES, tq), lambda b, hh, i: (b, hh, i)),
                  pl.BlockSpec((1, S, 2 * LANES), lambda b, hh, i: (b, 0, hh)),
                  pl.BlockSpec((1, LANES, S), lambda b, hh, i: (b, hh, 0))],
        out_specs=pl.BlockSpec((1, tq, LANES), lambda b, hh, i: (b, i, hh)),
        out_shape=jax.ShapeDtypeStruct((B, S, heads * LANES), BF16),
        scratch_shapes=[pltpu.VMEM((2, tk, tq), BF16), pltpu.VMEM((2, tk, tq), F32), pltpu.VMEM((2, 1, tq), F32),
                        pltpu.VMEM((1, tq), F32), pltpu.VMEM((1, tq), F32), pltpu.VMEM((LANES, tq), F32)],
        compiler_params=_params(("parallel", "parallel", "arbitrary")),
        name="mla_flash",
    )(qt, k, vt)

    to = _pick(S, 512)
    vec = pl.BlockSpec((1, D), lambda b, i: (0, 0))
    return pl.pallas_call(
        functools.partial(_oproj_ln_kernel, alpha=alpha),
        grid=(B, S // to),
        in_specs=[pl.BlockSpec((1, to, heads * vd), lambda b, i: (b, i, 0)),
                  pl.BlockSpec((1, to, D), lambda b, i: (b, i, 0)),
                  pl.BlockSpec((heads * vd, D), lambda b, i: (0, 0)), vec, vec],
        out_specs=pl.BlockSpec((1, to, D), lambda b, i: (b, i, 0)),
        out_shape=jax.ShapeDtypeStruct((B, S, D), F32),
        compiler_params=_params(("parallel", "parallel")),
        name="mla_oproj_ln",
    )(o, h, w_o.astype(BF16), g.reshape(1, D), beta.reshape(1, D))


def kernel(x, pool_w, pool_b, pool_scale, mla_w_a, mla_q_norm_g, mla_kv_norm_g, mla_w_uq, mla_w_uk,
           mla_w_uv, mla_w_o, ln_mix_g, ln_mix_b, router_w, router_b, exp_w_gate, exp_w_up, exp_w_down,
           sh_w_gate, sh_w_up, sh_w_down, ln_ffn_g, ln_ffn_b):
    B, S, D = x.shape
    depth = ln_mix_g.shape[0]
    alpha = (2 * depth) ** 0.25
    h = x
    for i in range(depth):
        j = i // 2
        if i % 2 == 0:
            h = _pool_ln(h, pool_w[j], pool_b[j], pool_scale[j], ln_mix_g[i], ln_mix_b[i], alpha)
        else:
            h = _mla_ln(h, mla_w_a[j], mla_q_norm_g[j], mla_kv_norm_g[j], mla_w_uq[j], mla_w_uk[j],
                        mla_w_uv[j], mla_w_o[j], ln_mix_g[i], ln_mix_b[i], alpha)
        h = _moe_ln(h.reshape(B * S, D), router_w[i], router_b[i], exp_w_gate, exp_w_up, exp_w_down, i,
                    sh_w_gate[i], sh_w_up[i], sh_w_down[i], ln_ffn_g[i], ln_ffn_b[i],
                    alpha).reshape(B, S, D)
    return h
```

```python
import functools
import math

import jax
import jax.numpy as jnp
from jax import lax
from jax.experimental import pallas as pl
from jax.experimental.pallas import tpu as pltpu

F32 = jnp.float32
BF16 = jnp.bfloat16

LN_EPS = 1e-5
RMS_EPS = 1e-6
POOL_WINDOWS = (2, 4, 8, 16)
POOL_HALO = 8
ROPE_THETA = 10000.0
TOP_K = 8
N_GROUPS = 8
TOPK_GROUPS = 4
ROUTED_SCALE = 2.5
LANES = 128
VMEM_LIMIT = 48 * 1024 * 1024

_NT = (((1,), (1,)), ((), ()))


def _pick(n, pref):
    t = min(pref, n)
    while n % t:
        t //= 2
    assert t >= 8 and n % t == 0, (n, pref)
    return t


def _layer_norm(z, g, b):
    mu = jnp.mean(z, axis=-1, keepdims=True)
    zc = z - mu
    var = jnp.mean(zc * zc, axis=-1, keepdims=True)
    return zc * lax.rsqrt(var + LN_EPS) * g + b


def _sigmoid(x):
    return 1.0 / (1.0 + jnp.exp(-x))


def _params(sem):
    return pltpu.CompilerParams(dimension_semantics=sem, vmem_limit_bytes=VMEM_LIMIT)


def _pool_ln_kernel(xc_ref, xp_ref, xn_ref, w_ref, b_ref, sc_ref, g_ref, beta_ref, o_ref, ext_ref,
                    *, tile, seq, alpha):
    i = pl.program_id(1)
    n_tiles = pl.num_programs(1)
    xc = xc_ref[0]
    d_model = xc.shape[-1]
    ch = d_model // len(POOL_WINDOWS)
    ext_ref[0:POOL_HALO, :] = jnp.where(i > 0, xp_ref[0], 0.0)
    ext_ref[POOL_HALO:POOL_HALO + tile, :] = xc
    ext_ref[POOL_HALO + tile:2 * POOL_HALO + tile, :] = jnp.where(i < n_tiles - 1, xn_ref[0], 0.0)
    pos = i * tile + lax.broadcasted_iota(jnp.int32, (tile, 1), 0)
    outs = []
    for g, win in enumerate(POOL_WINDOWS):
        half = win // 2
        cols = slice(g * ch, (g + 1) * ch)
        acc = ext_ref[POOL_HALO - half:POOL_HALO - half + tile, cols]
        for j in range(-half + 1, half):
            acc = acc + ext_ref[POOL_HALO + j:POOL_HALO + j + tile, cols]
        cnt = (jnp.minimum(pos + half, seq) - jnp.maximum(pos - half, 0)).astype(F32)
        pooled = acc / cnt - xc[:, cols]
        outs.append(jnp.dot(pooled.astype(BF16), w_ref[g], preferred_element_type=F32))
    y = jnp.concatenate(outs, axis=-1)
    y = (y + b_ref[...]) * sc_ref[...]
    o_ref[0] = _layer_norm(alpha * xc + y, g_ref[...], beta_ref[...])


def _pool_ln(x, w, b, sc, g, beta, alpha):
    B, S, D = x.shape
    tile = _pick(S, 512)
    nh = tile // POOL_HALO
    last = S // POOL_HALO - 1
    row = lambda v: v.reshape(1, D)
    vec = pl.BlockSpec((1, D), lambda bi, i: (0, 0))
    return pl.pallas_call(
        functools.partial(_pool_ln_kernel, tile=tile, seq=S, alpha=alpha),
        grid=(B, S // tile),
        in_specs=[
            pl.BlockSpec((1, tile, D), lambda bi, i: (bi, i, 0)),
            pl.BlockSpec((1, POOL_HALO, D), lambda bi, i: (bi, jnp.maximum(i * nh - 1, 0), 0)),
            pl.BlockSpec((1, POOL_HALO, D), lambda bi, i: (bi, jnp.minimum((i + 1) * nh, last), 0)),
            pl.BlockSpec(w.shape, lambda bi, i: (0, 0, 0)),
            vec, vec, vec, vec,
        ],
        out_specs=pl.BlockSpec((1, tile, D), lambda bi, i: (bi, i, 0)),
        out_shape=jax.ShapeDtypeStruct((B, S, D), F32),
        scratch_shapes=[pltpu.VMEM((tile + 2 * POOL_HALO, D), F32)],
        compiler_params=_params(("parallel", "parallel")),
        name="pool_ln",
    )(x, x, x, w.astype(BF16), row(b), row(sc), row(g), row(beta))


def _router_kernel(h_ref, whi_ref, wlo_ref, rb_ref, tri_ref, ltri_ref,
                   gate_ref, lpos_ref, cnt_ref, tcnt_ref, toff_ref, tbase_ref, ttot_ref):
    i = pl.program_id(0)

    @pl.when(i == 0)
    def _():
        cnt_ref[...] = jnp.zeros_like(cnt_ref)

    x = h_ref[...]
    xhi = x.astype(BF16)
    xlo = (x - xhi.astype(F32)).astype(BF16)
    whi = whi_ref[...]
    logits = (lax.dot_general(whi, xhi, _NT, preferred_element_type=F32)
              + lax.dot_general(whi, xlo, _NT, preferred_element_type=F32)
              + lax.dot_general(wlo_ref[...], xhi, _NT, preferred_element_type=F32))
    n_exp, tile = logits.shape
    per = n_exp // N_GROUPS
    scores = _sigmoid(logits)
    biased = scores + rb_ref[...]
    neg = -jnp.inf

    sub = lax.broadcasted_iota(jnp.int32, (per, tile), 0).astype(F32)
    gscore = []
    for g in range(N_GROUPS):
        bg = biased[g * per:(g + 1) * per, :]
        m1 = jnp.max(bg, axis=0, keepdims=True)
        i1 = jnp.min(jnp.where(bg == m1, sub, float(per)), axis=0, keepdims=True)
        m2 = jnp.max(jnp.where(sub == i1, neg, bg), axis=0, keepdims=True)
        gscore.append(m1 + m2)
    masked = []
    for g in range(N_GROUPS):
        beat = jnp.zeros_like(gscore[g])
        for o in range(N_GROUPS):
            if o == g:
                continue
            wins = (gscore[o] >= gscore[g]) if o < g else (gscore[o] > gscore[g])
            beat = beat + jnp.where(wins, 1.0, 0.0)
        keep = jnp.broadcast_to(beat < float(TOPK_GROUPS), (per, tile))
        masked.append(jnp.where(keep, biased[g * per:(g + 1) * per, :], neg))
    masked = jnp.concatenate(masked, axis=0)

    row = lax.broadcasted_iota(jnp.int32, (n_exp, tile), 0).astype(F32)
    chosen = jnp.zeros((n_exp, tile), F32)
    idxs, gates = [], []
    for _ in range(TOP_K):
        m = jnp.max(masked, axis=0, keepdims=True)
        idx = jnp.min(jnp.where(masked == m, row, float(n_exp)), axis=0, keepdims=True)
        hit = row == idx
        gates.append(jnp.sum(jnp.where(hit, scores, 0.0), axis=0, keepdims=True))
        masked = jnp.where(hit, neg, masked)
        chosen = jnp.where(hit, 1.0, chosen)
        idxs.append(idx)
    gsum = gates[0]
    for k in range(1, TOP_K):
        gsum = gsum + gates[k]

    earlier = jnp.dot(chosen.astype(BF16), tri_ref[...], preferred_element_type=F32)
    count = jnp.sum(chosen, axis=1, keepdims=True)
    run = count + (count - 2.0 * jnp.floor(0.5 * count))
    lower = jnp.dot(ltri_ref[...], jnp.broadcast_to(run, (n_exp, LANES)).astype(BF16),
                    preferred_element_type=F32)[:, :1]
    order = earlier + lower
    for k in range(TOP_K):
        gate_ref[k:k + 1, :] = gates[k] / gsum * ROUTED_SCALE
        lpos_k = jnp.sum(jnp.where(row == idxs[k], order, 0.0), axis=0, keepdims=True)
        lpos_ref[k:k + 1, :] = lpos_k.astype(jnp.int32)
    tcnt_ref[...] = run.astype(jnp.int32)
    toff_ref[...] = lower.astype(jnp.int32)
    tbase_ref[...] = cnt_ref[...].astype(jnp.int32)
    ttot_ref[...] = jnp.sum(run, axis=0, keepdims=True).astype(jnp.int32)
    cnt_ref[...] = cnt_ref[...] + run


def _router(h2, rw, rb):
    N, D = h2.shape
    E = rw.shape[1]
    tile = _pick(N, MOE_TILE)
    assert tile <= 256
    n_tiles = N // tile
    wt = rw.T
    whi = wt.astype(BF16)
    wlo = (wt - whi.astype(F32)).astype(BF16)
    ar = jnp.arange(tile)
    tri = (ar[:, None] < ar[None, :]).astype(BF16)
    ae = jnp.arange(E)
    ltri = (ae[None, :] < ae[:, None]).astype(BF16)
    const = lambda shape: pl.BlockSpec(shape, lambda i: (0, 0))
    kn = pl.BlockSpec((TOP_K, tile), lambda i: (0, i))
    table = pl.BlockSpec((None, E, 1), lambda i: (i, 0, 0))
    table_shape = jax.ShapeDtypeStruct((n_tiles, E, 1), jnp.int32)
    return pl.pallas_call(
        _router_kernel,
        grid=(n_tiles,),
        in_specs=[pl.BlockSpec((tile, D), lambda i: (i, 0)), const((E, D)), const((E, D)),
                  const((E, 1)), const((tile, tile)), const((E, E))],
        out_specs=[kn, kn, const((E, 1)), table, table, table,
                   pl.BlockSpec((None, 1, 1), lambda i: (i, 0, 0))],
        out_shape=[jax.ShapeDtypeStruct((TOP_K, N), F32), jax.ShapeDtypeStruct((TOP_K, N), jnp.int32),
                   jax.ShapeDtypeStruct((E, 1), F32), table_shape, table_shape, table_shape,
                   jax.ShapeDtypeStruct((n_tiles, 1, 1), jnp.int32)],
        compiler_params=_params(("arbitrary",)),
        name="moe_router",
    )(h2, whi, wlo, rb.reshape(E, 1), tri, ltri)


def _pack_pairs(x, rounded=False):
    half = x.shape[-1] // 2
    bits = lambda v: lax.bitcast_convert_type(v if rounded else v.astype(BF16).astype(F32), jnp.uint32)
    return (bits(x[:, :half]) & jnp.uint32(0xFFFF0000)) | (bits(x[:, half:]) >> 16)


def _unpack_pairs(w):
    hi = lax.bitcast_convert_type(w & jnp.uint32(0xFFFF0000), F32)
    lo = lax.bitcast_convert_type(w << 16, F32)
    return jnp.concatenate([hi.astype(BF16), lo.astype(BF16)], axis=-1)


def _to_slab(slab_ref, x, rounded=False, first=0):
    w = _pack_pairs(x, rounded)
    lines = w.shape[1] // LANES
    for a in range(lines):
        slab_ref[pl.ds(first * lines + a, w.shape[0], stride=lines), :] = w[:, a * LANES:(a + 1) * LANES]


def _from_slab(slab_ref, rows, lines, valid, first=0):
    w = jnp.concatenate([slab_ref[pl.ds(first * lines + a, rows, stride=lines), :] for a in range(lines)],
                        axis=-1)
    r = first + lax.broadcasted_iota(jnp.int32, (rows, 1), 0)
    return _unpack_pairs(jnp.where(r < valid, w, jnp.uint32(0)))


def _start_runs(toff_ref, gst_ref, tcnt_ref, local, remote, sem, lines, to_remote):
    def body(e, c):
        cnt, off, gst = tcnt_ref[0, 0, e], toff_ref[0, 0, e], gst_ref[0, 0, e]

        @pl.when(cnt > 0)
        def _():
            loc = local.at[pl.ds(pl.multiple_of(off * lines, 8), cnt * lines), :]
            rem = remote.at[pl.ds(pl.multiple_of(gst * lines, 8), cnt * lines), :]
            (pltpu.make_async_copy(loc, rem, sem) if to_remote else pltpu.make_async_copy(rem, loc, sem)).start()
        return c
    lax.fori_loop(0, tcnt_ref.shape[2], body, 0, unroll=RUN_UNROLL)


def _wait_all_runs(slab, remote, sem, total_lines):
    n = pl.multiple_of(total_lines, 8)
    pltpu.make_async_copy(slab.at[pl.ds(0, n), :], remote.at[pl.ds(0, n), :], sem).wait()


def _sorted_rows(tile, n_exp):
    return TOP_K * tile + n_exp


def _dispatch_kernel(toff_ref, gst_ref, tcnt_ref, ttot_ref, tprev_ref, lpos_ref, h_ref, xs_ref, slabs, sems,
                     *, lines):
    i = pl.program_id(0)
    slot = lax.rem(i, 2)
    slab = slabs.at[slot]
    tile = h_ref.shape[0]
    rows = slab.shape[0] // lines
    hb = h_ref[...].astype(BF16)
    chunk = max(c for c in range(8, 257, 8) if rows % c == 0)
    srow = lax.broadcasted_iota(jnp.int32, (chunk, tile), 0).astype(F32).astype(BF16)
    one, zero = jnp.ones((), BF16), jnp.zeros((), BF16)
    for first in range(0, rows, chunk):
        perm = jnp.full((chunk, tile), zero)
        for k in range(TOP_K):
            local = lpos_ref[k:k + 1, :] - first
            inside = jnp.logical_and(local >= 0, local < chunk)
            local = jnp.where(inside, local, -1).astype(F32).astype(BF16)
            perm = jnp.where(srow == local, one, perm)
        _to_slab(slab, jnp.dot(perm, hb, preferred_element_type=F32), rounded=True, first=first)
    _start_runs(toff_ref, gst_ref, tcnt_ref, slab, xs_ref, sems.at[slot], lines, to_remote=True)

    @pl.when(i > 0)
    def _():
        _wait_all_runs(slabs.at[1 - slot], xs_ref, sems.at[1 - slot], tprev_ref[0, 0, 0] * lines)

    @pl.when(i == pl.num_programs(0) - 1)
    def _():
        _wait_all_runs(slab, xs_ref, sems.at[slot], ttot_ref[0, 0, 0] * lines)


def _run_specs(n_exp):
    smem = lambda n: pl.BlockSpec((1, 1, n), lambda i: (i, 0, 0), memory_space=pltpu.SMEM)
    return [smem(n_exp), smem(n_exp), smem(n_exp), smem(1)]


def _slab_lines(d_model):
    lines = d_model // (2 * LANES)
    assert lines * 2 * LANES == d_model and (2 * lines) % 8 == 0
    return lines


def _dispatch(h2, lpos, runs, n_slots):
    N, D = h2.shape
    E = runs[0].shape[2]
    tile = _pick(N, MOE_TILE)
    lines = _slab_lines(D)
    return pl.pallas_call(
        functools.partial(_dispatch_kernel, lines=lines),
        grid=(N // tile,),
        in_specs=_run_specs(E) + [
            pl.BlockSpec((1, 1, 1), lambda i: (jnp.maximum(i - 1, 0), 0, 0), memory_space=pltpu.SMEM),
            pl.BlockSpec((TOP_K, tile), lambda i: (0, i)), pl.BlockSpec((tile, D), lambda i: (i, 0))],
        out_specs=pl.BlockSpec(memory_space=pl.ANY),
        out_shape=jax.ShapeDtypeStruct((n_slots * lines, LANES), jnp.uint32),
        scratch_shapes=[pltpu.VMEM((2, _sorted_rows(tile, E) * lines, LANES), jnp.uint32),
                        pltpu.SemaphoreType.DMA((2,))],
        compiler_params=_params(("arbitrary",)),
        name="moe_dispatch",
    )(*runs, runs[3], lpos, h2)


def _expert_kernel(pstart_ref, cnt_ref, xs_hbm, wg_hbm, wu_hbm, wd_hbm, ys_hbm,
                   xbuf, ybuf, xo, yo, wg_f, wu_f, wd_f, wg_s, wu_s, wd_s, wsem, xsem, ysem, osem,
                   *, layer, lines, cap, chunk):
    e = pl.program_id(0)
    last = pl.num_programs(0) - 1
    n_in = xbuf.shape[0]
    islot = lax.rem(e, n_in)
    slot = lax.rem(e, 2)

    def weight_copies(ex, sl):
        return [pltpu.make_async_copy(w.at[layer, ex], f.at[sl], wsem.at[sl])
                for w, f in ((wg_hbm, wg_f), (wu_hbm, wu_f), (wd_hbm, wd_f))]

    def head_lines(ex):
        return jnp.minimum(cnt_ref[ex], cap) * lines

    def first_line(ex, row=0):
        return pl.multiple_of((pstart_ref[ex] + row) * lines, 8)

    def x_copy(ex, sl):
        n = head_lines(ex)
        return pltpu.make_async_copy(xs_hbm.at[pl.ds(first_line(ex), n), :], xbuf.at[sl, pl.ds(0, n), :],
                                     xsem.at[sl])

    def y_copy(ex, sl):
        n = head_lines(ex)
        return pltpu.make_async_copy(ybuf.at[sl, pl.ds(0, n), :], ys_hbm.at[pl.ds(first_line(ex), n), :],
                                     ysem.at[sl])

    def fetch(ex, sl):
        for cp in weight_copies(ex, sl):
            cp.start()

        @pl.when(cnt_ref[ex] > 0)
        def _():
            x_copy(ex, sl).start()

    @pl.when(e == 0)
    def _():
        for first in range(n_in - 1):
            @pl.when(first <= last)
            def _(first=first):
                fetch(first, first)

    for cp in weight_copies(e, islot):
        cp.wait()
    ahead = jnp.minimum(e + n_in - 1, last)

    @pl.when(e + n_in - 1 <= last)
    def _():
        fetch(ahead, lax.rem(ahead, n_in))

    wg_s[...] = wg_f[islot].astype(BF16)
    wu_s[...] = wu_f[islot].astype(BF16)
    wd_s[...] = wd_f[islot].astype(BF16)

    before = jnp.maximum(e - 2, 0)

    @pl.when(jnp.logical_and(e >= 2, cnt_ref[before] > 0))
    def _():
        y_copy(before, slot).wait()

    def ffn(x_ref, rows, valid):
        x = _from_slab(x_ref, rows, lines, valid)
        hg = jnp.dot(x, wg_s[...], preferred_element_type=F32)
        hu = jnp.dot(x, wu_s[...], preferred_element_type=F32)
        hb = hg * _sigmoid(hg) * hu
        return jnp.dot(hb.astype(BF16), wd_s[...], preferred_element_type=F32)

    count = cnt_ref[e]

    @pl.when(count > 0)
    def _():
        x_copy(e, islot).wait()
        head = jnp.minimum(count, cap)
        lower = 0
        for rows in range(chunk, cap + 1, chunk):
            @pl.when(jnp.logical_and(head > lower, head <= rows))
            def _(rows=rows):
                _to_slab(ybuf.at[slot], ffn(xbuf.at[islot], rows, head))
            lower = rows
        y_copy(e, slot).start()

        @pl.when(count > cap)
        def _():
            def extra(j, carry):
                row = cap + j * chunk
                n = jnp.minimum(chunk, count - row) * lines
                fetch = pltpu.make_async_copy(xs_hbm.at[pl.ds(first_line(e, row), n), :],
                                              xo.at[pl.ds(0, n), :], osem)
                fetch.start()
                fetch.wait()
                _to_slab(yo, ffn(xo, chunk, count - row))
                store = pltpu.make_async_copy(yo.at[pl.ds(0, n), :],
                                              ys_hbm.at[pl.ds(first_line(e, row), n), :], osem)
                store.start()
                store.wait()
                return carry
            lax.fori_loop(0, (count - cap + chunk - 1) // chunk, extra, 0)

    @pl.when(e == last)
    def _():
        @pl.when(count > 0)
        def _():
            y_copy(e, slot).wait()
        prev = jnp.maximum(e - 1, 0)

        @pl.when(jnp.logical_and(e >= 1, cnt_ref[prev] > 0))
        def _():
            y_copy(prev, 1 - slot).wait()


def _experts(xs, pstart, counts, wg, wu, wd, layer):
    E, D, Fe = wg.shape[-3:]
    lines = _slab_lines(D)
    cap, chunk = EXPERT_CAP, EXPERT_CHUNK
    assert cap % chunk == 0 and (chunk * lines) % 8 == 0
    n_in = EXPERT_AHEAD + 1
    hbm = pl.BlockSpec(memory_space=pl.ANY)
    slab = lambda rows: pltpu.VMEM(rows, jnp.uint32)
    return pl.pallas_call(
        functools.partial(_expert_kernel, layer=layer, lines=lines, cap=cap, chunk=chunk),
        grid_spec=pltpu.PrefetchScalarGridSpec(
            num_scalar_prefetch=2,
            grid=(E,),
            in_specs=[hbm, hbm, hbm, hbm],
            out_specs=hbm,
            scratch_shapes=[slab((n_in, cap * lines, LANES)), slab((2, cap * lines, LANES)),
                            slab((chunk * lines, LANES)), slab((chunk * lines, LANES)),
                            pltpu.VMEM((n_in, D, Fe), F32), pltpu.VMEM((n_in, D, Fe), F32),
                            pltpu.VMEM((n_in, Fe, D), F32),
                            pltpu.VMEM((D, Fe), BF16), pltpu.VMEM((D, Fe), BF16), pltpu.VMEM((Fe, D), BF16),
                            pltpu.SemaphoreType.DMA((n_in,)), pltpu.SemaphoreType.DMA((n_in,)),
                            pltpu.SemaphoreType.DMA((2,)), pltpu.SemaphoreType.DMA(())],
        ),
        out_shape=jax.ShapeDtypeStruct(xs.shape, jnp.uint32),
        compiler_params=_params(("arbitrary",)),
        name="moe_experts",
    )(pstart, counts, xs, wg, wu, wd)


def _combine_kernel(toff_ref, gst_ref, tcnt_ref, ttot_ref, ntoff_ref, ngst_ref, ntcnt_ref, lpos_ref, gate_ref,
                    h_ref, ys_ref, sg_ref, su_ref, sd_ref, g_ref, beta_ref, o_ref, slabs, sems, *, lines, alpha):
    i = pl.program_id(0)
    slot = lax.rem(i, 2)
    slab, sem = slabs.at[slot], sems.at[slot]

    @pl.when(i == 0)
    def _():
        _start_runs(toff_ref, gst_ref, tcnt_ref, slab, ys_ref, sem, lines, to_remote=False)

    @pl.when(i < pl.num_programs(0) - 1)
    def _():
        _start_runs(ntoff_ref, ngst_ref, ntcnt_ref, slabs.at[1 - slot], ys_ref, sems.at[1 - slot], lines,
                    to_remote=False)

    h = h_ref[...]
    tile = h.shape[0]
    rows = slab.shape[0] // lines
    hb = h.astype(BF16)
    sgate = jnp.dot(hb, sg_ref[...], preferred_element_type=F32)
    sup = jnp.dot(hb, su_ref[...], preferred_element_type=F32)
    mid = sgate * _sigmoid(sgate) * sup
    acc = jnp.dot(mid.astype(BF16), sd_ref[...], preferred_element_type=F32)
    total = ttot_ref[0, 0, 0]
    _wait_all_runs(slab, ys_ref, sem, total * lines)
    chunk = max(c for c in range(8, 257, 8) if rows % c == 0)
    scol = lax.broadcasted_iota(jnp.int32, (tile, chunk), 1).astype(F32).astype(BF16)
    gates = [gate_ref[:, k:k + 1].astype(BF16) for k in range(TOP_K)]
    for first in range(0, rows, chunk):
        weights = jnp.zeros((tile, chunk), BF16)
        for k in range(TOP_K):
            local = lpos_ref[:, k:k + 1] - first
            inside = jnp.logical_and(local >= 0, local < chunk)
            local = jnp.where(inside, local, -1).astype(F32).astype(BF16)
            weights = jnp.where(scol == local, gates[k], weights)
        acc = acc + jnp.dot(weights, _from_slab(slab, chunk, lines, total, first),
                            preferred_element_type=F32)
    o_ref[...] = _layer_norm(alpha * h + acc, g_ref[...], beta_ref[...])


def _combine(h2, lpos_nk, gate_nk, runs, ys, sg, su, sd, g, beta, alpha):
    N, D = h2.shape
    Fs = sg.shape[1]
    E = runs[0].shape[2]
    tile = _pick(N, MOE_TILE)
    lines = _slab_lines(D)
    const = lambda shape: pl.BlockSpec(shape, lambda i: (0, 0))
    nk = pl.BlockSpec((tile, TOP_K), lambda i: (i, 0))
    last = N // tile - 1
    following = pl.BlockSpec((1, 1, E), lambda i: (jnp.minimum(i + 1, last), 0, 0), memory_space=pltpu.SMEM)
    return pl.pallas_call(
        functools.partial(_combine_kernel, lines=lines, alpha=alpha),
        grid=(N // tile,),
        in_specs=_run_specs(E) + [following] * 3 + [
            nk, nk, pl.BlockSpec((tile, D), lambda i: (i, 0)), pl.BlockSpec(memory_space=pl.ANY),
            const((D, Fs)), const((D, Fs)), const((Fs, D)), const((1, D)), const((1, D))],
        out_specs=pl.BlockSpec((tile, D), lambda i: (i, 0)),
        out_shape=jax.ShapeDtypeStruct((N, D), F32),
        scratch_shapes=[pltpu.VMEM((2, _sorted_rows(tile, E) * lines, LANES), jnp.uint32),
                        pltpu.SemaphoreType.DMA((2,))],
        compiler_params=_params(("arbitrary",)),
        name="moe_combine",
    )(*runs, *runs[:3], lpos_nk, gate_nk, h2, ys, sg.astype(BF16), su.astype(BF16), sd.astype(BF16),
      g.reshape(1, D), beta.reshape(1, D))


FLASH_TQ = 512
FLASH_TK = 512
MOE_TILE = 256
RUN_UNROLL = 8
EXPERT_CAP = 1024
EXPERT_CHUNK = 64
EXPERT_AHEAD = 2


def _moe_ln(h2, rw, rb, wg, wu, wd, layer, sg, su, sd, g, beta, alpha):
    N, D = h2.shape
    E = rw.shape[1]
    gate, lpos, cnt, tcnt, toff, tbase, ttot = _router(h2, rw, rb)
    n_tiles = ttot.shape[0]
    counts = cnt[:, 0].astype(jnp.int32)
    pstart = (jnp.cumsum(counts) - counts).astype(jnp.int32)
    n_slots = N * TOP_K + n_tiles * E
    per_tile = lambda a: a.reshape(n_tiles, 1, E)
    runs = (per_tile(toff), per_tile(tbase) + pstart[None, None, :], per_tile(tcnt), ttot)
    xs = _dispatch(h2, lpos, runs, n_slots)
    ys = _experts(xs, pstart, counts, wg, wu, wd, layer)
    return _combine(h2, lpos.T, gate.T, runs, ys, sg, su, sd, g, beta, alpha)


def _mla_proj_kernel(h_ref, wa_ref, qg_ref, kvg_ref, wqt_ref, wk_ref, wvt_ref, cos_ref, sin_ref,
                     cost_ref, sint_ref, qt_ref, k_ref, vt_ref, *, q_lora, kv_lora, heads, q_scale):
    x = h_ref[0].astype(BF16)
    a = jnp.dot(x, wa_ref[...], preferred_element_type=F32)

    def rms(v, g):
        return v * lax.rsqrt(jnp.mean(v * v, axis=-1, keepdims=True) + RMS_EPS) * g

    q_lat = rms(a[:, :q_lora], qg_ref[...]).astype(BF16)
    kv_lat = rms(a[:, q_lora:q_lora + kv_lora], kvg_ref[...]).astype(BF16)
    o = q_lora + kv_lora
    k_rope = (a[:, o:o + LANES] * cos_ref[...] + a[:, o + LANES:o + 2 * LANES] * sin_ref[...]).astype(BF16)
    qa = lax.dot_general(wqt_ref[...], q_lat, _NT, preferred_element_type=F32)
    kn = jnp.dot(kv_lat, wk_ref[...], preferred_element_type=F32)
    vt_ref[0] = lax.dot_general(wvt_ref[...], kv_lat, _NT, preferred_element_type=F32).astype(BF16)
    cost = cost_ref[...]
    sint = sint_ref[...]
    hd = heads * LANES
    for h in range(heads):
        r0 = slice(h * LANES, (h + 1) * LANES)
        r1 = slice(hd + h * LANES, hd + (h + 1) * LANES)
        r2 = slice(2 * hd + h * LANES, 2 * hd + (h + 1) * LANES)
        qt_ref[0, 2 * h * LANES:(2 * h + 1) * LANES, :] = (qa[r0] * q_scale).astype(BF16)
        qt_ref[0, (2 * h + 1) * LANES:(2 * h + 2) * LANES, :] = (
            (qa[r1] * cost + qa[r2] * sint) * q_scale).astype(BF16)
        k_ref[0, :, 2 * h * LANES:(2 * h + 1) * LANES] = kn[:, r0].astype(BF16)
        k_ref[0, :, (2 * h + 1) * LANES:(2 * h + 2) * LANES] = k_rope


def _rot_half(w):
    r = w.shape[-1] // 2
    return jnp.concatenate([-w[..., r:], w[..., :r]], axis=-1)


def _pad_lanes(w):
    return jnp.pad(w, [(0, 0)] * (w.ndim - 1) + [(0, LANES - w.shape[-1])])


FLASH_HEADROOM = 64.0
FLASH_PROBE = 16


def _flash_kernel(qt_ref, k_ref, vt_ref, o_ref, p_ref, s_ref, mx_ref, m_ref, l_ref, acc_ref, *, tk):
    qt = qt_ref[0]
    nk = k_ref.shape[1] // tk
    assert nk % 2 == 0 and nk >= 2

    def scores(c):
        off = pl.multiple_of(c * tk, tk)
        return jnp.dot(k_ref[0, pl.ds(off, tk), :], qt, preferred_element_type=F32)

    m0 = jnp.max(jnp.dot(k_ref[0, 0:FLASH_PROBE, :], qt, preferred_element_type=F32), axis=0, keepdims=True)
    acc_ref[...] = jnp.zeros_like(acc_ref)

    def probs(c, slot, carry):
        l, top = carry
        s = scores(c)
        p = jnp.exp2(s - m0)
        p_ref[slot] = p.astype(BF16)
        return l + jnp.sum(p, axis=0, keepdims=True), jnp.maximum(top, jnp.max(s, axis=0, keepdims=True))

    def accumulate(c, slot):
        off = pl.multiple_of(c * tk, tk)
        acc_ref[...] += jnp.dot(vt_ref[0, :, pl.ds(off, tk)], p_ref[slot], preferred_element_type=F32)

    carry = probs(0, 0, (jnp.zeros_like(m0), m0))
    for c in range(nk - 1):
        carry = probs(c + 1, (c + 1) % 2, carry)
        accumulate(c, c % 2)
    accumulate(nk - 1, (nk - 1) % 2)
    l, top = carry
    o_ref[0] = (acc_ref[...] / l).T.astype(o_ref.dtype)

    @pl.when(jnp.max(top - m0) > FLASH_HEADROOM)
    def _():
        _flash_online(qt, k_ref, vt_ref, o_ref, s_ref, mx_ref, m_ref, l_ref, acc_ref, tk)


def _flash_online(qt, k_ref, vt_ref, o_ref, s_ref, mx_ref, m_ref, l_ref, acc_ref, tk):
    nk = k_ref.shape[1] // tk
    assert nk % 2 == 0 and nk >= 2
    m_ref[...] = jnp.full_like(m_ref, -jnp.inf)
    l_ref[...] = jnp.zeros_like(l_ref)
    acc_ref[...] = jnp.zeros_like(acc_ref)

    def scores(c, slot):
        off = pl.multiple_of(c * tk, tk)
        s = jnp.dot(k_ref[0, pl.ds(off, tk), :], qt, preferred_element_type=F32)
        s_ref[slot] = s
        mx_ref[slot] = jnp.max(s, axis=0, keepdims=True)

    def update(c, slot):
        off = pl.multiple_of(c * tk, tk)
        vt = vt_ref[0, :, pl.ds(off, tk)]
        m_prev = m_ref[...]
        m_new = jnp.maximum(m_prev, mx_ref[slot])
        p = jnp.exp2(s_ref[slot] - m_new)
        a = jnp.exp2(m_prev - m_new)
        l_ref[...] = a * l_ref[...] + jnp.sum(p, axis=0, keepdims=True)
        acc_ref[...] = a * acc_ref[...] + jnp.dot(vt, p.astype(BF16), preferred_element_type=F32)
        m_ref[...] = m_new

    scores(0, 0)

    def body(i, carry):
        c = 2 * i
        scores(c + 1, 1)
        update(c, 0)
        scores(c + 2, 0)
        update(c + 1, 1)
        return carry

    lax.fori_loop(0, nk // 2 - 1, body, 0)
    scores(nk - 1, 1)
    update(nk - 2, 0)
    update(nk - 1, 1)
    o_ref[0] = (acc_ref[...] / l_ref[...]).T.astype(o_ref.dtype)


def _oproj_ln_kernel(o_ref, h_ref, w_ref, g_ref, beta_ref, out_ref, *, alpha):
    m = jnp.dot(o_ref[0], w_ref[...], preferred_element_type=F32)
    out_ref[0] = _layer_norm(alpha * h_ref[0] + m, g_ref[...], beta_ref[...])


def _mla_ln(h, w_a, qg, kvg, w_uq, w_uk, w_uv, w_o, g, beta, alpha):
    B, S, D = h.shape
    q_lora, heads, qk = w_uq.shape
    kv_lora, _, nope = w_uk.shape
    vd = w_uv.shape[-1]
    rope = qk - nope
    assert nope == LANES and vd == LANES and rope <= LANES and rope % 2 == 0
    kr = w_a[:, q_lora + kv_lora:]
    wa = jnp.concatenate([w_a[:, :q_lora + kv_lora], _pad_lanes(kr), _pad_lanes(_rot_half(kr))],
                         axis=1).astype(BF16)
    wq_r = w_uq[:, :, nope:]
    wq = jnp.concatenate([w_uq[:, :, :nope].reshape(q_lora, heads * nope),
                          _pad_lanes(wq_r).reshape(q_lora, heads * LANES),
                          _pad_lanes(_rot_half(wq_r)).reshape(q_lora, heads * LANES)], axis=1).astype(BF16)
    wk = w_uk.reshape(kv_lora, heads * nope).astype(BF16)
    wvt = w_uv.reshape(kv_lora, heads * vd).T.astype(BF16)
    wqt = wq.T
    pos = jnp.arange(S, dtype=F32)
    inv = 1.0 / (ROPE_THETA ** (jnp.arange(0, rope, 2, dtype=F32) / rope))
    ang = pos[:, None] * inv[None, :]
    cos = _pad_lanes(jnp.concatenate([jnp.cos(ang), jnp.cos(ang)], axis=1))
    sin = _pad_lanes(jnp.concatenate([jnp.sin(ang), jnp.sin(ang)], axis=1))
    q_scale = float(qk) ** -0.5 * math.log2(math.e)

    tp = _pick(S, 256)
    const = lambda a: pl.BlockSpec(a.shape, lambda b, i: (0,) * a.ndim)
    qg2, kvg2 = qg.reshape(1, q_lora), kvg.reshape(1, kv_lora)
    tok_major = pl.BlockSpec((tp, LANES), lambda b, i: (i, 0))
    feat_major = pl.BlockSpec((LANES, tp), lambda b, i: (0, i))
    qt, k, vt = pl.pallas_call(
        functools.partial(_mla_proj_kernel, q_lora=q_lora, kv_lora=kv_lora, heads=heads, q_scale=q_scale),
        grid=(B, S // tp),
        in_specs=[pl.BlockSpec((1, tp, D), lambda b, i: (b, i, 0)), const(wa), const(qg2), const(kvg2),
                  const(wqt), const(wk), const(wvt), tok_major, tok_major, feat_major, feat_major],
        out_specs=[pl.BlockSpec((1, 2 * heads * LANES, tp), lambda b, i: (b, 0, i)),
                   pl.BlockSpec((1, tp, 2 * heads * LANES), lambda b, i: (b, i, 0)),
                   pl.BlockSpec((1, heads * LANES, tp), lambda b, i: (b, 0, i))],
        out_shape=[jax.ShapeDtypeStruct((B, 2 * heads * LANES, S), BF16),
                   jax.ShapeDtypeStruct((B, S, 2 * heads * LANES), BF16),
                   jax.ShapeDtypeStruct((B, heads * LANES, S), BF16)],
        compiler_params=_params(("parallel", "parallel")),
        name="mla_proj",
    )(h, wa, qg2, kvg2, wqt, wk, wvt, cos, sin, cos.T, sin.T)

    tq = _pick(S, FLASH_TQ)
    tk = _pick(S, FLASH_TK)
    o = pl.pallas_call(
        functools.partial(_flash_kernel, tk=tk),
        grid=(B, heads, S // tq),
        in_specs=[pl.BlockSpec((1, 2 * LANES, tq), lambda b, hh, i: (b, hh, i)),
                  pl.BlockSpec((1, S, 2 * LANES), lambda b, hh, i: (b, 0, hh)),
                  pl.BlockSpec((1, LANES, S), lambda b, hh, i: (b, hh, 0))],
        out_specs=pl.BlockSpec((1, tq, LANES), lambda b, hh, i: (b, i, hh)),
        out_shape=jax.ShapeDtypeStruct((B, S, heads * LANES), BF16),
        scratch_shapes=[pltpu.VMEM((2, tk, tq), BF16), pltpu.VMEM((2, tk, tq), F32), pltpu.VMEM((2, 1, tq), F32),
                        pltpu.VMEM((1, tq), F32), pltpu.VMEM((1, tq), F32), pltpu.VMEM((LANES, tq), F32)],
        compiler_params=_params(("parallel", "parallel", "arbitrary")),
        name="mla_flash",
    )(qt, k, vt)

    to = _pick(S, 512)
    vec = pl.BlockSpec((1, D), lambda b, i: (0, 0))
    return pl.pallas_call(
        functools.partial(_oproj_ln_kernel, alpha=alpha),
        grid=(B, S // to),
        in_specs=[pl.BlockSpec((1, to, heads * vd), lambda b, i: (b, i, 0)),
                  pl.BlockSpec((1, to, D), lambda b, i: (b, i, 0)),
                  pl.BlockSpec((heads * vd, D), lambda b, i: (0, 0)), vec, vec],
        out_specs=pl.BlockSpec((1, to, D), lambda b, i: (b, i, 0)),
        out_shape=jax.ShapeDtypeStruct((B, S, D), F32),
        compiler_params=_params(("parallel", "parallel")),
        name="mla_oproj_ln",
    )(o, h, w_o.astype(BF16), g.reshape(1, D), beta.reshape(1, D))


def kernel(x, pool_w, pool_b, pool_scale, mla_w_a, mla_q_norm_g, mla_kv_norm_g, mla_w_uq, mla_w_uk,
           mla_w_uv, mla_w_o, ln_mix_g, ln_mix_b, router_w, router_b, exp_w_gate, exp_w_up, exp_w_down,
           sh_w_gate, sh_w_up, sh_w_down, ln_ffn_g, ln_ffn_b):
    B, S, D = x.shape
    depth = ln_mix_g.shape[0]
    alpha = (2 * depth) ** 0.25
    h = x
    for i in range(depth):
        j = i // 2
        if i % 2 == 0:
            h = _pool_ln(h, pool_w[j], pool_b[j], pool_scale[j], ln_mix_g[i], ln_mix_b[i], alpha)
        else:
            h = _mla_ln(h, mla_w_a[j], mla_q_norm_g[j], mla_kv_norm_g[j], mla_w_uq[j], mla_w_uk[j],
                        mla_w_uv[j], mla_w_o[j], ln_mix_g[i], ln_mix_b[i], alpha)
        h = _moe_ln(h.reshape(B * S, D), router_w[i], router_b[i], exp_w_gate, exp_w_up, exp_w_down, i,
                    sh_w_gate[i], sh_w_up[i], sh_w_down[i], ln_ffn_g[i], ln_ffn_b[i],
                    alpha).reshape(B, S, D)
    return h
```

```python
import functools
import math

import jax
import jax.numpy as jnp
from jax import lax
from jax.experimental import pallas as pl
from jax.experimental.pallas import tpu as pltpu

F32 = jnp.float32
BF16 = jnp.bfloat16

LN_EPS = 1e-5
RMS_EPS = 1e-6
POOL_WINDOWS = (2, 4, 8, 16)
POOL_HALO = 8
ROPE_THETA = 10000.0
TOP_K = 8
N_GROUPS = 8
TOPK_GROUPS = 4
ROUTED_SCALE = 2.5
LANES = 128
VMEM_LIMIT = 48 * 1024 * 1024

_NT = (((1,), (1,)), ((), ()))


def _pick(n, pref):
    t = min(pref, n)
    while n % t:
        t //= 2
    assert t >= 8 and n % t == 0, (n, pref)
    return t


def _layer_norm(z, g, b):
    mu = jnp.mean(z, axis=-1, keepdims=True)
    zc = z - mu
    var = jnp.mean(zc * zc, axis=-1, keepdims=True)
    return zc * lax.rsqrt(var + LN_EPS) * g + b


def _sigmoid(x):
    return 1.0 / (1.0 + jnp.exp(-x))


def _params(sem):
    return pltpu.CompilerParams(dimension_semantics=sem, vmem_limit_bytes=VMEM_LIMIT)


def _pool_ln_kernel(xc_ref, xp_ref, xn_ref, w_ref, b_ref, sc_ref, g_ref, beta_ref, o_ref, ext_ref,
                    *, tile, seq, alpha):
    i = pl.program_id(1)
    n_tiles = pl.num_programs(1)
    xc = xc_ref[0]
    d_model = xc.shape[-1]
    ch = d_model // len(POOL_WINDOWS)
    ext_ref[0:POOL_HALO, :] = jnp.where(i > 0, xp_ref[0], 0.0)
    ext_ref[POOL_HALO:POOL_HALO + tile, :] = xc
    ext_ref[POOL_HALO + tile:2 * POOL_HALO + tile, :] = jnp.where(i < n_tiles - 1, xn_ref[0], 0.0)
    pos = i * tile + lax.broadcasted_iota(jnp.int32, (tile, 1), 0)
    outs = []
    for g, win in enumerate(POOL_WINDOWS):
        half = win // 2
        cols = slice(g * ch, (g + 1) * ch)
        acc = ext_ref[POOL_HALO - half:POOL_HALO - half + tile, cols]
        for j in range(-half + 1, half):
            acc = acc + ext_ref[POOL_HALO + j:POOL_HALO + j + tile, cols]
        cnt = (jnp.minimum(pos + half, seq) - jnp.maximum(pos - half, 0)).astype(F32)
        pooled = acc / cnt - xc[:, cols]
        outs.append(jnp.dot(pooled.astype(BF16), w_ref[g], preferred_element_type=F32))
    y = jnp.concatenate(outs, axis=-1)
    y = (y + b_ref[...]) * sc_ref[...]
    o_ref[0] = _layer_norm(alpha * xc + y, g_ref[...], beta_ref[...])


def _pool_ln(x, w, b, sc, g, beta, alpha):
    B, S, D = x.shape
    tile = _pick(S, 512)
    nh = tile // POOL_HALO
    last = S // POOL_HALO - 1
    row = lambda v: v.reshape(1, D)
    vec = pl.BlockSpec((1, D), lambda bi, i: (0, 0))
    return pl.pallas_call(
        functools.partial(_pool_ln_kernel, tile=tile, seq=S, alpha=alpha),
        grid=(B, S // tile),
        in_specs=[
            pl.BlockSpec((1, tile, D), lambda bi, i: (bi, i, 0)),
            pl.BlockSpec((1, POOL_HALO, D), lambda bi, i: (bi, jnp.maximum(i * nh - 1, 0), 0)),
            pl.BlockSpec((1, POOL_HALO, D), lambda bi, i: (bi, jnp.minimum((i + 1) * nh, last), 0)),
            pl.BlockSpec(w.shape, lambda bi, i: (0, 0, 0)),
            vec, vec, vec, vec,
        ],
        out_specs=pl.BlockSpec((1, tile, D), lambda bi, i: (bi, i, 0)),
        out_shape=jax.ShapeDtypeStruct((B, S, D), F32),
        scratch_shapes=[pltpu.VMEM((tile + 2 * POOL_HALO, D), F32)],
        compiler_params=_params(("parallel", "parallel")),
        name="pool_ln",
    )(x, x, x, w.astype(BF16), row(b), row(sc), row(g), row(beta))


def _router_kernel(h_ref, whi_ref, wlo_ref, rb_ref, tri_ref, ltri_ref,
                   gate_ref, lpos_ref, cnt_ref, tcnt_ref, toff_ref, tbase_ref, ttot_ref):
    i = pl.program_id(0)

    @pl.when(i == 0)
    def _():
        cnt_ref[...] = jnp.zeros_like(cnt_ref)

    x = h_ref[...]
    xhi = x.astype(BF16)
    xlo = (x - xhi.astype(F32)).astype(BF16)
    whi = whi_ref[...]
    logits = (lax.dot_general(whi, xhi, _NT, preferred_element_type=F32)
              + lax.dot_general(whi, xlo, _NT, preferred_element_type=F32)
              + lax.dot_general(wlo_ref[...], xhi, _NT, preferred_element_type=F32))
    n_exp, tile = logits.shape
    per = n_exp // N_GROUPS
    scores = _sigmoid(logits)
    biased = scores + rb_ref[...]
    neg = -jnp.inf

    sub = lax.broadcasted_iota(jnp.int32, (per, tile), 0).astype(F32)
    gscore = []
    for g in range(N_GROUPS):
        bg = biased[g * per:(g + 1) * per, :]
        m1 = jnp.max(bg, axis=0, keepdims=True)
        i1 = jnp.min(jnp.where(bg == m1, sub, float(per)), axis=0, keepdims=True)
        m2 = jnp.max(jnp.where(sub == i1, neg, bg), axis=0, keepdims=True)
        gscore.append(m1 + m2)
    masked = []
    for g in range(N_GROUPS):
        beat = jnp.zeros_like(gscore[g])
        for o in range(N_GROUPS):
            if o == g:
                continue
            wins = (gscore[o] >= gscore[g]) if o < g else (gscore[o] > gscore[g])
            beat = beat + jnp.where(wins, 1.0, 0.0)
        keep = jnp.broadcast_to(beat < float(TOPK_GROUPS), (per, tile))
        masked.append(jnp.where(keep, biased[g * per:(g + 1) * per, :], neg))
    masked = jnp.concatenate(masked, axis=0)

    row = lax.broadcasted_iota(jnp.int32, (n_exp, tile), 0).astype(F32)
    chosen = jnp.zeros((n_exp, tile), F32)
    idxs, gates = [], []
    for _ in range(TOP_K):
        m = jnp.max(masked, axis=0, keepdims=True)
        idx = jnp.min(jnp.where(masked == m, row, float(n_exp)), axis=0, keepdims=True)
        hit = row == idx
        gates.append(jnp.sum(jnp.where(hit, scores, 0.0), axis=0, keepdims=True))
        masked = jnp.where(hit, neg, masked)
        chosen = jnp.where(hit, 1.0, chosen)
        idxs.append(idx)
    gsum = gates[0]
    for k in range(1, TOP_K):
        gsum = gsum + gates[k]

    earlier = jnp.dot(chosen.astype(BF16), tri_ref[...], preferred_element_type=F32)
    count = jnp.sum(chosen, axis=1, keepdims=True)
    run = count + (count - 2.0 * jnp.floor(0.5 * count))
    lower = jnp.dot(ltri_ref[...], jnp.broadcast_to(run, (n_exp, LANES)).astype(BF16),
                    preferred_element_type=F32)[:, :1]
    order = earlier + lower
    for k in range(TOP_K):
        gate_ref[k:k + 1, :] = gates[k] / gsum * ROUTED_SCALE
        lpos_k = jnp.sum(jnp.where(row == idxs[k], order, 0.0), axis=0, keepdims=True)
        lpos_ref[k:k + 1, :] = lpos_k.astype(jnp.int32)
    tcnt_ref[...] = run.astype(jnp.int32)
    toff_ref[...] = lower.astype(jnp.int32)
    tbase_ref[...] = cnt_ref[...].astype(jnp.int32)
    ttot_ref[...] = jnp.sum(run, axis=0, keepdims=True).astype(jnp.int32)
    cnt_ref[...] = cnt_ref[...] + run


def _router(h2, rw, rb):
    N, D = h2.shape
    E = rw.shape[1]
    tile = _pick(N, MOE_TILE)
    assert tile <= 256
    n_tiles = N // tile
    wt = rw.T
    whi = wt.astype(BF16)
    wlo = (wt - whi.astype(F32)).astype(BF16)
    ar = jnp.arange(tile)
    tri = (ar[:, None] < ar[None, :]).astype(BF16)
    ae = jnp.arange(E)
    ltri = (ae[None, :] < ae[:, None]).astype(BF16)
    const = lambda shape: pl.BlockSpec(shape, lambda i: (0, 0))
    kn = pl.BlockSpec((TOP_K, tile), lambda i: (0, i))
    table = pl.BlockSpec((None, E, 1), lambda i: (i, 0, 0))
    table_shape = jax.ShapeDtypeStruct((n_tiles, E, 1), jnp.int32)
    return pl.pallas_call(
        _router_kernel,
        grid=(n_tiles,),
        in_specs=[pl.BlockSpec((tile, D), lambda i: (i, 0)), const((E, D)), const((E, D)),
                  const((E, 1)), const((tile, tile)), const((E, E))],
        out_specs=[kn, kn, const((E, 1)), table, table, table,
                   pl.BlockSpec((None, 1, 1), lambda i: (i, 0, 0))],
        out_shape=[jax.ShapeDtypeStruct((TOP_K, N), F32), jax.ShapeDtypeStruct((TOP_K, N), jnp.int32),
                   jax.ShapeDtypeStruct((E, 1), F32), table_shape, table_shape, table_shape,
                   jax.ShapeDtypeStruct((n_tiles, 1, 1), jnp.int32)],
        compiler_params=_params(("arbitrary",)),
        name="moe_router",
    )(h2, whi, wlo, rb.reshape(E, 1), tri, ltri)


def _pack_pairs(x, rounded=False):
    half = x.shape[-1] // 2
    bits = lambda v: lax.bitcast_convert_type(v if rounded else v.astype(BF16).astype(F32), jnp.uint32)
    return (bits(x[:, :half]) & jnp.uint32(0xFFFF0000)) | (bits(x[:, half:]) >> 16)


def _unpack_pairs(w):
    hi = lax.bitcast_convert_type(w & jnp.uint32(0xFFFF0000), F32)
    lo = lax.bitcast_convert_type(w << 16, F32)
    return jnp.concatenate([hi.astype(BF16), lo.astype(BF16)], axis=-1)


def _to_slab(slab_ref, x, rounded=False, first=0):
    w = _pack_pairs(x, rounded)
    lines = w.shape[1] // LANES
    for a in range(lines):
        slab_ref[pl.ds(first * lines + a, w.shape[0], stride=lines), :] = w[:, a * LANES:(a + 1) * LANES]


def _from_slab(slab_ref, rows, lines, valid, first=0):
    w = jnp.concatenate([slab_ref[pl.ds(first * lines + a, rows, stride=lines), :] for a in range(lines)],
                        axis=-1)
    r = first + lax.broadcasted_iota(jnp.int32, (rows, 1), 0)
    return _unpack_pairs(jnp.where(r < valid, w, jnp.uint32(0)))


def _start_runs(toff_ref, gst_ref, tcnt_ref, local, remote, sem, lines, to_remote, enable=True,
                unroll=None):
    def body(e, c):
        cnt, off, gst = tcnt_ref[0, 0, e], toff_ref[0, 0, e], gst_ref[0, 0, e]

        @pl.when(jnp.logical_and(cnt > 0, enable))
        def _():
            loc = local.at[pl.ds(pl.multiple_of(off * lines, 8), cnt * lines), :]
            rem = remote.at[pl.ds(pl.multiple_of(gst * lines, 8), cnt * lines), :]
            (pltpu.make_async_copy(loc, rem, sem) if to_remote else pltpu.make_async_copy(rem, loc, sem)).start()
        return c
    lax.fori_loop(0, tcnt_ref.shape[2], body, 0, unroll=RUN_UNROLL if unroll is None else unroll)


def _wait_all_runs(slab, remote, sem, total_lines):
    n = pl.multiple_of(total_lines, 8)
    pltpu.make_async_copy(slab.at[pl.ds(0, n), :], remote.at[pl.ds(0, n), :], sem).wait()


def _sorted_rows(tile, n_exp):
    return TOP_K * tile + n_exp


def _dispatch_kernel(toff_ref, gst_ref, tcnt_ref, ttot_ref, tprev_ref, lpos_ref, h_ref, xs_ref, slabs, sems,
                     *, lines):
    i = pl.program_id(0)
    slot = lax.rem(i, 2)
    slab = slabs.at[slot]
    tile = h_ref.shape[0]
    rows = slab.shape[0] // lines
    hb = h_ref[...].astype(BF16)
    chunk = max(c for c in range(8, 257, 8) if rows % c == 0)
    srow = lax.broadcasted_iota(jnp.int32, (chunk, tile), 0).astype(F32).astype(BF16)
    one, zero = jnp.ones((), BF16), jnp.zeros((), BF16)
    for first in range(0, rows, chunk):
        perm = jnp.full((chunk, tile), zero)
        for k in range(TOP_K):
            local = lpos_ref[k:k + 1, :] - first
            inside = jnp.logical_and(local >= 0, local < chunk)
            local = jnp.where(inside, local, -1).astype(F32).astype(BF16)
            perm = jnp.where(srow == local, one, perm)
        _to_slab(slab, jnp.dot(perm, hb, preferred_element_type=F32), rounded=True, first=first)
    _start_runs(toff_ref, gst_ref, tcnt_ref, slab, xs_ref, sems.at[slot], lines, to_remote=True)

    @pl.when(i > 0)
    def _():
        _wait_all_runs(slabs.at[1 - slot], xs_ref, sems.at[1 - slot], tprev_ref[0, 0, 0] * lines)

    @pl.when(i == pl.num_programs(0) - 1)
    def _():
        _wait_all_runs(slab, xs_ref, sems.at[slot], ttot_ref[0, 0, 0] * lines)


def _run_specs(n_exp):
    smem = lambda n: pl.BlockSpec((1, 1, n), lambda i: (i, 0, 0), memory_space=pltpu.SMEM)
    return [smem(n_exp), smem(n_exp), smem(n_exp), smem(1)]


def _slab_lines(d_model):
    lines = d_model // (2 * LANES)
    assert lines * 2 * LANES == d_model and (2 * lines) % 8 == 0
    return lines


def _dispatch(h2, lpos, runs, n_slots):
    N, D = h2.shape
    E = runs[0].shape[2]
    tile = _pick(N, MOE_TILE)
    lines = _slab_lines(D)
    return pl.pallas_call(
        functools.partial(_dispatch_kernel, lines=lines),
        grid=(N // tile,),
        in_specs=_run_specs(E) + [
            pl.BlockSpec((1, 1, 1), lambda i: (jnp.maximum(i - 1, 0), 0, 0), memory_space=pltpu.SMEM),
            pl.BlockSpec((TOP_K, tile), lambda i: (0, i)), pl.BlockSpec((tile, D), lambda i: (i, 0))],
        out_specs=pl.BlockSpec(memory_space=pl.ANY),
        out_shape=jax.ShapeDtypeStruct((n_slots * lines, LANES), jnp.uint32),
        scratch_shapes=[pltpu.VMEM((2, _sorted_rows(tile, E) * lines, LANES), jnp.uint32),
                        pltpu.SemaphoreType.DMA((2,))],
        compiler_params=_params(("arbitrary",)),
        name="moe_dispatch",
    )(*runs, runs[3], lpos, h2)


def _expert_kernel(pstart_ref, cnt_ref, xs_hbm, wg_hbm, wu_hbm, wd_hbm, ys_hbm,
                   xbuf, ybuf, xo, yo, wg_f, wu_f, wd_f, wg_s, wu_s, wd_s, wsem, xsem, ysem, osem,
                   *, layer, lines, cap, chunk):
    e = pl.program_id(0)
    last = pl.num_programs(0) - 1
    n_in = xbuf.shape[0]
    islot = lax.rem(e, n_in)
    slot = lax.rem(e, 2)

    def weight_copies(ex, sl):
        return [pltpu.make_async_copy(w.at[layer, ex], f.at[sl], wsem.at[sl])
                for w, f in ((wg_hbm, wg_f), (wu_hbm, wu_f), (wd_hbm, wd_f))]

    def head_lines(ex):
        return jnp.minimum(cnt_ref[ex], cap) * lines

    def first_line(ex, row=0):
        return pl.multiple_of((pstart_ref[ex] + row) * lines, 8)

    def x_copy(ex, sl):
        n = head_lines(ex)
        return pltpu.make_async_copy(xs_hbm.at[pl.ds(first_line(ex), n), :], xbuf.at[sl, pl.ds(0, n), :],
                                     xsem.at[sl])

    def y_copy(ex, sl):
        n = head_lines(ex)
        return pltpu.make_async_copy(ybuf.at[sl, pl.ds(0, n), :], ys_hbm.at[pl.ds(first_line(ex), n), :],
                                     ysem.at[sl])

    def fetch(ex, sl):
        for cp in weight_copies(ex, sl):
            cp.start()

        @pl.when(cnt_ref[ex] > 0)
        def _():
            x_copy(ex, sl).start()

    @pl.when(e == 0)
    def _():
        for first in range(n_in - 1):
            @pl.when(first <= last)
            def _(first=first):
                fetch(first, first)

    for cp in weight_copies(e, islot):
        cp.wait()
    ahead = jnp.minimum(e + n_in - 1, last)

    @pl.when(e + n_in - 1 <= last)
    def _():
        fetch(ahead, lax.rem(ahead, n_in))

    wg_s[...] = wg_f[islot].astype(BF16)
    wu_s[...] = wu_f[islot].astype(BF16)
    wd_s[...] = wd_f[islot].astype(BF16)

    before = jnp.maximum(e - 2, 0)

    @pl.when(jnp.logical_and(e >= 2, cnt_ref[before] > 0))
    def _():
        y_copy(before, slot).wait()

    def ffn(x_ref, rows, valid):
        x = _from_slab(x_ref, rows, lines, valid)
        hg = jnp.dot(x, wg_s[...], preferred_element_type=F32)
        hu = jnp.dot(x, wu_s[...], preferred_element_type=F32)
        hb = hg * _sigmoid(hg) * hu
        return jnp.dot(hb.astype(BF16), wd_s[...], preferred_element_type=F32)

    count = cnt_ref[e]

    @pl.when(count > 0)
    def _():
        x_copy(e, islot).wait()
        head = jnp.minimum(count, cap)
        lower = 0
        for rows in range(chunk, cap + 1, chunk):
            @pl.when(jnp.logical_and(head > lower, head <= rows))
            def _(rows=rows):
                _to_slab(ybuf.at[slot], ffn(xbuf.at[islot], rows, head))
            lower = rows
        y_copy(e, slot).start()

        @pl.when(count > cap)
        def _():
            def extra(j, carry):
                row = cap + j * chunk
                n = jnp.minimum(chunk, count - row) * lines
                fetch = pltpu.make_async_copy(xs_hbm.at[pl.ds(first_line(e, row), n), :],
                                              xo.at[pl.ds(0, n), :], osem)
                fetch.start()
                fetch.wait()
                _to_slab(yo, ffn(xo, chunk, count - row))
                store = pltpu.make_async_copy(yo.at[pl.ds(0, n), :],
                                              ys_hbm.at[pl.ds(first_line(e, row), n), :], osem)
                store.start()
                store.wait()
                return carry
            lax.fori_loop(0, (count - cap + chunk - 1) // chunk, extra, 0)

    @pl.when(e == last)
    def _():
        @pl.when(count > 0)
        def _():
            y_copy(e, slot).wait()
        prev = jnp.maximum(e - 1, 0)

        @pl.when(jnp.logical_and(e >= 1, cnt_ref[prev] > 0))
        def _():
            y_copy(prev, 1 - slot).wait()


def _experts(xs, pstart, counts, wg, wu, wd, layer):
    E, D, Fe = wg.shape[-3:]
    lines = _slab_lines(D)
    cap, chunk = EXPERT_CAP, EXPERT_CHUNK
    assert cap % chunk == 0 and (chunk * lines) % 8 == 0
    n_in = EXPERT_AHEAD + 1
    hbm = pl.BlockSpec(memory_space=pl.ANY)
    slab = lambda rows: pltpu.VMEM(rows, jnp.uint32)
    return pl.pallas_call(
        functools.partial(_expert_kernel, layer=layer, lines=lines, cap=cap, chunk=chunk),
        grid_spec=pltpu.PrefetchScalarGridSpec(
            num_scalar_prefetch=2,
            grid=(E,),
            in_specs=[hbm, hbm, hbm, hbm],
            out_specs=hbm,
            scratch_shapes=[slab((n_in, cap * lines, LANES)), slab((2, cap * lines, LANES)),
                            slab((chunk * lines, LANES)), slab((chunk * lines, LANES)),
                            pltpu.VMEM((n_in, D, Fe), F32), pltpu.VMEM((n_in, D, Fe), F32),
                            pltpu.VMEM((n_in, Fe, D), F32),
                            pltpu.VMEM((D, Fe), BF16), pltpu.VMEM((D, Fe), BF16), pltpu.VMEM((Fe, D), BF16),
                            pltpu.SemaphoreType.DMA((n_in,)), pltpu.SemaphoreType.DMA((n_in,)),
                            pltpu.SemaphoreType.DMA((2,)), pltpu.SemaphoreType.DMA(())],
        ),
        out_shape=jax.ShapeDtypeStruct(xs.shape, jnp.uint32),
        compiler_params=_params(("arbitrary",)),
        name="moe_experts",
    )(pstart, counts, xs, wg, wu, wd)


def _combine_kernel(toff_ref, gst_ref, tcnt_ref, ttot_ref, ntoff_ref, ngst_ref, ntcnt_ref, lpos_ref, gate_ref,
                    h_ref, ys_ref, sg_ref, su_ref, sd_ref, g_ref, beta_ref, o_ref, slabs, sems, *, lines, alpha):
    i = pl.program_id(0)
    slot = lax.rem(i, 2)
    slab, sem = slabs.at[slot], sems.at[slot]

    @pl.when(i == 0)
    def _():
        _start_runs(toff_ref, gst_ref, tcnt_ref, slab, ys_ref, sem, lines, to_remote=False)

    _start_runs(ntoff_ref, ngst_ref, ntcnt_ref, slabs.at[1 - slot], ys_ref, sems.at[1 - slot], lines,
                to_remote=False, enable=i < pl.num_programs(0) - 1, unroll=True)

    h = h_ref[...]
    tile = h.shape[0]
    rows = slab.shape[0] // lines
    hb = h.astype(BF16)
    sgate = jnp.dot(hb, sg_ref[...], preferred_element_type=F32)
    sup = jnp.dot(hb, su_ref[...], preferred_element_type=F32)
    mid = sgate * _sigmoid(sgate) * sup
    acc = jnp.dot(mid.astype(BF16), sd_ref[...], preferred_element_type=F32)
    total = ttot_ref[0, 0, 0]
    _wait_all_runs(slab, ys_ref, sem, total * lines)
    chunk = max(c for c in range(8, 257, 8) if rows % c == 0)
    scol = lax.broadcasted_iota(jnp.int32, (tile, chunk), 1).astype(F32).astype(BF16)
    gates = [gate_ref[:, k:k + 1].astype(BF16) for k in range(TOP_K)]
    for first in range(0, rows, chunk):
        weights = jnp.zeros((tile, chunk), BF16)
        for k in range(TOP_K):
            local = lpos_ref[:, k:k + 1] - first
            inside = jnp.logical_and(local >= 0, local < chunk)
            local = jnp.where(inside, local, -1).astype(F32).astype(BF16)
            weights = jnp.where(scol == local, gates[k], weights)
        acc = acc + jnp.dot(weights, _from_slab(slab, chunk, lines, total, first),
                            preferred_element_type=F32)
    o_ref[...] = _layer_norm(alpha * h + acc, g_ref[...], beta_ref[...])


def _combine(h2, lpos_nk, gate_nk, runs, ys, sg, su, sd, g, beta, alpha):
    N, D = h2.shape
    Fs = sg.shape[1]
    E = runs[0].shape[2]
    tile = _pick(N, MOE_TILE)
    lines = _slab_lines(D)
    const = lambda shape: pl.BlockSpec(shape, lambda i: (0, 0))
    nk = pl.BlockSpec((tile, TOP_K), lambda i: (i, 0))
    last = N // tile - 1
    following = pl.BlockSpec((1, 1, E), lambda i: (jnp.minimum(i + 1, last), 0, 0), memory_space=pltpu.SMEM)
    return pl.pallas_call(
        functools.partial(_combine_kernel, lines=lines, alpha=alpha),
        grid=(N // tile,),
        in_specs=_run_specs(E) + [following] * 3 + [
            nk, nk, pl.BlockSpec((tile, D), lambda i: (i, 0)), pl.BlockSpec(memory_space=pl.ANY),
            const((D, Fs)), const((D, Fs)), const((Fs, D)), const((1, D)), const((1, D))],
        out_specs=pl.BlockSpec((tile, D), lambda i: (i, 0)),
        out_shape=jax.ShapeDtypeStruct((N, D), F32),
        scratch_shapes=[pltpu.VMEM((2, _sorted_rows(tile, E) * lines, LANES), jnp.uint32),
                        pltpu.SemaphoreType.DMA((2,))],
        compiler_params=_params(("arbitrary",)),
        name="moe_combine",
    )(*runs, *runs[:3], lpos_nk, gate_nk, h2, ys, sg.astype(BF16), su.astype(BF16), sd.astype(BF16),
      g.reshape(1, D), beta.reshape(1, D))


FLASH_TQ = 512
FLASH_TK = 512
MOE_TILE = 256
RUN_UNROLL = 8
EXPERT_CAP = 1024
EXPERT_CHUNK = 64
EXPERT_AHEAD = 2


def _moe_ln(h2, rw, rb, wg, wu, wd, layer, sg, su, sd, g, beta, alpha):
    N, D = h2.shape
    E = rw.shape[1]
    gate, lpos, cnt, tcnt, toff, tbase, ttot = _router(h2, rw, rb)
    n_tiles = ttot.shape[0]
    counts = cnt[:, 0].astype(jnp.int32)
    pstart = (jnp.cumsum(counts) - counts).astype(jnp.int32)
    n_slots = N * TOP_K + n_tiles * E
    per_tile = lambda a: a.reshape(n_tiles, 1, E)
    runs = (per_tile(toff), per_tile(tbase) + pstart[None, None, :], per_tile(tcnt), ttot)
    xs = _dispatch(h2, lpos, runs, n_slots)
    ys = _experts(xs, pstart, counts, wg, wu, wd, layer)
    return _combine(h2, lpos.T, gate.T, runs, ys, sg, su, sd, g, beta, alpha)


def _mla_proj_kernel(h_ref, wa_ref, qg_ref, kvg_ref, wqt_ref, wk_ref, wvt_ref, cos_ref, sin_ref,
                     cost_ref, sint_ref, qt_ref, k_ref, vt_ref, *, q_lora, kv_lora, heads, q_scale):
    x = h_ref[0].astype(BF16)
    a = jnp.dot(x, wa_ref[...], preferred_element_type=F32)

    def rms(v, g):
        return v * lax.rsqrt(jnp.mean(v * v, axis=-1, keepdims=True) + RMS_EPS) * g

    q_lat = rms(a[:, :q_lora], qg_ref[...]).astype(BF16)
    kv_lat = rms(a[:, q_lora:q_lora + kv_lora], kvg_ref[...]).astype(BF16)
    o = q_lora + kv_lora
    k_rope = (a[:, o:o + LANES] * cos_ref[...] + a[:, o + LANES:o + 2 * LANES] * sin_ref[...]).astype(BF16)
    qa = lax.dot_general(wqt_ref[...], q_lat, _NT, preferred_element_type=F32)
    kn = jnp.dot(kv_lat, wk_ref[...], preferred_element_type=F32)
    vt_ref[0] = lax.dot_general(wvt_ref[...], kv_lat, _NT, preferred_element_type=F32).astype(BF16)
    cost = cost_ref[...]
    sint = sint_ref[...]
    hd = heads * LANES
    for h in range(heads):
        r0 = slice(h * LANES, (h + 1) * LANES)
        r1 = slice(hd + h * LANES, hd + (h + 1) * LANES)
        r2 = slice(2 * hd + h * LANES, 2 * hd + (h + 1) * LANES)
        qt_ref[0, 2 * h * LANES:(2 * h + 1) * LANES, :] = (qa[r0] * q_scale).astype(BF16)
        qt_ref[0, (2 * h + 1) * LANES:(2 * h + 2) * LANES, :] = (
            (qa[r1] * cost + qa[r2] * sint) * q_scale).astype(BF16)
        k_ref[0, :, 2 * h * LANES:(2 * h + 1) * LANES] = kn[:, r0].astype(BF16)
        k_ref[0, :, (2 * h + 1) * LANES:(2 * h + 2) * LANES] = k_rope


def _rot_half(w):
    r = w.shape[-1] // 2
    return jnp.concatenate([-w[..., r:], w[..., :r]], axis=-1)


def _pad_lanes(w):
    return jnp.pad(w, [(0, 0)] * (w.ndim - 1) + [(0, LANES - w.shape[-1])])


FLASH_HEADROOM = 64.0
FLASH_PROBE = 16


def _flash_kernel(qt_ref, k_ref, vt_ref, o_ref, p_ref, s_ref, mx_ref, m_ref, l_ref, acc_ref, *, tk):
    qt = qt_ref[0]
    nk = k_ref.shape[1] // tk
    assert nk % 2 == 0 and nk >= 2

    def scores(c):
        off = pl.multiple_of(c * tk, tk)
        return jnp.dot(k_ref[0, pl.ds(off, tk), :], qt, preferred_element_type=F32)

    m0 = jnp.max(jnp.dot(k_ref[0, 0:FLASH_PROBE, :], qt, preferred_element_type=F32), axis=0, keepdims=True)
    acc_ref[...] = jnp.zeros_like(acc_ref)

    def probs(c, slot, carry):
        l, top = carry
        s = scores(c)
        p = jnp.exp2(s - m0)
        p_ref[slot] = p.astype(BF16)
        return l + jnp.sum(p, axis=0, keepdims=True), jnp.maximum(top, jnp.max(s, axis=0, keepdims=True))

    def accumulate(c, slot):
        off = pl.multiple_of(c * tk, tk)
        acc_ref[...] += jnp.dot(vt_ref[0, :, pl.ds(off, tk)], p_ref[slot], preferred_element_type=F32)

    carry = probs(0, 0, (jnp.zeros_like(m0), m0))
    for c in range(nk - 1):
        carry = probs(c + 1, (c + 1) % 2, carry)
        accumulate(c, c % 2)
    accumulate(nk - 1, (nk - 1) % 2)
    l, top = carry
    o_ref[0] = (acc_ref[...] / l).T.astype(o_ref.dtype)

    @pl.when(jnp.max(top - m0) > FLASH_HEADROOM)
    def _():
        _flash_online(qt, k_ref, vt_ref, o_ref, s_ref, mx_ref, m_ref, l_ref, acc_ref, tk)


def _flash_online(qt, k_ref, vt_ref, o_ref, s_ref, mx_ref, m_ref, l_ref, acc_ref, tk):
    nk = k_ref.shape[1] // tk
    assert nk % 2 == 0 and nk >= 2
    m_ref[...] = jnp.full_like(m_ref, -jnp.inf)
    l_ref[...] = jnp.zeros_like(l_ref)
    acc_ref[...] = jnp.zeros_like(acc_ref)

    def scores(c, slot):
        off = pl.multiple_of(c * tk, tk)
        s = jnp.dot(k_ref[0, pl.ds(off, tk), :], qt, preferred_element_type=F32)
        s_ref[slot] = s
        mx_ref[slot] = jnp.max(s, axis=0, keepdims=True)

    def update(c, slot):
        off = pl.multiple_of(c * tk, tk)
        vt = vt_ref[0, :, pl.ds(off, tk)]
        m_prev = m_ref[...]
        m_new = jnp.maximum(m_prev, mx_ref[slot])
        p = jnp.exp2(s_ref[slot] - m_new)
        a = jnp.exp2(m_prev - m_new)
        l_ref[...] = a * l_ref[...] + jnp.sum(p, axis=0, keepdims=True)
        acc_ref[...] = a * acc_ref[...] + jnp.dot(vt, p.astype(BF16), preferred_element_type=F32)
        m_ref[...] = m_new

    scores(0, 0)

    def body(i, carry):
        c = 2 * i
        scores(c + 1, 1)
        update(c, 0)
        scores(c + 2, 0)
        update(c + 1, 1)
        return carry

    lax.fori_loop(0, nk // 2 - 1, body, 0)
    scores(nk - 1, 1)
    update(nk - 2, 0)
    update(nk - 1, 1)
    o_ref[0] = (acc_ref[...] / l_ref[...]).T.astype(o_ref.dtype)


def _oproj_ln_kernel(o_ref, h_ref, w_ref, g_ref, beta_ref, out_ref, *, alpha):
    m = jnp.dot(o_ref[0], w_ref[...], preferred_element_type=F32)
    out_ref[0] = _layer_norm(alpha * h_ref[0] + m, g_ref[...], beta_ref[...])


def _mla_ln(h, w_a, qg, kvg, w_uq, w_uk, w_uv, w_o, g, beta, alpha):
    B, S, D = h.shape
    q_lora, heads, qk = w_uq.shape
    kv_lora, _, nope = w_uk.shape
    vd = w_uv.shape[-1]
    rope = qk - nope
    assert nope == LANES and vd == LANES and rope <= LANES and rope % 2 == 0
    kr = w_a[:, q_lora + kv_lora:]
    wa = jnp.concatenate([w_a[:, :q_lora + kv_lora], _pad_lanes(kr), _pad_lanes(_rot_half(kr))],
                         axis=1).astype(BF16)
    wq_r = w_uq[:, :, nope:]
    wq = jnp.concatenate([w_uq[:, :, :nope].reshape(q_lora, heads * nope),
                          _pad_lanes(wq_r).reshape(q_lora, heads * LANES),
                          _pad_lanes(_rot_half(wq_r)).reshape(q_lora, heads * LANES)], axis=1).astype(BF16)
    wk = w_uk.reshape(kv_lora, heads * nope).astype(BF16)
    wvt = w_uv.reshape(kv_lora, heads * vd).T.astype(BF16)
    wqt = wq.T
    pos = jnp.arange(S, dtype=F32)
    inv = 1.0 / (ROPE_THETA ** (jnp.arange(0, rope, 2, dtype=F32) / rope))
    ang = pos[:, None] * inv[None, :]
    cos = _pad_lanes(jnp.concatenate([jnp.cos(ang), jnp.cos(ang)], axis=1))
    sin = _pad_lanes(jnp.concatenate([jnp.sin(ang), jnp.sin(ang)], axis=1))
    q_scale = float(qk) ** -0.5 * math.log2(math.e)

    tp = _pick(S, 256)
    const = lambda a: pl.BlockSpec(a.shape, lambda b, i: (0,) * a.ndim)
    qg2, kvg2 = qg.reshape(1, q_lora), kvg.reshape(1, kv_lora)
    tok_major = pl.BlockSpec((tp, LANES), lambda b, i: (i, 0))
    feat_major = pl.BlockSpec((LANES, tp), lambda b, i: (0, i))
    qt, k, vt = pl.pallas_call(
        functools.partial(_mla_proj_kernel, q_lora=q_lora, kv_lora=kv_lora, heads=heads, q_scale=q_scale),
        grid=(B, S // tp),
        in_specs=[pl.BlockSpec((1, tp, D), lambda b, i: (b, i, 0)), const(wa), const(qg2), const(kvg2),
                  const(wqt), const(wk), const(wvt), tok_major, tok_major, feat_major, feat_major],
        out_specs=[pl.BlockSpec((1, 2 * heads * LANES, tp), lambda b, i: (b, 0, i)),
                   pl.BlockSpec((1, tp, 2 * heads * LANES), lambda b, i: (b, i, 0)),
                   pl.BlockSpec((1, heads * LANES, tp), lambda b, i: (b, 0, i))],
        out_shape=[jax.ShapeDtypeStruct((B, 2 * heads * LANES, S), BF16),
                   jax.ShapeDtypeStruct((B, S, 2 * heads * LANES), BF16),
                   jax.ShapeDtypeStruct((B, heads * LANES, S), BF16)],
        compiler_params=_params(("parallel", "parallel")),
        name="mla_proj",
    )(h, wa, qg2, kvg2, wqt, wk, wvt, cos, sin, cos.T, sin.T)

    tq = _pick(S, FLASH_TQ)
    tk = _pick(S, FLASH_TK)
    o = pl.pallas_call(
        functools.partial(_flash_kernel, tk=tk),
        grid=(B, heads, S // tq),
        in_specs=[pl.BlockSpec((1, 2 * LANES, tq), lambda b, hh, i: (b, hh, i)),
                  pl.BlockSpec((1, S, 2 * LANES), lambda b, hh, i: (b, 0, hh)),
                  pl.BlockSpec((1, LANES, S), lambda b, hh, i: (b, hh, 0))],
        out_specs=pl.BlockSpec((1, tq, LANES), lambda b, hh, i: (b, i, hh)),
        out_shape=jax.ShapeDtypeStruct((B, S, heads * LANES), BF16),
        scratch_shapes=[pltpu.VMEM((2, tk, tq), BF16), pltpu.VMEM((2, tk, tq), F32), pltpu.VMEM((2, 1, tq), F32),
                        pltpu.VMEM((1, tq), F32), pltpu.VMEM((1, tq), F32), pltpu.VMEM((LANES, tq), F32)],
        compiler_params=_params(("parallel", "parallel", "arbitrary")),
        name="mla_flash",
    )(qt, k, vt)

    to = _pick(S, 512)
    vec = pl.BlockSpec((1, D), lambda b, i: (0, 0))
    return pl.pallas_call(
        functools.partial(_oproj_ln_kernel, alpha=alpha),
        grid=(B, S // to),
        in_specs=[pl.BlockSpec((1, to, heads * vd), lambda b, i: (b, i, 0)),
                  pl.BlockSpec((1, to, D), lambda b, i: (b, i, 0)),
                  pl.BlockSpec((heads * vd, D), lambda b, i: (0, 0)), vec, vec],
        out_specs=pl.BlockSpec((1, to, D), lambda b, i: (b, i, 0)),
        out_shape=jax.ShapeDtypeStruct((B, S, D), F32),
        compiler_params=_params(("parallel", "parallel")),
        name="mla_oproj_ln",
    )(o, h, w_o.astype(BF16), g.reshape(1, D), beta.reshape(1, D))


def kernel(x, pool_w, pool_b, pool_scale, mla_w_a, mla_q_norm_g, mla_kv_norm_g, mla_w_uq, mla_w_uk,
           mla_w_uv, mla_w_o, ln_mix_g, ln_mix_b, router_w, router_b, exp_w_gate, exp_w_up, exp_w_down,
           sh_w_gate, sh_w_up, sh_w_down, ln_ffn_g, ln_ffn_b):
    B, S, D = x.shape
    depth = ln_mix_g.shape[0]
    alpha = (2 * depth) ** 0.25
    h = x
    for i in range(depth):
        j = i // 2
        if i % 2 == 0:
            h = _pool_ln(h, pool_w[j], pool_b[j], pool_scale[j], ln_mix_g[i], ln_mix_b[i], alpha)
        else:
            h = _mla_ln(h, mla_w_a[j], mla_q_norm_g[j], mla_kv_norm_g[j], mla_w_uq[j], mla_w_uk[j],
                        mla_w_uv[j], mla_w_o[j], ln_mix_g[i], ln_mix_b[i], alpha)
        h = _moe_ln(h.reshape(B * S, D), router_w[i], router_b[i], exp_w_gate, exp_w_up, exp_w_down, i,
                    sh_w_gate[i], sh_w_up[i], sh_w_down[i], ln_ffn_g[i], ln_ffn_b[i],
                    alpha).reshape(B, S, D)
    return h
```

```python
import functools
import math

import jax
import jax.numpy as jnp
from jax import lax
from jax.experimental import pallas as pl
from jax.experimental.pallas import tpu as pltpu

F32 = jnp.float32
BF16 = jnp.bfloat16

LN_EPS = 1e-5
RMS_EPS = 1e-6
POOL_WINDOWS = (2, 4, 8, 16)
POOL_HALO = 8
ROPE_THETA = 10000.0
TOP_K = 8
N_GROUPS = 8
TOPK_GROUPS = 4
ROUTED_SCALE = 2.5
LANES = 128
VMEM_LIMIT = 48 * 1024 * 1024

_NT = (((1,), (1,)), ((), ()))


def _pick(n, pref):
    t = min(pref, n)
    while n % t:
        t //= 2
    assert t >= 8 and n % t == 0, (n, pref)
    return t


def _layer_norm(z, g, b):
    mu = jnp.mean(z, axis=-1, keepdims=True)
    zc = z - mu
    var = jnp.mean(zc * zc, axis=-1, keepdims=True)
    return zc * lax.rsqrt(var + LN_EPS) * g + b


def _sigmoid(x):
    return 1.0 / (1.0 + jnp.exp(-x))


def _params(sem):
    return pltpu.CompilerParams(dimension_semantics=sem, vmem_limit_bytes=VMEM_LIMIT)


def _pool_ln_kernel(xc_ref, xp_ref, xn_ref, w_ref, b_ref, sc_ref, g_ref, beta_ref, o_ref, ext_ref,
                    *, tile, seq, alpha):
    i = pl.program_id(1)
    n_tiles = pl.num_programs(1)
    xc = xc_ref[0]
    d_model = xc.shape[-1]
    ch = d_model // len(POOL_WINDOWS)
    ext_ref[0:POOL_HALO, :] = jnp.where(i > 0, xp_ref[0], 0.0)
    ext_ref[POOL_HALO:POOL_HALO + tile, :] = xc
    ext_ref[POOL_HALO + tile:2 * POOL_HALO + tile, :] = jnp.where(i < n_tiles - 1, xn_ref[0], 0.0)
    pos = i * tile + lax.broadcasted_iota(jnp.int32, (tile, 1), 0)
    outs = []
    for g, win in enumerate(POOL_WINDOWS):
        half = win // 2
        cols = slice(g * ch, (g + 1) * ch)
        acc = ext_ref[POOL_HALO - half:POOL_HALO - half + tile, cols]
        for j in range(-half + 1, half):
            acc = acc + ext_ref[POOL_HALO + j:POOL_HALO + j + tile, cols]
        cnt = (jnp.minimum(pos + half, seq) - jnp.maximum(pos - half, 0)).astype(F32)
        pooled = acc / cnt - xc[:, cols]
        outs.append(jnp.dot(pooled.astype(BF16), w_ref[g], preferred_element_type=F32))
    y = jnp.concatenate(outs, axis=-1)
    y = (y + b_ref[...]) * sc_ref[...]
    o_ref[0] = _layer_norm(alpha * xc + y, g_ref[...], beta_ref[...])


def _pool_ln(x, w, b, sc, g, beta, alpha):
    B, S, D = x.shape
    tile = _pick(S, 512)
    nh = tile // POOL_HALO
    last = S // POOL_HALO - 1
    row = lambda v: v.reshape(1, D)
    vec = pl.BlockSpec((1, D), lambda bi, i: (0, 0))
    return pl.pallas_call(
        functools.partial(_pool_ln_kernel, tile=tile, seq=S, alpha=alpha),
        grid=(B, S // tile),
        in_specs=[
            pl.BlockSpec((1, tile, D), lambda bi, i: (bi, i, 0)),
            pl.BlockSpec((1, POOL_HALO, D), lambda bi, i: (bi, jnp.maximum(i * nh - 1, 0), 0)),
            pl.BlockSpec((1, POOL_HALO, D), lambda bi, i: (bi, jnp.minimum((i + 1) * nh, last), 0)),
            pl.BlockSpec(w.shape, lambda bi, i: (0, 0, 0)),
            vec, vec, vec, vec,
        ],
        out_specs=pl.BlockSpec((1, tile, D), lambda bi, i: (bi, i, 0)),
        out_shape=jax.ShapeDtypeStruct((B, S, D), F32),
        scratch_shapes=[pltpu.VMEM((tile + 2 * POOL_HALO, D), F32)],
        compiler_params=_params(("parallel", "parallel")),
        name="pool_ln",
    )(x, x, x, w.astype(BF16), row(b), row(sc), row(g), row(beta))


def _router_kernel(h_ref, whi_ref, wlo_ref, rb_ref, tri_ref, ltri_ref,
                   gate_ref, lpos_ref, cnt_ref, tcnt_ref, toff_ref, tbase_ref, ttot_ref, pstart_ref,
                   gate_nk_ref, lpos_nk_ref):
    i = pl.program_id(0)

    @pl.when(i == 0)
    def _():
        cnt_ref[...] = jnp.zeros_like(cnt_ref)

    x = h_ref[...]
    xhi = x.astype(BF16)
    xlo = (x - xhi.astype(F32)).astype(BF16)
    whi = whi_ref[...]
    logits = (lax.dot_general(whi, xhi, _NT, preferred_element_type=F32)
              + lax.dot_general(whi, xlo, _NT, preferred_element_type=F32)
              + lax.dot_general(wlo_ref[...], xhi, _NT, preferred_element_type=F32))
    n_exp, tile = logits.shape
    per = n_exp // N_GROUPS
    scores = _sigmoid(logits)
    biased = scores + rb_ref[...]
    neg = -jnp.inf

    sub = lax.broadcasted_iota(jnp.int32, (per, tile), 0).astype(F32)
    gscore = []
    for g in range(N_GROUPS):
        bg = biased[g * per:(g + 1) * per, :]
        m1 = jnp.max(bg, axis=0, keepdims=True)
        i1 = jnp.min(jnp.where(bg == m1, sub, float(per)), axis=0, keepdims=True)
        m2 = jnp.max(jnp.where(sub == i1, neg, bg), axis=0, keepdims=True)
        gscore.append(m1 + m2)
    masked = []
    for g in range(N_GROUPS):
        beat = jnp.zeros_like(gscore[g])
        for o in range(N_GROUPS):
            if o == g:
                continue
            wins = (gscore[o] >= gscore[g]) if o < g else (gscore[o] > gscore[g])
            beat = beat + jnp.where(wins, 1.0, 0.0)
        keep = jnp.broadcast_to(beat < float(TOPK_GROUPS), (per, tile))
        masked.append(jnp.where(keep, biased[g * per:(g + 1) * per, :], neg))
    masked = jnp.concatenate(masked, axis=0)

    row = lax.broadcasted_iota(jnp.int32, (n_exp, tile), 0).astype(F32)
    chosen = jnp.zeros((n_exp, tile), F32)
    idxs, gates = [], []
    for _ in range(TOP_K):
        m = jnp.max(masked, axis=0, keepdims=True)
        idx = jnp.min(jnp.where(masked == m, row, float(n_exp)), axis=0, keepdims=True)
        hit = row == idx
        gates.append(jnp.sum(jnp.where(hit, scores, 0.0), axis=0, keepdims=True))
        masked = jnp.where(hit, neg, masked)
        chosen = jnp.where(hit, 1.0, chosen)
        idxs.append(idx)
    gsum = gates[0]
    for k in range(1, TOP_K):
        gsum = gsum + gates[k]

    earlier = jnp.dot(chosen.astype(BF16), tri_ref[...], preferred_element_type=F32)
    count = jnp.sum(chosen, axis=1, keepdims=True)
    run = count + (count - 2.0 * jnp.floor(0.5 * count))
    lower = jnp.dot(ltri_ref[...], jnp.broadcast_to(run, (n_exp, LANES)).astype(BF16),
                    preferred_element_type=F32)[:, :1]
    order = earlier + lower
    for k in range(TOP_K):
        gate_ref[k:k + 1, :] = gates[k] / gsum * ROUTED_SCALE
        lpos_k = jnp.sum(jnp.where(row == idxs[k], order, 0.0), axis=0, keepdims=True)
        lpos_ref[k:k + 1, :] = lpos_k.astype(jnp.int32)
    gate_nk_ref[...] = gate_ref[...].T
    lpos_nk_ref[...] = lpos_ref[...].astype(F32).T.astype(jnp.int32)
    tcnt_ref[...] = run.astype(jnp.int32)
    toff_ref[...] = lower.astype(jnp.int32)
    tbase_ref[...] = cnt_ref[...].astype(jnp.int32)
    ttot_ref[...] = jnp.sum(run, axis=0, keepdims=True).astype(jnp.int32)
    cnt_ref[...] = cnt_ref[...] + run

    @pl.when(i == pl.num_programs(0) - 1)
    def _():
        total = cnt_ref[...]
        hi = jnp.floor(total * (1.0 / 4096.0))
        mid = jnp.floor((total - 4096.0 * hi) * (1.0 / 64.0))
        lo = total - 4096.0 * hi - 64.0 * mid
        below = lambda d: jnp.dot(ltri_ref[...], jnp.broadcast_to(d, (n_exp, LANES)).astype(BF16),
                                  preferred_element_type=F32)[:, :1]
        pstart_ref[...] = (4096.0 * below(hi) + 64.0 * below(mid) + below(lo)).astype(jnp.int32)


def _router(h2, rw, rb):
    N, D = h2.shape
    E = rw.shape[1]
    tile = _pick(N, MOE_TILE)
    assert tile <= 256
    n_tiles = N // tile
    assert N * TOP_K + n_tiles * E < 2 ** 20
    wt = rw.T
    whi = wt.astype(BF16)
    wlo = (wt - whi.astype(F32)).astype(BF16)
    ar = jnp.arange(tile)
    tri = (ar[:, None] < ar[None, :]).astype(BF16)
    ae = jnp.arange(E)
    ltri = (ae[None, :] < ae[:, None]).astype(BF16)
    const = lambda shape: pl.BlockSpec(shape, lambda i: (0, 0))
    kn = pl.BlockSpec((TOP_K, tile), lambda i: (0, i))
    nk = pl.BlockSpec((tile, TOP_K), lambda i: (i, 0))
    table = pl.BlockSpec((None, E, 1), lambda i: (i, 0, 0))
    table_shape = jax.ShapeDtypeStruct((n_tiles, E, 1), jnp.int32)
    return pl.pallas_call(
        _router_kernel,
        grid=(n_tiles,),
        in_specs=[pl.BlockSpec((tile, D), lambda i: (i, 0)), const((E, D)), const((E, D)),
                  const((E, 1)), const((tile, tile)), const((E, E))],
        out_specs=[kn, kn, const((E, 1)), table, table, table,
                   pl.BlockSpec((None, 1, 1), lambda i: (i, 0, 0)), const((E, 1)), nk, nk],
        out_shape=[jax.ShapeDtypeStruct((TOP_K, N), F32), jax.ShapeDtypeStruct((TOP_K, N), jnp.int32),
                   jax.ShapeDtypeStruct((E, 1), F32), table_shape, table_shape, table_shape,
                   jax.ShapeDtypeStruct((n_tiles, 1, 1), jnp.int32), jax.ShapeDtypeStruct((E, 1), jnp.int32),
                   jax.ShapeDtypeStruct((N, TOP_K), F32), jax.ShapeDtypeStruct((N, TOP_K), jnp.int32)],
        compiler_params=_params(("arbitrary",)),
        name="moe_router",
    )(h2, whi, wlo, rb.reshape(E, 1), tri, ltri)


def _pack_pairs(x, rounded=False):
    half = x.shape[-1] // 2
    bits = lambda v: lax.bitcast_convert_type(v if rounded else v.astype(BF16).astype(F32), jnp.uint32)
    return (bits(x[:, :half]) & jnp.uint32(0xFFFF0000)) | (bits(x[:, half:]) >> 16)


def _unpack_pairs(w):
    hi = lax.bitcast_convert_type(w & jnp.uint32(0xFFFF0000), F32)
    lo = lax.bitcast_convert_type(w << 16, F32)
    return jnp.concatenate([hi.astype(BF16), lo.astype(BF16)], axis=-1)


def _to_slab(slab_ref, x, rounded=False, first=0):
    w = _pack_pairs(x, rounded)
    lines = w.shape[1] // LANES
    for a in range(lines):
        slab_ref[pl.ds(first * lines + a, w.shape[0], stride=lines), :] = w[:, a * LANES:(a + 1) * LANES]


def _from_slab(slab_ref, rows, lines, valid, first=0):
    w = jnp.concatenate([slab_ref[pl.ds(first * lines + a, rows, stride=lines), :] for a in range(lines)],
                        axis=-1)
    r = first + lax.broadcasted_iota(jnp.int32, (rows, 1), 0)
    return _unpack_pairs(jnp.where(r < valid, w, jnp.uint32(0)))


def _start_runs(toff_ref, gst_ref, tcnt_ref, local, remote, sem, lines, to_remote, enable=True,
                unroll=None):
    def body(e, c):
        cnt, off, gst = tcnt_ref[0, 0, e], toff_ref[0, 0, e], gst_ref[0, 0, e]

        @pl.when(jnp.logical_and(cnt > 0, enable))
        def _():
            loc = local.at[pl.ds(pl.multiple_of(off * lines, 8), cnt * lines), :]
            rem = remote.at[pl.ds(pl.multiple_of(gst * lines, 8), cnt * lines), :]
            (pltpu.make_async_copy(loc, rem, sem) if to_remote else pltpu.make_async_copy(rem, loc, sem)).start()
        return c
    lax.fori_loop(0, tcnt_ref.shape[2], body, 0, unroll=RUN_UNROLL if unroll is None else unroll)


def _wait_all_runs(slab, remote, sem, total_lines):
    n = pl.multiple_of(total_lines, 8)
    pltpu.make_async_copy(slab.at[pl.ds(0, n), :], remote.at[pl.ds(0, n), :], sem).wait()


def _sorted_rows(tile, n_exp):
    return TOP_K * tile + n_exp


def _dispatch_kernel(toff_ref, gst_ref, tcnt_ref, ttot_ref, tprev_ref, lpos_ref, h_ref, xs_ref, slabs, sems,
                     *, lines):
    i = pl.program_id(0)
    slot = lax.rem(i, 2)
    slab = slabs.at[slot]
    tile = h_ref.shape[0]
    rows = slab.shape[0] // lines
    hb = h_ref[...].astype(BF16)
    chunk = max(c for c in range(8, 257, 8) if rows % c == 0)
    srow = lax.broadcasted_iota(jnp.int32, (chunk, tile), 0).astype(F32).astype(BF16)
    one, zero = jnp.ones((), BF16), jnp.zeros((), BF16)
    for first in range(0, rows, chunk):
        perm = jnp.full((chunk, tile), zero)
        for k in range(TOP_K):
            local = lpos_ref[k:k + 1, :] - first
            inside = jnp.logical_and(local >= 0, local < chunk)
            local = jnp.where(inside, local, -1).astype(F32).astype(BF16)
            perm = jnp.where(srow == local, one, perm)
        _to_slab(slab, jnp.dot(perm, hb, preferred_element_type=F32), rounded=True, first=first)
    _start_runs(toff_ref, gst_ref, tcnt_ref, slab, xs_ref, sems.at[slot], lines, to_remote=True)

    @pl.when(i > 0)
    def _():
        _wait_all_runs(slabs.at[1 - slot], xs_ref, sems.at[1 - slot], tprev_ref[0, 0, 0] * lines)

    @pl.when(i == pl.num_programs(0) - 1)
    def _():
        _wait_all_runs(slab, xs_ref, sems.at[slot], ttot_ref[0, 0, 0] * lines)


def _run_specs(n_exp):
    smem = lambda n: pl.BlockSpec((1, 1, n), lambda i: (i, 0, 0), memory_space=pltpu.SMEM)
    return [smem(n_exp), smem(n_exp), smem(n_exp), smem(1)]


def _slab_lines(d_model):
    lines = d_model // (2 * LANES)
    assert lines * 2 * LANES == d_model and (2 * lines) % 8 == 0
    return lines


def _dispatch(h2, lpos, runs, n_slots):
    N, D = h2.shape
    E = runs[0].shape[2]
    tile = _pick(N, MOE_TILE)
    lines = _slab_lines(D)
    return pl.pallas_call(
        functools.partial(_dispatch_kernel, lines=lines),
        grid=(N // tile,),
        in_specs=_run_specs(E) + [
            pl.BlockSpec((1, 1, 1), lambda i: (jnp.maximum(i - 1, 0), 0, 0), memory_space=pltpu.SMEM),
            pl.BlockSpec((TOP_K, tile), lambda i: (0, i)), pl.BlockSpec((tile, D), lambda i: (i, 0))],
        out_specs=pl.BlockSpec(memory_space=pl.ANY),
        out_shape=jax.ShapeDtypeStruct((n_slots * lines, LANES), jnp.uint32),
        scratch_shapes=[pltpu.VMEM((2, _sorted_rows(tile, E) * lines, LANES), jnp.uint32),
                        pltpu.SemaphoreType.DMA((2,))],
        compiler_params=_params(("arbitrary",)),
        name="moe_dispatch",
    )(*runs, runs[3], lpos, h2)


def _expert_kernel(pstart_ref, cnt_ref, xs_hbm, wg_hbm, wu_hbm, wd_hbm, ys_hbm,
                   xbuf, ybuf, xo, yo, wg_f, wu_f, wd_f, wg_s, wu_s, wd_s, wsem, xsem, ysem, osem,
                   *, layer, lines, cap, chunk):
    e = pl.program_id(0)
    last = pl.num_programs(0) - 1
    n_in = xbuf.shape[0]
    islot = lax.rem(e, n_in)
    slot = lax.rem(e, 2)

    def weight_copies(ex, sl):
        return [pltpu.make_async_copy(w.at[layer, ex], f.at[sl], wsem.at[sl])
                for w, f in ((wg_hbm, wg_f), (wu_hbm, wu_f), (wd_hbm, wd_f))]

    def head_lines(ex):
        return jnp.minimum(cnt_ref[ex], cap) * lines

    def first_line(ex, row=0):
        return pl.multiple_of((pstart_ref[ex] + row) * lines, 8)

    def x_copy(ex, sl):
        n = head_lines(ex)
        return pltpu.make_async_copy(xs_hbm.at[pl.ds(first_line(ex), n), :], xbuf.at[sl, pl.ds(0, n), :],
                                     xsem.at[sl])

    def y_copy(ex, sl):
        n = head_lines(ex)
        return pltpu.make_async_copy(ybuf.at[sl, pl.ds(0, n), :], ys_hbm.at[pl.ds(first_line(ex), n), :],
                                     ysem.at[sl])

    def fetch(ex, sl):
        for cp in weight_copies(ex, sl):
            cp.start()

        @pl.when(cnt_ref[ex] > 0)
        def _():
            x_copy(ex, sl).start()

    @pl.when(e == 0)
    def _():
        for first in range(n_in - 1):
            @pl.when(first <= last)
            def _(first=first):
                fetch(first, first)

    for cp in weight_copies(e, islot):
        cp.wait()
    ahead = jnp.minimum(e + n_in - 1, last)

    @pl.when(e + n_in - 1 <= last)
    def _():
        fetch(ahead, lax.rem(ahead, n_in))

    wg_s[...] = wg_f[islot].astype(BF16)
    wu_s[...] = wu_f[islot].astype(BF16)
    wd_s[...] = wd_f[islot].astype(BF16)

    before = jnp.maximum(e - 2, 0)

    @pl.when(jnp.logical_and(e >= 2, cnt_ref[before] > 0))
    def _():
        y_copy(before, slot).wait()

    def ffn(x_ref, rows, valid):
        x = _from_slab(x_ref, rows, lines, valid)
        hg = jnp.dot(x, wg_s[...], preferred_element_type=F32)
        hu = jnp.dot(x, wu_s[...], preferred_element_type=F32)
        hb = hg * _sigmoid(hg) * hu
        return jnp.dot(hb.astype(BF16), wd_s[...], preferred_element_type=F32)

    count = cnt_ref[e]

    @pl.when(count > 0)
    def _():
        x_copy(e, islot).wait()
        head = jnp.minimum(count, cap)
        lower = 0
        for rows in range(chunk, cap + 1, chunk):
            @pl.when(jnp.logical_and(head > lower, head <= rows))
            def _(rows=rows):
                _to_slab(ybuf.at[slot], ffn(xbuf.at[islot], rows, head))
            lower = rows
        y_copy(e, slot).start()

        @pl.when(count > cap)
        def _():
            def extra(j, carry):
                row = cap + j * chunk
                n = jnp.minimum(chunk, count - row) * lines
                fetch = pltpu.make_async_copy(xs_hbm.at[pl.ds(first_line(e, row), n), :],
                                              xo.at[pl.ds(0, n), :], osem)
                fetch.start()
                fetch.wait()
                _to_slab(yo, ffn(xo, chunk, count - row))
                store = pltpu.make_async_copy(yo.at[pl.ds(0, n), :],
                                              ys_hbm.at[pl.ds(first_line(e, row), n), :], osem)
                store.start()
                store.wait()
                return carry
            lax.fori_loop(0, (count - cap + chunk - 1) // chunk, extra, 0)

    @pl.when(e == last)
    def _():
        @pl.when(count > 0)
        def _():
            y_copy(e, slot).wait()
        prev = jnp.maximum(e - 1, 0)

        @pl.when(jnp.logical_and(e >= 1, cnt_ref[prev] > 0))
        def _():
            y_copy(prev, 1 - slot).wait()


def _experts(xs, pstart, counts, wg, wu, wd, layer):
    E, D, Fe = wg.shape[-3:]
    lines = _slab_lines(D)
    cap, chunk = EXPERT_CAP, EXPERT_CHUNK
    assert cap % chunk == 0 and (chunk * lines) % 8 == 0
    n_in = EXPERT_AHEAD + 1
    hbm = pl.BlockSpec(memory_space=pl.ANY)
    slab = lambda rows: pltpu.VMEM(rows, jnp.uint32)
    return pl.pallas_call(
        functools.partial(_expert_kernel, layer=layer, lines=lines, cap=cap, chunk=chunk),
        grid_spec=pltpu.PrefetchScalarGridSpec(
            num_scalar_prefetch=2,
            grid=(E,),
            in_specs=[hbm, hbm, hbm, hbm],
            out_specs=hbm,
            scratch_shapes=[slab((n_in, cap * lines, LANES)), slab((2, cap * lines, LANES)),
                            slab((chunk * lines, LANES)), slab((chunk * lines, LANES)),
                            pltpu.VMEM((n_in, D, Fe), F32), pltpu.VMEM((n_in, D, Fe), F32),
                            pltpu.VMEM((n_in, Fe, D), F32),
                            pltpu.VMEM((D, Fe), BF16), pltpu.VMEM((D, Fe), BF16), pltpu.VMEM((Fe, D), BF16),
                            pltpu.SemaphoreType.DMA((n_in,)), pltpu.SemaphoreType.DMA((n_in,)),
                            pltpu.SemaphoreType.DMA((2,)), pltpu.SemaphoreType.DMA(())],
        ),
        out_shape=jax.ShapeDtypeStruct(xs.shape, jnp.uint32),
        compiler_params=_params(("arbitrary",)),
        name="moe_experts",
    )(pstart, counts, xs, wg, wu, wd)


def _combine_kernel(toff_ref, gst_ref, tcnt_ref, ttot_ref, ntoff_ref, ngst_ref, ntcnt_ref, lpos_ref, gate_ref,
                    h_ref, ys_ref, sg_ref, su_ref, sd_ref, g_ref, beta_ref, o_ref, slabs, sems, *, lines, alpha):
    i = pl.program_id(0)
    slot = lax.rem(i, 2)
    slab, sem = slabs.at[slot], sems.at[slot]

    @pl.when(i == 0)
    def _():
        _start_runs(toff_ref, gst_ref, tcnt_ref, slab, ys_ref, sem, lines, to_remote=False)

    _start_runs(ntoff_ref, ngst_ref, ntcnt_ref, slabs.at[1 - slot], ys_ref, sems.at[1 - slot], lines,
                to_remote=False, enable=i < pl.num_programs(0) - 1, unroll=True)

    h = h_ref[...]
    tile = h.shape[0]
    rows = slab.shape[0] // lines
    hb = h.astype(BF16)
    sgate = jnp.dot(hb, sg_ref[...], preferred_element_type=F32)
    sup = jnp.dot(hb, su_ref[...], preferred_element_type=F32)
    mid = sgate * _sigmoid(sgate) * sup
    acc = jnp.dot(mid.astype(BF16), sd_ref[...], preferred_element_type=F32)
    total = ttot_ref[0, 0, 0]
    _wait_all_runs(slab, ys_ref, sem, total * lines)
    chunk = max(c for c in range(8, 257, 8) if rows % c == 0)
    scol = lax.broadcasted_iota(jnp.int32, (tile, chunk), 1).astype(F32).astype(BF16)
    gates = [gate_ref[:, k:k + 1].astype(BF16) for k in range(TOP_K)]
    for first in range(0, rows, chunk):
        weights = jnp.zeros((tile, chunk), BF16)
        for k in range(TOP_K):
            local = lpos_ref[:, k:k + 1] - first
            inside = jnp.logical_and(local >= 0, local < chunk)
            local = jnp.where(inside, local, -1).astype(F32).astype(BF16)
            weights = jnp.where(scol == local, gates[k], weights)
        acc = acc + jnp.dot(weights, _from_slab(slab, chunk, lines, total, first),
                            preferred_element_type=F32)
    o_ref[...] = _layer_norm(alpha * h + acc, g_ref[...], beta_ref[...])


def _combine(h2, lpos_nk, gate_nk, runs, ys, sg, su, sd, g, beta, alpha):
    N, D = h2.shape
    Fs = sg.shape[1]
    E = runs[0].shape[2]
    tile = _pick(N, MOE_TILE)
    lines = _slab_lines(D)
    const = lambda shape: pl.BlockSpec(shape, lambda i: (0, 0))
    nk = pl.BlockSpec((tile, TOP_K), lambda i: (i, 0))
    last = N // tile - 1
    following = pl.BlockSpec((1, 1, E), lambda i: (jnp.minimum(i + 1, last), 0, 0), memory_space=pltpu.SMEM)
    return pl.pallas_call(
        functools.partial(_combine_kernel, lines=lines, alpha=alpha),
        grid=(N // tile,),
        in_specs=_run_specs(E) + [following] * 3 + [
            nk, nk, pl.BlockSpec((tile, D), lambda i: (i, 0)), pl.BlockSpec(memory_space=pl.ANY),
            const((D, Fs)), const((D, Fs)), const((Fs, D)), const((1, D)), const((1, D))],
        out_specs=pl.BlockSpec((tile, D), lambda i: (i, 0)),
        out_shape=jax.ShapeDtypeStruct((N, D), F32),
        scratch_shapes=[pltpu.VMEM((2, _sorted_rows(tile, E) * lines, LANES), jnp.uint32),
                        pltpu.SemaphoreType.DMA((2,))],
        compiler_params=_params(("arbitrary",)),
        name="moe_combine",
    )(*runs, *runs[:3], lpos_nk, gate_nk, h2, ys, sg.astype(BF16), su.astype(BF16), sd.astype(BF16),
      g.reshape(1, D), beta.reshape(1, D))


FLASH_TQ = 512
FLASH_TK = 512
MOE_TILE = 256
RUN_UNROLL = 8
EXPERT_CAP = 1024
EXPERT_CHUNK = 64
EXPERT_AHEAD = 2


def _moe_ln(h2, rw, rb, wg, wu, wd, layer, sg, su, sd, g, beta, alpha):
    N, D = h2.shape
    E = rw.shape[1]
    _, lpos, cnt, tcnt, toff, tbase, ttot, pstart, gate_nk, lpos_nk = _router(h2, rw, rb)
    n_tiles = ttot.shape[0]
    counts = cnt[:, 0].astype(jnp.int32)
    pstart = pstart[:, 0]
    n_slots = N * TOP_K + n_tiles * E
    per_tile = lambda a: a.reshape(n_tiles, 1, E)
    runs = (per_tile(toff), per_tile(tbase) + pstart[None, None, :], per_tile(tcnt), ttot)
    xs = _dispatch(h2, lpos, runs, n_slots)
    ys = _experts(xs, pstart, counts, wg, wu, wd, layer)
    return _combine(h2, lpos_nk, gate_nk, runs, ys, sg, su, sd, g, beta, alpha)


def _mla_proj_kernel(h_ref, wa_ref, qg_ref, kvg_ref, wqt_ref, wk_ref, wvt_ref, cos_ref, sin_ref,
                     cost_ref, sint_ref, qt_ref, k_ref, vt_ref, *, q_lora, kv_lora, heads, q_scale):
    x = h_ref[0].astype(BF16)
    a = jnp.dot(x, wa_ref[...], preferred_element_type=F32)

    def rms(v, g):
        return v * lax.rsqrt(jnp.mean(v * v, axis=-1, keepdims=True) + RMS_EPS) * g

    q_lat = rms(a[:, :q_lora], qg_ref[...]).astype(BF16)
    kv_lat = rms(a[:, q_lora:q_lora + kv_lora], kvg_ref[...]).astype(BF16)
    o = q_lora + kv_lora
    k_rope = (a[:, o:o + LANES] * cos_ref[...] + a[:, o + LANES:o + 2 * LANES] * sin_ref[...]).astype(BF16)
    qa = lax.dot_general(wqt_ref[...], q_lat, _NT, preferred_element_type=F32)
    kn = jnp.dot(kv_lat, wk_ref[...], preferred_element_type=F32)
    vt_ref[0] = lax.dot_general(wvt_ref[...], kv_lat, _NT, preferred_element_type=F32).astype(BF16)
    cost = cost_ref[...]
    sint = sint_ref[...]
    hd = heads * LANES
    for h in range(heads):
        r0 = slice(h * LANES, (h + 1) * LANES)
        r1 = slice(hd + h * LANES, hd + (h + 1) * LANES)
        r2 = slice(2 * hd + h * LANES, 2 * hd + (h + 1) * LANES)
        qt_ref[0, 2 * h * LANES:(2 * h + 1) * LANES, :] = (qa[r0] * q_scale).astype(BF16)
        qt_ref[0, (2 * h + 1) * LANES:(2 * h + 2) * LANES, :] = (
            (qa[r1] * cost + qa[r2] * sint) * q_scale).astype(BF16)
        k_ref[0, :, 2 * h * LANES:(2 * h + 1) * LANES] = kn[:, r0].astype(BF16)
        k_ref[0, :, (2 * h + 1) * LANES:(2 * h + 2) * LANES] = k_rope


def _rot_half(w):
    r = w.shape[-1] // 2
    return jnp.concatenate([-w[..., r:], w[..., :r]], axis=-1)


def _pad_lanes(w):
    return jnp.pad(w, [(0, 0)] * (w.ndim - 1) + [(0, LANES - w.shape[-1])])


FLASH_HEADROOM = 64.0
FLASH_PROBE = 16


def _flash_kernel(qt_ref, k_ref, vt_ref, o_ref, p_ref, s_ref, mx_ref, m_ref, l_ref, acc_ref, *, tk):
    qt = qt_ref[0]
    nk = k_ref.shape[1] // tk
    assert nk % 2 == 0 and nk >= 2

    def scores(c):
        off = pl.multiple_of(c * tk, tk)
        return jnp.dot(k_ref[0, pl.ds(off, tk), :], qt, preferred_element_type=F32)

    m0 = jnp.max(jnp.dot(k_ref[0, 0:FLASH_PROBE, :], qt, preferred_element_type=F32), axis=0, keepdims=True)
    acc_ref[...] = jnp.zeros_like(acc_ref)

    def probs(c, slot, carry):
        l, top = carry
        s = scores(c)
        p = jnp.exp2(s - m0)
        p_ref[slot] = p.astype(BF16)
        return l + jnp.sum(p, axis=0, keepdims=True), jnp.maximum(top, jnp.max(s, axis=0, keepdims=True))

    def accumulate(c, slot):
        off = pl.multiple_of(c * tk, tk)
        acc_ref[...] += jnp.dot(vt_ref[0, :, pl.ds(off, tk)], p_ref[slot], preferred_element_type=F32)

    carry = probs(0, 0, (jnp.zeros_like(m0), m0))
    for c in range(nk - 1):
        carry = probs(c + 1, (c + 1) % 2, carry)
        accumulate(c, c % 2)
    accumulate(nk - 1, (nk - 1) % 2)
    l, top = carry
    o_ref[0] = (acc_ref[...] / l).T.astype(o_ref.dtype)

    @pl.when(jnp.max(top - m0) > FLASH_HEADROOM)
    def _():
        _flash_online(qt, k_ref, vt_ref, o_ref, s_ref, mx_ref, m_ref, l_ref, acc_ref, tk)


def _flash_online(qt, k_ref, vt_ref, o_ref, s_ref, mx_ref, m_ref, l_ref, acc_ref, tk):
    nk = k_ref.shape[1] // tk
    assert nk % 2 == 0 and nk >= 2
    m_ref[...] = jnp.full_like(m_ref, -jnp.inf)
    l_ref[...] = jnp.zeros_like(l_ref)
    acc_ref[...] = jnp.zeros_like(acc_ref)

    def scores(c, slot):
        off = pl.multiple_of(c * tk, tk)
        s = jnp.dot(k_ref[0, pl.ds(off, tk), :], qt, preferred_element_type=F32)
        s_ref[slot] = s
        mx_ref[slot] = jnp.max(s, axis=0, keepdims=True)

    def update(c, slot):
        off = pl.multiple_of(c * tk, tk)
        vt = vt_ref[0, :, pl.ds(off, tk)]
        m_prev = m_ref[...]
        m_new = jnp.maximum(m_prev, mx_ref[slot])
        p = jnp.exp2(s_ref[slot] - m_new)
        a = jnp.exp2(m_prev - m_new)
        l_ref[...] = a * l_ref[...] + jnp.sum(p, axis=0, keepdims=True)
        acc_ref[...] = a * acc_ref[...] + jnp.dot(vt, p.astype(BF16), preferred_element_type=F32)
        m_ref[...] = m_new

    scores(0, 0)

    def body(i, carry):
        c = 2 * i
        scores(c + 1, 1)
        update(c, 0)
        scores(c + 2, 0)
        update(c + 1, 1)
        return carry

    lax.fori_loop(0, nk // 2 - 1, body, 0)
    scores(nk - 1, 1)
    update(nk - 2, 0)
    update(nk - 1, 1)
    o_ref[0] = (acc_ref[...] / l_ref[...]).T.astype(o_ref.dtype)


def _oproj_ln_kernel(o_ref, h_ref, w_ref, g_ref, beta_ref, out_ref, *, alpha):
    m = jnp.dot(o_ref[0], w_ref[...], preferred_element_type=F32)
    out_ref[0] = _layer_norm(alpha * h_ref[0] + m, g_ref[...], beta_ref[...])


def _mla_ln(h, w_a, qg, kvg, w_uq, w_uk, w_uv, w_o, g, beta, alpha):
    B, S, D = h.shape
    q_lora, heads, qk = w_uq.shape
    kv_lora, _, nope = w_uk.shape
    vd = w_uv.shape[-1]
    rope = qk - nope
    assert nope == LANES and vd == LANES and rope <= LANES and rope % 2 == 0
    kr = w_a[:, q_lora + kv_lora:]
    wa = jnp.concatenate([w_a[:, :q_lora + kv_lora], _pad_lanes(kr), _pad_lanes(_rot_half(kr))],
                         axis=1).astype(BF16)
    wq_r = w_uq[:, :, nope:]
    wq = jnp.concatenate([w_uq[:, :, :nope].reshape(q_lora, heads * nope),
                          _pad_lanes(wq_r).reshape(q_lora, heads * LANES),
                          _pad_lanes(_rot_half(wq_r)).reshape(q_lora, heads * LANES)], axis=1).astype(BF16)
    wk = w_uk.reshape(kv_lora, heads * nope).astype(BF16)
    wvt = w_uv.reshape(kv_lora, heads * vd).T.astype(BF16)
    wqt = wq.T
    pos = jnp.arange(S, dtype=F32)
    inv = 1.0 / (ROPE_THETA ** (jnp.arange(0, rope, 2, dtype=F32) / rope))
    ang = pos[:, None] * inv[None, :]
    cos = _pad_lanes(jnp.concatenate([jnp.cos(ang), jnp.cos(ang)], axis=1))
    sin = _pad_lanes(jnp.concatenate([jnp.sin(ang), jnp.sin(ang)], axis=1))
    q_scale = float(qk) ** -0.5 * math.log2(math.e)

    tp = _pick(S, 256)
    const = lambda a: pl.BlockSpec(a.shape, lambda b, i: (0,) * a.ndim)
    qg2, kvg2 = qg.reshape(1, q_lora), kvg.reshape(1, kv_lora)
    tok_major = pl.BlockSpec((tp, LANES), lambda b, i: (i, 0))
    feat_major = pl.BlockSpec((LANES, tp), lambda b, i: (0, i))
    qt, k, vt = pl.pallas_call(
        functools.partial(_mla_proj_kernel, q_lora=q_lora, kv_lora=kv_lora, heads=heads, q_scale=q_scale),
        grid=(B, S // tp),
        in_specs=[pl.BlockSpec((1, tp, D), lambda b, i: (b, i, 0)), const(wa), const(qg2), const(kvg2),
                  const(wqt), const(wk), const(wvt), tok_major, tok_major, feat_major, feat_major],
        out_specs=[pl.BlockSpec((1, 2 * heads * LANES, tp), lambda b, i: (b, 0, i)),
                   pl.BlockSpec((1, tp, 2 * heads * LANES), lambda b, i: (b, i, 0)),
                   pl.BlockSpec((1, heads * LANES, tp), lambda b, i: (b, 0, i))],
        out_shape=[jax.ShapeDtypeStruct((B, 2 * heads * LANES, S), BF16),
                   jax.ShapeDtypeStruct((B, S, 2 * heads * LANES), BF16),
                   jax.ShapeDtypeStruct((B, heads * LANES, S), BF16)],
        compiler_params=_params(("parallel", "parallel")),
        name="mla_proj",
    )(h, wa, qg2, kvg2, wqt, wk, wvt, cos, sin, cos.T, sin.T)

    tq = _pick(S, FLASH_TQ)
    tk = _pick(S, FLASH_TK)
    o = pl.pallas_call(
        functools.partial(_flash_kernel, tk=tk),
        grid=(B, heads, S // tq),
        in_specs=[pl.BlockSpec((1, 2 * LANES, tq), lambda b, hh, i: (b, hh, i)),
                  pl.BlockSpec((1, S, 2 * LANES), lambda b, hh, i: (b, 0, hh)),
                  pl.BlockSpec((1, LANES, S), lambda b, hh, i: (b, hh, 0))],
        out_specs=pl.BlockSpec((1, tq, LANES), lambda b, hh, i: (b, i, hh)),
        out_shape=jax.ShapeDtypeStruct((B, S, heads * LANES), BF16),
        scratch_shapes=[pltpu.VMEM((2, tk, tq), BF16), pltpu.VMEM((2, tk, tq), F32), pltpu.VMEM((2, 1, tq), F32),
                        pltpu.VMEM((1, tq), F32), pltpu.VMEM((1, tq), F32), pltpu.VMEM((LANES, tq), F32)],
        compiler_params=_params(("parallel", "parallel", "arbitrary")),
        name="mla_flash",
    )(qt, k, vt)

    to = _pick(S, 512)
    vec = pl.BlockSpec((1, D), lambda b, i: (0, 0))
    return pl.pallas_call(
        functools.partial(_oproj_ln_kernel, alpha=alpha),
        grid=(B, S // to),
        in_specs=[pl.BlockSpec((1, to, heads * vd), lambda b, i: (b, i, 0)),
                  pl.BlockSpec((1, to, D), lambda b, i: (b, i, 0)),
                  pl.BlockSpec((heads * vd, D), lambda b, i: (0, 0)), vec, vec],
        out_specs=pl.BlockSpec((1, to, D), lambda b, i: (b, i, 0)),
        out_shape=jax.ShapeDtypeStruct((B, S, D), F32),
        compiler_params=_params(("parallel", "parallel")),
        name="mla_oproj_ln",
    )(o, h, w_o.astype(BF16), g.reshape(1, D), beta.reshape(1, D))


def kernel(x, pool_w, pool_b, pool_scale, mla_w_a, mla_q_norm_g, mla_kv_norm_g, mla_w_uq, mla_w_uk,
           mla_w_uv, mla_w_o, ln_mix_g, ln_mix_b, router_w, router_b, exp_w_gate, exp_w_up, exp_w_down,
           sh_w_gate, sh_w_up, sh_w_down, ln_ffn_g, ln_ffn_b):
    B, S, D = x.shape
    depth = ln_mix_g.shape[0]
    alpha = (2 * depth) ** 0.25
    h = x
    for i in range(depth):
        j = i // 2
        if i % 2 == 0:
            h = _pool_ln(h, pool_w[j], pool_b[j], pool_scale[j], ln_mix_g[i], ln_mix_b[i], alpha)
        else:
            h = _mla_ln(h, mla_w_a[j], mla_q_norm_g[j], mla_kv_norm_g[j], mla_w_uq[j], mla_w_uk[j],
                        mla_w_uv[j], mla_w_o[j], ln_mix_g[i], ln_mix_b[i], alpha)
        h = _moe_ln(h.reshape(B * S, D), router_w[i], router_b[i], exp_w_gate, exp_w_up, exp_w_down, i,
                    sh_w_gate[i], sh_w_up[i], sh_w_down[i], ln_ffn_g[i], ln_ffn_b[i],
                    alpha).reshape(B, S, D)
    return h
```

```python
import functools
import math

import jax
import jax.numpy as jnp
from jax import lax
from jax.experimental import pallas as pl
from jax.experimental.pallas import tpu as pltpu

F32 = jnp.float32
BF16 = jnp.bfloat16

LN_EPS = 1e-5
RMS_EPS = 1e-6
POOL_WINDOWS = (2, 4, 8, 16)
POOL_HALO = 8
ROPE_THETA = 10000.0
TOP_K = 8
N_GROUPS = 8
TOPK_GROUPS = 4
ROUTED_SCALE = 2.5
LANES = 128
VMEM_LIMIT = 48 * 1024 * 1024

_NT = (((1,), (1,)), ((), ()))


def _pick(n, pref):
    t = min(pref, n)
    while n % t:
        t //= 2
    assert t >= 8 and n % t == 0, (n, pref)
    return t


def _layer_norm(z, g, b):
    mu = jnp.mean(z, axis=-1, keepdims=True)
    zc = z - mu
    var = jnp.mean(zc * zc, axis=-1, keepdims=True)
    return zc * lax.rsqrt(var + LN_EPS) * g + b


def _sigmoid(x):
    return 1.0 / (1.0 + jnp.exp(-x))


def _params(sem):
    return pltpu.CompilerParams(dimension_semantics=sem, vmem_limit_bytes=VMEM_LIMIT)


def _pool_ln_kernel(xc_ref, xp_ref, xn_ref, w_ref, b_ref, sc_ref, g_ref, beta_ref, o_ref, ext_ref,
                    *, tile, seq, alpha):
    i = pl.program_id(1)
    n_tiles = pl.num_programs(1)
    xc = xc_ref[0]
    d_model = xc.shape[-1]
    ch = d_model // len(POOL_WINDOWS)
    ext_ref[0:POOL_HALO, :] = jnp.where(i > 0, xp_ref[0], 0.0)
    ext_ref[POOL_HALO:POOL_HALO + tile, :] = xc
    ext_ref[POOL_HALO + tile:2 * POOL_HALO + tile, :] = jnp.where(i < n_tiles - 1, xn_ref[0], 0.0)
    pos = i * tile + lax.broadcasted_iota(jnp.int32, (tile, 1), 0)
    n_ext = tile + 2 * POOL_HALO
    back = lambda v, s: pltpu.roll(v, s, axis=0)
    fwd = lambda v, s: pltpu.roll(v, n_ext - s, axis=0)
    outs = []
    for g, win in enumerate(POOL_WINDOWS):
        half = win // 2
        cols = slice(g * ch, (g + 1) * ch)
        acc = ext_ref[:, cols]
        acc = acc + back(acc, 1)
        step = 1
        while 2 * step < win:
            acc = back(acc, step) + fwd(acc, step)
            step *= 2
        acc = acc[POOL_HALO:POOL_HALO + tile]
        cnt = (jnp.minimum(pos + half, seq) - jnp.maximum(pos - half, 0)).astype(F32)
        pooled = acc / cnt - xc[:, cols]
        outs.append(jnp.dot(pooled.astype(BF16), w_ref[g], preferred_element_type=F32))
    y = jnp.concatenate(outs, axis=-1)
    y = (y + b_ref[...]) * sc_ref[...]
    o_ref[0] = _layer_norm(alpha * xc + y, g_ref[...], beta_ref[...])


def _pool_ln(x, w, b, sc, g, beta, alpha):
    B, S, D = x.shape
    tile = _pick(S, 512)
    nh = tile // POOL_HALO
    last = S // POOL_HALO - 1
    row = lambda v: v.reshape(1, D)
    vec = pl.BlockSpec((1, D), lambda bi, i: (0, 0))
    return pl.pallas_call(
        functools.partial(_pool_ln_kernel, tile=tile, seq=S, alpha=alpha),
        grid=(B, S // tile),
        in_specs=[
            pl.BlockSpec((1, tile, D), lambda bi, i: (bi, i, 0)),
            pl.BlockSpec((1, POOL_HALO, D), lambda bi, i: (bi, jnp.maximum(i * nh - 1, 0), 0)),
            pl.BlockSpec((1, POOL_HALO, D), lambda bi, i: (bi, jnp.minimum((i + 1) * nh, last), 0)),
            pl.BlockSpec(w.shape, lambda bi, i: (0, 0, 0)),
            vec, vec, vec, vec,
        ],
        out_specs=pl.BlockSpec((1, tile, D), lambda bi, i: (bi, i, 0)),
        out_shape=jax.ShapeDtypeStruct((B, S, D), F32),
        scratch_shapes=[pltpu.VMEM((tile + 2 * POOL_HALO, D), F32)],
        compiler_params=_params(("parallel", "parallel")),
        name="pool_ln",
    )(x, x, x, w.astype(BF16), row(b), row(sc), row(g), row(beta))


def _router_kernel(h_ref, whi_ref, wlo_ref, rb_ref, tri_ref, ltri_ref,
                   gate_ref, lpos_ref, cnt_ref, tcnt_ref, toff_ref, tbase_ref, ttot_ref, pstart_ref,
                   gate_nk_ref, lpos_nk_ref):
    i = pl.program_id(0)

    @pl.when(i == 0)
    def _():
        cnt_ref[...] = jnp.zeros_like(cnt_ref)

    x = h_ref[...]
    xhi = x.astype(BF16)
    xlo = (x - xhi.astype(F32)).astype(BF16)
    whi = whi_ref[...]
    logits = (lax.dot_general(whi, xhi, _NT, preferred_element_type=F32)
              + lax.dot_general(whi, xlo, _NT, preferred_element_type=F32)
              + lax.dot_general(wlo_ref[...], xhi, _NT, preferred_element_type=F32))
    n_exp, tile = logits.shape
    per = n_exp // N_GROUPS
    scores = _sigmoid(logits)
    biased = scores + rb_ref[...]
    neg = -jnp.inf

    sub = lax.broadcasted_iota(jnp.int32, (per, tile), 0).astype(F32)
    gscore = []
    for g in range(N_GROUPS):
        bg = biased[g * per:(g + 1) * per, :]
        m1 = jnp.max(bg, axis=0, keepdims=True)
        i1 = jnp.min(jnp.where(bg == m1, sub, float(per)), axis=0, keepdims=True)
        m2 = jnp.max(jnp.where(sub == i1, neg, bg), axis=0, keepdims=True)
        gscore.append(m1 + m2)
    masked = []
    for g in range(N_GROUPS):
        beat = jnp.zeros_like(gscore[g])
        for o in range(N_GROUPS):
            if o == g:
                continue
            wins = (gscore[o] >= gscore[g]) if o < g else (gscore[o] > gscore[g])
            beat = beat + jnp.where(wins, 1.0, 0.0)
        keep = jnp.broadcast_to(beat < float(TOPK_GROUPS), (per, tile))
        masked.append(jnp.where(keep, biased[g * per:(g + 1) * per, :], neg))
    masked = jnp.concatenate(masked, axis=0)

    row = lax.broadcasted_iota(jnp.int32, (n_exp, tile), 0).astype(F32)
    chosen = jnp.zeros((n_exp, tile), F32)
    idxs, gates = [], []
    for _ in range(TOP_K):
        m = jnp.max(masked, axis=0, keepdims=True)
        idx = jnp.min(jnp.where(masked == m, row, float(n_exp)), axis=0, keepdims=True)
        hit = row == idx
        gates.append(jnp.sum(jnp.where(hit, scores, 0.0), axis=0, keepdims=True))
        masked = jnp.where(hit, neg, masked)
        chosen = jnp.where(hit, 1.0, chosen)
        idxs.append(idx)
    gsum = gates[0]
    for k in range(1, TOP_K):
        gsum = gsum + gates[k]

    earlier = jnp.dot(chosen.astype(BF16), tri_ref[...], preferred_element_type=F32)
    count = jnp.sum(chosen, axis=1, keepdims=True)
    run = count + (count - 2.0 * jnp.floor(0.5 * count))
    lower = jnp.dot(ltri_ref[...], jnp.broadcast_to(run, (n_exp, LANES)).astype(BF16),
                    preferred_element_type=F32)[:, :1]
    order = earlier + lower
    for k in range(TOP_K):
        gate_ref[k:k + 1, :] = gates[k] / gsum * ROUTED_SCALE
        lpos_k = jnp.sum(jnp.where(row == idxs[k], order, 0.0), axis=0, keepdims=True)
        lpos_ref[k:k + 1, :] = lpos_k.astype(jnp.int32)
    gate_nk_ref[...] = gate_ref[...].T
    lpos_nk_ref[...] = lpos_ref[...].astype(F32).T.astype(jnp.int32)
    tcnt_ref[...] = run.astype(jnp.int32)
    toff_ref[...] = lower.astype(jnp.int32)
    tbase_ref[...] = cnt_ref[...].astype(jnp.int32)
    ttot_ref[...] = jnp.sum(run, axis=0, keepdims=True).astype(jnp.int32)
    cnt_ref[...] = cnt_ref[...] + run

    @pl.when(i == pl.num_programs(0) - 1)
    def _():
        total = cnt_ref[...]
        hi = jnp.floor(total * (1.0 / 4096.0))
        mid = jnp.floor((total - 4096.0 * hi) * (1.0 / 64.0))
        lo = total - 4096.0 * hi - 64.0 * mid
        below = lambda d: jnp.dot(ltri_ref[...], jnp.broadcast_to(d, (n_exp, LANES)).astype(BF16),
                                  preferred_element_type=F32)[:, :1]
        pstart_ref[...] = (4096.0 * below(hi) + 64.0 * below(mid) + below(lo)).astype(jnp.int32)


def _router(h2, rw, rb):
    N, D = h2.shape
    E = rw.shape[1]
    tile = _pick(N, MOE_TILE)
    assert tile <= 256
    n_tiles = N // tile
    assert N * TOP_K + n_tiles * E < 2 ** 20
    wt = rw.T
    whi = wt.astype(BF16)
    wlo = (wt - whi.astype(F32)).astype(BF16)
    ar = jnp.arange(tile)
    tri = (ar[:, None] < ar[None, :]).astype(BF16)
    ae = jnp.arange(E)
    ltri = (ae[None, :] < ae[:, None]).astype(BF16)
    const = lambda shape: pl.BlockSpec(shape, lambda i: (0, 0))
    kn = pl.BlockSpec((TOP_K, tile), lambda i: (0, i))
    nk = pl.BlockSpec((tile, TOP_K), lambda i: (i, 0))
    table = pl.BlockSpec((None, E, 1), lambda i: (i, 0, 0))
    table_shape = jax.ShapeDtypeStruct((n_tiles, E, 1), jnp.int32)
    return pl.pallas_call(
        _router_kernel,
        grid=(n_tiles,),
        in_specs=[pl.BlockSpec((tile, D), lambda i: (i, 0)), const((E, D)), const((E, D)),
                  const((E, 1)), const((tile, tile)), const((E, E))],
        out_specs=[kn, kn, const((E, 1)), table, table, table,
                   pl.BlockSpec((None, 1, 1), lambda i: (i, 0, 0)), const((E, 1)), nk, nk],
        out_shape=[jax.ShapeDtypeStruct((TOP_K, N), F32), jax.ShapeDtypeStruct((TOP_K, N), jnp.int32),
                   jax.ShapeDtypeStruct((E, 1), F32), table_shape, table_shape, table_shape,
                   jax.ShapeDtypeStruct((n_tiles, 1, 1), jnp.int32), jax.ShapeDtypeStruct((E, 1), jnp.int32),
                   jax.ShapeDtypeStruct((N, TOP_K), F32), jax.ShapeDtypeStruct((N, TOP_K), jnp.int32)],
        compiler_params=_params(("arbitrary",)),
        name="moe_router",
    )(h2, whi, wlo, rb.reshape(E, 1), tri, ltri)


def _pack_pairs(x, rounded=False):
    half = x.shape[-1] // 2
    bits = lambda v: lax.bitcast_convert_type(v if rounded else v.astype(BF16).astype(F32), jnp.uint32)
    return (bits(x[:, :half]) & jnp.uint32(0xFFFF0000)) | (bits(x[:, half:]) >> 16)


def _unpack_pairs(w):
    hi = lax.bitcast_convert_type(w & jnp.uint32(0xFFFF0000), F32)
    lo = lax.bitcast_convert_type(w << 16, F32)
    return jnp.concatenate([hi.astype(BF16), lo.astype(BF16)], axis=-1)


def _to_slab(slab_ref, x, rounded=False, first=0):
    w = _pack_pairs(x, rounded)
    lines = w.shape[1] // LANES
    for a in range(lines):
        slab_ref[pl.ds(first * lines + a, w.shape[0], stride=lines), :] = w[:, a * LANES:(a + 1) * LANES]


def _from_slab(slab_ref, rows, lines, valid, first=0):
    w = jnp.concatenate([slab_ref[pl.ds(first * lines + a, rows, stride=lines), :] for a in range(lines)],
                        axis=-1)
    r = first + lax.broadcasted_iota(jnp.int32, (rows, 1), 0)
    return _unpack_pairs(jnp.where(r < valid, w, jnp.uint32(0)))


def _start_runs(toff_ref, gst_ref, tcnt_ref, local, remote, sem, lines, to_remote, enable=True,
                unroll=None):
    def body(e, c):
        cnt, off, gst = tcnt_ref[0, 0, e], toff_ref[0, 0, e], gst_ref[0, 0, e]

        @pl.when(jnp.logical_and(cnt > 0, enable))
        def _():
            loc = local.at[pl.ds(pl.multiple_of(off * lines, 8), cnt * lines), :]
            rem = remote.at[pl.ds(pl.multiple_of(gst * lines, 8), cnt * lines), :]
            (pltpu.make_async_copy(loc, rem, sem) if to_remote else pltpu.make_async_copy(rem, loc, sem)).start()
        return c
    lax.fori_loop(0, tcnt_ref.shape[2], body, 0, unroll=RUN_UNROLL if unroll is None else unroll)


def _wait_all_runs(slab, remote, sem, total_lines):
    n = pl.multiple_of(total_lines, 8)
    pltpu.make_async_copy(slab.at[pl.ds(0, n), :], remote.at[pl.ds(0, n), :], sem).wait()


def _sorted_rows(tile, n_exp):
    return TOP_K * tile + n_exp


def _dispatch_kernel(toff_ref, gst_ref, tcnt_ref, ttot_ref, tprev_ref, lpos_ref, h_ref, xs_ref, slabs, sems,
                     *, lines):
    i = pl.program_id(0)
    slot = lax.rem(i, 2)
    slab = slabs.at[slot]
    tile = h_ref.shape[0]
    rows = slab.shape[0] // lines
    hb = h_ref[...].astype(BF16)
    chunk = max(c for c in range(8, 257, 8) if rows % c == 0)
    srow = lax.broadcasted_iota(jnp.int32, (chunk, tile), 0).astype(F32).astype(BF16)
    one, zero = jnp.ones((), BF16), jnp.zeros((), BF16)
    for first in range(0, rows, chunk):
        perm = jnp.full((chunk, tile), zero)
        for k in range(TOP_K):
            local = lpos_ref[k:k + 1, :] - first
            inside = jnp.logical_and(local >= 0, local < chunk)
            local = jnp.where(inside, local, -1).astype(F32).astype(BF16)
            perm = jnp.where(srow == local, one, perm)
        _to_slab(slab, jnp.dot(perm, hb, preferred_element_type=F32), rounded=True, first=first)
    _start_runs(toff_ref, gst_ref, tcnt_ref, slab, xs_ref, sems.at[slot], lines, to_remote=True)

    @pl.when(i > 0)
    def _():
        _wait_all_runs(slabs.at[1 - slot], xs_ref, sems.at[1 - slot], tprev_ref[0, 0, 0] * lines)

    @pl.when(i == pl.num_programs(0) - 1)
    def _():
        _wait_all_runs(slab, xs_ref, sems.at[slot], ttot_ref[0, 0, 0] * lines)


def _run_specs(n_exp):
    smem = lambda n: pl.BlockSpec((1, 1, n), lambda i: (i, 0, 0), memory_space=pltpu.SMEM)
    return [smem(n_exp), smem(n_exp), smem(n_exp), smem(1)]


def _slab_lines(d_model):
    lines = d_model // (2 * LANES)
    assert lines * 2 * LANES == d_model and (2 * lines) % 8 == 0
    return lines


def _dispatch(h2, lpos, runs, n_slots):
    N, D = h2.shape
    E = runs[0].shape[2]
    tile = _pick(N, MOE_TILE)
    lines = _slab_lines(D)
    return pl.pallas_call(
        functools.partial(_dispatch_kernel, lines=lines),
        grid=(N // tile,),
        in_specs=_run_specs(E) + [
            pl.BlockSpec((1, 1, 1), lambda i: (jnp.maximum(i - 1, 0), 0, 0), memory_space=pltpu.SMEM),
            pl.BlockSpec((TOP_K, tile), lambda i: (0, i)), pl.BlockSpec((tile, D), lambda i: (i, 0))],
        out_specs=pl.BlockSpec(memory_space=pl.ANY),
        out_shape=jax.ShapeDtypeStruct((n_slots * lines, LANES), jnp.uint32),
        scratch_shapes=[pltpu.VMEM((2, _sorted_rows(tile, E) * lines, LANES), jnp.uint32),
                        pltpu.SemaphoreType.DMA((2,))],
        compiler_params=_params(("arbitrary",)),
        name="moe_dispatch",
    )(*runs, runs[3], lpos, h2)


def _expert_kernel(pstart_ref, cnt_ref, xs_hbm, wg_hbm, wu_hbm, wd_hbm, ys_hbm,
                   xbuf, ybuf, xo, yo, wg_f, wu_f, wd_f, wg_s, wu_s, wd_s, wsem, xsem, ysem, osem,
                   *, layer, lines, cap, chunk):
    e = pl.program_id(0)
    last = pl.num_programs(0) - 1
    n_in = xbuf.shape[0]
    islot = lax.rem(e, n_in)
    slot = lax.rem(e, 2)

    def weight_copies(ex, sl):
        return [pltpu.make_async_copy(w.at[layer, ex], f.at[sl], wsem.at[sl])
                for w, f in ((wg_hbm, wg_f), (wu_hbm, wu_f), (wd_hbm, wd_f))]

    def head_lines(ex):
        return jnp.minimum(cnt_ref[ex], cap) * lines

    def first_line(ex, row=0):
        return pl.multiple_of((pstart_ref[ex] + row) * lines, 8)

    def x_copy(ex, sl):
        n = head_lines(ex)
        return pltpu.make_async_copy(xs_hbm.at[pl.ds(first_line(ex), n), :], xbuf.at[sl, pl.ds(0, n), :],
                                     xsem.at[sl])

    def y_copy(ex, sl):
        n = head_lines(ex)
        return pltpu.make_async_copy(ybuf.at[sl, pl.ds(0, n), :], ys_hbm.at[pl.ds(first_line(ex), n), :],
                                     ysem.at[sl])

    def fetch(ex, sl):
        for cp in weight_copies(ex, sl):
            cp.start()

        @pl.when(cnt_ref[ex] > 0)
        def _():
            x_copy(ex, sl).start()

    @pl.when(e == 0)
    def _():
        for first in range(n_in - 1):
            @pl.when(first <= last)
            def _(first=first):
                fetch(first, first)

    for cp in weight_copies(e, islot):
        cp.wait()
    ahead = jnp.minimum(e + n_in - 1, last)

    @pl.when(e + n_in - 1 <= last)
    def _():
        fetch(ahead, lax.rem(ahead, n_in))

    wg_s[...] = wg_f[islot].astype(BF16)
    wu_s[...] = wu_f[islot].astype(BF16)
    wd_s[...] = wd_f[islot].astype(BF16)

    before = jnp.maximum(e - 2, 0)

    @pl.when(jnp.logical_and(e >= 2, cnt_ref[before] > 0))
    def _():
        y_copy(before, slot).wait()

    def ffn(x_ref, rows, valid):
        x = _from_slab(x_ref, rows, lines, valid)
        hg = jnp.dot(x, wg_s[...], preferred_element_type=F32)
        hu = jnp.dot(x, wu_s[...], preferred_element_type=F32)
        hb = hg * _sigmoid(hg) * hu
        return jnp.dot(hb.astype(BF16), wd_s[...], preferred_element_type=F32)

    count = cnt_ref[e]

    @pl.when(count > 0)
    def _():
        x_copy(e, islot).wait()
        head = jnp.minimum(count, cap)
        lower = 0
        for rows in range(chunk, cap + 1, chunk):
            @pl.when(jnp.logical_and(head > lower, head <= rows))
            def _(rows=rows):
                _to_slab(ybuf.at[slot], ffn(xbuf.at[islot], rows, head))
            lower = rows
        y_copy(e, slot).start()

        @pl.when(count > cap)
        def _():
            def extra(j, carry):
                row = cap + j * chunk
                n = jnp.minimum(chunk, count - row) * lines
                fetch = pltpu.make_async_copy(xs_hbm.at[pl.ds(first_line(e, row), n), :],
                                              xo.at[pl.ds(0, n), :], osem)
                fetch.start()
                fetch.wait()
                _to_slab(yo, ffn(xo, chunk, count - row))
                store = pltpu.make_async_copy(yo.at[pl.ds(0, n), :],
                                              ys_hbm.at[pl.ds(first_line(e, row), n), :], osem)
                store.start()
                store.wait()
                return carry
            lax.fori_loop(0, (count - cap + chunk - 1) // chunk, extra, 0)

    @pl.when(e == last)
    def _():
        @pl.when(count > 0)
        def _():
            y_copy(e, slot).wait()
        prev = jnp.maximum(e - 1, 0)

        @pl.when(jnp.logical_and(e >= 1, cnt_ref[prev] > 0))
        def _():
            y_copy(prev, 1 - slot).wait()


def _experts(xs, pstart, counts, wg, wu, wd, layer):
    E, D, Fe = wg.shape[-3:]
    lines = _slab_lines(D)
    cap, chunk = EXPERT_CAP, EXPERT_CHUNK
    assert cap % chunk == 0 and (chunk * lines) % 8 == 0
    n_in = EXPERT_AHEAD + 1
    hbm = pl.BlockSpec(memory_space=pl.ANY)
    slab = lambda rows: pltpu.VMEM(rows, jnp.uint32)
    return pl.pallas_call(
        functools.partial(_expert_kernel, layer=layer, lines=lines, cap=cap, chunk=chunk),
        grid_spec=pltpu.PrefetchScalarGridSpec(
            num_scalar_prefetch=2,
            grid=(E,),
            in_specs=[hbm, hbm, hbm, hbm],
            out_specs=hbm,
            scratch_shapes=[slab((n_in, cap * lines, LANES)), slab((2, cap * lines, LANES)),
                            slab((chunk * lines, LANES)), slab((chunk * lines, LANES)),
                            pltpu.VMEM((n_in, D, Fe), F32), pltpu.VMEM((n_in, D, Fe), F32),
                            pltpu.VMEM((n_in, Fe, D), F32),
                            pltpu.VMEM((D, Fe), BF16), pltpu.VMEM((D, Fe), BF16), pltpu.VMEM((Fe, D), BF16),
                            pltpu.SemaphoreType.DMA((n_in,)), pltpu.SemaphoreType.DMA((n_in,)),
                            pltpu.SemaphoreType.DMA((2,)), pltpu.SemaphoreType.DMA(())],
        ),
        out_shape=jax.ShapeDtypeStruct(xs.shape, jnp.uint32),
        compiler_params=_params(("arbitrary",)),
        name="moe_experts",
    )(pstart, counts, xs, wg, wu, wd)


def _combine_kernel(toff_ref, gst_ref, tcnt_ref, ttot_ref, ntoff_ref, ngst_ref, ntcnt_ref, lpos_ref, gate_ref,
                    h_ref, ys_ref, sg_ref, su_ref, sd_ref, g_ref, beta_ref, o_ref, slabs, sems, *, lines, alpha):
    i = pl.program_id(0)
    slot = lax.rem(i, 2)
    slab, sem = slabs.at[slot], sems.at[slot]

    @pl.when(i == 0)
    def _():
        _start_runs(toff_ref, gst_ref, tcnt_ref, slab, ys_ref, sem, lines, to_remote=False)

    _start_runs(ntoff_ref, ngst_ref, ntcnt_ref, slabs.at[1 - slot], ys_ref, sems.at[1 - slot], lines,
                to_remote=False, enable=i < pl.num_programs(0) - 1, unroll=True)

    h = h_ref[...]
    tile = h.shape[0]
    rows = slab.shape[0] // lines
    hb = h.astype(BF16)
    sgate = jnp.dot(hb, sg_ref[...], preferred_element_type=F32)
    sup = jnp.dot(hb, su_ref[...], preferred_element_type=F32)
    mid = sgate * _sigmoid(sgate) * sup
    acc = jnp.dot(mid.astype(BF16), sd_ref[...], preferred_element_type=F32)
    total = ttot_ref[0, 0, 0]
    _wait_all_runs(slab, ys_ref, sem, total * lines)
    chunk = max(c for c in range(8, 257, 8) if rows % c == 0)
    scol = lax.broadcasted_iota(jnp.int32, (tile, chunk), 1).astype(F32).astype(BF16)
    gates = [gate_ref[:, k:k + 1].astype(BF16) for k in range(TOP_K)]
    for first in range(0, rows, chunk):
        weights = jnp.zeros((tile, chunk), BF16)
        for k in range(TOP_K):
            local = lpos_ref[:, k:k + 1] - first
            inside = jnp.logical_and(local >= 0, local < chunk)
            local = jnp.where(inside, local, -1).astype(F32).astype(BF16)
            weights = jnp.where(scol == local, gates[k], weights)
        acc = acc + jnp.dot(weights, _from_slab(slab, chunk, lines, total, first),
                            preferred_element_type=F32)
    o_ref[...] = _layer_norm(alpha * h + acc, g_ref[...], beta_ref[...])


def _combine(h2, lpos_nk, gate_nk, runs, ys, sg, su, sd, g, beta, alpha):
    N, D = h2.shape
    Fs = sg.shape[1]
    E = runs[0].shape[2]
    tile = _pick(N, MOE_TILE)
    lines = _slab_lines(D)
    const = lambda shape: pl.BlockSpec(shape, lambda i: (0, 0))
    nk = pl.BlockSpec((tile, TOP_K), lambda i: (i, 0))
    last = N // tile - 1
    following = pl.BlockSpec((1, 1, E), lambda i: (jnp.minimum(i + 1, last), 0, 0), memory_space=pltpu.SMEM)
    return pl.pallas_call(
        functools.partial(_combine_kernel, lines=lines, alpha=alpha),
        grid=(N // tile,),
        in_specs=_run_specs(E) + [following] * 3 + [
            nk, nk, pl.BlockSpec((tile, D), lambda i: (i, 0)), pl.BlockSpec(memory_space=pl.ANY),
            const((D, Fs)), const((D, Fs)), const((Fs, D)), const((1, D)), const((1, D))],
        out_specs=pl.BlockSpec((tile, D), lambda i: (i, 0)),
        out_shape=jax.ShapeDtypeStruct((N, D), F32),
        scratch_shapes=[pltpu.VMEM((2, _sorted_rows(tile, E) * lines, LANES), jnp.uint32),
                        pltpu.SemaphoreType.DMA((2,))],
        compiler_params=_params(("arbitrary",)),
        name="moe_combine",
    )(*runs, *runs[:3], lpos_nk, gate_nk, h2, ys, sg.astype(BF16), su.astype(BF16), sd.astype(BF16),
      g.reshape(1, D), beta.reshape(1, D))


FLASH_TQ = 512
FLASH_TK = 512
MOE_TILE = 256
RUN_UNROLL = 8
EXPERT_CAP = 1024
EXPERT_CHUNK = 64
EXPERT_AHEAD = 2


def _moe_ln(h2, rw, rb, wg, wu, wd, layer, sg, su, sd, g, beta, alpha):
    N, D = h2.shape
    E = rw.shape[1]
    _, lpos, cnt, tcnt, toff, tbase, ttot, pstart, gate_nk, lpos_nk = _router(h2, rw, rb)
    n_tiles = ttot.shape[0]
    counts = cnt[:, 0].astype(jnp.int32)
    pstart = pstart[:, 0]
    n_slots = N * TOP_K + n_tiles * E
    per_tile = lambda a: a.reshape(n_tiles, 1, E)
    runs = (per_tile(toff), per_tile(tbase) + pstart[None, None, :], per_tile(tcnt), ttot)
    xs = _dispatch(h2, lpos, runs, n_slots)
    ys = _experts(xs, pstart, counts, wg, wu, wd, layer)
    return _combine(h2, lpos_nk, gate_nk, runs, ys, sg, su, sd, g, beta, alpha)


def _mla_proj_kernel(h_ref, wa_ref, qg_ref, kvg_ref, wqt_ref, wk_ref, wvt_ref, cos_ref, sin_ref,
                     cost_ref, sint_ref, qt_ref, k_ref, vt_ref, *, q_lora, kv_lora, heads, q_scale):
    x = h_ref[0].astype(BF16)
    a = jnp.dot(x, wa_ref[...], preferred_element_type=F32)

    def rms(v, g):
        return v * lax.rsqrt(jnp.mean(v * v, axis=-1, keepdims=True) + RMS_EPS) * g

    q_lat = rms(a[:, :q_lora], qg_ref[...]).astype(BF16)
    kv_lat = rms(a[:, q_lora:q_lora + kv_lora], kvg_ref[...]).astype(BF16)
    o = q_lora + kv_lora
    k_rope = (a[:, o:o + LANES] * cos_ref[...] + a[:, o + LANES:o + 2 * LANES] * sin_ref[...]).astype(BF16)
    qa = lax.dot_general(wqt_ref[...], q_lat, _NT, preferred_element_type=F32)
    kn = jnp.dot(kv_lat, wk_ref[...], preferred_element_type=F32)
    vt_ref[0] = lax.dot_general(wvt_ref[...], kv_lat, _NT, preferred_element_type=F32).astype(BF16)
    cost = cost_ref[...]
    sint = sint_ref[...]
    hd = heads * LANES
    for h in range(heads):
        r0 = slice(h * LANES, (h + 1) * LANES)
        r1 = slice(hd + h * LANES, hd + (h + 1) * LANES)
        r2 = slice(2 * hd + h * LANES, 2 * hd + (h + 1) * LANES)
        qt_ref[0, 2 * h * LANES:(2 * h + 1) * LANES, :] = (qa[r0] * q_scale).astype(BF16)
        qt_ref[0, (2 * h + 1) * LANES:(2 * h + 2) * LANES, :] = (
            (qa[r1] * cost + qa[r2] * sint) * q_scale).astype(BF16)
        k_ref[0, :, 2 * h * LANES:(2 * h + 1) * LANES] = kn[:, r0].astype(BF16)
        k_ref[0, :, (2 * h + 1) * LANES:(2 * h + 2) * LANES] = k_rope


def _rot_half(w):
    r = w.shape[-1] // 2
    return jnp.concatenate([-w[..., r:], w[..., :r]], axis=-1)


def _pad_lanes(w):
    return jnp.pad(w, [(0, 0)] * (w.ndim - 1) + [(0, LANES - w.shape[-1])])


FLASH_HEADROOM = 64.0
FLASH_PROBE = 16


def _flash_kernel(qt_ref, k_ref, vt_ref, o_ref, p_ref, s_ref, mx_ref, m_ref, l_ref, acc_ref, *, tk):
    qt = qt_ref[0]
    nk = k_ref.shape[1] // tk
    assert nk % 2 == 0 and nk >= 2

    def scores(c):
        off = pl.multiple_of(c * tk, tk)
        return jnp.dot(k_ref[0, pl.ds(off, tk), :], qt, preferred_element_type=F32)

    m0 = jnp.max(jnp.dot(k_ref[0, 0:FLASH_PROBE, :], qt, preferred_element_type=F32), axis=0, keepdims=True)
    acc_ref[...] = jnp.zeros_like(acc_ref)

    def probs(c, slot, carry):
        l, top = carry
        s = scores(c)
        p = jnp.exp2(s - m0)
        p_ref[slot] = p.astype(BF16)
        return l + jnp.sum(p, axis=0, keepdims=True), jnp.maximum(top, jnp.max(s, axis=0, keepdims=True))

    def accumulate(c, slot):
        off = pl.multiple_of(c * tk, tk)
        acc_ref[...] += jnp.dot(vt_ref[0, :, pl.ds(off, tk)], p_ref[slot], preferred_element_type=F32)

    carry = probs(0, 0, (jnp.zeros_like(m0), m0))
    for c in range(nk - 1):
        carry = probs(c + 1, (c + 1) % 2, carry)
        accumulate(c, c % 2)
    accumulate(nk - 1, (nk - 1) % 2)
    l, top = carry
    o_ref[0] = (acc_ref[...] / l).T.astype(o_ref.dtype)

    @pl.when(jnp.max(top - m0) > FLASH_HEADROOM)
    def _():
        _flash_online(qt, k_ref, vt_ref, o_ref, s_ref, mx_ref, m_ref, l_ref, acc_ref, tk)


def _flash_online(qt, k_ref, vt_ref, o_ref, s_ref, mx_ref, m_ref, l_ref, acc_ref, tk):
    nk = k_ref.shape[1] // tk
    assert nk % 2 == 0 and nk >= 2
    m_ref[...] = jnp.full_like(m_ref, -jnp.inf)
    l_ref[...] = jnp.zeros_like(l_ref)
    acc_ref[...] = jnp.zeros_like(acc_ref)

    def scores(c, slot):
        off = pl.multiple_of(c * tk, tk)
        s = jnp.dot(k_ref[0, pl.ds(off, tk), :], qt, preferred_element_type=F32)
        s_ref[slot] = s
        mx_ref[slot] = jnp.max(s, axis=0, keepdims=True)

    def update(c, slot):
        off = pl.multiple_of(c * tk, tk)
        vt = vt_ref[0, :, pl.ds(off, tk)]
        m_prev = m_ref[...]
        m_new = jnp.maximum(m_prev, mx_ref[slot])
        p = jnp.exp2(s_ref[slot] - m_new)
        a = jnp.exp2(m_prev - m_new)
        l_ref[...] = a * l_ref[...] + jnp.sum(p, axis=0, keepdims=True)
        acc_ref[...] = a * acc_ref[...] + jnp.dot(vt, p.astype(BF16), preferred_element_type=F32)
        m_ref[...] = m_new

    scores(0, 0)

    def body(i, carry):
        c = 2 * i
        scores(c + 1, 1)
        update(c, 0)
        scores(c + 2, 0)
        update(c + 1, 1)
        return carry

    lax.fori_loop(0, nk // 2 - 1, body, 0)
    scores(nk - 1, 1)
    update(nk - 2, 0)
    update(nk - 1, 1)
    o_ref[0] = (acc_ref[...] / l_ref[...]).T.astype(o_ref.dtype)


def _oproj_ln_kernel(o_ref, h_ref, w_ref, g_ref, beta_ref, out_ref, *, alpha):
    m = jnp.dot(o_ref[0], w_ref[...], preferred_element_type=F32)
    out_ref[0] = _layer_norm(alpha * h_ref[0] + m, g_ref[...], beta_ref[...])


def _mla_ln(h, w_a, qg, kvg, w_uq, w_uk, w_uv, w_o, g, beta, alpha):
    B, S, D = h.shape
    q_lora, heads, qk = w_uq.shape
    kv_lora, _, nope = w_uk.shape
    vd = w_uv.shape[-1]
    rope = qk - nope
    assert nope == LANES and vd == LANES and rope <= LANES and rope % 2 == 0
    kr = w_a[:, q_lora + kv_lora:]
    wa = jnp.concatenate([w_a[:, :q_lora + kv_lora], _pad_lanes(kr), _pad_lanes(_rot_half(kr))],
                         axis=1).astype(BF16)
    wq_r = w_uq[:, :, nope:]
    wq = jnp.concatenate([w_uq[:, :, :nope].reshape(q_lora, heads * nope),
                          _pad_lanes(wq_r).reshape(q_lora, heads * LANES),
                          _pad_lanes(_rot_half(wq_r)).reshape(q_lora, heads * LANES)], axis=1).astype(BF16)
    wk = w_uk.reshape(kv_lora, heads * nope).astype(BF16)
    wvt = w_uv.reshape(kv_lora, heads * vd).T.astype(BF16)
    wqt = wq.T
    pos = jnp.arange(S, dtype=F32)
    inv = 1.0 / (ROPE_THETA ** (jnp.arange(0, rope, 2, dtype=F32) / rope))
    ang = pos[:, None] * inv[None, :]
    cos = _pad_lanes(jnp.concatenate([jnp.cos(ang), jnp.cos(ang)], axis=1))
    sin = _pad_lanes(jnp.concatenate([jnp.sin(ang), jnp.sin(ang)], axis=1))
    q_scale = float(qk) ** -0.5 * math.log2(math.e)

    tp = _pick(S, 256)
    const = lambda a: pl.BlockSpec(a.shape, lambda b, i: (0,) * a.ndim)
    qg2, kvg2 = qg.reshape(1, q_lora), kvg.reshape(1, kv_lora)
    tok_major = pl.BlockSpec((tp, LANES), lambda b, i: (i, 0))
    feat_major = pl.BlockSpec((LANES, tp), lambda b, i: (0, i))
    qt, k, vt = pl.pallas_call(
        functools.partial(_mla_proj_kernel, q_lora=q_lora, kv_lora=kv_lora, heads=heads, q_scale=q_scale),
        grid=(B, S // tp),
        in_specs=[pl.BlockSpec((1, tp, D), lambda b, i: (b, i, 0)), const(wa), const(qg2), const(kvg2),
                  const(wqt), const(wk), const(wvt), tok_major, tok_major, feat_major, feat_major],
        out_specs=[pl.BlockSpec((1, 2 * heads * LANES, tp), lambda b, i: (b, 0, i)),
                   pl.BlockSpec((1, tp, 2 * heads * LANES), lambda b, i: (b, i, 0)),
                   pl.BlockSpec((1, heads * LANES, tp), lambda b, i: (b, 0, i))],
        out_shape=[jax.ShapeDtypeStruct((B, 2 * heads * LANES, S), BF16),
                   jax.ShapeDtypeStruct((B, S, 2 * heads * LANES), BF16),
                   jax.ShapeDtypeStruct((B, heads * LANES, S), BF16)],
        compiler_params=_params(("parallel", "parallel")),
        name="mla_proj",
    )(h, wa, qg2, kvg2, wqt, wk, wvt, cos, sin, cos.T, sin.T)

    tq = _pick(S, FLASH_TQ)
    tk = _pick(S, FLASH_TK)
    o = pl.pallas_call(
        functools.partial(_flash_kernel, tk=tk),
        grid=(B, heads, S // tq),
        in_specs=[pl.BlockSpec((1, 2 * LANES, tq), lambda b, hh, i: (b, hh, i)),
                  pl.BlockSpec((1, S, 2 * LANES), lambda b, hh, i: (b, 0, hh)),
                  pl.BlockSpec((1, LANES, S), lambda b, hh, i: (b, hh, 0))],
        out_specs=pl.BlockSpec((1, tq, LANES), lambda b, hh, i: (b, i, hh)),
        out_shape=jax.ShapeDtypeStruct((B, S, heads * LANES), BF16),
        scratch_shapes=[pltpu.VMEM((2, tk, tq), BF16), pltpu.VMEM((2, tk, tq), F32), pltpu.VMEM((2, 1, tq), F32),
                        pltpu.VMEM((1, tq), F32), pltpu.VMEM((1, tq), F32), pltpu.VMEM((LANES, tq), F32)],
        compiler_params=_params(("parallel", "parallel", "arbitrary")),
        name="mla_flash",
    )(qt, k, vt)

    to = _pick(S, 512)
    vec = pl.BlockSpec((1, D), lambda b, i: (0, 0))
    return pl.pallas_call(
        functools.partial(_oproj_ln_kernel, alpha=alpha),
        grid=(B, S // to),
        in_specs=[pl.BlockSpec((1, to, heads * vd), lambda b, i: (b, i, 0)),
                  pl.BlockSpec((1, to, D), lambda b, i: (b, i, 0)),
                  pl.BlockSpec((heads * vd, D), lambda b, i: (0, 0)), vec, vec],
        out_specs=pl.BlockSpec((1, to, D), lambda b, i: (b, i, 0)),
        out_shape=jax.ShapeDtypeStruct((B, S, D), F32),
        compiler_params=_params(("parallel", "parallel")),
        name="mla_oproj_ln",
    )(o, h, w_o.astype(BF16), g.reshape(1, D), beta.reshape(1, D))


def kernel(x, pool_w, pool_b, pool_scale, mla_w_a, mla_q_norm_g, mla_kv_norm_g, mla_w_uq, mla_w_uk,
           mla_w_uv, mla_w_o, ln_mix_g, ln_mix_b, router_w, router_b, exp_w_gate, exp_w_up, exp_w_down,
           sh_w_gate, sh_w_up, sh_w_down, ln_ffn_g, ln_ffn_b):
    B, S, D = x.shape
    depth = ln_mix_g.shape[0]
    alpha = (2 * depth) ** 0.25
    h = x
    for i in range(depth):
        j = i // 2
        if i % 2 == 0:
            h = _pool_ln(h, pool_w[j], pool_b[j], pool_scale[j], ln_mix_g[i], ln_mix_b[i], alpha)
        else:
            h = _mla_ln(h, mla_w_a[j], mla_q_norm_g[j], mla_kv_norm_g[j], mla_w_uq[j], mla_w_uk[j],
                        mla_w_uv[j], mla_w_o[j], ln_mix_g[i], ln_mix_b[i], alpha)
        h = _moe_ln(h.reshape(B * S, D), router_w[i], router_b[i], exp_w_gate, exp_w_up, exp_w_down, i,
                    sh_w_gate[i], sh_w_up[i], sh_w_down[i], ln_ffn_g[i], ln_ffn_b[i],
                    alpha).reshape(B, S, D)
    return h
```
